```python
import jax
import jax.numpy as jnp
from jax import lax
import numpy as np

D_MODEL = 2048
BATCH = 8
SEQ = 2048
DEPTH = 2

GRID_W = 64
CTX_LEN = 256
N_BRANCH = 4
BRANCH_W = D_MODEL // 4
FN_GROUPS = 4
FN_W = BRANCH_W
NA_HEADS = 8
NA_DH = BRANCH_W // NA_HEADS
NA_KH = 8
NA_KW = 16
NA_SCALE = NA_DH ** -0.5
MLA_HEADS = 4
MLA_NOPE = 128
MLA_ROPE = 64
MLA_V = BRANCH_W // MLA_HEADS
MLA_QLORA = D_MODEL // 4
MLA_KVLORA = D_MODEL // 8
MLA_SCALE = (MLA_NOPE + MLA_ROPE) ** -0.5
RW_DH = 64
RW_HEADS = BRANCH_W // RW_DH
RW_W = RW_HEADS * RW_DH
RW_DECAY_LORA = 32
RW_AAA_LORA = 32
RW_GATE_LORA = 96
RW_GN_EPS = 64e-5
N_EXPERTS = 16
EXPERT_FF = D_MODEL // 2
CAPACITY_FACTOR = 2
ROPE_THETA = 10000.0
NORM_EPS = 1e-6
Q_BLOCK = 128
NEG_INF = -1e30
F32 = jnp.float32

RW_KS_SPEC = (('k', RW_W), ('v', RW_W), ('wf', RW_DECAY_LORA), ('wb', RW_DECAY_LORA),
              ('af', RW_AAA_LORA), ('ab', RW_AAA_LORA))
RW_QS_SPEC = (('r', RW_W), ('g', RW_GATE_LORA))
RW_KS = 2 * RW_W + 2 * RW_DECAY_LORA + 2 * RW_AAA_LORA
RW_QS = RW_W + RW_GATE_LORA
KEY_SPEC = (('na_k', NA_HEADS * NA_DH), ('na_v', NA_HEADS * NA_DH), ('mla_ckv', MLA_KVLORA),
            ('mla_kpe', MLA_ROPE), ('rw_ks', RW_KS))
QUERY_SPEC = (('na_q', NA_HEADS * NA_DH), ('mla_cq', MLA_QLORA), ('rw_qs', RW_QS), ('fn', FN_W),
              ('gate', N_BRANCH * D_MODEL))
KEY_COLS = 2 * NA_HEADS * NA_DH + MLA_KVLORA + MLA_ROPE + RW_KS
Z_COLS = KEY_COLS + NA_HEADS * NA_DH + MLA_QLORA + RW_QS + FN_W + N_BRANCH * D_MODEL

kernel_name = 'hybrid_gated_diffusion_block'


def _split(z, spec):
    out, off = {}, 0
    for name, width in spec:
        out[name] = z[..., off:off + width]
        off += width
    return out


def _rmsnorm(x, g):
    xf = x.astype(F32)
    y = xf * lax.rsqrt(jnp.mean(xf * xf, axis=-1, keepdims=True) + NORM_EPS)
    return (y * g.astype(F32)).astype(x.dtype)


def _modulate(x, g, shift, scale):
    return _rmsnorm(x, g) * (1 + scale) + shift


def _heads(t, n_heads, gain=None):
    t = t.reshape(t.shape[0], t.shape[1], n_heads, -1)
    return t if gain is None else _rmsnorm(t, gain)


def _rope_1d(x, pos):
    half = x.shape[-1] // 2
    freqs = ROPE_THETA ** (-jnp.arange(half, dtype=F32) / half)
    ang = pos.astype(F32)[:, None] * freqs[None, :]
    cos, sin = jnp.cos(ang)[None, :, None, :], jnp.sin(ang)[None, :, None, :]
    xf = x.astype(F32)
    x1, x2 = xf[..., :half], xf[..., half:]
    return jnp.concatenate([x1 * cos - x2 * sin, x1 * sin + x2 * cos], -1).astype(x.dtype)


def _rope_2d(x, prow, pcol):
    half = x.shape[-1] // 2
    return jnp.concatenate([_rope_1d(x[..., :half], prow), _rope_1d(x[..., half:], pcol)], -1)


def _rope_pe(t, prow, pcol):
    return jnp.concatenate([t[..., :MLA_NOPE], _rope_2d(t[..., MLA_NOPE:], prow, pcol)], -1)


def _softmax_attn(q, k, v, scale):
    s = jnp.einsum('bqhd,bkhd->bhqk', q, k).astype(F32) * scale
    p = jax.nn.softmax(s, axis=-1).astype(v.dtype)
    return jnp.einsum('bhqk,bkhd->bqhd', p, v)


def _blocked_attn(q, k, v, scale):
    B, L, H, d = q.shape
    qb = jnp.moveaxis(q.reshape(B, L // Q_BLOCK, Q_BLOCK, H, d), 1, 0)
    o = lax.map(lambda qi: _softmax_attn(qi, k, v, scale), qb)
    return jnp.moveaxis(o, 0, 1).reshape(B, L, H, v.shape[-1])


def _neigh_attn_latent(q, k, v, kc, vc, rpb):
    B, L, H, dh = q.shape
    rows = L // GRID_W
    kh = min(NA_KH, rows)
    r = jnp.arange(rows)
    col = jnp.arange(GRID_W)
    rs = jnp.clip(r - kh // 2, 0, rows - kh)
    band = rs[:, None] + jnp.arange(kh)[None, :]
    cs = jnp.clip(col - NA_KW // 2, 0, GRID_W - NA_KW)
    valid = (col[None, :] >= cs[:, None]) & (col[None, :] < cs[:, None] + NA_KW)
    dr_idx = band - r[:, None] + NA_KH - 1
    dc_idx = jnp.clip(col[None, :] - col[:, None] + NA_KW - 1, 0, 2 * NA_KW - 2)
    bias = rpb.astype(F32)[:, dr_idx[:, None, :, None], dc_idx[None, :, None, :]]
    bias = jnp.where(valid[None, None, :, None, :], bias, NEG_INF)
    qg = q.reshape(B, rows, GRID_W, H, dh)
    kb = k.reshape(B, rows, GRID_W, H, dh)[:, band]
    vb = v.reshape(B, rows, GRID_W, H, dh)[:, band]
    s_loc = jnp.einsum('brqhd,brikhd->bhrqik', qg, kb).astype(F32) * NA_SCALE + bias[None]
    s_ctx = jnp.einsum('brqhd,bmhd->bhrqm', qg, kc).astype(F32) * NA_SCALE
    n_loc = kh * GRID_W
    s = jnp.concatenate([s_loc.reshape(B, H, rows, GRID_W, n_loc), s_ctx], axis=-1)
    p = jax.nn.softmax(s, axis=-1).astype(v.dtype)
    p_loc = p[..., :n_loc].reshape(B, H, rows, GRID_W, kh, GRID_W)
    o = (jnp.einsum('bhrqik,brikhd->brqhd', p_loc, vb)
         + jnp.einsum('bhrqm,bmhd->brqhd', p[..., n_loc:], vc))
    return o.reshape(B, L, H, dh)


def _mla_q(cq, p, prow=None, pcol=None):
    B, N, _ = cq.shape
    q = (_rmsnorm(cq, p['mla_cq_norm']) @ p['mla_w_uq']).reshape(B, N, MLA_HEADS, MLA_NOPE + MLA_ROPE)
    q = _rmsnorm(q, p['mla_q_norm'])
    return q if prow is None else _rope_pe(q, prow, pcol)


def _mla_kv(ckv, kpe, p, prow=None, pcol=None):
    B, N, _ = ckv.shape
    kv = (_rmsnorm(ckv, p['mla_ckv_norm']) @ p['mla_w_ukv']).reshape(B, N, MLA_HEADS, MLA_NOPE + MLA_V)
    k_pe = jnp.broadcast_to(kpe[:, :, None, :], (B, N, MLA_HEADS, MLA_ROPE))
    k = _rmsnorm(jnp.concatenate([kv[..., :MLA_NOPE], k_pe], axis=-1), p['mla_k_norm'])
    if prow is not None:
        k = _rope_pe(k, prow, pcol)
    return k, kv[..., MLA_NOPE:]


def _fourier(zf):
    B, N, _ = zf.shape
    g = zf.astype(F32).reshape(B, N, FN_GROUPS, FN_W // FN_GROUPS).transpose(0, 2, 1, 3)
    y = jnp.real(jnp.fft.fft2(g, norm='ortho'))
    return y.transpose(0, 2, 1, 3).reshape(B, N, FN_W).astype(zf.dtype)


def _shift_mix(z, mu):
    zp = jnp.pad(z, ((0, 0), (1, 1), (0, 0)))
    sh = 0.5 * (zp[:, :-2] + zp[:, 2:])
    return z + (sh - z) * mu


def _rwkv_prepare(zks, p):
    B, N, _ = zks.shape
    s = _split(zks.astype(F32), RW_KS_SPEC)
    hd = lambda t: t.reshape(B, N, RW_HEADS, RW_DH)
    k = s['k']
    v = hd(s['v'])
    kk = hd(k * p['rw_k_k'])
    kk = kk / jnp.maximum(jnp.sqrt(jnp.sum(kk * kk, axis=-1, keepdims=True)), 1e-12)
    dirs = []
    for d, (wn, an) in enumerate((('wf', 'af'), ('wb', 'ab'))):
        w = p['rw_w0'][d] + jnp.tanh(s[wn]) @ p['rw_w2'][d]
        decay = jnp.exp(-jnp.exp(-jax.nn.softplus(-w) - 0.5))
        a = jax.nn.sigmoid(p['rw_a0'][d] + s[an] @ p['rw_a2'][d])
        kd = k * (1.0 + (a - 1.0) * p['rw_k_a'])
        dirs.append((hd(decay), kk * hd(a), hd(kd)))
    return v, kk, dirs


def _wkv7(S0, decay, kk, b, k, v, r=None, reverse=False):
    seq = [jnp.moveaxis(t, 1, 0) for t in (decay, kk, b, k, v)]
    if r is not None:
        seq.append(jnp.moveaxis(r, 1, 0))

    def step(S, xs):
        w_t, kk_t, b_t, k_t, v_t = xs[:5]
        sa = jnp.einsum('bhvk,bhk->bhv', S, kk_t)
        S = S * w_t[:, :, None, :] - sa[..., None] * b_t[:, :, None, :] + v_t[..., None] * k_t[:, :, None, :]
        y = jnp.einsum('bhvk,bhk->bhv', S, xs[5]) if r is not None else None
        return S, y

    S, ys = lax.scan(step, S0, tuple(seq), reverse=reverse)
    return S, (None if r is None else jnp.moveaxis(ys, 0, 1))


def _rwkv_out(y, r, k_sum, v, g_lora, p):
    B, N = y.shape[:2]
    mu = jnp.mean(y, axis=-1, keepdims=True)
    var = jnp.mean(jnp.square(y - mu), axis=-1, keepdims=True)
    yn = ((y - mu) * lax.rsqrt(var + RW_GN_EPS)).reshape(B, N, RW_W) * p['rw_ln_w'] + p['rw_ln_b']
    bonus = jnp.sum(r * k_sum * p['rw_r_k'], axis=-1, keepdims=True) * v
    gate = jax.nn.sigmoid(g_lora) @ p['rw_g2']
    return (yn + bonus.reshape(B, N, RW_W)) * gate


def _rwkv_branch(zks, zqs, zks_c, zqs_c, p):
    B = zks.shape[0]
    v, kk, dirs = _rwkv_prepare(_shift_mix(zks, p['rw_mu_ks']), p)
    vc, kkc, dirs_c = _rwkv_prepare(_shift_mix(zks_c, p['rw_mu_ks']), p)
    qs = _split(_shift_mix(zqs, p['rw_mu_qs']).astype(F32), RW_QS_SPEC)
    r = qs['r'].reshape(B, -1, RW_HEADS, RW_DH)
    if zqs_c is not None:
        qsc = _split(_shift_mix(zqs_c, p['rw_mu_qs']).astype(F32), RW_QS_SPEC)
        rc = qsc['r'].reshape(B, -1, RW_HEADS, RW_DH)
    else:
        qsc, rc = None, None
    S0 = jnp.zeros((B, RW_HEADS, RW_DH, RW_DH), F32)
    ys, ycs = [], []
    for d, reverse in enumerate((False, True)):
        decay_c, b_c, k_c = dirs_c[d]
        S_c, y_c = _wkv7(S0, decay_c, kkc, b_c, k_c, vc, rc, reverse)
        decay_l, b_l, k_l = dirs[d]
        _, y_l = _wkv7(S_c, decay_l, kk, b_l, k_l, v, r, reverse)
        ys.append(y_l)
        ycs.append(y_c)
    out = _rwkv_out(ys[0] + ys[1], r, dirs[0][2] + dirs[1][2], v, qs['g'], p).astype(zks.dtype)
    if rc is None:
        return out, None
    out_c = _rwkv_out(ycs[0] + ycs[1], rc, dirs_c[0][2] + dirs_c[1][2], vc, qsc['g'], p).astype(zks.dtype)
    return out, out_c


def _merge(branches, gate_cols, w_br, w_out):
    B, N, _ = gate_cols.shape
    gates = jax.nn.sigmoid(gate_cols.astype(F32)).astype(gate_cols.dtype).reshape(B, N, N_BRANCH, D_MODEL)
    acc = gates[:, :, 0] * (branches[0] @ w_br[0])
    for i in range(1, N_BRANCH):
        acc = acc + gates[:, :, i] * (branches[i] @ w_br[i])
    return acc @ w_out


def _mixer(h, hc, p, ctx_out):
    B, L, _ = h.shape
    n_ctx = hc.shape[1]
    pos = jnp.arange(L)
    prow, pcol = pos // GRID_W, pos % GRID_W
    z = h @ p['w_in']
    zk = _split(z[..., :KEY_COLS], KEY_SPEC)
    zq = _split(z[..., KEY_COLS:], QUERY_SPEC)
    zc = hc @ (p['w_in'] if ctx_out else p['w_in'][:, :KEY_COLS])
    zck = _split(zc[..., :KEY_COLS], KEY_SPEC)
    zcq = _split(zc[..., KEY_COLS:], QUERY_SPEC) if ctx_out else None

    na_k = _heads(zk['na_k'], NA_HEADS, p['na_k_norm'])
    na_v = _heads(zk['na_v'], NA_HEADS)
    na_kc = _heads(zck['na_k'], NA_HEADS, p['na_k_norm'])
    na_vc = _heads(zck['na_v'], NA_HEADS)
    na_q = _heads(zq['na_q'], NA_HEADS, p['na_q_norm'])
    y_na = _neigh_attn_latent(na_q, na_k, na_v, na_kc, na_vc, p['na_rpb']).reshape(B, L, BRANCH_W)

    m_k, m_v = _mla_kv(zk['mla_ckv'], zk['mla_kpe'], p, prow, pcol)
    m_kc, m_vc = _mla_kv(zck['mla_ckv'], zck['mla_kpe'], p)
    m_q = _mla_q(zq['mla_cq'], p, prow, pcol)
    y_mla = _blocked_attn(m_q, jnp.concatenate([m_k, m_kc], axis=1), jnp.concatenate([m_v, m_vc], axis=1),
                          MLA_SCALE).reshape(B, L, BRANCH_W)

    y_rw, yc_rw = _rwkv_branch(zk['rw_ks'], zq['rw_qs'], zck['rw_ks'],
                               zcq['rw_qs'] if ctx_out else None, p)

    y_fn = _fourier(zq['fn'])

    y = _merge((y_fn, y_na, y_mla, y_rw), zq['gate'], p['w_br'], p['w_out'])
    if not ctx_out:
        return y, None
    yc_na = _softmax_attn(_heads(zcq['na_q'], NA_HEADS, p['na_q_norm']), na_kc, na_vc,
                          NA_SCALE).reshape(B, n_ctx, BRANCH_W)
    yc_mla = _softmax_attn(_mla_q(zcq['mla_cq'], p), m_kc, m_vc, MLA_SCALE).reshape(B, n_ctx, BRANCH_W)
    yc_fn = _fourier(zcq['fn'])
    yc = _merge((yc_fn, yc_na, yc_mla, yc_rw), zcq['gate'], p['w_br'], p['w_out'])
    return y, yc


def _expert_choice_moe(h, w_router, w1, w3, w2):
    B, N, D = h.shape
    cap = CAPACITY_FACTOR * N // N_EXPERTS
    aff = jax.nn.softmax((h @ w_router).astype(F32), axis=-1)
    gate, idx = lax.top_k(jnp.swapaxes(aff, 1, 2), cap)
    xe = jax.vmap(lambda hb, ib: hb[ib])(h, idx)
    a = jnp.einsum('becd,edf->becf', xe, w1)
    u = jnp.einsum('becd,edf->becf', xe, w3)
    ye = jnp.einsum('becf,efd->becd', jax.nn.silu(a) * u, w2) * gate[..., None].astype(h.dtype)
    return jax.vmap(lambda yb, ib: jnp.zeros((N, D), yb.dtype).at[ib.reshape(-1)].add(yb.reshape(-1, D)))(ye, idx)


def setup_inputs(seed: int = 0) -> dict:
    key = jax.random.key(seed)
    keys = iter(jax.random.split(key, 48))

    def nrm(shape, scale):
        return jax.random.normal(next(keys), shape, F32) * scale

    def gain(shape):
        return 1.0 + nrm(shape, 0.02)

    def unif(shape, lo, hi):
        return jax.random.uniform(next(keys), shape, F32, lo, hi)

    return {
        'x': nrm((BATCH, SEQ, D_MODEL), 1.0),
        'c': nrm((BATCH, D_MODEL), 1.0),
        'ctx': nrm((BATCH, CTX_LEN, D_MODEL), 1.0),
        'c_ctx': nrm((D_MODEL,), 1.0),
        'ada_w': nrm((DEPTH, D_MODEL, 6 * D_MODEL), 0.5 * D_MODEL ** -0.5),
        'ada_b': nrm((DEPTH, 6 * D_MODEL), 0.01),
        'norm1_g': gain((DEPTH, D_MODEL)),
        'norm2_g': gain((DEPTH, D_MODEL)),
        'w_in': nrm((DEPTH, D_MODEL, Z_COLS), D_MODEL ** -0.5),
        'na_q_norm': gain((DEPTH, NA_DH)),
        'na_k_norm': gain((DEPTH, NA_DH)),
        'na_rpb': nrm((DEPTH, NA_HEADS, 2 * NA_KH - 1, 2 * NA_KW - 1), 0.1),
        'mla_cq_norm': gain((DEPTH, MLA_QLORA)),
        'mla_ckv_norm': gain((DEPTH, MLA_KVLORA)),
        'mla_w_uq': nrm((DEPTH, MLA_QLORA, MLA_HEADS * (MLA_NOPE + MLA_ROPE)), MLA_QLORA ** -0.5),
        'mla_w_ukv': nrm((DEPTH, MLA_KVLORA, MLA_HEADS * (MLA_NOPE + MLA_V)), MLA_KVLORA ** -0.5),
        'mla_q_norm': gain((DEPTH, MLA_NOPE + MLA_ROPE)),
        'mla_k_norm': gain((DEPTH, MLA_NOPE + MLA_ROPE)),
        'rw_mu_ks': unif((DEPTH, RW_KS), 0.0, 1.0),
        'rw_mu_qs': unif((DEPTH, RW_QS), 0.0, 1.0),
        'rw_w0': unif((DEPTH, 2, RW_W), -4.0, 1.0),
        'rw_w2': nrm((DEPTH, 2, RW_DECAY_LORA, RW_W), 0.1),
        'rw_a0': nrm((DEPTH, 2, RW_W), 0.1),
        'rw_a2': nrm((DEPTH, 2, RW_AAA_LORA, RW_W), 0.5 * RW_AAA_LORA ** -0.5),
        'rw_g2': nrm((DEPTH, RW_GATE_LORA, RW_W), RW_GATE_LORA ** -0.5),
        'rw_k_k': 0.85 + nrm((DEPTH, RW_W), 0.05),
        'rw_k_a': 1.0 + nrm((DEPTH, RW_W), 0.05),
        'rw_r_k': nrm((DEPTH, RW_HEADS, RW_DH), 0.1),
        'rw_ln_w': gain((DEPTH, RW_W)),
        'rw_ln_b': nrm((DEPTH, RW_W), 0.01),
        'w_br': nrm((DEPTH, N_BRANCH, BRANCH_W, D_MODEL), BRANCH_W ** -0.5),
        'w_out': nrm((DEPTH, D_MODEL, D_MODEL), D_MODEL ** -0.5),
        'moe_router': nrm((DEPTH, D_MODEL, N_EXPERTS), D_MODEL ** -0.5),
        'moe_w1': nrm((DEPTH, N_EXPERTS, D_MODEL, EXPERT_FF), D_MODEL ** -0.5),
        'moe_w3': nrm((DEPTH, N_EXPERTS, D_MODEL, EXPERT_FF), D_MODEL ** -0.5),
        'moe_w2': nrm((DEPTH, N_EXPERTS, EXPERT_FF, D_MODEL), EXPERT_FF ** -0.5),
    }


def reference(x, c, ctx, c_ctx, ada_w, ada_b, norm1_g, norm2_g, w_in, na_q_norm, na_k_norm, na_rpb,
              mla_cq_norm, mla_ckv_norm, mla_w_uq, mla_w_ukv, mla_q_norm, mla_k_norm,
              rw_mu_ks, rw_mu_qs, rw_w0, rw_w2, rw_a0, rw_a2, rw_g2, rw_k_k, rw_k_a, rw_r_k,
              rw_ln_w, rw_ln_b, w_br, w_out, moe_router, moe_w1, moe_w3, moe_w2):
    silu_c = jax.nn.silu(c)
    silu_cc = jax.nn.silu(c_ctx)[None, :]
    xc = ctx
    for layer in range(DEPTH):
        last = layer == DEPTH - 1
        p = {
            'w_in': w_in[layer], 'na_q_norm': na_q_norm[layer], 'na_k_norm': na_k_norm[layer],
            'na_rpb': na_rpb[layer], 'mla_cq_norm': mla_cq_norm[layer], 'mla_ckv_norm': mla_ckv_norm[layer],
            'mla_w_uq': mla_w_uq[layer], 'mla_w_ukv': mla_w_ukv[layer], 'mla_q_norm': mla_q_norm[layer],
            'mla_k_norm': mla_k_norm[layer], 'rw_mu_ks': rw_mu_ks[layer], 'rw_mu_qs': rw_mu_qs[layer],
            'rw_w0': rw_w0[layer], 'rw_w2': rw_w2[layer], 'rw_a0': rw_a0[layer], 'rw_a2': rw_a2[layer],
            'rw_g2': rw_g2[layer], 'rw_k_k': rw_k_k[layer], 'rw_k_a': rw_k_a[layer], 'rw_r_k': rw_r_k[layer],
            'rw_ln_w': rw_ln_w[layer], 'rw_ln_b': rw_ln_b[layer], 'w_br': w_br[layer], 'w_out': w_out[layer],
        }
        mod = (silu_c @ ada_w[layer] + ada_b[layer])[:, None, :]
        mod_c = (silu_cc @ ada_w[layer] + ada_b[layer])[:, None, :]
        sh1, sc1, g1, sh2, sc2, g2 = jnp.split(mod, 6, axis=-1)
        csh1, csc1, cg1, csh2, csc2, cg2 = jnp.split(mod_c, 6, axis=-1)
        h = _modulate(x, norm1_g[layer], sh1, sc1)
        hc = _modulate(xc, norm1_g[layer], csh1, csc1)
        y, yc = _mixer(h, hc, p, not last)
        x = x + g1 * y
        x = x + g2 * _expert_choice_moe(_modulate(x, norm2_g[layer], sh2, sc2), moe_router[layer],
                                        moe_w1[layer], moe_w3[layer], moe_w2[layer])
        if not last:
            xc = xc + cg1 * yc
            xc = xc + cg2 * _expert_choice_moe(_modulate(xc, norm2_g[layer], csh2, csc2), moe_router[layer],
                                               moe_w1[layer], moe_w3[layer], moe_w2[layer])
    return x
```

```python
import functools
import math

import jax
import jax.numpy as jnp
from jax import lax
from jax.experimental import pallas as pl
from jax.experimental.pallas import tpu as pltpu

F32 = jnp.float32
BF16 = jnp.bfloat16

GRID_W = 64
N_BRANCH = 4
BRANCH_W = 512
NA_HEADS, NA_DH, NA_KH, NA_KW = 8, 64, 8, 16
NA_SCALE = NA_DH ** -0.5
MLA_HEADS, MLA_NOPE, MLA_ROPE, MLA_V = 4, 128, 64, 128
MLA_QLORA, MLA_KVLORA = 512, 256
MLA_SCALE = (MLA_NOPE + MLA_ROPE) ** -0.5
RW_HEADS, RW_DH, RW_W = 8, 64, 512
RW_LORA, RW_GATE_LORA = 32, 96
RW_GN_EPS = 64e-5
N_EXPERTS = 16
CAPACITY_FACTOR = 2
ROPE_THETA = 10000.0
NORM_EPS = 1e-6
NEG_INF = -1e30
RW_CHUNK = 64
KEY_COLS = 2496
NZ = 5120
ZC_NA = 0
ZC_CQ = 1536
ZC_RWK = 2048
ZC_FN = 3584
ZC_CKV = 4096
ZC_KPE = 4352
ZC_L4 = 4480
ZC_G = 4608

VMEM_LIMIT = 56 * 2 ** 20


def _call(body, grid, in_specs, out_specs, out_shape, scratch=(), sem=None, name=None):
    return pl.pallas_call(
        body, grid=grid, in_specs=in_specs, out_specs=out_specs, out_shape=out_shape,
        scratch_shapes=list(scratch), name=name,
        compiler_params=pltpu.CompilerParams(
            dimension_semantics=sem or ("parallel",) * len(grid), vmem_limit_bytes=VMEM_LIMIT))


def _full(a):
    nd = a.ndim
    return pl.BlockSpec(a.shape, lambda *_: (0,) * nd)


def _sds(shape, dtype):
    return jax.ShapeDtypeStruct(shape, dtype)


def _dot(a, b):
    return jnp.dot(a, b, preferred_element_type=F32)


def _dot_nt(a, b):
    return lax.dot_general(a, b, (((1,), (1,)), ((), ())), preferred_element_type=F32)


def _split2(x):
    hi = x.astype(BF16)
    return hi, (x - hi.astype(F32)).astype(BF16)


def _split3(x):
    hi = x.astype(BF16)
    r = x - hi.astype(F32)
    mid = r.astype(BF16)
    return hi, mid, (r - mid.astype(F32)).astype(BF16)


def _dot3(a, b):
    ah, al = _split2(a)
    bh, bl = _split2(b)
    return _dot(ah, bh) + (_dot(ah, bl) + _dot(al, bh))


def _dot_sel(sel_bf16, x):
    h, m, l = _split3(x)
    return _dot(sel_bf16, h) + (_dot(sel_bf16, m) + _dot(sel_bf16, l))


def _bmm(spec, a, b, passes):
    e = functools.partial(jnp.einsum, spec, preferred_element_type=F32)
    if passes == 1:
        return e(a.astype(BF16), b.astype(BF16))
    ah, al = _split2(a)
    bh, bl = _split2(b)
    return e(ah, bh) + (e(ah, bl) + e(al, bh))


def _pick_tile(n, cands):
    for c in cands:
        if n % c == 0:
            return c
    raise ValueError(f"no tile for {n}")


def _mod_body(c_ref, w_ref, b_ref, o_ref):
    cc = c_ref[...]
    s = cc * jax.nn.sigmoid(cc)
    o_ref[0] = _dot3(s, w_ref[0]) + b_ref[0]


def _modulation(cc, ada_w, ada_b):
    depth, d, n6 = ada_w.shape
    r = cc.shape[0]
    tn = 1024
    return _call(
        _mod_body, (depth, n6 // tn),
        [pl.BlockSpec((r, d), lambda l, n: (0, 0)),
         pl.BlockSpec((1, d, tn), lambda l, n: (l, 0, n)),
         pl.BlockSpec((1, 1, tn), lambda l, n: (l, 0, n))],
        pl.BlockSpec((1, r, tn), lambda l, n: (l, 0, n)),
        _sds((depth, r, n6), F32), name="adaln_mod")(cc, ada_w, ada_b.reshape(depth, 1, n6))


def _modnorm(x, g, shift, scale):
    y = x * lax.rsqrt(jnp.mean(x * x, axis=-1, keepdims=True) + NORM_EPS)
    return (y * g) * (1.0 + scale) + shift


def _norm1_body(x_ref, g_ref, md_ref, h_ref):
    h = _modnorm(x_ref[0], g_ref[...], md_ref[0, 0, 0:1, :], md_ref[0, 0, 1:2, :])
    h_ref[0] = h.astype(BF16)


def _norm2_body(x_ref, g_ref, md_ref, wr_ref, h_ref, lg_ref):
    h = _modnorm(x_ref[0], g_ref[...], md_ref[0, 0, 3:4, :], md_ref[0, 0, 4:5, :])
    h_ref[0] = h.astype(BF16)
    lg_ref[0] = _dot3(h, wr_ref[...])


def _md_spec(d, nl):
    return pl.BlockSpec((1, 1, 6, d), lambda b, i: (b, i // nl, 0, 0))


def _norm1(xs, g, md, tr, nl):
    b, t, d = xs.shape
    return _call(
        _norm1_body, (b, t // tr),
        [pl.BlockSpec((1, tr, d), lambda b, i: (b, i, 0)), _full(g), _md_spec(d, nl)],
        pl.BlockSpec((1, tr, d), lambda b, i: (b, i, 0)),
        _sds((b, t, d), BF16), name="norm1")(xs, g, md)


def _norm2_router(x1, g, md, wr, tr, nl):
    b, t, d = x1.shape
    return _call(
        _norm2_body, (b, t // tr),
        [pl.BlockSpec((1, tr, d), lambda b, i: (b, i, 0)), _full(g), _md_spec(d, nl), _full(wr)],
        [pl.BlockSpec((1, tr, d), lambda b, i: (b, i, 0)), pl.BlockSpec((1, tr, 128), lambda b, i: (b, i, 0))],
        [_sds((b, t, d), BF16), _sds((b, t, 128), F32)], name="norm2_router")(x1, g, md, wr)


def _mm_body(a_ref, w_ref, o_ref):
    o_ref[...] = _dot(a_ref[...], w_ref[...]).astype(o_ref.dtype)


def _matmul(a, w, out_dtype):
    r, k = a.shape
    n = w.shape[1]
    tm = _pick_tile(r, (1024, 768, 512, 384, 256, 128, 64, 32, 16, 8))
    tn = _pick_tile(n, (1024, 512, 256, 128))
    return _call(
        _mm_body, (r // tm, n // tn),
        [pl.BlockSpec((tm, k), lambda i, j: (i, 0)), pl.BlockSpec((k, tn), lambda i, j: (0, j))],
        pl.BlockSpec((tm, tn), lambda i, j: (i, j)),
        _sds((r, n), out_dtype), name="in_proj")(a, w)


def _head_rms(x, g):
    return x * lax.rsqrt(jnp.mean(x * x, axis=-1, keepdims=True) + NORM_EPS) * g


def _na_prep_body(z_ref, gq_ref, gk_ref, q_ref, k_ref, v_ref):
    z = z_ref[0]
    for h in range(NA_HEADS):
        lo = h * NA_DH
        q = z[:, lo:lo + NA_DH]
        k = z[:, 512 + lo:512 + lo + NA_DH]
        v = z[:, 1024 + lo:1024 + lo + NA_DH]
        q_ref[0, h] = (_head_rms(q, gq_ref[...]) * NA_SCALE).astype(BF16)
        k_ref[0, h] = _head_rms(k, gk_ref[...]).astype(BF16)
        v_ref[0, h] = v.astype(BF16)


def _na_prep(z3, gq, gk, tr):
    b, t, _ = z3.shape
    hm = pl.BlockSpec((1, NA_HEADS, tr, NA_DH), lambda b, i: (b, 0, i, 0))
    shp = _sds((b, NA_HEADS, t, NA_DH), BF16)
    return _call(
        _na_prep_body, (b, t // tr),
        [pl.BlockSpec((1, tr, 1536), lambda b, i: (b, i, 0)), _full(gq), _full(gk)],
        [hm, hm, hm], [shp, shp, shp], name="na_prep")(z3, gq, gk)


def _softmax_pv(s, vs):
    m = s[0].max(axis=-1, keepdims=True)
    for t in s[1:]:
        m = jnp.maximum(m, t.max(axis=-1, keepdims=True))
    den = None
    acc = None
    for t, v in zip(s, vs):
        p = jnp.exp(t - m)
        ps = p.sum(axis=-1, keepdims=True)
        o = _dot(p.astype(BF16), v)
        den = ps if den is None else den + ps
        acc = o if acc is None else acc + o
    return acc / den


def _na_lat_body(q_ref, k_ref, v_ref, bias_ref, o_ref, *, rows, kh, l_tok, m_tok):
    r = pl.program_id(1)
    rs = jnp.clip(r - kh // 2, 0, rows - kh)
    delta = r - rs
    start = pl.multiple_of(rs * GRID_W, GRID_W)
    outs = []
    for h in range(NA_HEADS):
        q = q_ref[0, h]
        kb = k_ref[0, h, pl.ds(start, kh * GRID_W), :]
        vb = v_ref[0, h, pl.ds(start, kh * GRID_W), :]
        kc = k_ref[0, h, l_tok:l_tok + m_tok, :]
        vc = v_ref[0, h, l_tok:l_tok + m_tok, :]
        s_loc = _dot_nt(q, kb) + bias_ref[h, delta]
        s_ctx = _dot_nt(q, kc)
        outs.append(_softmax_pv([s_loc, s_ctx], [vb, vc]))
    o_ref[0] = jnp.concatenate(outs, axis=-1).astype(BF16)


def _na_latent(q, k, v, bias, l_tok, m_tok):
    b, _, t, _ = q.shape
    rows = l_tok // GRID_W
    kh = min(NA_KH, rows)
    kv = pl.BlockSpec((1, NA_HEADS, t, NA_DH), lambda b, r: (b, 0, 0, 0))
    return _call(
        functools.partial(_na_lat_body, rows=rows, kh=kh, l_tok=l_tok, m_tok=m_tok), (b, rows),
        [pl.BlockSpec((1, NA_HEADS, GRID_W, NA_DH), lambda b, r: (b, 0, r, 0)), kv, kv, _full(bias)],
        pl.BlockSpec((1, GRID_W, BRANCH_W), lambda b, r: (b, r, 0)),
        _sds((b, l_tok, BRANCH_W), BF16), name="na_latent")(q, k, v, bias)


def _na_bias_table(rpb, rows):
    kh = min(NA_KH, rows)
    col = jnp.arange(GRID_W)
    cs = jnp.clip(col - NA_KW // 2, 0, GRID_W - NA_KW)
    valid = (col[None, :] >= cs[:, None]) & (col[None, :] < cs[:, None] + NA_KW)
    dr = jnp.arange(kh)[None, :] - jnp.arange(kh)[:, None] + NA_KH - 1
    dc = jnp.clip(col[None, :] - col[:, None] + NA_KW - 1, 0, 2 * NA_KW - 2)
    bias = rpb.astype(F32)[:, dr[:, None, :, None], dc[None, :, None, :]]
    bias = jnp.where(valid[None, None, :, None, :], bias, NEG_INF)
    return bias.reshape(rpb.shape[0], kh, GRID_W, kh * GRID_W)


def _attn_body(*refs, nparts, heads):
    q_refs, k_refs, v_ref, o_ref = refs[:nparts], refs[nparts:2 * nparts], refs[2 * nparts], refs[-1]
    outs = []
    for h in range(heads):
        s = None
        for qr, kr in zip(q_refs, k_refs):
            t = _dot_nt(qr[0, h], kr[0, h])
            s = t if s is None else s + t
        outs.append(_softmax_pv([s], [v_ref[0, h]]))
    o_ref[0] = jnp.concatenate(outs, axis=-1).astype(BF16)


def _attention(qs, ks, v, *, tq, q0, nq, tk, kblk, name):
    b, heads, _, dv = v.shape
    qspec = [pl.BlockSpec((1, heads, tq, q.shape[-1]), lambda b, i: (b, 0, q0 + i, 0)) for q in qs]
    kspec = [pl.BlockSpec((1, heads, tk, k.shape[-1]), lambda b, i: (b, 0, kblk, 0)) for k in ks]
    vspec = pl.BlockSpec((1, heads, tk, dv), lambda b, i: (b, 0, kblk, 0))
    return _call(
        functools.partial(_attn_body, nparts=len(qs), heads=heads), (b, nq),
        qspec + kspec + [vspec],
        pl.BlockSpec((1, tq, heads * dv), lambda b, i: (b, i, 0)),
        _sds((b, nq * tq, heads * dv), BF16), name=name)(*qs, *ks, v)


def _mla_q_body(z_ref, gc_ref, w_ref, gh_ref, ct_ref, st_ref, qn_ref, qp_ref):
    cq = z_ref[0]
    cqn = (cq * lax.rsqrt(jnp.mean(cq * cq, axis=-1, keepdims=True) + NORM_EPS) * gc_ref[...]).astype(BF16)
    q = _dot(cqn, w_ref[...])
    gh = gh_ref[...]
    ct, st = ct_ref[...], st_ref[...]
    for h in range(MLA_HEADS):
        qh = q[:, h * 256:(h + 1) * 256]
        nope, pe, sw = qh[:, :128], qh[:, 128:192], qh[:, 192:256]
        ms = (jnp.sum(nope * nope, axis=-1, keepdims=True) + jnp.sum(pe * pe, axis=-1, keepdims=True)) \
            / (MLA_NOPE + MLA_ROPE)
        rinv = lax.rsqrt(ms + NORM_EPS)
        qn_ref[0, h] = (nope * rinv * gh[:, :128] * MLA_SCALE).astype(BF16)
        rot = (pe * rinv * gh[:, 128:192]) * ct + (sw * rinv * gh[:, 192:256]) * st
        qp_ref[0, h] = (rot * MLA_SCALE).astype(BF16)


def _mla_kv_body(zc_ref, zp_ref, gc_ref, w_ref, gh_ref, ct_ref, st_ref, kn_ref, kp_ref, v_ref):
    ckv = zc_ref[0]
    cn = (ckv * lax.rsqrt(jnp.mean(ckv * ckv, axis=-1, keepdims=True) + NORM_EPS) * gc_ref[...]).astype(BF16)
    kv = _dot(cn, w_ref[...])
    zp = zp_ref[0]
    pe, sw = zp[:, :64], zp[:, 64:128]
    pe2 = jnp.sum(pe * pe, axis=-1, keepdims=True)
    gh = gh_ref[...]
    ct, st = ct_ref[...], st_ref[...]
    for h in range(MLA_HEADS):
        nope = kv[:, h * 256:h * 256 + 128]
        ms = (jnp.sum(nope * nope, axis=-1, keepdims=True) + pe2) / (MLA_NOPE + MLA_ROPE)
        rinv = lax.rsqrt(ms + NORM_EPS)
        kn_ref[0, h] = (nope * rinv * gh[:, :128]).astype(BF16)
        kp_ref[0, h] = ((pe * rinv * gh[:, 128:192]) * ct + (sw * rinv * gh[:, 192:256]) * st).astype(BF16)
        v_ref[0, h] = kv[:, h * 256 + 128:(h + 1) * 256].astype(BF16)


def _mla_prep(z3, p, ct, st, tr):
    b, t, _ = z3.shape
    rope = pl.BlockSpec((tr, MLA_ROPE), lambda b, i: (i, 0))

    def hm(d):
        return pl.BlockSpec((1, MLA_HEADS, tr, d), lambda b, i: (b, 0, i, 0))

    def shp(d):
        return _sds((b, MLA_HEADS, t, d), BF16)

    qn, qp = _call(
        _mla_q_body, (b, t // tr),
        [pl.BlockSpec((1, tr, 512), lambda b, i: (b, i, ZC_CQ // 512)), _full(p['g_cq']), _full(p['w_uq']),
         _full(p['g_q']), rope, rope],
        [hm(128), hm(64)], [shp(128), shp(64)], name="mla_q_prep")(z3, p['g_cq'], p['w_uq'], p['g_q'], ct, st)
    kn, kp, v = _call(
        _mla_kv_body, (b, t // tr),
        [pl.BlockSpec((1, tr, 256), lambda b, i: (b, i, ZC_CKV // 256)),
         pl.BlockSpec((1, tr, 128), lambda b, i: (b, i, ZC_KPE // 128)),
         _full(p['g_ckv']), _full(p['w_ukv']), _full(p['g_k']), rope, rope],
        [hm(128), hm(64), hm(128)], [shp(128), shp(64), shp(128)], name="mla_kv_prep")(
            z3, z3, p['g_ckv'], p['w_ukv'], p['g_k'], ct, st)
    return qn, qp, kn, kp, v


def _rope_tables(l_tok, m_tok):
    half = MLA_ROPE // 4
    freqs = ROPE_THETA ** (-jnp.arange(half, dtype=F32) / half)
    pos = jnp.arange(l_tok)
    ar = (pos // GRID_W).astype(F32)[:, None] * freqs[None, :]
    ac = (pos % GRID_W).astype(F32)[:, None] * freqs[None, :]
    ct = jnp.concatenate([jnp.cos(ar), jnp.cos(ar), jnp.cos(ac), jnp.cos(ac)], axis=-1)
    st = jnp.concatenate([-jnp.sin(ar), jnp.sin(ar), -jnp.sin(ac), jnp.sin(ac)], axis=-1)
    ct = jnp.concatenate([ct, jnp.ones((m_tok, MLA_ROPE), F32)], axis=0)
    st = jnp.concatenate([st, jnp.zeros((m_tok, MLA_ROPE), F32)], axis=0)
    return ct, st


_ROPE_SWAP = tuple(list(range(16, 32)) + list(range(0, 16)) + list(range(48, 64)) + list(range(32, 48)))


def _dft1_body(x_ref, bh_ref, bl_ref, oh_ref, ol_ref):
    xh, xl = _split2(x_ref[0])
    r = _dot(xh, bh_ref[...]) + (_dot(xh, bl_ref[...]) + _dot(xl, bh_ref[...]))
    rh, rl = _split2(r)
    oh_ref[0] = rh
    ol_ref[0] = rl


def _dft2_body(ch_ref, cl_ref, sh_ref, sl_ref, xch_ref, xcl_ref, xsh_ref, xsl_ref, o_ref):
    ch, cl, sh, sl = ch_ref[...], cl_ref[...], sh_ref[...], sl_ref[...]
    xch, xcl, xsh, xsl = xch_ref[0], xcl_ref[0], xsh_ref[0], xsl_ref[0]
    yc = _dot(ch, xch) + (_dot(ch, xcl) + _dot(cl, xch))
    ys = _dot(sh, xsh) + (_dot(sh, xsl) + _dot(sl, xsh))
    o_ref[0] = (yc - ys).astype(BF16)


def _dft_mats(n):
    k = jnp.arange(n, dtype=jnp.int32)
    ang = ((k[:, None] * k[None, :]) % n).astype(F32) * (2.0 * math.pi / n)
    s = 1.0 / math.sqrt(n)
    return jnp.cos(ang) * s, jnp.sin(ang) * s


def _fourier_tables(l_tok, m_tok, ctx_out):
    cw, sw = _dft_mats(BRANCH_W // 4)
    eye = jnp.eye(4, dtype=F32)
    bd = jnp.concatenate([jnp.kron(eye, cw), jnp.kron(eye, sw)], axis=1)
    tabs = {'bd': _split2(bd), 'lat': sum((_split2(m) for m in _dft_mats(l_tok)), ())}
    if ctx_out:
        tabs['ctx'] = sum((_split2(m) for m in _dft_mats(m_tok)), ())
    return tabs


def _fourier(z3, tabs, tr, l_tok, m_tok, ctx_out):
    b, t, _ = z3.shape
    bh, bl = tabs['bd']
    row = pl.BlockSpec((1, tr, 1024), lambda b, i: (b, i, 0))
    nrow = (t if ctx_out else l_tok) // tr
    xh, xl = _call(
        _dft1_body, (b, nrow),
        [pl.BlockSpec((1, tr, 512), lambda b, i: (b, i, ZC_FN // 512)), _full(bh), _full(bl)],
        [row, row], [_sds((b, nrow * tr, 1024), BF16)] * 2, name="dft_channels")(z3, bh, bl)

    def seq_dft(n, blk, mats, name):
        tm = _pick_tile(n, (256, 128, 64, 32, 16, 8))
        mspec = pl.BlockSpec((tm, n), lambda b, i: (i, 0))
        xc = pl.BlockSpec((1, n, 512), lambda b, i: (b, blk, 0))
        xs = pl.BlockSpec((1, n, 512), lambda b, i: (b, blk, 1))
        return _call(
            _dft2_body, (b, n // tm), [mspec] * 4 + [xc, xc, xs, xs],
            pl.BlockSpec((1, tm, 512), lambda b, i: (b, i, 0)),
            _sds((b, n, 512), BF16), name=name)(*mats, xh, xl, xh, xl)

    y = seq_dft(l_tok, 0, tabs['lat'], "dft_seq_latent")
    yc = seq_dft(m_tok, l_tok // m_tok, tabs['ctx'], "dft_seq_ctx") if ctx_out else None
    return y, yc


def _rw_prep_body(zk_ref, zv_ref, zr_ref, zl_ref, zg_ref, halo_ref, mu_ref, kk_ref, ka_ref, w0_ref, a0_ref,
                  w2_ref, a2_ref, trf_ref, trb_ref,
                  v_ref, r_ref, g_ref, cl_ref, kd_ref, b_ref, kx_ref, *, tr):
    row = lax.broadcasted_iota(jnp.int32, (tr, 1), 0)
    halo = halo_ref[0, 0]
    mu = mu_ref[...]

    def mix(x, lo, hi):
        xp = jnp.where(row == 0, halo[0:1, lo:hi], pltpu.roll(x, 1, 0))
        xn = jnp.where(row == tr - 1, halo[1:2, lo:hi], pltpu.roll(x, tr - 1, 0))
        return x + (0.5 * (xp + xn) - x) * mu[:, lo:hi]

    k = mix(zk_ref[0], 0, 512)
    v = mix(zv_ref[0], 512, 1024)
    r = mix(zr_ref[0], 1024, 1536)
    l4 = mix(zl_ref[0], 1536, 1664)
    g_ref[0] = mix(zg_ref[0], 1664, 1792)

    wcat = w0_ref[...] + _dot3(jnp.tanh(l4), w2_ref[...])
    acat = a0_ref[...] + _dot3(l4, a2_ref[...])
    kk = k * kk_ref[...]
    for d, tri_ref in enumerate((trf_ref, trb_ref)):
        w = wcat[:, d * 512:(d + 1) * 512]
        sp = jnp.maximum(-w, 0.0) + jnp.log1p(jnp.exp(-jnp.abs(w)))
        lw = -jnp.exp(-sp - 0.5)
        a = jax.nn.sigmoid(acat[:, d * 512:(d + 1) * 512])
        kd = k * (1.0 + (a - 1.0) * ka_ref[...])
        cl = _dot_sel(tri_ref[...], lw)
        einv = jnp.exp(-lw)
        for h in range(RW_HEADS):
            lo = h * RW_DH
            kkh = kk[:, lo:lo + RW_DH]
            nrm = jnp.sqrt(jnp.sum(kkh * kkh, axis=-1, keepdims=True))
            kkn = kkh / jnp.maximum(nrm, 1e-12)
            cl_ref[0, d, h] = cl[:, lo:lo + RW_DH]
            kd_ref[0, d, h] = kd[:, lo:lo + RW_DH]
            b_ref[0, d, h] = kkn * a[:, lo:lo + RW_DH]
            kx_ref[0, d, h] = kkn * einv[:, lo:lo + RW_DH]
    for h in range(RW_HEADS):
        lo = h * RW_DH
        v_ref[0, h] = v[:, lo:lo + RW_DH]
        r_ref[0, h] = r[:, lo:lo + RW_DH]


def _rw_prep(z3, halo, p, tr):
    b, t, _ = z3.shape
    c = RW_CHUNK
    ti = jnp.arange(tr)
    same = (ti[:, None] // c) == (ti[None, :] // c)
    trf = (same & (ti[None, :] <= ti[:, None])).astype(BF16)
    trb = (same & (ti[None, :] >= ti[:, None])).astype(BF16)

    def zcol(off, w):
        return pl.BlockSpec((1, tr, w), lambda b, i: (b, i, off // w))

    hm = pl.BlockSpec((1, RW_HEADS, tr, RW_DH), lambda b, i: (b, 0, i, 0))
    hmd = pl.BlockSpec((1, 2, RW_HEADS, tr, RW_DH), lambda b, i: (b, 0, 0, i, 0))
    s1 = _sds((b, RW_HEADS, t, RW_DH), F32)
    s2 = _sds((b, 2, RW_HEADS, t, RW_DH), F32)
    consts = [p['rw_mu'], p['rw_k_k'], p['rw_k_a'], p['rw_w0'], p['rw_a0'], p['rw_w2'], p['rw_a2'], trf, trb]
    return _call(
        functools.partial(_rw_prep_body, tr=tr), (b, t // tr),
        [zcol(ZC_RWK, 512), zcol(ZC_RWK + 512, 512), zcol(ZC_RWK + 1024, 512), zcol(ZC_L4, 128), zcol(ZC_G, 128),
         pl.BlockSpec((1, 1, 2, 1792), lambda b, i: (b, i, 0, 0))] + [_full(a) for a in consts],
        [hm, hm, pl.BlockSpec((1, tr, 128), lambda b, i: (b, i, 0)), hmd, hmd, hmd, hmd],
        [s1, s1, _sds((b, t, 128), F32), s2, s2, s2, s2], name="rw_prep")(z3, z3, z3, z3, z3, halo, *consts)


RW_P_HI = 3
RW_P_LO = 1


def _rw_scan_body(r_ref, v_ref, cl_ref, kd_ref, b_ref, kx_ref, y_ref, s_ref, *, tr, rev):
    c = RW_CHUNK
    nc = tr // c
    g = RW_HEADS * nc

    @pl.when(pl.program_id(1) == 0)
    def _():
        s_ref[...] = jnp.zeros_like(s_ref)

    def ld(x):
        return x.reshape(g, c, RW_DH)

    r, v = ld(r_ref[0]), ld(v_ref[0])
    cl, kd, bb, kx = ld(cl_ref[0, 0]), ld(kd_ref[0, 0]), ld(b_ref[0, 0]), ld(kx_ref[0, 0])
    last = 0 if rev else c - 1
    ctot = cl[:, last:last + 1, :]
    e = jnp.exp(cl)
    ei = jnp.exp(-cl)
    ec = jnp.exp(ctot - cl)
    kkt, rt = kx * e, r * e
    kw, bw = kd * ei, bb * ei
    kc, bc = kd * ec, bb * ec
    a_cat = jnp.concatenate([kkt, rt], axis=1)
    p1 = _bmm('gtd,gsd->gts', a_cat, kw, RW_P_HI)
    p2 = _bmm('gtd,gsd->gts', a_cat, bw, RW_P_HI)
    ti = lax.broadcasted_iota(jnp.int32, (c, c), 0)
    si = lax.broadcasted_iota(jnp.int32, (c, c), 1)
    strict = (si > ti) if rev else (si < ti)
    incl = (si >= ti) if rev else (si <= ti)
    nmat = jnp.where(strict, p1[:, :c], 0.0)
    ark = jnp.where(incl, p1[:, c:], 0.0)
    x = -jnp.where(strict, p2[:, :c], 0.0)
    arb = jnp.where(incl, p2[:, c:], 0.0)
    eye_c = (ti == si).astype(F32)
    tm = eye_c + x
    xp = x
    for i in range(int(math.log2(c)) - 1):
        xp = _bmm('gts,gsu->gtu', xp, xp, RW_P_HI if i == 0 else RW_P_LO)
        tm = tm + _bmm('gts,gsu->gtu', tm, xp, RW_P_HI if i == 0 else RW_P_LO)
    nv = _bmm('gts,gsd->gtd', nmat, v, RW_P_HI)
    a1 = _bmm('gts,gsd->gtd', tm, kkt, RW_P_HI)
    u0 = _bmm('gts,gsd->gtd', tm, nv, RW_P_HI)
    a2 = rt - _bmm('gts,gsd->gtd', arb, a1, RW_P_HI)
    y0 = _bmm('gts,gsd->gtd', ark, v, RW_P_HI) - _bmm('gts,gsd->gtd', arb, u0, RW_P_HI)
    di = lax.broadcasted_iota(jnp.int32, (RW_DH, RW_DH), 0)
    dj = lax.broadcasted_iota(jnp.int32, (RW_DH, RW_DH), 1)
    gm = jnp.where(di == dj, jnp.exp(ctot), 0.0) - _bmm('gtd,gte->gde', a1, bc, RW_P_HI)
    hm = _bmm('gtv,gtk->gvk', v, kc, RW_P_HI) - _bmm('gtv,gtk->gvk', u0, bc, RW_P_HI)

    def per_chunk(x):
        return x.reshape(RW_HEADS, nc, x.shape[1], x.shape[2])

    a2, y0, gm, hm = per_chunk(a2), per_chunk(y0), per_chunk(gm), per_chunk(hm)
    s = s_ref[...]
    ys = [None] * nc
    for ci in (range(nc - 1, -1, -1) if rev else range(nc)):
        ys[ci] = _bmm('htk,hvk->htv', a2[:, ci], s, RW_P_HI) + y0[:, ci]
        s = _bmm('hvk,hke->hve', s, gm[:, ci], RW_P_HI) + hm[:, ci]
    s_ref[...] = s
    y_ref[0] = jnp.concatenate(ys, axis=1)


def _rw_scan(r, v, cl, kd, bb, kx, tr, nl, nm, rev):
    b, _, t, _ = r.shape
    nt = nl + nm
    d = 1 if rev else 0
    if rev:
        tile = lambda j: nt - 1 - j
    else:
        tile = lambda j: jnp.where(j < nm, nl + j, j - nm)
    hm = pl.BlockSpec((1, RW_HEADS, tr, RW_DH), lambda b, j: (b, 0, tile(j), 0))
    hmd = pl.BlockSpec((1, 1, RW_HEADS, tr, RW_DH), lambda b, j: (b, d, 0, tile(j), 0))
    return _call(
        functools.partial(_rw_scan_body, tr=tr, rev=rev), (b, nt),
        [hm, hm, hmd, hmd, hmd, hmd], hm, _sds((b, RW_HEADS, t, RW_DH), F32),
        scratch=[pltpu.VMEM((RW_HEADS, RW_DH, RW_DH), F32)], sem=("parallel", "arbitrary"),
        name="rw_scan_bwd" if rev else "rw_scan_fwd")(r, v, cl, kd, bb, kx)


def _rw_out_body(yf_ref, yb_ref, r_ref, v_ref, kd_ref, g_ref, lnw_ref, lnb_ref, rk_ref, g2_ref, o_ref):
    y = yf_ref[0] + yb_ref[0]
    mu = jnp.mean(y, axis=-1, keepdims=True)
    var = jnp.mean(jnp.square(y - mu), axis=-1, keepdims=True)
    yn = (y - mu) * lax.rsqrt(var + RW_GN_EPS) * lnw_ref[...] + lnb_ref[...]
    ksum = kd_ref[0, 0] + kd_ref[0, 1]
    bonus = jnp.sum(r_ref[0] * ksum * rk_ref[...], axis=-1, keepdims=True) * v_ref[0]
    o = yn + bonus
    o = jnp.concatenate([o[h] for h in range(RW_HEADS)], axis=-1)
    gate = _dot3(jax.nn.sigmoid(g_ref[0]), g2_ref[...])
    o_ref[0] = (o * gate).astype(BF16)


def _rw_out(yf, yb, r, v, kd, g, p, tr, nrow):
    b = r.shape[0]
    hm = pl.BlockSpec((1, RW_HEADS, tr, RW_DH), lambda b, i: (b, 0, i, 0))
    consts = [p['rw_ln_w'], p['rw_ln_b'], p['rw_r_k'], p['rw_g2']]
    return _call(
        _rw_out_body, (b, nrow),
        [hm, hm, hm, hm, pl.BlockSpec((1, 2, RW_HEADS, tr, RW_DH), lambda b, i: (b, 0, 0, i, 0)),
         pl.BlockSpec((1, tr, 128), lambda b, i: (b, i, 0))] + [_full(a) for a in consts],
        pl.BlockSpec((1, tr, RW_W), lambda b, i: (b, i, 0)),
        _sds((b, nrow * tr, RW_W), BF16), name="rw_out")(yf, yb, r, v, kd, g, *consts)


def _merge_body(h_ref, y0_ref, y1_ref, y2_ref, y3_ref, g0_ref, g1_ref, g2_ref, g3_ref, wb_ref, o_ref):
    h = h_ref[0]
    acc = None
    for i, (y_ref, wg_ref) in enumerate(zip((y0_ref, y1_ref, y2_ref, y3_ref), (g0_ref, g1_ref, g2_ref, g3_ref))):
        t = jax.nn.sigmoid(_dot(h, wg_ref[...])) * _dot(y_ref[0], wb_ref[i])
        acc = t if acc is None else acc + t
    o_ref[0] = acc.astype(BF16)


def _merge(h, ys, wg, wbr, tr, nrow):
    b, _, d = h.shape
    tn = 512
    nn = d // tn
    row = lambda w: pl.BlockSpec((1, tr, w), lambda n, b, i: (b, i, 0))
    gspecs = [pl.BlockSpec((d, tn), functools.partial(lambda n, b, i, k: (0, k * nn + n), k=k))
              for k in range(N_BRANCH)]
    return _call(
        _merge_body, (nn, b, nrow),
        [row(d)] + [row(BRANCH_W)] * 4 + gspecs + [pl.BlockSpec((N_BRANCH, BRANCH_W, tn), lambda n, b, i: (0, 0, n))],
        pl.BlockSpec((1, tr, tn), lambda n, b, i: (b, i, n)),
        _sds((b, nrow * tr, d), BF16), name="merge")(h, *ys, wg, wg, wg, wg, wbr)


def _outproj_body(a_ref, w_ref, x_ref, md_ref, o_ref):
    o_ref[0] = x_ref[0] + md_ref[0, 0, 2:3, :] * _dot(a_ref[0], w_ref[...])


def _outproj(acc, w_out, xs, md, tr, nl, nrow):
    b, _, d = acc.shape
    row = pl.BlockSpec((1, tr, d), lambda b, i: (b, i, 0))
    return _call(
        _outproj_body, (b, nrow), [row, _full(w_out), row, _md_spec(d, nl)], row,
        _sds((b, nrow * tr, d), F32), name="out_proj")(acc, w_out, xs, md)


def _select_body(lg_ref, slot_ref, aff_ref, slotc_ref, *, n, cap):
    lg = lg_ref[0]
    lane = lax.broadcasted_iota(jnp.int32, lg.shape, 1)
    lg = jnp.where(lane < N_EXPERTS, lg, NEG_INF)
    ex = jnp.exp(lg - lg.max(axis=-1, keepdims=True))
    aff = ex / ex.sum(axis=-1, keepdims=True)
    aff_t = aff.T[:N_EXPERTS]
    bits = lax.bitcast_convert_type(aff_t, jnp.int32)
    thr = jnp.zeros((N_EXPERTS, 1), jnp.int32)
    for bit in range(30, -1, -1):
        cand = thr | jnp.int32(1 << bit)
        cnt = jnp.sum((bits >= cand).astype(F32), axis=1, keepdims=True)
        thr = jnp.where(cnt >= cap, cand, thr)
    gt = bits > thr
    eq = bits == thr
    need = cap - jnp.sum(gt.astype(F32), axis=1, keepdims=True)
    both = jnp.concatenate([gt.astype(BF16), eq.astype(BF16)], axis=0)
    cw = min(n, 512)
    pre = []
    for cb in range(n // cw):
        tp = lax.broadcasted_iota(jnp.int32, (n, cw), 0)
        tt = lax.broadcasted_iota(jnp.int32, (n, cw), 1) + cb * cw
        pre.append(_dot(both, (tp < tt).astype(BF16)))
    pre = jnp.concatenate(pre, axis=1) if len(pre) > 1 else pre[0]
    pre_gt, pre_eq = pre[:N_EXPERTS], pre[N_EXPERTS:]
    sel = gt | (eq & (pre_eq < need))
    slot = jnp.where(sel, pre_gt + jnp.minimum(pre_eq, need), -1.0)
    slot_ref[0] = slot.astype(jnp.int32)
    aff_ref[0] = aff_t
    pad = jnp.full((128 - N_EXPERTS, n), -1.0, F32)
    slotc_ref[0] = jnp.concatenate([slot, pad], axis=0).T.astype(jnp.int32)


def _select(logits, n, blk, cap, name):
    b = logits.shape[0]
    er = pl.BlockSpec((1, N_EXPERTS, n), lambda b: (b, 0, 0))
    return _call(
        functools.partial(_select_body, n=n, cap=cap), (b,),
        [pl.BlockSpec((1, n, 128), lambda b: (b, blk, 0))],
        [er, er, pl.BlockSpec((1, n, 128), lambda b: (b, 0, 0))],
        [_sds((b, N_EXPERTS, n), jnp.int32), _sds((b, N_EXPERTS, n), F32), _sds((b, n, 128), jnp.int32)],
        name=name)(logits)


def _ffn_body(h_ref, slot_ref, aff_ref, w1_ref, w3_ref, w2_ref, o_ref, *, grp, n, cap):
    xs, gates = [], []
    jrow = lax.broadcasted_iota(jnp.int32, (cap, n), 0)
    for gi in range(grp):
        pick = jrow == slot_ref[gi, 0]
        xs.append(_dot(pick.astype(BF16), h_ref[gi]).astype(BF16))
        gates.append(jnp.sum(jnp.where(pick, aff_ref[gi, 0], 0.0), axis=1, keepdims=True))
    xe = jnp.concatenate(xs, axis=0) if grp > 1 else xs[0]
    gate = jnp.concatenate(gates, axis=0) if grp > 1 else gates[0]
    a = _dot(xe, w1_ref[0])
    u = _dot(xe, w3_ref[0])
    hm = (a * jax.nn.sigmoid(a) * u).astype(BF16)
    ye = (_dot(hm, w2_ref[0]) * gate).astype(BF16)
    for gi in range(grp):
        o_ref[0, gi] = ye[gi * cap:(gi + 1) * cap]


def _moe_ffn(h2, slot, aff, w1, w3, w2, n, blk, cap, grp, name):
    b, _, d = h2.shape
    ff = w1.shape[-1]
    sr = pl.BlockSpec((grp, 1, 1, n), lambda e, g: (g, e, 0, 0))
    return _call(
        functools.partial(_ffn_body, grp=grp, n=n, cap=cap), (N_EXPERTS, b // grp),
        [pl.BlockSpec((grp, n, d), lambda e, g: (g, blk, 0)), sr, sr,
         pl.BlockSpec((1, d, ff), lambda e, g: (e, 0, 0)), pl.BlockSpec((1, d, ff), lambda e, g: (e, 0, 0)),
         pl.BlockSpec((1, ff, d), lambda e, g: (e, 0, 0))],
        pl.BlockSpec((1, grp, cap, d), lambda e, g: (e, g, 0, 0)),
        _sds((N_EXPERTS, b, cap, d), BF16), name=name)(
            h2, slot.reshape(b, N_EXPERTS, 1, n), aff.reshape(b, N_EXPERTS, 1, n), w1, w3, w2)


def _combine_body(sc_ref, ye_ref, x_ref, md_ref, o_ref, *, cap):
    sc = sc_ref[0]
    jj = lax.broadcasted_iota(jnp.int32, (sc.shape[0], cap), 1)
    acc = None
    for e in range(N_EXPERTS):
        put = (sc[:, e:e + 1] == jj).astype(BF16)
        t = _dot(put, ye_ref[e, 0])
        acc = t if acc is None else acc + t
    o_ref[0] = x_ref[0] + md_ref[0, 0, 5:6, :] * acc


def _moe_combine(slotc, ye, x1, md, n, blk_rows, is_ctx, cap, name):
    b, _, d = x1.shape
    tm = _pick_tile(n, (256, 128, 64, 32, 16, 8))
    off = blk_rows // tm
    return _call(
        functools.partial(_combine_body, cap=cap), (b, n // tm),
        [pl.BlockSpec((1, tm, 128), lambda b, i: (b, i, 0)),
         pl.BlockSpec((N_EXPERTS, 1, cap, d), lambda b, i: (0, b, 0, 0)),
         pl.BlockSpec((1, tm, d), lambda b, i: (b, off + i, 0)),
         pl.BlockSpec((1, 1, 6, d), lambda b, i: (b, 1 if is_ctx else 0, 0, 0))],
        pl.BlockSpec((1, tm, d), lambda b, i: (b, i, 0)),
        _sds((b, n, d), F32), name=name)(slotc, ye, x1, md)


def _moe(x1, h2, logits, md, w1, w3, w2, l_tok, m_tok, ctx_out):
    b = x1.shape[0]
    cap_l = CAPACITY_FACTOR * l_tok // N_EXPERTS
    slot, aff, slotc = _select(logits, l_tok, 0, cap_l, "moe_select_latent")
    ye = _moe_ffn(h2, slot, aff, w1, w3, w2, l_tok, 0, cap_l, 1, "moe_ffn_latent")
    x2 = _moe_combine(slotc, ye, x1, md, l_tok, 0, False, cap_l, "moe_combine_latent")
    if not ctx_out:
        return x2, None
    cap_c = CAPACITY_FACTOR * m_tok // N_EXPERTS
    blk = l_tok // m_tok
    slot, aff, slotc = _select(logits, m_tok, blk, cap_c, "moe_select_ctx")
    ye = _moe_ffn(h2, slot, aff, w1, w3, w2, m_tok, blk, cap_c, b, "moe_ffn_ctx")
    xc2 = _moe_combine(slotc, ye, x1, md, m_tok, l_tok, True, cap_c, "moe_combine_ctx")
    return x2, xc2


def _layer_params(l, w_in, na_q_norm, na_k_norm, na_rpb, mla_cq_norm, mla_ckv_norm, mla_w_uq, mla_w_ukv,
                  mla_q_norm, mla_k_norm, rw_mu_ks, rw_mu_qs, rw_w0, rw_w2, rw_a0, rw_a2, rw_g2, rw_k_k, rw_k_a,
                  rw_r_k, rw_ln_w, rw_ln_b, w_br, w_out, rows):
    w = w_in[l]
    d = w.shape[0]
    kq = KEY_COLS
    kpe = w[:, 1280:1344]
    zpad = lambda n: jnp.zeros((d, n), w.dtype)
    wz = jnp.concatenate([
        w[:, kq:kq + 512], w[:, 0:512], w[:, 512:1024],
        w[:, kq + 512:kq + 1024],
        w[:, 1344:1856], w[:, 1856:2368], w[:, kq + 1024:kq + 1536],
        w[:, kq + 1632:kq + 2144],
        w[:, 1024:1280],
        kpe, kpe[:, jnp.array(_ROPE_SWAP)],
        w[:, 2368:2496],
        w[:, kq + 1536:kq + 1632], zpad(32),
        zpad(NZ - 4736)], axis=1).astype(BF16)
    wg = w[:, kq + 2144:].astype(BF16)
    swap = jnp.array(_ROPE_SWAP)

    def head_gain(g):
        return jnp.concatenate([g[:128], g[128:], g[128:][swap]])[None, :]

    wq = mla_w_uq[l].reshape(MLA_QLORA, MLA_HEADS, MLA_NOPE + MLA_ROPE)
    wq = jnp.concatenate([wq, wq[:, :, MLA_NOPE:][:, :, swap]], axis=-1).reshape(MLA_QLORA, MLA_HEADS * 256)
    w2cat = jnp.zeros((128, 1024), F32).at[0:32, 0:512].set(rw_w2[l, 0]).at[32:64, 512:].set(rw_w2[l, 1])
    a2cat = jnp.zeros((128, 1024), F32).at[64:96, 0:512].set(rw_a2[l, 0]).at[96:128, 512:].set(rw_a2[l, 1])
    mu = jnp.concatenate([rw_mu_ks[l][:1024], rw_mu_qs[l][:512], rw_mu_ks[l][1024:], rw_mu_qs[l][512:],
                          jnp.zeros((32,), F32)])[None, :]
    hd = lambda a: a.reshape(RW_HEADS, 1, RW_DH)
    return {
        'wz': wz, 'wg': wg,
        'na_gq': na_q_norm[l][None, :], 'na_gk': na_k_norm[l][None, :],
        'na_bias': _na_bias_table(na_rpb[l], rows),
        'g_cq': mla_cq_norm[l][None, :], 'g_ckv': mla_ckv_norm[l][None, :],
        'w_uq': wq.astype(BF16), 'w_ukv': mla_w_ukv[l].astype(BF16),
        'g_q': head_gain(mla_q_norm[l]), 'g_k': head_gain(mla_k_norm[l]),
        'rw_mu': mu, 'rw_k_k': rw_k_k[l][None, :], 'rw_k_a': rw_k_a[l][None, :],
        'rw_w0': rw_w0[l].reshape(1, 1024), 'rw_a0': rw_a0[l].reshape(1, 1024), 'rw_w2': w2cat, 'rw_a2': a2cat,
        'rw_ln_w': hd(rw_ln_w[l]), 'rw_ln_b': hd(rw_ln_b[l]), 'rw_r_k': hd(rw_r_k[l]),
        'rw_g2': jnp.concatenate([rw_g2[l], jnp.zeros((32, RW_W), F32)], axis=0),
        'w_br': w_br[l].astype(BF16), 'w_out': w_out[l].astype(BF16),
    }


def _rw_halo(z3, tr, nl):
    b, t, _ = z3.shape
    nt = t // tr
    cols = jnp.concatenate([z3[:, :, ZC_RWK:ZC_RWK + 1536], z3[:, :, ZC_L4:ZC_L4 + 256]], axis=-1)
    cols = cols.reshape(b, nt, tr, 1792)
    first, last = cols[:, :, 0, :], cols[:, :, tr - 1, :]
    zero = jnp.zeros_like(first[:, :1])
    prev = jnp.concatenate([zero, last[:, :-1]], axis=1)
    nxt = jnp.concatenate([first[:, 1:], zero], axis=1)
    tile = jnp.arange(nt)[None, :, None]
    prev = jnp.where(tile == nl, 0.0, prev)
    nxt = jnp.where(tile == nl - 1, 0.0, nxt)
    return jnp.stack([prev, nxt], axis=2)


def _layer(xs, md, p, g1n, g2n, wr, w1, w3, w2, tabs, l_tok, m_tok, tr, ctx_out):
    b, t, d = xs.shape
    nl, nm = l_tok // tr, m_tok // tr
    nrow = (nl + nm) if ctx_out else nl
    blk = l_tok // m_tok

    h = _norm1(xs, g1n, md, tr, nl)
    z3 = _matmul(h.reshape(b * t, d), p['wz'], F32).reshape(b, t, NZ)

    nq, nk, nv = _na_prep(z3, p['na_gq'], p['na_gk'], tr)
    y_na = _na_latent(nq, nk, nv, p['na_bias'], l_tok, m_tok)
    qn, qp, kn, kp, mv = _mla_prep(z3, p, tabs['rope_c'], tabs['rope_s'], tr)
    tq = _pick_tile(l_tok, (256, 128, 64))
    y_mla = _attention([qn, qp], [kn, kp], mv, tq=tq, q0=0, nq=l_tok // tq, tk=t, kblk=0, name="mla_latent")
    rv, rr, rg, cl, kd, bb, kx = _rw_prep(z3, _rw_halo(z3, tr, nl), p, tr)
    yf = _rw_scan(rr, rv, cl, kd, bb, kx, tr, nl, nm, False)
    yb = _rw_scan(rr, rv, cl, kd, bb, kx, tr, nl, nm, True)
    y_rw = _rw_out(yf, yb, rr, rv, kd, rg, p, tr, nrow)
    y_fn, yc_fn = _fourier(z3, tabs, tr, l_tok, m_tok, ctx_out)

    if ctx_out:
        yc_na = _attention([nq], [nk], nv, tq=m_tok, q0=blk, nq=1, tk=m_tok, kblk=blk, name="na_ctx")
        yc_mla = _attention([qn, qp], [kn, kp], mv, tq=m_tok, q0=blk, nq=1, tk=m_tok, kblk=blk, name="mla_ctx")
        y_fn = jnp.concatenate([y_fn, yc_fn], axis=1)
        y_na = jnp.concatenate([y_na, yc_na], axis=1)
        y_mla = jnp.concatenate([y_mla, yc_mla], axis=1)

    acc = _merge(h, (y_fn, y_na, y_mla, y_rw), p['wg'], p['w_br'], tr, nrow)
    x1 = _outproj(acc, p['w_out'], xs, md, tr, nl, nrow)
    h2, logits = _norm2_router(x1, g2n, md, wr, tr, nl)
    return _moe(x1, h2, logits, md, w1, w3, w2, l_tok, m_tok, ctx_out)


def kernel(x, c, ctx, c_ctx, ada_w, ada_b, norm1_g, norm2_g, w_in, na_q_norm, na_k_norm, na_rpb, mla_cq_norm, mla_ckv_norm, mla_w_uq, mla_w_ukv, mla_q_norm, mla_k_norm, rw_mu_ks, rw_mu_qs, rw_w0, rw_w2, rw_a0, rw_a2, rw_g2, rw_k_k, rw_k_a, rw_r_k, rw_ln_w, rw_ln_b, w_br, w_out, moe_router, moe_w1, moe_w3, moe_w2):
    b, l_tok, d = x.shape
    m_tok = ctx.shape[1]
    depth = ada_w.shape[0]
    tr = min(m_tok, 256)
    assert l_tok % tr == 0 and m_tok % tr == 0 and l_tok % m_tok == 0 and tr % RW_CHUNK == 0
    assert l_tok % GRID_W == 0 and m_tok % 128 == 0

    nr = -(-(b + 1) // 8) * 8
    cc = jnp.concatenate([c, c_ctx[None, :], jnp.zeros((nr - b - 1, d), F32)], axis=0)
    mod = _modulation(cc, ada_w, ada_b)
    tabs = {}
    tabs['rope_c'], tabs['rope_s'] = _rope_tables(l_tok, m_tok)

    xs = jnp.concatenate([x, ctx], axis=1)
    for l in range(depth):
        ctx_out = l < depth - 1
        tabs.update(_fourier_tables(l_tok, m_tok, ctx_out))
        p = _layer_params(l, w_in, na_q_norm, na_k_norm, na_rpb, mla_cq_norm, mla_ckv_norm, mla_w_uq, mla_w_ukv,
                          mla_q_norm, mla_k_norm, rw_mu_ks, rw_mu_qs, rw_w0, rw_w2, rw_a0, rw_a2, rw_g2, rw_k_k,
                          rw_k_a, rw_r_k, rw_ln_w, rw_ln_b, w_br, w_out, l_tok // GRID_W)
        md = jnp.stack([mod[l, :b], jnp.broadcast_to(mod[l, b], (b, 6 * d))], axis=1).reshape(b, 2, 6, d)
        wr = jnp.concatenate([moe_router[l], jnp.zeros((d, 128 - N_EXPERTS), F32)], axis=1)
        x2, xc2 = _layer(xs, md, p, norm1_g[l][None, :], norm2_g[l][None, :], wr,
                         moe_w1[l].astype(BF16), moe_w3[l].astype(BF16), moe_w2[l].astype(BF16),
                         tabs, l_tok, m_tok, tr, ctx_out)
        xs = jnp.concatenate([x2, xc2], axis=1) if ctx_out else x2
    return xs
```

```python
import functools
import math

import jax
import jax.numpy as jnp
from jax import lax
from jax.experimental import pallas as pl
from jax.experimental.pallas import tpu as pltpu

F32 = jnp.float32
BF16 = jnp.bfloat16

GRID_W = 64
N_BRANCH = 4
BRANCH_W = 512
NA_HEADS, NA_DH, NA_KH, NA_KW = 8, 64, 8, 16
NA_SCALE = NA_DH ** -0.5
MLA_HEADS, MLA_NOPE, MLA_ROPE, MLA_V = 4, 128, 64, 128
MLA_QLORA, MLA_KVLORA = 512, 256
MLA_SCALE = (MLA_NOPE + MLA_ROPE) ** -0.5
RW_HEADS, RW_DH, RW_W = 8, 64, 512
RW_LORA, RW_GATE_LORA = 32, 96
RW_GN_EPS = 64e-5
N_EXPERTS = 16
CAPACITY_FACTOR = 2
ROPE_THETA = 10000.0
NORM_EPS = 1e-6
NEG_INF = -1e30
RW_CHUNK = 64
KEY_COLS = 2496
NZ = 5120
ZC_NA = 0
ZC_CQ = 1536
ZC_RWK = 2048
ZC_FN = 3584
ZC_CKV = 4096
ZC_KPE = 4352
ZC_L4 = 4480
ZC_G = 4608

VMEM_LIMIT = 56 * 2 ** 20


def _call(body, grid, in_specs, out_specs, out_shape, scratch=(), sem=None, name=None):
    return pl.pallas_call(
        body, grid=grid, in_specs=in_specs, out_specs=out_specs, out_shape=out_shape,
        scratch_shapes=list(scratch), name=name,
        compiler_params=pltpu.CompilerParams(
            dimension_semantics=sem or ("parallel",) * len(grid), vmem_limit_bytes=VMEM_LIMIT))


def _full(a):
    nd = a.ndim
    return pl.BlockSpec(a.shape, lambda *_: (0,) * nd)


def _sds(shape, dtype):
    return jax.ShapeDtypeStruct(shape, dtype)


def _dot(a, b):
    return jnp.dot(a, b, preferred_element_type=F32)


def _dot_nt(a, b):
    return lax.dot_general(a, b, (((1,), (1,)), ((), ())), preferred_element_type=F32)


def _split2(x):
    hi = x.astype(BF16)
    return hi, (x - hi.astype(F32)).astype(BF16)


def _split3(x):
    hi = x.astype(BF16)
    r = x - hi.astype(F32)
    mid = r.astype(BF16)
    return hi, mid, (r - mid.astype(F32)).astype(BF16)


def _dot3(a, b):
    ah, al = _split2(a)
    bh, bl = _split2(b)
    return _dot(ah, bh) + (_dot(ah, bl) + _dot(al, bh))


def _dot_sel(sel_bf16, x):
    h, m, l = _split3(x)
    return _dot(sel_bf16, h) + (_dot(sel_bf16, m) + _dot(sel_bf16, l))


def _bmm(spec, a, b, passes):
    e = functools.partial(jnp.einsum, spec, preferred_element_type=F32)
    if passes == 1:
        return e(a.astype(BF16), b.astype(BF16))
    ah, al = _split2(a)
    bh, bl = _split2(b)
    return e(ah, bh) + (e(ah, bl) + e(al, bh))


def _pick_tile(n, cands):
    for c in cands:
        if n % c == 0:
            return c
    raise ValueError(f"no tile for {n}")


def _mod_body(c_ref, w_ref, b_ref, o_ref):
    cc = c_ref[...]
    s = cc * jax.nn.sigmoid(cc)
    o_ref[0] = _dot3(s, w_ref[0]) + b_ref[0]


def _modulation(cc, ada_w, ada_b):
    depth, d, n6 = ada_w.shape
    r = cc.shape[0]
    tn = 1024
    return _call(
        _mod_body, (depth, n6 // tn),
        [pl.BlockSpec((r, d), lambda l, n: (0, 0)),
         pl.BlockSpec((1, d, tn), lambda l, n: (l, 0, n)),
         pl.BlockSpec((1, 1, tn), lambda l, n: (l, 0, n))],
        pl.BlockSpec((1, r, tn), lambda l, n: (l, 0, n)),
        _sds((depth, r, n6), F32), name="adaln_mod")(cc, ada_w, ada_b.reshape(depth, 1, n6))


def _modnorm(x, g, shift, scale):
    y = x * lax.rsqrt(jnp.mean(x * x, axis=-1, keepdims=True) + NORM_EPS)
    return (y * g) * (1.0 + scale) + shift


def _norm1_body(x_ref, g_ref, md_ref, h_ref):
    h = _modnorm(x_ref[0], g_ref[...], md_ref[0, 0, 0:1, :], md_ref[0, 0, 1:2, :])
    h_ref[0] = h.astype(BF16)


def _norm2_body(x_ref, g_ref, md_ref, wr_ref, h_ref, lg_ref):
    h = _modnorm(x_ref[0], g_ref[...], md_ref[0, 0, 3:4, :], md_ref[0, 0, 4:5, :])
    h_ref[0] = h.astype(BF16)
    lg_ref[0] = _dot3(h, wr_ref[...])


def _md_spec(d, nl):
    return pl.BlockSpec((1, 1, 6, d), lambda b, i: (b, i // nl, 0, 0))


def _norm1(xs, g, md, tr, nl):
    b, t, d = xs.shape
    return _call(
        _norm1_body, (b, t // tr),
        [pl.BlockSpec((1, tr, d), lambda b, i: (b, i, 0)), _full(g), _md_spec(d, nl)],
        pl.BlockSpec((1, tr, d), lambda b, i: (b, i, 0)),
        _sds((b, t, d), BF16), name="norm1")(xs, g, md)


def _norm2_router(x1, g, md, wr, tr, nl):
    b, t, d = x1.shape
    return _call(
        _norm2_body, (b, t // tr),
        [pl.BlockSpec((1, tr, d), lambda b, i: (b, i, 0)), _full(g), _md_spec(d, nl), _full(wr)],
        [pl.BlockSpec((1, tr, d), lambda b, i: (b, i, 0)), pl.BlockSpec((1, tr, 128), lambda b, i: (b, i, 0))],
        [_sds((b, t, d), BF16), _sds((b, t, 128), F32)], name="norm2_router")(x1, g, md, wr)


def _mm_body(a_ref, w_ref, o_ref):
    o_ref[...] = _dot(a_ref[...], w_ref[...]).astype(o_ref.dtype)


def _matmul(a, w, out_dtype):
    r, k = a.shape
    n = w.shape[1]
    tm = _pick_tile(r, (1024, 768, 512, 384, 256, 128, 64, 32, 16, 8))
    tn = _pick_tile(n, (1024, 512, 256, 128))
    return _call(
        _mm_body, (r // tm, n // tn),
        [pl.BlockSpec((tm, k), lambda i, j: (i, 0)), pl.BlockSpec((k, tn), lambda i, j: (0, j))],
        pl.BlockSpec((tm, tn), lambda i, j: (i, j)),
        _sds((r, n), out_dtype), name="in_proj")(a, w)


def _head_rms(x, g):
    return x * lax.rsqrt(jnp.mean(x * x, axis=-1, keepdims=True) + NORM_EPS) * g


def _na_prep_body(z_ref, gq_ref, gk_ref, q_ref, k_ref, v_ref):
    z = z_ref[0]
    for h in range(NA_HEADS):
        lo = h * NA_DH
        q = z[:, lo:lo + NA_DH]
        k = z[:, 512 + lo:512 + lo + NA_DH]
        v = z[:, 1024 + lo:1024 + lo + NA_DH]
        q_ref[0, h] = (_head_rms(q, gq_ref[...]) * NA_SCALE).astype(BF16)
        k_ref[0, h] = _head_rms(k, gk_ref[...]).astype(BF16)
        v_ref[0, h] = v.astype(BF16)


def _na_prep(z3, gq, gk, tr):
    b, t, _ = z3.shape
    hm = pl.BlockSpec((1, NA_HEADS, tr, NA_DH), lambda b, i: (b, 0, i, 0))
    shp = _sds((b, NA_HEADS, t, NA_DH), BF16)
    return _call(
        _na_prep_body, (b, t // tr),
        [pl.BlockSpec((1, tr, 1536), lambda b, i: (b, i, 0)), _full(gq), _full(gk)],
        [hm, hm, hm], [shp, shp, shp], name="na_prep")(z3, gq, gk)


def _softmax_pv(s, vs):
    m = s[0].max(axis=-1, keepdims=True)
    for t in s[1:]:
        m = jnp.maximum(m, t.max(axis=-1, keepdims=True))
    den = None
    acc = None
    for t, v in zip(s, vs):
        p = jnp.exp(t - m)
        ps = p.sum(axis=-1, keepdims=True)
        o = jnp.einsum('hqk,hkd->hqd', p.astype(BF16), v, preferred_element_type=F32)
        den = ps if den is None else den + ps
        acc = o if acc is None else acc + o
    return acc / den


def _qk(q, k):
    return jnp.einsum('hqd,hkd->hqk', q, k, preferred_element_type=F32)


def _heads_to_lanes(o):
    return jnp.concatenate([o[h] for h in range(o.shape[0])], axis=-1)


NA_ROWS_PER_STEP = 4


def _na_lat_body(q_ref, k_ref, v_ref, bias_ref, o_ref, *, rows, kh, l_tok, m_tok, rps):
    for rr in range(rps):
        r = pl.program_id(1) * rps + rr
        rs = jnp.clip(r - kh // 2, 0, rows - kh)
        delta = r - rs
        start = pl.multiple_of(rs * GRID_W, GRID_W)
        q = q_ref[0, :, rr * GRID_W:(rr + 1) * GRID_W, :]
        kb = k_ref[0, :, pl.ds(start, kh * GRID_W), :]
        vb = v_ref[0, :, pl.ds(start, kh * GRID_W), :]
        kc = k_ref[0, :, l_tok:l_tok + m_tok, :]
        vc = v_ref[0, :, l_tok:l_tok + m_tok, :]
        o = _softmax_pv([_qk(q, kb) + bias_ref[delta], _qk(q, kc)], [vb, vc])
        o_ref[0, rr * GRID_W:(rr + 1) * GRID_W, :] = _heads_to_lanes(o).astype(BF16)


def _na_latent(q, k, v, bias, l_tok, m_tok):
    b, _, t, _ = q.shape
    rows = l_tok // GRID_W
    kh = min(NA_KH, rows)
    rps = math.gcd(rows, NA_ROWS_PER_STEP)
    kv = pl.BlockSpec((1, NA_HEADS, t, NA_DH), lambda b, r: (b, 0, 0, 0))
    return _call(
        functools.partial(_na_lat_body, rows=rows, kh=kh, l_tok=l_tok, m_tok=m_tok, rps=rps), (b, rows // rps),
        [pl.BlockSpec((1, NA_HEADS, rps * GRID_W, NA_DH), lambda b, r: (b, 0, r, 0)), kv, kv, _full(bias)],
        pl.BlockSpec((1, rps * GRID_W, BRANCH_W), lambda b, r: (b, r, 0)),
        _sds((b, l_tok, BRANCH_W), BF16), name="na_latent")(q, k, v, bias)


def _toeplitz_body(r_ref, oh_ref, valid_ref, o_ref):
    h, m, l = _split3(r_ref[...])
    oh = oh_ref[...]
    t = _dot(h, oh) + (_dot(m, oh) + _dot(l, oh))
    o_ref[...] = jnp.where(valid_ref[...] > 0.0, t, NEG_INF)


def _na_bias_table(rpb, rows):
    kh = min(NA_KH, rows)
    nh, ndr, ndc = rpb.shape
    col = jnp.arange(GRID_W)
    cs = jnp.clip(col - NA_KW // 2, 0, GRID_W - NA_KW)
    valid = (col[None, :] >= cs[:, None]) & (col[None, :] < cs[:, None] + NA_KW)
    dc = jnp.clip(col[None, :] - col[:, None] + NA_KW - 1, 0, 2 * NA_KW - 2)
    onehot = (jnp.arange(128)[:, None] == dc.reshape(1, -1)).astype(BF16)
    rp = jnp.pad(rpb.astype(F32).reshape(nh * ndr, ndc), ((0, 0), (0, 128 - ndc)))
    vmask = valid.reshape(1, -1).astype(F32)
    toep = pl.pallas_call(_toeplitz_body, out_shape=_sds((nh * ndr, GRID_W * GRID_W), F32),
                          name="na_bias")(rp, onehot, vmask)
    toep = toep.reshape(nh, ndr, GRID_W, GRID_W)
    per_delta = []
    for delta in range(kh):
        lo = NA_KH - 1 - delta
        t = toep[:, lo:lo + kh].transpose(0, 2, 1, 3)
        per_delta.append(t.reshape(nh, GRID_W, kh * GRID_W))
    return jnp.stack(per_delta, axis=0)


def _attn_body(*refs, nparts, heads):
    q_refs, k_refs, v_ref, o_ref = refs[:nparts], refs[nparts:2 * nparts], refs[2 * nparts], refs[-1]
    s = None
    for qr, kr in zip(q_refs, k_refs):
        t = _qk(qr[0], kr[0])
        s = t if s is None else s + t
    o_ref[0] = _heads_to_lanes(_softmax_pv([s], [v_ref[0]])).astype(BF16)


def _attention(qs, ks, v, *, tq, q0, nq, tk, kblk, name):
    b, heads, _, dv = v.shape
    qspec = [pl.BlockSpec((1, heads, tq, q.shape[-1]), lambda b, i: (b, 0, q0 + i, 0)) for q in qs]
    kspec = [pl.BlockSpec((1, heads, tk, k.shape[-1]), lambda b, i: (b, 0, kblk, 0)) for k in ks]
    vspec = pl.BlockSpec((1, heads, tk, dv), lambda b, i: (b, 0, kblk, 0))
    return _call(
        functools.partial(_attn_body, nparts=len(qs), heads=heads), (b, nq),
        qspec + kspec + [vspec],
        pl.BlockSpec((1, tq, heads * dv), lambda b, i: (b, i, 0)),
        _sds((b, nq * tq, heads * dv), BF16), name=name)(*qs, *ks, v)


def _mla_q_body(z_ref, gc_ref, w_ref, gh_ref, ct_ref, st_ref, q_ref):
    cq = z_ref[0]
    cqn = (cq * lax.rsqrt(jnp.mean(cq * cq, axis=-1, keepdims=True) + NORM_EPS) * gc_ref[...]).astype(BF16)
    q = _dot(cqn, w_ref[...])
    gh = gh_ref[...]
    ct, st = ct_ref[...], st_ref[...]
    for h in range(MLA_HEADS):
        qh = q[:, h * 256:(h + 1) * 256]
        nope, pe, sw = qh[:, :128], qh[:, 128:192], qh[:, 192:256]
        ms = (jnp.sum(nope * nope, axis=-1, keepdims=True) + jnp.sum(pe * pe, axis=-1, keepdims=True)) \
            / (MLA_NOPE + MLA_ROPE)
        rinv = lax.rsqrt(ms + NORM_EPS)
        rot = (pe * rinv * gh[:, 128:192]) * ct + (sw * rinv * gh[:, 192:256]) * st
        qh = jnp.concatenate([nope * rinv * gh[:, :128], rot, jnp.zeros_like(rot)], axis=-1)
        q_ref[0, h] = (qh * MLA_SCALE).astype(BF16)


def _mla_kv_body(zc_ref, zp_ref, gc_ref, w_ref, gh_ref, ct_ref, st_ref, k_ref, v_ref):
    ckv = zc_ref[0]
    cn = (ckv * lax.rsqrt(jnp.mean(ckv * ckv, axis=-1, keepdims=True) + NORM_EPS) * gc_ref[...]).astype(BF16)
    kv = _dot(cn, w_ref[...])
    zp = zp_ref[0]
    pe, sw = zp[:, :64], zp[:, 64:128]
    pe2 = jnp.sum(pe * pe, axis=-1, keepdims=True)
    gh = gh_ref[...]
    ct, st = ct_ref[...], st_ref[...]
    for h in range(MLA_HEADS):
        nope = kv[:, h * 256:h * 256 + 128]
        ms = (jnp.sum(nope * nope, axis=-1, keepdims=True) + pe2) / (MLA_NOPE + MLA_ROPE)
        rinv = lax.rsqrt(ms + NORM_EPS)
        rot = (pe * rinv * gh[:, 128:192]) * ct + (sw * rinv * gh[:, 192:256]) * st
        k_ref[0, h] = jnp.concatenate([nope * rinv * gh[:, :128], rot, jnp.zeros_like(rot)], axis=-1).astype(BF16)
        v_ref[0, h] = kv[:, h * 256 + 128:(h + 1) * 256].astype(BF16)


def _mla_prep(z3, p, ct, st, tr):
    b, t, _ = z3.shape
    rope = pl.BlockSpec((tr, MLA_ROPE), lambda b, i: (i, 0))

    def hm(d):
        return pl.BlockSpec((1, MLA_HEADS, tr, d), lambda b, i: (b, 0, i, 0))

    def shp(d):
        return _sds((b, MLA_HEADS, t, d), BF16)

    q = _call(
        _mla_q_body, (b, t // tr),
        [pl.BlockSpec((1, tr, 512), lambda b, i: (b, i, ZC_CQ // 512)), _full(p['g_cq']), _full(p['w_uq']),
         _full(p['g_q']), rope, rope],
        hm(256), shp(256), name="mla_q_prep")(z3, p['g_cq'], p['w_uq'], p['g_q'], ct, st)
    k, v = _call(
        _mla_kv_body, (b, t // tr),
        [pl.BlockSpec((1, tr, 256), lambda b, i: (b, i, ZC_CKV // 256)),
         pl.BlockSpec((1, tr, 128), lambda b, i: (b, i, ZC_KPE // 128)),
         _full(p['g_ckv']), _full(p['w_ukv']), _full(p['g_k']), rope, rope],
        [hm(256), hm(128)], [shp(256), shp(128)], name="mla_kv_prep")(
            z3, z3, p['g_ckv'], p['w_ukv'], p['g_k'], ct, st)
    return q, k, v


def _rope_tables(l_tok, m_tok):
    half = MLA_ROPE // 4
    freqs = ROPE_THETA ** (-jnp.arange(half, dtype=F32) / half)
    pos = jnp.arange(l_tok)
    ar = (pos // GRID_W).astype(F32)[:, None] * freqs[None, :]
    ac = (pos % GRID_W).astype(F32)[:, None] * freqs[None, :]
    ct = jnp.concatenate([jnp.cos(ar), jnp.cos(ar), jnp.cos(ac), jnp.cos(ac)], axis=-1)
    st = jnp.concatenate([-jnp.sin(ar), jnp.sin(ar), -jnp.sin(ac), jnp.sin(ac)], axis=-1)
    ct = jnp.concatenate([ct, jnp.ones((m_tok, MLA_ROPE), F32)], axis=0)
    st = jnp.concatenate([st, jnp.zeros((m_tok, MLA_ROPE), F32)], axis=0)
    return ct, st


_ROPE_SWAP = tuple(list(range(16, 32)) + list(range(0, 16)) + list(range(48, 64)) + list(range(32, 48)))


def _dft1_body(x_ref, bh_ref, bl_ref, oh_ref, ol_ref):
    xh, xl = _split2(x_ref[0])
    r = _dot(xh, bh_ref[...]) + (_dot(xh, bl_ref[...]) + _dot(xl, bh_ref[...]))
    rh, rl = _split2(r)
    oh_ref[0] = rh
    ol_ref[0] = rl


def _dft2_body(ch_ref, cl_ref, sh_ref, sl_ref, xch_ref, xcl_ref, xsh_ref, xsl_ref, o_ref):
    ch, cl, sh, sl = ch_ref[...], cl_ref[...], sh_ref[...], sl_ref[...]
    xch, xcl, xsh, xsl = xch_ref[0], xcl_ref[0], xsh_ref[0], xsl_ref[0]
    yc = _dot(ch, xch) + (_dot(ch, xcl) + _dot(cl, xch))
    ys = _dot(sh, xsh) + (_dot(sh, xsl) + _dot(sl, xsh))
    o_ref[0] = (yc - ys).astype(BF16)


def _dft_mats(n):
    k = jnp.arange(n, dtype=jnp.int32)
    ang = ((k[:, None] * k[None, :]) % n).astype(F32) * (2.0 * math.pi / n)
    s = 1.0 / math.sqrt(n)
    return jnp.cos(ang) * s, jnp.sin(ang) * s


def _fourier_tables(l_tok, m_tok, ctx_out):
    cw, sw = _dft_mats(BRANCH_W // 4)
    eye = jnp.eye(4, dtype=F32)
    bd = jnp.concatenate([jnp.kron(eye, cw), jnp.kron(eye, sw)], axis=1)
    tabs = {'bd': _split2(bd), 'lat': sum((_split2(m) for m in _dft_mats(l_tok)), ())}
    if ctx_out:
        tabs['ctx'] = sum((_split2(m) for m in _dft_mats(m_tok)), ())
    return tabs


def _fourier(z3, tabs, tr, l_tok, m_tok, ctx_out):
    b, t, _ = z3.shape
    bh, bl = tabs['bd']
    row = pl.BlockSpec((1, tr, 1024), lambda b, i: (b, i, 0))
    nrow = (t if ctx_out else l_tok) // tr
    xh, xl = _call(
        _dft1_body, (b, nrow),
        [pl.BlockSpec((1, tr, 512), lambda b, i: (b, i, ZC_FN // 512)), _full(bh), _full(bl)],
        [row, row], [_sds((b, nrow * tr, 1024), BF16)] * 2, name="dft_channels")(z3, bh, bl)

    def seq_dft(n, blk, mats, name):
        tm = _pick_tile(n, (256, 128, 64, 32, 16, 8))
        mspec = pl.BlockSpec((tm, n), lambda b, i: (i, 0))
        xc = pl.BlockSpec((1, n, 512), lambda b, i: (b, blk, 0))
        xs = pl.BlockSpec((1, n, 512), lambda b, i: (b, blk, 1))
        return _call(
            _dft2_body, (b, n // tm), [mspec] * 4 + [xc, xc, xs, xs],
            pl.BlockSpec((1, tm, 512), lambda b, i: (b, i, 0)),
            _sds((b, n, 512), BF16), name=name)(*mats, xh, xl, xh, xl)

    y = seq_dft(l_tok, 0, tabs['lat'], "dft_seq_latent")
    yc = seq_dft(m_tok, l_tok // m_tok, tabs['ctx'], "dft_seq_ctx") if ctx_out else None
    return y, yc


def _rw_prep_body(zk_ref, zv_ref, zr_ref, zl_ref, zg_ref, halo_ref, mu_ref, kk_ref, ka_ref, w0_ref, a0_ref,
                  w2_ref, a2_ref, trf_ref, trb_ref,
                  v_ref, r_ref, g_ref, cl_ref, kd_ref, b_ref, kx_ref, *, tr):
    row = lax.broadcasted_iota(jnp.int32, (tr, 1), 0)
    halo = halo_ref[0, 0]
    mu = mu_ref[...]

    def mix(x, lo, hi):
        xp = jnp.where(row == 0, halo[0:1, lo:hi], pltpu.roll(x, 1, 0))
        xn = jnp.where(row == tr - 1, halo[1:2, lo:hi], pltpu.roll(x, tr - 1, 0))
        return x + (0.5 * (xp + xn) - x) * mu[:, lo:hi]

    k = mix(zk_ref[0], 0, 512)
    v = mix(zv_ref[0], 512, 1024)
    r = mix(zr_ref[0], 1024, 1536)
    l4 = mix(zl_ref[0], 1536, 1664)
    g_ref[0] = mix(zg_ref[0], 1664, 1792)

    wcat = w0_ref[...] + _dot3(jnp.tanh(l4), w2_ref[...])
    acat = a0_ref[...] + _dot3(l4, a2_ref[...])
    kk = k * kk_ref[...]
    for d, tri_ref in enumerate((trf_ref, trb_ref)):
        w = wcat[:, d * 512:(d + 1) * 512]
        sp = jnp.maximum(-w, 0.0) + jnp.log1p(jnp.exp(-jnp.abs(w)))
        lw = -jnp.exp(-sp - 0.5)
        a = jax.nn.sigmoid(acat[:, d * 512:(d + 1) * 512])
        kd = k * (1.0 + (a - 1.0) * ka_ref[...])
        cl = _dot_sel(tri_ref[...], lw)
        einv = jnp.exp(-lw)
        for h in range(RW_HEADS):
            lo = h * RW_DH
            kkh = kk[:, lo:lo + RW_DH]
            nrm = jnp.sqrt(jnp.sum(kkh * kkh, axis=-1, keepdims=True))
            kkn = kkh / jnp.maximum(nrm, 1e-12)
            cl_ref[0, d, h] = cl[:, lo:lo + RW_DH]
            kd_ref[0, d, h] = kd[:, lo:lo + RW_DH]
            b_ref[0, d, h] = kkn * a[:, lo:lo + RW_DH]
            kx_ref[0, d, h] = kkn * einv[:, lo:lo + RW_DH]
    for h in range(RW_HEADS):
        lo = h * RW_DH
        v_ref[0, h] = v[:, lo:lo + RW_DH]
        r_ref[0, h] = r[:, lo:lo + RW_DH]


def _rw_prep(z3, halo, p, tr):
    b, t, _ = z3.shape
    c = RW_CHUNK
    ti = jnp.arange(tr)
    same = (ti[:, None] // c) == (ti[None, :] // c)
    trf = (same & (ti[None, :] <= ti[:, None])).astype(BF16)
    trb = (same & (ti[None, :] >= ti[:, None])).astype(BF16)

    def zcol(off, w):
        return pl.BlockSpec((1, tr, w), lambda b, i: (b, i, off // w))

    hm = pl.BlockSpec((1, RW_HEADS, tr, RW_DH), lambda b, i: (b, 0, i, 0))
    hmd = pl.BlockSpec((1, 2, RW_HEADS, tr, RW_DH), lambda b, i: (b, 0, 0, i, 0))
    s1 = _sds((b, RW_HEADS, t, RW_DH), F32)
    s2 = _sds((b, 2, RW_HEADS, t, RW_DH), F32)
    consts = [p['rw_mu'], p['rw_k_k'], p['rw_k_a'], p['rw_w0'], p['rw_a0'], p['rw_w2'], p['rw_a2'], trf, trb]
    return _call(
        functools.partial(_rw_prep_body, tr=tr), (b, t // tr),
        [zcol(ZC_RWK, 512), zcol(ZC_RWK + 512, 512), zcol(ZC_RWK + 1024, 512), zcol(ZC_L4, 128), zcol(ZC_G, 128),
         pl.BlockSpec((1, 1, 2, 1792), lambda b, i: (b, i, 0, 0))] + [_full(a) for a in consts],
        [hm, hm, pl.BlockSpec((1, tr, 128), lambda b, i: (b, i, 0)), hmd, hmd, hmd, hmd],
        [s1, s1, _sds((b, t, 128), F32), s2, s2, s2, s2], name="rw_prep")(z3, z3, z3, z3, z3, halo, *consts)


RW_P_HI = 1
RW_P_LO = 1


def _rw_scan_body(r_ref, v_ref, cl_ref, kd_ref, b_ref, kx_ref, y_ref, s_ref, *, tr, rev):
    c = RW_CHUNK
    nc = tr // c
    g = RW_HEADS * nc

    @pl.when(pl.program_id(1) == 0)
    def _():
        s_ref[...] = jnp.zeros_like(s_ref)

    def ld(x):
        return x.reshape(g, c, RW_DH)

    r, v = ld(r_ref[0]), ld(v_ref[0])
    cl, kd, bb, kx = ld(cl_ref[0, 0]), ld(kd_ref[0, 0]), ld(b_ref[0, 0]), ld(kx_ref[0, 0])
    last = 0 if rev else c - 1
    ctot = cl[:, last:last + 1, :]
    e = jnp.exp(cl)
    ei = jnp.exp(-cl)
    ec = jnp.exp(ctot - cl)
    kkt, rt = kx * e, r * e
    kw, bw = kd * ei, bb * ei
    kc, bc = kd * ec, bb * ec
    a_cat = jnp.concatenate([kkt, rt], axis=1)
    p1 = _bmm('gtd,gsd->gts', a_cat, kw, RW_P_HI)
    p2 = _bmm('gtd,gsd->gts', a_cat, bw, RW_P_HI)
    ti = lax.broadcasted_iota(jnp.int32, (c, c), 0)
    si = lax.broadcasted_iota(jnp.int32, (c, c), 1)
    strict = (si > ti) if rev else (si < ti)
    incl = (si >= ti) if rev else (si <= ti)
    nmat = jnp.where(strict, p1[:, :c], 0.0)
    ark = jnp.where(incl, p1[:, c:], 0.0)
    x = -jnp.where(strict, p2[:, :c], 0.0)
    arb = jnp.where(incl, p2[:, c:], 0.0)
    eye_c = (ti == si).astype(F32)
    tm = eye_c + x
    xp = _bmm('gts,gsu->gtu', x, x, RW_P_HI)
    nsq = int(math.log2(c)) - 1
    for i in range(nsq):
        if i < nsq - 1:
            both = _bmm('gts,gsu->gtu', jnp.concatenate([tm, xp], axis=1), xp, RW_P_LO)
            tm, xp = tm + both[:, :c], both[:, c:]
        else:
            tm = tm + _bmm('gts,gsu->gtu', tm, xp, RW_P_LO)
    nav = _bmm('gts,gsd->gtd', jnp.concatenate([nmat, ark], axis=1), v, RW_P_HI)
    nv, arkv = nav[:, :c], nav[:, c:]
    ta = _bmm('gts,gsd->gtd', tm, jnp.concatenate([kkt, nv], axis=2), RW_P_HI)
    ata = _bmm('gts,gsd->gtd', arb, ta, RW_P_HI)
    a2 = rt - ata[:, :, :RW_DH]
    y0 = arkv - ata[:, :, RW_DH:]
    tb = _bmm('gtd,gte->gde', ta, bc, RW_P_HI)
    di = lax.broadcasted_iota(jnp.int32, (RW_DH, RW_DH), 0)
    dj = lax.broadcasted_iota(jnp.int32, (RW_DH, RW_DH), 1)
    gm = jnp.where(di == dj, jnp.exp(ctot), 0.0) - tb[:, :RW_DH]
    hm = _bmm('gtv,gtk->gvk', v, kc, RW_P_HI) - tb[:, RW_DH:]

    def per_chunk(x):
        return x.reshape(RW_HEADS, nc, x.shape[1], x.shape[2])

    a2, y0, gm, hm = per_chunk(a2), per_chunk(y0), per_chunk(gm), per_chunk(hm)
    s = s_ref[...]
    ys = [None] * nc
    for ci in (range(nc - 1, -1, -1) if rev else range(nc)):
        ys[ci] = _bmm('htk,hvk->htv', a2[:, ci], s, RW_P_HI) + y0[:, ci]
        s = _bmm('hvk,hke->hve', s, gm[:, ci], RW_P_HI) + hm[:, ci]
    s_ref[...] = s
    y_ref[0] = jnp.concatenate(ys, axis=1)


def _rw_scan(r, v, cl, kd, bb, kx, tr, nl, nm, rev):
    b, _, t, _ = r.shape
    nt = nl + nm
    d = 1 if rev else 0
    if rev:
        tile = lambda j: nt - 1 - j
    else:
        tile = lambda j: jnp.where(j < nm, nl + j, j - nm)
    hm = pl.BlockSpec((1, RW_HEADS, tr, RW_DH), lambda b, j: (b, 0, tile(j), 0))
    hmd = pl.BlockSpec((1, 1, RW_HEADS, tr, RW_DH), lambda b, j: (b, d, 0, tile(j), 0))
    return _call(
        functools.partial(_rw_scan_body, tr=tr, rev=rev), (b, nt),
        [hm, hm, hmd, hmd, hmd, hmd], hm, _sds((b, RW_HEADS, t, RW_DH), F32),
        scratch=[pltpu.VMEM((RW_HEADS, RW_DH, RW_DH), F32)], sem=("parallel", "arbitrary"),
        name="rw_scan_bwd" if rev else "rw_scan_fwd")(r, v, cl, kd, bb, kx)


def _rw_out_body(yf_ref, yb_ref, r_ref, v_ref, kd_ref, g_ref, lnw_ref, lnb_ref, rk_ref, g2_ref, o_ref):
    y = yf_ref[0] + yb_ref[0]
    mu = jnp.mean(y, axis=-1, keepdims=True)
    var = jnp.mean(jnp.square(y - mu), axis=-1, keepdims=True)
    yn = (y - mu) * lax.rsqrt(var + RW_GN_EPS) * lnw_ref[...] + lnb_ref[...]
    ksum = kd_ref[0, 0] + kd_ref[0, 1]
    bonus = jnp.sum(r_ref[0] * ksum * rk_ref[...], axis=-1, keepdims=True) * v_ref[0]
    o = yn + bonus
    o = jnp.concatenate([o[h] for h in range(RW_HEADS)], axis=-1)
    gate = _dot3(jax.nn.sigmoid(g_ref[0]), g2_ref[...])
    o_ref[0] = (o * gate).astype(BF16)


def _rw_out(yf, yb, r, v, kd, g, p, tr, nrow):
    b = r.shape[0]
    hm = pl.BlockSpec((1, RW_HEADS, tr, RW_DH), lambda b, i: (b, 0, i, 0))
    consts = [p['rw_ln_w'], p['rw_ln_b'], p['rw_r_k'], p['rw_g2']]
    return _call(
        _rw_out_body, (b, nrow),
        [hm, hm, hm, hm, pl.BlockSpec((1, 2, RW_HEADS, tr, RW_DH), lambda b, i: (b, 0, 0, i, 0)),
         pl.BlockSpec((1, tr, 128), lambda b, i: (b, i, 0))] + [_full(a) for a in consts],
        pl.BlockSpec((1, tr, RW_W), lambda b, i: (b, i, 0)),
        _sds((b, nrow * tr, RW_W), BF16), name="rw_out")(yf, yb, r, v, kd, g, *consts)


def _merge_body(h_ref, y0_ref, y1_ref, y2_ref, y3_ref, g0_ref, g1_ref, g2_ref, g3_ref, wb_ref, o_ref):
    h = h_ref[0]
    acc = None
    for i, (y_ref, wg_ref) in enumerate(zip((y0_ref, y1_ref, y2_ref, y3_ref), (g0_ref, g1_ref, g2_ref, g3_ref))):
        t = jax.nn.sigmoid(_dot(h, wg_ref[...])) * _dot(y_ref[0], wb_ref[i])
        acc = t if acc is None else acc + t
    o_ref[0] = acc.astype(BF16)


def _merge(h, ys, wg, wbr, tr, nrow):
    b, _, d = h.shape
    tn = 512
    nn = d // tn
    row = lambda w: pl.BlockSpec((1, tr, w), lambda n, b, i: (b, i, 0))
    gspecs = [pl.BlockSpec((d, tn), functools.partial(lambda n, b, i, k: (0, k * nn + n), k=k))
              for k in range(N_BRANCH)]
    return _call(
        _merge_body, (nn, b, nrow),
        [row(d)] + [row(BRANCH_W)] * 4 + gspecs + [pl.BlockSpec((N_BRANCH, BRANCH_W, tn), lambda n, b, i: (0, 0, n))],
        pl.BlockSpec((1, tr, tn), lambda n, b, i: (b, i, n)),
        _sds((b, nrow * tr, d), BF16), name="merge")(h, *ys, wg, wg, wg, wg, wbr)


def _outproj_body(a_ref, w_ref, x_ref, md_ref, o_ref):
    o_ref[0] = x_ref[0] + md_ref[0, 0, 2:3, :] * _dot(a_ref[0], w_ref[...])


def _outproj(acc, w_out, xs, md, tr, nl, nrow):
    b, _, d = acc.shape
    row = pl.BlockSpec((1, tr, d), lambda b, i: (b, i, 0))
    return _call(
        _outproj_body, (b, nrow), [row, _full(w_out), row, _md_spec(d, nl)], row,
        _sds((b, nrow * tr, d), F32), name="out_proj")(acc, w_out, xs, md)


def _select_body(lg_ref, slot_ref, aff_ref, slotc_ref, *, n, cap):
    lg = lg_ref[0]
    lane = lax.broadcasted_iota(jnp.int32, lg.shape, 1)
    lg = jnp.where(lane < N_EXPERTS, lg, NEG_INF)
    ex = jnp.exp(lg - lg.max(axis=-1, keepdims=True))
    aff = ex / ex.sum(axis=-1, keepdims=True)
    aff_t = aff.T[:N_EXPERTS]
    bits = lax.bitcast_convert_type(aff_t, jnp.int32)
    thr = jnp.zeros((N_EXPERTS, 1), jnp.int32)
    for bit in range(30, -1, -1):
        cand = thr | jnp.int32(1 << bit)
        cnt = jnp.sum((bits >= cand).astype(F32), axis=1, keepdims=True)
        thr = jnp.where(cnt >= cap, cand, thr)
    gt = bits > thr
    eq = bits == thr
    need = cap - jnp.sum(gt.astype(F32), axis=1, keepdims=True)
    both = jnp.concatenate([gt.astype(BF16), eq.astype(BF16)], axis=0)
    cw = min(n, 512)
    pre = []
    for cb in range(n // cw):
        tp = lax.broadcasted_iota(jnp.int32, (n, cw), 0)
        tt = lax.broadcasted_iota(jnp.int32, (n, cw), 1) + cb * cw
        pre.append(_dot(both, (tp < tt).astype(BF16)))
    pre = jnp.concatenate(pre, axis=1) if len(pre) > 1 else pre[0]
    pre_gt, pre_eq = pre[:N_EXPERTS], pre[N_EXPERTS:]
    sel = gt | (eq & (pre_eq < need))
    slot = jnp.where(sel, pre_gt + jnp.minimum(pre_eq, need), -1.0)
    slot_ref[0] = slot.astype(jnp.int32)
    aff_ref[0] = aff_t
    pad = jnp.full((128 - N_EXPERTS, n), -1.0, F32)
    slotc_ref[0] = jnp.concatenate([slot, pad], axis=0).T.astype(jnp.int32)


def _select(logits, n, blk, cap, name):
    b = logits.shape[0]
    er = pl.BlockSpec((1, N_EXPERTS, n), lambda b: (b, 0, 0))
    return _call(
        functools.partial(_select_body, n=n, cap=cap), (b,),
        [pl.BlockSpec((1, n, 128), lambda b: (b, blk, 0))],
        [er, er, pl.BlockSpec((1, n, 128), lambda b: (b, 0, 0))],
        [_sds((b, N_EXPERTS, n), jnp.int32), _sds((b, N_EXPERTS, n), F32), _sds((b, n, 128), jnp.int32)],
        name=name)(logits)


def _ffn_body(h_ref, slot_ref, aff_ref, w1_ref, w3_ref, w2_ref, o_ref, *, grp, n, cap):
    xs, gates = [], []
    jrow = lax.broadcasted_iota(jnp.int32, (cap, n), 0)
    for gi in range(grp):
        pick = jrow == slot_ref[gi, 0]
        xs.append(_dot(pick.astype(BF16), h_ref[gi]).astype(BF16))
        gates.append(jnp.sum(jnp.where(pick, aff_ref[gi, 0], 0.0), axis=1, keepdims=True))
    xe = jnp.concatenate(xs, axis=0) if grp > 1 else xs[0]
    gate = jnp.concatenate(gates, axis=0) if grp > 1 else gates[0]
    a = _dot(xe, w1_ref[0])
    u = _dot(xe, w3_ref[0])
    hm = (a * jax.nn.sigmoid(a) * u).astype(BF16)
    ye = (_dot(hm, w2_ref[0]) * gate).astype(BF16)
    for gi in range(grp):
        o_ref[0, gi] = ye[gi * cap:(gi + 1) * cap]


def _moe_ffn(h2, slot, aff, w1, w3, w2, n, blk, cap, grp, name):
    b, _, d = h2.shape
    ff = w1.shape[-1]
    sr = pl.BlockSpec((grp, 1, 1, n), lambda e, g: (g, e, 0, 0))
    return _call(
        functools.partial(_ffn_body, grp=grp, n=n, cap=cap), (N_EXPERTS, b // grp),
        [pl.BlockSpec((grp, n, d), lambda e, g: (g, blk, 0)), sr, sr,
         pl.BlockSpec((1, d, ff), lambda e, g: (e, 0, 0)), pl.BlockSpec((1, d, ff), lambda e, g: (e, 0, 0)),
         pl.BlockSpec((1, ff, d), lambda e, g: (e, 0, 0))],
        pl.BlockSpec((1, grp, cap, d), lambda e, g: (e, g, 0, 0)),
        _sds((N_EXPERTS, b, cap, d), BF16), name=name)(
            h2, slot.reshape(b, N_EXPERTS, 1, n), aff.reshape(b, N_EXPERTS, 1, n), w1, w3, w2)


def _combine_body(sc_ref, ye_ref, x_ref, md_ref, o_ref, *, cap):
    sc = sc_ref[0]
    jj = lax.broadcasted_iota(jnp.int32, (sc.shape[0], cap), 1)
    acc = None
    for e in range(N_EXPERTS):
        put = (sc[:, e:e + 1] == jj).astype(BF16)
        t = _dot(put, ye_ref[e, 0])
        acc = t if acc is None else acc + t
    o_ref[0] = x_ref[0] + md_ref[0, 0, 5:6, :] * acc


def _moe_combine(slotc, ye, x1, md, n, blk_rows, is_ctx, cap, name):
    b, _, d = x1.shape
    tm = _pick_tile(n, (256, 128, 64, 32, 16, 8))
    off = blk_rows // tm
    return _call(
        functools.partial(_combine_body, cap=cap), (b, n // tm),
        [pl.BlockSpec((1, tm, 128), lambda b, i: (b, i, 0)),
         pl.BlockSpec((N_EXPERTS, 1, cap, d), lambda b, i: (0, b, 0, 0)),
         pl.BlockSpec((1, tm, d), lambda b, i: (b, off + i, 0)),
         pl.BlockSpec((1, 1, 6, d), lambda b, i: (b, 1 if is_ctx else 0, 0, 0))],
        pl.BlockSpec((1, tm, d), lambda b, i: (b, i, 0)),
        _sds((b, n, d), F32), name=name)(slotc, ye, x1, md)


def _moe(x1, h2, logits, md, w1, w3, w2, l_tok, m_tok, ctx_out):
    b = x1.shape[0]
    cap_l = CAPACITY_FACTOR * l_tok // N_EXPERTS
    slot, aff, slotc = _select(logits, l_tok, 0, cap_l, "moe_select_latent")
    ye = _moe_ffn(h2, slot, aff, w1, w3, w2, l_tok, 0, cap_l, 1, "moe_ffn_latent")
    x2 = _moe_combine(slotc, ye, x1, md, l_tok, 0, False, cap_l, "moe_combine_latent")
    if not ctx_out:
        return x2, None
    cap_c = CAPACITY_FACTOR * m_tok // N_EXPERTS
    blk = l_tok // m_tok
    slot, aff, slotc = _select(logits, m_tok, blk, cap_c, "moe_select_ctx")
    ye = _moe_ffn(h2, slot, aff, w1, w3, w2, m_tok, blk, cap_c, b, "moe_ffn_ctx")
    xc2 = _moe_combine(slotc, ye, x1, md, m_tok, l_tok, True, cap_c, "moe_combine_ctx")
    return x2, xc2


def _layer_params(l, w_in, na_q_norm, na_k_norm, na_rpb, mla_cq_norm, mla_ckv_norm, mla_w_uq, mla_w_ukv,
                  mla_q_norm, mla_k_norm, rw_mu_ks, rw_mu_qs, rw_w0, rw_w2, rw_a0, rw_a2, rw_g2, rw_k_k, rw_k_a,
                  rw_r_k, rw_ln_w, rw_ln_b, w_br, w_out, rows):
    w = w_in[l]
    d = w.shape[0]
    kq = KEY_COLS
    kpe = w[:, 1280:1344]
    zpad = lambda n: jnp.zeros((d, n), w.dtype)
    wz = jnp.concatenate([
        w[:, kq:kq + 512], w[:, 0:512], w[:, 512:1024],
        w[:, kq + 512:kq + 1024],
        w[:, 1344:1856], w[:, 1856:2368], w[:, kq + 1024:kq + 1536],
        w[:, kq + 1632:kq + 2144],
        w[:, 1024:1280],
        kpe, kpe[:, jnp.array(_ROPE_SWAP)],
        w[:, 2368:2496],
        w[:, kq + 1536:kq + 1632], zpad(32),
        zpad(NZ - 4736)], axis=1).astype(BF16)
    wg = w[:, kq + 2144:].astype(BF16)
    swap = jnp.array(_ROPE_SWAP)

    def head_gain(g):
        return jnp.concatenate([g[:128], g[128:], g[128:][swap]])[None, :]

    wq = mla_w_uq[l].reshape(MLA_QLORA, MLA_HEADS, MLA_NOPE + MLA_ROPE)
    wq = jnp.concatenate([wq, wq[:, :, MLA_NOPE:][:, :, swap]], axis=-1).reshape(MLA_QLORA, MLA_HEADS * 256)
    w2cat = jnp.zeros((128, 1024), F32).at[0:32, 0:512].set(rw_w2[l, 0]).at[32:64, 512:].set(rw_w2[l, 1])
    a2cat = jnp.zeros((128, 1024), F32).at[64:96, 0:512].set(rw_a2[l, 0]).at[96:128, 512:].set(rw_a2[l, 1])
    mu = jnp.concatenate([rw_mu_ks[l][:1024], rw_mu_qs[l][:512], rw_mu_ks[l][1024:], rw_mu_qs[l][512:],
                          jnp.zeros((32,), F32)])[None, :]
    hd = lambda a: a.reshape(RW_HEADS, 1, RW_DH)
    return {
        'wz': wz, 'wg': wg,
        'na_gq': na_q_norm[l][None, :], 'na_gk': na_k_norm[l][None, :],
        'na_bias': _na_bias_table(na_rpb[l], rows),
        'g_cq': mla_cq_norm[l][None, :], 'g_ckv': mla_ckv_norm[l][None, :],
        'w_uq': wq.astype(BF16), 'w_ukv': mla_w_ukv[l].astype(BF16),
        'g_q': head_gain(mla_q_norm[l]), 'g_k': head_gain(mla_k_norm[l]),
        'rw_mu': mu, 'rw_k_k': rw_k_k[l][None, :], 'rw_k_a': rw_k_a[l][None, :],
        'rw_w0': rw_w0[l].reshape(1, 1024), 'rw_a0': rw_a0[l].reshape(1, 1024), 'rw_w2': w2cat, 'rw_a2': a2cat,
        'rw_ln_w': hd(rw_ln_w[l]), 'rw_ln_b': hd(rw_ln_b[l]), 'rw_r_k': hd(rw_r_k[l]),
        'rw_g2': jnp.concatenate([rw_g2[l], jnp.zeros((32, RW_W), F32)], axis=0),
        'w_br': w_br[l].astype(BF16), 'w_out': w_out[l].astype(BF16),
    }


def _rw_halo(z3, tr, nl):
    b, t, _ = z3.shape
    nt = t // tr
    z4 = z3.reshape(b, nt, tr, z3.shape[-1])
    pick = lambda a: jnp.concatenate([a[..., ZC_RWK:ZC_RWK + 1536], a[..., ZC_L4:ZC_L4 + 256]], axis=-1)
    first, last = pick(z4[:, :, 0, :]), pick(z4[:, :, tr - 1, :])
    zero = jnp.zeros_like(first[:, :1])
    prev = jnp.concatenate([zero, last[:, :-1]], axis=1)
    nxt = jnp.concatenate([first[:, 1:], zero], axis=1)
    tile = jnp.arange(nt)[None, :, None]
    prev = jnp.where(tile == nl, 0.0, prev)
    nxt = jnp.where(tile == nl - 1, 0.0, nxt)
    return jnp.stack([prev, nxt], axis=2)


def _layer(xs, md, p, g1n, g2n, wr, w1, w3, w2, tabs, l_tok, m_tok, tr, ctx_out):
    b, t, d = xs.shape
    nl, nm = l_tok // tr, m_tok // tr
    nrow = (nl + nm) if ctx_out else nl
    blk = l_tok // m_tok

    h = _norm1(xs, g1n, md, tr, nl)
    z3 = _matmul(h.reshape(b * t, d), p['wz'], F32).reshape(b, t, NZ)

    nq, nk, nv = _na_prep(z3, p['na_gq'], p['na_gk'], tr)
    y_na = _na_latent(nq, nk, nv, p['na_bias'], l_tok, m_tok)
    mq, mk, mv = _mla_prep(z3, p, tabs['rope_c'], tabs['rope_s'], tr)
    tq = _pick_tile(l_tok, (256, 128, 64))
    y_mla = _attention([mq], [mk], mv, tq=tq, q0=0, nq=l_tok // tq, tk=t, kblk=0, name="mla_latent")
    rv, rr, rg, cl, kd, bb, kx = _rw_prep(z3, _rw_halo(z3, tr, nl), p, tr)
    yf = _rw_scan(rr, rv, cl, kd, bb, kx, tr, nl, nm, False)
    yb = _rw_scan(rr, rv, cl, kd, bb, kx, tr, nl, nm, True)
    y_rw = _rw_out(yf, yb, rr, rv, kd, rg, p, tr, nrow)
    y_fn, yc_fn = _fourier(z3, tabs, tr, l_tok, m_tok, ctx_out)

    if ctx_out:
        yc_na = _attention([nq], [nk], nv, tq=m_tok, q0=blk, nq=1, tk=m_tok, kblk=blk, name="na_ctx")
        yc_mla = _attention([mq], [mk], mv, tq=m_tok, q0=blk, nq=1, tk=m_tok, kblk=blk, name="mla_ctx")
        y_fn = jnp.concatenate([y_fn, yc_fn], axis=1)
        y_na = jnp.concatenate([y_na, yc_na], axis=1)
        y_mla = jnp.concatenate([y_mla, yc_mla], axis=1)

    acc = _merge(h, (y_fn, y_na, y_mla, y_rw), p['wg'], p['w_br'], tr, nrow)
    x1 = _outproj(acc, p['w_out'], xs, md, tr, nl, nrow)
    h2, logits = _norm2_router(x1, g2n, md, wr, tr, nl)
    return _moe(x1, h2, logits, md, w1, w3, w2, l_tok, m_tok, ctx_out)


def kernel(x, c, ctx, c_ctx, ada_w, ada_b, norm1_g, norm2_g, w_in, na_q_norm, na_k_norm, na_rpb, mla_cq_norm, mla_ckv_norm, mla_w_uq, mla_w_ukv, mla_q_norm, mla_k_norm, rw_mu_ks, rw_mu_qs, rw_w0, rw_w2, rw_a0, rw_a2, rw_g2, rw_k_k, rw_k_a, rw_r_k, rw_ln_w, rw_ln_b, w_br, w_out, moe_router, moe_w1, moe_w3, moe_w2):
    b, l_tok, d = x.shape
    m_tok = ctx.shape[1]
    depth = ada_w.shape[0]
    tr = min(m_tok, 256)
    assert l_tok % tr == 0 and m_tok % tr == 0 and l_tok % m_tok == 0 and tr % RW_CHUNK == 0
    assert l_tok % GRID_W == 0 and m_tok % 128 == 0

    nr = -(-(b + 1) // 8) * 8
    cc = jnp.concatenate([c, c_ctx[None, :], jnp.zeros((nr - b - 1, d), F32)], axis=0)
    mod = _modulation(cc, ada_w, ada_b)
    tabs = {}
    tabs['rope_c'], tabs['rope_s'] = _rope_tables(l_tok, m_tok)

    xs = jnp.concatenate([x, ctx], axis=1)
    for l in range(depth):
        ctx_out = l < depth - 1
        tabs.update(_fourier_tables(l_tok, m_tok, ctx_out))
        p = _layer_params(l, w_in, na_q_norm, na_k_norm, na_rpb, mla_cq_norm, mla_ckv_norm, mla_w_uq, mla_w_ukv,
                          mla_q_norm, mla_k_norm, rw_mu_ks, rw_mu_qs, rw_w0, rw_w2, rw_a0, rw_a2, rw_g2, rw_k_k,
                          rw_k_a, rw_r_k, rw_ln_w, rw_ln_b, w_br, w_out, l_tok // GRID_W)
        md = jnp.stack([mod[l, :b], jnp.broadcast_to(mod[l, b], (b, 6 * d))], axis=1).reshape(b, 2, 6, d)
        wr = jnp.concatenate([moe_router[l], jnp.zeros((d, 128 - N_EXPERTS), F32)], axis=1)
        x2, xc2 = _layer(xs, md, p, norm1_g[l][None, :], norm2_g[l][None, :], wr,
                         moe_w1[l].astype(BF16), moe_w3[l].astype(BF16), moe_w2[l].astype(BF16),
                         tabs, l_tok, m_tok, tr, ctx_out)
        xs = jnp.concatenate([x2, xc2], axis=1) if ctx_out else x2
    return xs
```

```python
import functools
import math

import jax
import jax.numpy as jnp
from jax import lax
from jax.experimental import pallas as pl
from jax.experimental.pallas import tpu as pltpu

F32 = jnp.float32
BF16 = jnp.bfloat16

GRID_W = 64
N_BRANCH = 4
BRANCH_W = 512
NA_HEADS, NA_DH, NA_KH, NA_KW = 8, 64, 8, 16
NA_SCALE = NA_DH ** -0.5
MLA_HEADS, MLA_NOPE, MLA_ROPE, MLA_V = 4, 128, 64, 128
MLA_QLORA, MLA_KVLORA = 512, 256
MLA_SCALE = (MLA_NOPE + MLA_ROPE) ** -0.5
RW_HEADS, RW_DH, RW_W = 8, 64, 512
RW_LORA, RW_GATE_LORA = 32, 96
RW_GN_EPS = 64e-5
N_EXPERTS = 16
CAPACITY_FACTOR = 2
ROPE_THETA = 10000.0
NORM_EPS = 1e-6
NEG_INF = -1e30
RW_CHUNK = 64
KEY_COLS = 2496
NZ = 5120
ZC_NA = 0
ZC_CQ = 1536
ZC_RWK = 2048
ZC_FN = 3584
ZC_CKV = 4096
ZC_KPE = 4352
ZC_L4 = 4480
ZC_G = 4608

VMEM_LIMIT = 56 * 2 ** 20


def _call(body, grid, in_specs, out_specs, out_shape, scratch=(), sem=None, name=None):
    return pl.pallas_call(
        body, grid=grid, in_specs=in_specs, out_specs=out_specs, out_shape=out_shape,
        scratch_shapes=list(scratch), name=name,
        compiler_params=pltpu.CompilerParams(
            dimension_semantics=sem or ("parallel",) * len(grid), vmem_limit_bytes=VMEM_LIMIT))


def _full(a):
    nd = a.ndim
    return pl.BlockSpec(a.shape, lambda *_: (0,) * nd)


def _sds(shape, dtype):
    return jax.ShapeDtypeStruct(shape, dtype)


def _dot(a, b):
    return jnp.dot(a, b, preferred_element_type=F32)


def _dot_nt(a, b):
    return lax.dot_general(a, b, (((1,), (1,)), ((), ())), preferred_element_type=F32)


def _split2(x):
    hi = x.astype(BF16)
    return hi, (x - hi.astype(F32)).astype(BF16)


def _split3(x):
    hi = x.astype(BF16)
    r = x - hi.astype(F32)
    mid = r.astype(BF16)
    return hi, mid, (r - mid.astype(F32)).astype(BF16)


def _dot3(a, b):
    ah, al = _split2(a)
    bh, bl = _split2(b)
    return _dot(ah, bh) + (_dot(ah, bl) + _dot(al, bh))


def _dot_sel(sel_bf16, x):
    h, m, l = _split3(x)
    return _dot(sel_bf16, h) + (_dot(sel_bf16, m) + _dot(sel_bf16, l))


def _bmm(spec, a, b, passes):
    e = functools.partial(jnp.einsum, spec, preferred_element_type=F32)
    if passes == 1:
        return e(a.astype(BF16), b.astype(BF16))
    ah, al = _split2(a)
    bh, bl = _split2(b)
    return e(ah, bh) + (e(ah, bl) + e(al, bh))


def _pick_tile(n, cands):
    for c in cands:
        if n % c == 0:
            return c
    raise ValueError(f"no tile for {n}")


def _mod_body(c_ref, w_ref, b_ref, o_ref):
    cc = c_ref[...]
    s = cc * jax.nn.sigmoid(cc)
    o_ref[0] = _dot3(s, w_ref[0]) + b_ref[0]


def _modulation(cc, ada_w, ada_b):
    depth, d, n6 = ada_w.shape
    r = cc.shape[0]
    tn = 1024
    return _call(
        _mod_body, (depth, n6 // tn),
        [pl.BlockSpec((r, d), lambda l, n: (0, 0)),
         pl.BlockSpec((1, d, tn), lambda l, n: (l, 0, n)),
         pl.BlockSpec((1, 1, tn), lambda l, n: (l, 0, n))],
        pl.BlockSpec((1, r, tn), lambda l, n: (l, 0, n)),
        _sds((depth, r, n6), F32), name="adaln_mod")(cc, ada_w, ada_b.reshape(depth, 1, n6))


def _modnorm(x, g, shift, scale):
    y = x * lax.rsqrt(jnp.mean(x * x, axis=-1, keepdims=True) + NORM_EPS)
    return (y * g) * (1.0 + scale) + shift


def _norm1_body(x_ref, g_ref, md_ref, h_ref):
    h = _modnorm(x_ref[0], g_ref[...], md_ref[0, 0, 0:1, :], md_ref[0, 0, 1:2, :])
    h_ref[0] = h.astype(BF16)


def _norm2_body(x_ref, g_ref, md_ref, wr_ref, h_ref, lg_ref):
    h = _modnorm(x_ref[0], g_ref[...], md_ref[0, 0, 3:4, :], md_ref[0, 0, 4:5, :])
    h_ref[0] = h.astype(BF16)
    lg_ref[0] = _dot3(h, wr_ref[...])


def _md_spec(d, nl):
    return pl.BlockSpec((1, 1, 6, d), lambda b, i: (b, i // nl, 0, 0))


def _norm1(xs, g, md, tr, nl):
    b, t, d = xs.shape
    return _call(
        _norm1_body, (b, t // tr),
        [pl.BlockSpec((1, tr, d), lambda b, i: (b, i, 0)), _full(g), _md_spec(d, nl)],
        pl.BlockSpec((1, tr, d), lambda b, i: (b, i, 0)),
        _sds((b, t, d), BF16), name="norm1")(xs, g, md)


def _norm2_router(x1, g, md, wr, tr, nl):
    b, t, d = x1.shape
    return _call(
        _norm2_body, (b, t // tr),
        [pl.BlockSpec((1, tr, d), lambda b, i: (b, i, 0)), _full(g), _md_spec(d, nl), _full(wr)],
        [pl.BlockSpec((1, tr, d), lambda b, i: (b, i, 0)), pl.BlockSpec((1, tr, 128), lambda b, i: (b, i, 0))],
        [_sds((b, t, d), BF16), _sds((b, t, 128), F32)], name="norm2_router")(x1, g, md, wr)


def _mm_body(a_ref, w_ref, o_ref):
    o_ref[...] = _dot(a_ref[...], w_ref[...]).astype(o_ref.dtype)


def _matmul(a, w, out_dtype):
    r, k = a.shape
    n = w.shape[1]
    tm = _pick_tile(r, (1024, 768, 512, 384, 256, 128, 64, 32, 16, 8))
    tn = _pick_tile(n, (1024, 512, 256, 128))
    return _call(
        _mm_body, (r // tm, n // tn),
        [pl.BlockSpec((tm, k), lambda i, j: (i, 0)), pl.BlockSpec((k, tn), lambda i, j: (0, j))],
        pl.BlockSpec((tm, tn), lambda i, j: (i, j)),
        _sds((r, n), out_dtype), name="in_proj")(a, w)


def _head_sumsq(x, ones_bd):
    hi, lo = _split2(x * x)
    return _dot(hi, ones_bd) + _dot(lo, ones_bd)


def _head_ones(width, dh):
    i = jnp.arange(width) // dh
    return (i[:, None] == i[None, :]).astype(BF16)


def _na_prep_body(z_ref, gq_ref, gk_ref, e_ref, q_ref, k_ref, v_ref):
    z = z_ref[0]
    q, k, v = z[:, :512], z[:, 512:1024], z[:, 1024:]
    e = e_ref[...]
    qn = (q * lax.rsqrt(_head_sumsq(q, e) / NA_DH + NORM_EPS) * (gq_ref[...] * NA_SCALE)).astype(BF16)
    kn = (k * lax.rsqrt(_head_sumsq(k, e) / NA_DH + NORM_EPS) * gk_ref[...]).astype(BF16)
    vb = v.astype(BF16)
    for h in range(NA_HEADS):
        lo = h * NA_DH
        q_ref[0, h] = qn[:, lo:lo + NA_DH]
        k_ref[0, h] = kn[:, lo:lo + NA_DH]
        v_ref[0, h] = vb[:, lo:lo + NA_DH]


def _na_prep(z3, gq, gk, tr):
    b, t, _ = z3.shape
    hm = pl.BlockSpec((1, NA_HEADS, tr, NA_DH), lambda b, i: (b, 0, i, 0))
    shp = _sds((b, NA_HEADS, t, NA_DH), BF16)
    e = _head_ones(NA_HEADS * NA_DH, NA_DH)
    gq, gk = jnp.tile(gq, (1, NA_HEADS)), jnp.tile(gk, (1, NA_HEADS))
    return _call(
        _na_prep_body, (b, t // tr),
        [pl.BlockSpec((1, tr, 1536), lambda b, i: (b, i, 0)), _full(gq), _full(gk), _full(e)],
        [hm, hm, hm], [shp, shp, shp], name="na_prep")(z3, gq, gk, e)


def _softmax_pv(s, vs):
    m = s[0].max(axis=-1, keepdims=True)
    for t in s[1:]:
        m = jnp.maximum(m, t.max(axis=-1, keepdims=True))
    den = None
    acc = None
    for t, v in zip(s, vs):
        p = jnp.exp(t - m)
        ps = p.sum(axis=-1, keepdims=True)
        o = jnp.einsum('hqk,hkd->hqd', p.astype(BF16), v, preferred_element_type=F32)
        den = ps if den is None else den + ps
        acc = o if acc is None else acc + o
    return acc / den


def _qk(q, k):
    return jnp.einsum('hqd,hkd->hqk', q, k, preferred_element_type=F32)


def _heads_to_lanes(o):
    return jnp.concatenate([o[h] for h in range(o.shape[0])], axis=-1)


NA_ROWS_PER_STEP = 4


def _na_lat_body(q_ref, k_ref, v_ref, bias_ref, o_ref, *, rows, kh, l_tok, m_tok, rps):
    for rr in range(rps):
        r = pl.program_id(1) * rps + rr
        rs = jnp.clip(r - kh // 2, 0, rows - kh)
        delta = r - rs
        start = pl.multiple_of(rs * GRID_W, GRID_W)
        q = q_ref[0, :, rr * GRID_W:(rr + 1) * GRID_W, :]
        kb = k_ref[0, :, pl.ds(start, kh * GRID_W), :]
        vb = v_ref[0, :, pl.ds(start, kh * GRID_W), :]
        kc = k_ref[0, :, l_tok:l_tok + m_tok, :]
        vc = v_ref[0, :, l_tok:l_tok + m_tok, :]
        o = _softmax_pv([_qk(q, kb) + bias_ref[delta], _qk(q, kc)], [vb, vc])
        o_ref[0, rr * GRID_W:(rr + 1) * GRID_W, :] = _heads_to_lanes(o).astype(BF16)


def _na_latent(q, k, v, bias, l_tok, m_tok):
    b, _, t, _ = q.shape
    rows = l_tok // GRID_W
    kh = min(NA_KH, rows)
    rps = math.gcd(rows, NA_ROWS_PER_STEP)
    kv = pl.BlockSpec((1, NA_HEADS, t, NA_DH), lambda b, r: (b, 0, 0, 0))
    return _call(
        functools.partial(_na_lat_body, rows=rows, kh=kh, l_tok=l_tok, m_tok=m_tok, rps=rps), (b, rows // rps),
        [pl.BlockSpec((1, NA_HEADS, rps * GRID_W, NA_DH), lambda b, r: (b, 0, r, 0)), kv, kv, _full(bias)],
        pl.BlockSpec((1, rps * GRID_W, BRANCH_W), lambda b, r: (b, r, 0)),
        _sds((b, l_tok, BRANCH_W), BF16), name="na_latent")(q, k, v, bias)


def _toeplitz_body(r_ref, oh_ref, valid_ref, o_ref):
    h, m, l = _split3(r_ref[...])
    oh = oh_ref[...]
    t = _dot(h, oh) + (_dot(m, oh) + _dot(l, oh))
    o_ref[...] = jnp.where(valid_ref[...] > 0.0, t, NEG_INF)


def _na_bias_table(rpb, rows):
    kh = min(NA_KH, rows)
    nh, ndr, ndc = rpb.shape
    col = jnp.arange(GRID_W)
    cs = jnp.clip(col - NA_KW // 2, 0, GRID_W - NA_KW)
    valid = (col[None, :] >= cs[:, None]) & (col[None, :] < cs[:, None] + NA_KW)
    dc = jnp.clip(col[None, :] - col[:, None] + NA_KW - 1, 0, 2 * NA_KW - 2)
    onehot = (jnp.arange(128)[:, None] == dc.reshape(1, -1)).astype(BF16)
    rp = jnp.pad(rpb.astype(F32).reshape(nh * ndr, ndc), ((0, 0), (0, 128 - ndc)))
    vmask = valid.reshape(1, -1).astype(F32)
    toep = pl.pallas_call(_toeplitz_body, out_shape=_sds((nh * ndr, GRID_W * GRID_W), F32),
                          name="na_bias")(rp, onehot, vmask)
    toep = toep.reshape(nh, ndr, GRID_W, GRID_W)
    per_delta = []
    for delta in range(kh):
        lo = NA_KH - 1 - delta
        t = toep[:, lo:lo + kh].transpose(0, 2, 1, 3)
        per_delta.append(t.reshape(nh, GRID_W, kh * GRID_W))
    return jnp.stack(per_delta, axis=0)


def _attn_body(*refs, nparts, heads):
    q_refs, k_refs, v_ref, o_ref = refs[:nparts], refs[nparts:2 * nparts], refs[2 * nparts], refs[-1]
    s = None
    for qr, kr in zip(q_refs, k_refs):
        t = _qk(qr[0], kr[0])
        s = t if s is None else s + t
    o_ref[0] = _heads_to_lanes(_softmax_pv([s], [v_ref[0]])).astype(BF16)


def _attention(qs, ks, v, *, tq, q0, nq, tk, kblk, name):
    b, heads, _, dv = v.shape
    qspec = [pl.BlockSpec((1, heads, tq, q.shape[-1]), lambda b, i: (b, 0, q0 + i, 0)) for q in qs]
    kspec = [pl.BlockSpec((1, heads, tk, k.shape[-1]), lambda b, i: (b, 0, kblk, 0)) for k in ks]
    vspec = pl.BlockSpec((1, heads, tk, dv), lambda b, i: (b, 0, kblk, 0))
    return _call(
        functools.partial(_attn_body, nparts=len(qs), heads=heads), (b, nq),
        qspec + kspec + [vspec],
        pl.BlockSpec((1, tq, heads * dv), lambda b, i: (b, i, 0)),
        _sds((b, nq * tq, heads * dv), BF16), name=name)(*qs, *ks, v)


def _mla_q_body(z_ref, gc_ref, w_ref, gh_ref, ct_ref, st_ref, q_ref):
    cq = z_ref[0]
    cqn = (cq * lax.rsqrt(jnp.mean(cq * cq, axis=-1, keepdims=True) + NORM_EPS) * gc_ref[...]).astype(BF16)
    q = _dot(cqn, w_ref[...])
    gh = gh_ref[...]
    ct, st = ct_ref[...], st_ref[...]
    for h in range(MLA_HEADS):
        qh = q[:, h * 256:(h + 1) * 256]
        nope, pe, sw = qh[:, :128], qh[:, 128:192], qh[:, 192:256]
        ms = (jnp.sum(nope * nope, axis=-1, keepdims=True) + jnp.sum(pe * pe, axis=-1, keepdims=True)) \
            / (MLA_NOPE + MLA_ROPE)
        rinv = lax.rsqrt(ms + NORM_EPS)
        rot = (pe * rinv * gh[:, 128:192]) * ct + (sw * rinv * gh[:, 192:256]) * st
        qh = jnp.concatenate([nope * rinv * gh[:, :128], rot, jnp.zeros_like(rot)], axis=-1)
        q_ref[0, h] = (qh * MLA_SCALE).astype(BF16)


def _mla_kv_body(zc_ref, zp_ref, gc_ref, w_ref, gh_ref, ct_ref, st_ref, k_ref, v_ref):
    ckv = zc_ref[0]
    cn = (ckv * lax.rsqrt(jnp.mean(ckv * ckv, axis=-1, keepdims=True) + NORM_EPS) * gc_ref[...]).astype(BF16)
    kv = _dot(cn, w_ref[...])
    zp = zp_ref[0]
    pe, sw = zp[:, :64], zp[:, 64:128]
    pe2 = jnp.sum(pe * pe, axis=-1, keepdims=True)
    gh = gh_ref[...]
    ct, st = ct_ref[...], st_ref[...]
    for h in range(MLA_HEADS):
        nope = kv[:, h * 256:h * 256 + 128]
        ms = (jnp.sum(nope * nope, axis=-1, keepdims=True) + pe2) / (MLA_NOPE + MLA_ROPE)
        rinv = lax.rsqrt(ms + NORM_EPS)
        rot = (pe * rinv * gh[:, 128:192]) * ct + (sw * rinv * gh[:, 192:256]) * st
        k_ref[0, h] = jnp.concatenate([nope * rinv * gh[:, :128], rot, jnp.zeros_like(rot)], axis=-1).astype(BF16)
        v_ref[0, h] = kv[:, h * 256 + 128:(h + 1) * 256].astype(BF16)


def _mla_prep(z3, p, ct, st, tr):
    b, t, _ = z3.shape
    rope = pl.BlockSpec((tr, MLA_ROPE), lambda b, i: (i, 0))

    def hm(d):
        return pl.BlockSpec((1, MLA_HEADS, tr, d), lambda b, i: (b, 0, i, 0))

    def shp(d):
        return _sds((b, MLA_HEADS, t, d), BF16)

    q = _call(
        _mla_q_body, (b, t // tr),
        [pl.BlockSpec((1, tr, 512), lambda b, i: (b, i, ZC_CQ // 512)), _full(p['g_cq']), _full(p['w_uq']),
         _full(p['g_q']), rope, rope],
        hm(256), shp(256), name="mla_q_prep")(z3, p['g_cq'], p['w_uq'], p['g_q'], ct, st)
    k, v = _call(
        _mla_kv_body, (b, t // tr),
        [pl.BlockSpec((1, tr, 256), lambda b, i: (b, i, ZC_CKV // 256)),
         pl.BlockSpec((1, tr, 128), lambda b, i: (b, i, ZC_KPE // 128)),
         _full(p['g_ckv']), _full(p['w_ukv']), _full(p['g_k']), rope, rope],
        [hm(256), hm(128)], [shp(256), shp(128)], name="mla_kv_prep")(
            z3, z3, p['g_ckv'], p['w_ukv'], p['g_k'], ct, st)
    return q, k, v


def _rope_tables(l_tok, m_tok):
    half = MLA_ROPE // 4
    freqs = ROPE_THETA ** (-jnp.arange(half, dtype=F32) / half)
    pos = jnp.arange(l_tok)
    ar = (pos // GRID_W).astype(F32)[:, None] * freqs[None, :]
    ac = (pos % GRID_W).astype(F32)[:, None] * freqs[None, :]
    ct = jnp.concatenate([jnp.cos(ar), jnp.cos(ar), jnp.cos(ac), jnp.cos(ac)], axis=-1)
    st = jnp.concatenate([-jnp.sin(ar), jnp.sin(ar), -jnp.sin(ac), jnp.sin(ac)], axis=-1)
    ct = jnp.concatenate([ct, jnp.ones((m_tok, MLA_ROPE), F32)], axis=0)
    st = jnp.concatenate([st, jnp.zeros((m_tok, MLA_ROPE), F32)], axis=0)
    return ct, st


def _swap_halves(a):
    return jnp.concatenate([a[..., 16:32], a[..., 0:16], a[..., 48:64], a[..., 32:48]], axis=-1)


def _dft1_body(x_ref, bh_ref, bl_ref, o_ref):
    xh, xl = _split2(x_ref[0])
    r = _dot(xh, bh_ref[...]) + (_dot(xh, bl_ref[...]) + _dot(xl, bh_ref[...]))
    o_ref[0] = r.astype(BF16)


def _dft2_body(c_ref, s_ref, xc_ref, xs_ref, o_ref):
    o_ref[0] = (_dot(c_ref[...], xc_ref[0]) - _dot(s_ref[...], xs_ref[0])).astype(BF16)


def _dft_mats(n):
    k = jnp.arange(n, dtype=jnp.int32)
    ang = ((k[:, None] * k[None, :]) % n).astype(F32) * (2.0 * math.pi / n)
    s = 1.0 / math.sqrt(n)
    return jnp.cos(ang) * s, jnp.sin(ang) * s


def _fourier_tables(l_tok, m_tok, ctx_out):
    cw, sw = _dft_mats(BRANCH_W // 4)
    eye = jnp.eye(4, dtype=F32)
    bd = jnp.concatenate([jnp.kron(eye, cw), jnp.kron(eye, sw)], axis=1)
    tabs = {'bd': _split2(bd), 'lat': tuple(m.astype(BF16) for m in _dft_mats(l_tok))}
    if ctx_out:
        tabs['ctx'] = tuple(m.astype(BF16) for m in _dft_mats(m_tok))
    return tabs


def _fourier(z3, tabs, tr, l_tok, m_tok, ctx_out):
    b, t, _ = z3.shape
    bh, bl = tabs['bd']
    row = pl.BlockSpec((1, tr, 1024), lambda b, i: (b, i, 0))
    nrow = (t if ctx_out else l_tok) // tr
    xw = _call(
        _dft1_body, (b, nrow),
        [pl.BlockSpec((1, tr, 512), lambda b, i: (b, i, ZC_FN // 512)), _full(bh), _full(bl)],
        row, _sds((b, nrow * tr, 1024), BF16), name="dft_channels")(z3, bh, bl)

    def seq_dft(n, blk, mats, name):
        tm = _pick_tile(n, (512, 256, 128, 64, 32, 16, 8))
        mspec = pl.BlockSpec((tm, n), lambda b, i: (i, 0))
        xc = pl.BlockSpec((1, n, 512), lambda b, i: (b, blk, 0))
        xs = pl.BlockSpec((1, n, 512), lambda b, i: (b, blk, 1))
        return _call(
            _dft2_body, (b, n // tm), [mspec, mspec, xc, xs],
            pl.BlockSpec((1, tm, 512), lambda b, i: (b, i, 0)),
            _sds((b, n, 512), BF16), name=name)(*mats, xw, xw)

    y = seq_dft(l_tok, 0, tabs['lat'], "dft_seq_latent")
    yc = seq_dft(m_tok, l_tok // m_tok, tabs['ctx'], "dft_seq_ctx") if ctx_out else None
    return y, yc


def _rw_prep_body(zk_ref, zv_ref, zr_ref, zl_ref, zg_ref, halo_ref, mu_ref, kk_ref, ka_ref, w0_ref, a0_ref,
                  w2_ref, a2_ref, trf_ref, trb_ref, e_ref,
                  v_ref, r_ref, g_ref, cl_ref, kd_ref, b_ref, kx_ref, *, tr):
    row = lax.broadcasted_iota(jnp.int32, (tr, 1), 0)
    halo = halo_ref[0, 0]
    mu = mu_ref[...]

    def mix(x, lo, hi):
        xp = jnp.where(row == 0, halo[0:1, lo:hi], pltpu.roll(x, 1, 0))
        xn = jnp.where(row == tr - 1, halo[1:2, lo:hi], pltpu.roll(x, tr - 1, 0))
        return x + (0.5 * (xp + xn) - x) * mu[:, lo:hi]

    k = mix(zk_ref[0], 0, 512)
    v = mix(zv_ref[0], 512, 1024)
    r = mix(zr_ref[0], 1024, 1536)
    l4 = mix(zl_ref[0], 1536, 1664)
    g_ref[0] = mix(zg_ref[0], 1664, 1792)

    wcat = w0_ref[...] + _dot3(jnp.tanh(l4), w2_ref[...])
    acat = a0_ref[...] + _dot3(l4, a2_ref[...])
    kk = k * kk_ref[...]
    kkn = kk / jnp.maximum(jnp.sqrt(_head_sumsq(kk, e_ref[...])), 1e-12)
    for d, tri_ref in enumerate((trf_ref, trb_ref)):
        w = wcat[:, d * 512:(d + 1) * 512]
        lw = -math.exp(-0.5) * jax.nn.sigmoid(w)
        a = jax.nn.sigmoid(acat[:, d * 512:(d + 1) * 512])
        kd = k * (1.0 + (a - 1.0) * ka_ref[...])
        cl = _dot_sel(tri_ref[...], lw)
        bb = kkn * a
        kx = kkn * jnp.exp(-lw)
        for h in range(RW_HEADS):
            lo = h * RW_DH
            cl_ref[0, d, h] = cl[:, lo:lo + RW_DH]
            kd_ref[0, d, h] = kd[:, lo:lo + RW_DH]
            b_ref[0, d, h] = bb[:, lo:lo + RW_DH]
            kx_ref[0, d, h] = kx[:, lo:lo + RW_DH]
    for h in range(RW_HEADS):
        lo = h * RW_DH
        v_ref[0, h] = v[:, lo:lo + RW_DH]
        r_ref[0, h] = r[:, lo:lo + RW_DH]


def _rw_prep(z3, halo, p, tr):
    b, t, _ = z3.shape
    c = RW_CHUNK
    ti = jnp.arange(tr)
    same = (ti[:, None] // c) == (ti[None, :] // c)
    trf = (same & (ti[None, :] <= ti[:, None])).astype(BF16)
    trb = (same & (ti[None, :] >= ti[:, None])).astype(BF16)

    def zcol(off, w):
        return pl.BlockSpec((1, tr, w), lambda b, i: (b, i, off // w))

    hm = pl.BlockSpec((1, RW_HEADS, tr, RW_DH), lambda b, i: (b, 0, i, 0))
    hmd = pl.BlockSpec((1, 2, RW_HEADS, tr, RW_DH), lambda b, i: (b, 0, 0, i, 0))
    s1 = _sds((b, RW_HEADS, t, RW_DH), F32)
    s2 = _sds((b, 2, RW_HEADS, t, RW_DH), F32)
    consts = [p['rw_mu'], p['rw_k_k'], p['rw_k_a'], p['rw_w0'], p['rw_a0'], p['rw_w2'], p['rw_a2'], trf, trb,
              _head_ones(RW_W, RW_DH)]
    return _call(
        functools.partial(_rw_prep_body, tr=tr), (b, t // tr),
        [zcol(ZC_RWK, 512), zcol(ZC_RWK + 512, 512), zcol(ZC_RWK + 1024, 512), zcol(ZC_L4, 128), zcol(ZC_G, 128),
         pl.BlockSpec((1, 1, 2, 1792), lambda b, i: (b, i, 0, 0))] + [_full(a) for a in consts],
        [hm, hm, pl.BlockSpec((1, tr, 128), lambda b, i: (b, i, 0)), hmd, hmd, hmd, hmd],
        [s1, s1, _sds((b, t, 128), F32), s2, s2, s2, s2], name="rw_prep")(z3, z3, z3, z3, z3, halo, *consts)


RW_P_HI = 1
RW_P_LO = 1


def _rw_scan_tile(r_ref, v_ref, cl_ref, kd_ref, b_ref, kx_ref, y_ref, s_ref, *, tr, rev):
    c = RW_CHUNK
    nc = tr // c
    g = RW_HEADS * nc

    def ld(x):
        return x.reshape(g, c, RW_DH)

    r, v = ld(r_ref[0]), ld(v_ref[0])
    cl, kd, bb, kx = ld(cl_ref[0, 0]), ld(kd_ref[0, 0]), ld(b_ref[0, 0]), ld(kx_ref[0, 0])
    last = 0 if rev else c - 1
    ctot = cl[:, last:last + 1, :]
    e = jnp.exp(cl)
    ei = jnp.exp(-cl)
    ec = jnp.exp(ctot - cl)
    kkt, rt = kx * e, r * e
    kw, bw = kd * ei, bb * ei
    kc, bc = kd * ec, bb * ec
    a_cat = jnp.concatenate([kkt, rt], axis=1)
    p1 = _bmm('gtd,gsd->gts', a_cat, kw, RW_P_HI)
    p2 = _bmm('gtd,gsd->gts', a_cat, bw, RW_P_HI)
    ti = lax.broadcasted_iota(jnp.int32, (c, c), 0)
    si = lax.broadcasted_iota(jnp.int32, (c, c), 1)
    strict = (si > ti) if rev else (si < ti)
    incl = (si >= ti) if rev else (si <= ti)
    nmat = jnp.where(strict, p1[:, :c], 0.0)
    ark = jnp.where(incl, p1[:, c:], 0.0)
    x = -jnp.where(strict, p2[:, :c], 0.0)
    arb = jnp.where(incl, p2[:, c:], 0.0)
    eye_c = (ti == si).astype(F32)
    tm = eye_c + x
    xp = _bmm('gts,gsu->gtu', x, x, RW_P_HI)
    nsq = int(math.log2(c)) - 1
    for i in range(nsq):
        if i < nsq - 1:
            both = _bmm('gts,gsu->gtu', jnp.concatenate([tm, xp], axis=1), xp, RW_P_LO)
            tm, xp = tm + both[:, :c], both[:, c:]
        else:
            tm = tm + _bmm('gts,gsu->gtu', tm, xp, RW_P_LO)
    nav = _bmm('gts,gsd->gtd', jnp.concatenate([nmat, ark], axis=1), v, RW_P_HI)
    nv, arkv = nav[:, :c], nav[:, c:]
    ta = _bmm('gts,gsd->gtd', tm, jnp.concatenate([kkt, nv], axis=2), RW_P_HI)
    ata = _bmm('gts,gsd->gtd', arb, ta, RW_P_HI)
    a2 = rt - ata[:, :, :RW_DH]
    y0 = arkv - ata[:, :, RW_DH:]
    tb = _bmm('gtd,gte->gde', ta, bc, RW_P_HI)
    di = lax.broadcasted_iota(jnp.int32, (RW_DH, RW_DH), 0)
    dj = lax.broadcasted_iota(jnp.int32, (RW_DH, RW_DH), 1)
    gm = jnp.where(di == dj, jnp.exp(ctot), 0.0) - tb[:, :RW_DH]
    hm = _bmm('gtv,gtk->gvk', v, kc, RW_P_HI) - tb[:, RW_DH:]

    def per_chunk(x):
        return x.reshape(RW_HEADS, nc, x.shape[1], x.shape[2])

    a2, y0, gm, hm = per_chunk(a2), per_chunk(y0), per_chunk(gm), per_chunk(hm)
    s = s_ref[...]
    ys = [None] * nc
    for ci in (range(nc - 1, -1, -1) if rev else range(nc)):
        ys[ci] = _bmm('htk,hvk->htv', a2[:, ci], s, RW_P_HI) + y0[:, ci]
        s = _bmm('hvk,hke->hve', s, gm[:, ci], RW_P_HI) + hm[:, ci]
    s_ref[...] = s
    y_ref[0] = jnp.concatenate(ys, axis=1)


def _rw_scan_body(*refs, tr):
    fwd, bwd, (yf_ref, yb_ref, sf_ref, sb_ref) = refs[0:6], refs[6:12], refs[12:16]

    @pl.when(pl.program_id(1) == 0)
    def _():
        sf_ref[...] = jnp.zeros_like(sf_ref)
        sb_ref[...] = jnp.zeros_like(sb_ref)

    _rw_scan_tile(*fwd, yf_ref, sf_ref, tr=tr, rev=False)
    _rw_scan_tile(*bwd, yb_ref, sb_ref, tr=tr, rev=True)


def _rw_scan(r, v, cl, kd, bb, kx, tr, nl, nm):
    b, _, t, _ = r.shape
    nt = nl + nm
    tile_f = lambda j: jnp.where(j < nm, nl + j, j - nm)
    tile_b = lambda j: nt - 1 - j

    def specs(tile, d):
        hm = pl.BlockSpec((1, RW_HEADS, tr, RW_DH), lambda b, j: (b, 0, tile(j), 0))
        hmd = pl.BlockSpec((1, 1, RW_HEADS, tr, RW_DH), lambda b, j: (b, d, 0, tile(j), 0))
        return hm, hmd

    hf, hfd = specs(tile_f, 0)
    hb, hbd = specs(tile_b, 1)
    shp = _sds((b, RW_HEADS, t, RW_DH), F32)
    state = pltpu.VMEM((RW_HEADS, RW_DH, RW_DH), F32)
    return _call(
        functools.partial(_rw_scan_body, tr=tr), (b, nt),
        [hf, hf, hfd, hfd, hfd, hfd, hb, hb, hbd, hbd, hbd, hbd], [hf, hb], [shp, shp],
        scratch=[state, state], sem=("parallel", "arbitrary"),
        name="rw_scan")(r, v, cl, kd, bb, kx, r, v, cl, kd, bb, kx)


def _rw_out_body(yf_ref, yb_ref, r_ref, v_ref, kd_ref, g_ref, lnw_ref, lnb_ref, rk_ref, g2_ref, o_ref):
    y = yf_ref[0] + yb_ref[0]
    mu = jnp.mean(y, axis=-1, keepdims=True)
    var = jnp.mean(jnp.square(y - mu), axis=-1, keepdims=True)
    yn = (y - mu) * lax.rsqrt(var + RW_GN_EPS) * lnw_ref[...] + lnb_ref[...]
    ksum = kd_ref[0, 0] + kd_ref[0, 1]
    bonus = jnp.sum(r_ref[0] * ksum * rk_ref[...], axis=-1, keepdims=True) * v_ref[0]
    o = yn + bonus
    o = jnp.concatenate([o[h] for h in range(RW_HEADS)], axis=-1)
    gate = _dot3(jax.nn.sigmoid(g_ref[0]), g2_ref[...])
    o_ref[0] = (o * gate).astype(BF16)


def _rw_out(yf, yb, r, v, kd, g, p, tr, nrow):
    b = r.shape[0]
    hm = pl.BlockSpec((1, RW_HEADS, tr, RW_DH), lambda b, i: (b, 0, i, 0))
    consts = [p['rw_ln_w'], p['rw_ln_b'], p['rw_r_k'], p['rw_g2']]
    return _call(
        _rw_out_body, (b, nrow),
        [hm, hm, hm, hm, pl.BlockSpec((1, 2, RW_HEADS, tr, RW_DH), lambda b, i: (b, 0, 0, i, 0)),
         pl.BlockSpec((1, tr, 128), lambda b, i: (b, i, 0))] + [_full(a) for a in consts],
        pl.BlockSpec((1, tr, RW_W), lambda b, i: (b, i, 0)),
        _sds((b, nrow * tr, RW_W), BF16), name="rw_out")(yf, yb, r, v, kd, g, *consts)


def _merge_body(h_ref, y0_ref, y1_ref, y2_ref, y3_ref, g0_ref, g1_ref, g2_ref, g3_ref, wb_ref, o_ref):
    h = h_ref[0]
    acc = None
    for i, (y_ref, wg_ref) in enumerate(zip((y0_ref, y1_ref, y2_ref, y3_ref), (g0_ref, g1_ref, g2_ref, g3_ref))):
        t = jax.nn.sigmoid(_dot(h, wg_ref[...])) * _dot(y_ref[0], wb_ref[i])
        acc = t if acc is None else acc + t
    o_ref[0] = acc.astype(BF16)


def _merge(h, ys, wg, wbr, tr, nrow):
    b, _, d = h.shape
    tn = 512
    nn = d // tn
    row = lambda w: pl.BlockSpec((1, tr, w), lambda n, b, i: (b, i, 0))
    gspecs = [pl.BlockSpec((d, tn), functools.partial(lambda n, b, i, k: (0, k * nn + n), k=k))
              for k in range(N_BRANCH)]
    return _call(
        _merge_body, (nn, b, nrow),
        [row(d)] + [row(BRANCH_W)] * 4 + gspecs + [pl.BlockSpec((N_BRANCH, BRANCH_W, tn), lambda n, b, i: (0, 0, n))],
        pl.BlockSpec((1, tr, tn), lambda n, b, i: (b, i, n)),
        _sds((b, nrow * tr, d), BF16), name="merge")(h, *ys, wg, wg, wg, wg, wbr)


def _outproj_body(a_ref, w_ref, x_ref, md_ref, o_ref):
    o_ref[0] = x_ref[0] + md_ref[0, 0, 2:3, :] * _dot(a_ref[0], w_ref[...])


def _outproj(acc, w_out, xs, md, tr, nl, nrow):
    b, _, d = acc.shape
    row = pl.BlockSpec((1, tr, d), lambda b, i: (b, i, 0))
    return _call(
        _outproj_body, (b, nrow), [row, _full(w_out), row, _md_spec(d, nl)], row,
        _sds((b, nrow * tr, d), F32), name="out_proj")(acc, w_out, xs, md)


def _select_body(lg_ref, slot_ref, aff_ref, slotc_ref, *, n, cap):
    lg = lg_ref[0]
    lane = lax.broadcasted_iota(jnp.int32, lg.shape, 1)
    lg = jnp.where(lane < N_EXPERTS, lg, NEG_INF)
    ex = jnp.exp(lg - lg.max(axis=-1, keepdims=True))
    aff = ex / ex.sum(axis=-1, keepdims=True)
    aff_t = aff.T[:N_EXPERTS]
    bits = lax.bitcast_convert_type(aff_t, jnp.int32)
    thr = jnp.zeros((N_EXPERTS, 1), jnp.int32)
    for bit in range(30, -1, -1):
        cand = thr | jnp.int32(1 << bit)
        cnt = jnp.sum((bits >= cand).astype(F32), axis=1, keepdims=True)
        thr = jnp.where(cnt >= cap, cand, thr)
    gt = bits > thr
    eq = bits == thr
    need = cap - jnp.sum(gt.astype(F32), axis=1, keepdims=True)
    both = jnp.concatenate([gt.astype(BF16), eq.astype(BF16)], axis=0)
    cw = min(n, 512)
    pre = []
    for cb in range(n // cw):
        tp = lax.broadcasted_iota(jnp.int32, (n, cw), 0)
        tt = lax.broadcasted_iota(jnp.int32, (n, cw), 1) + cb * cw
        pre.append(_dot(both, (tp < tt).astype(BF16)))
    pre = jnp.concatenate(pre, axis=1) if len(pre) > 1 else pre[0]
    pre_gt, pre_eq = pre[:N_EXPERTS], pre[N_EXPERTS:]
    sel = gt | (eq & (pre_eq < need))
    slot = jnp.where(sel, pre_gt + jnp.minimum(pre_eq, need), -1.0)
    slot_ref[0] = slot.astype(jnp.int32)
    aff_ref[0] = aff_t
    pad = jnp.full((128 - N_EXPERTS, n), -1.0, F32)
    slotc_ref[0] = jnp.concatenate([slot, pad], axis=0).T.astype(jnp.int32)


def _select(logits, n, blk, cap, name):
    b = logits.shape[0]
    er = pl.BlockSpec((1, N_EXPERTS, n), lambda b: (b, 0, 0))
    return _call(
        functools.partial(_select_body, n=n, cap=cap), (b,),
        [pl.BlockSpec((1, n, 128), lambda b: (b, blk, 0))],
        [er, er, pl.BlockSpec((1, n, 128), lambda b: (b, 0, 0))],
        [_sds((b, N_EXPERTS, n), jnp.int32), _sds((b, N_EXPERTS, n), F32), _sds((b, n, 128), jnp.int32)],
        name=name)(logits)


def _ffn_body(h_ref, slot_ref, aff_ref, w1_ref, w3_ref, w2_ref, o_ref, *, grp, n, cap):
    xs, gates = [], []
    jrow = lax.broadcasted_iota(jnp.int32, (cap, n), 0)
    for gi in range(grp):
        pick = jrow == slot_ref[gi, 0]
        xs.append(_dot(pick.astype(BF16), h_ref[gi]).astype(BF16))
        gates.append(jnp.sum(jnp.where(pick, aff_ref[gi, 0], 0.0), axis=1, keepdims=True))
    xe = jnp.concatenate(xs, axis=0) if grp > 1 else xs[0]
    gate = jnp.concatenate(gates, axis=0) if grp > 1 else gates[0]
    a = _dot(xe, w1_ref[0])
    u = _dot(xe, w3_ref[0])
    hm = (a * jax.nn.sigmoid(a) * u).astype(BF16)
    ye = (_dot(hm, w2_ref[0]) * gate).astype(BF16)
    for gi in range(grp):
        o_ref[0, gi] = ye[gi * cap:(gi + 1) * cap]


def _moe_ffn(h2, slot, aff, w1, w3, w2, n, blk, cap, grp, name):
    b, _, d = h2.shape
    ff = w1.shape[-1]
    sr = pl.BlockSpec((grp, 1, 1, n), lambda e, g: (g, e, 0, 0))
    return _call(
        functools.partial(_ffn_body, grp=grp, n=n, cap=cap), (N_EXPERTS, b // grp),
        [pl.BlockSpec((grp, n, d), lambda e, g: (g, blk, 0)), sr, sr,
         pl.BlockSpec((1, d, ff), lambda e, g: (e, 0, 0)), pl.BlockSpec((1, d, ff), lambda e, g: (e, 0, 0)),
         pl.BlockSpec((1, ff, d), lambda e, g: (e, 0, 0))],
        pl.BlockSpec((1, grp, cap, d), lambda e, g: (e, g, 0, 0)),
        _sds((N_EXPERTS, b, cap, d), BF16), name=name)(
            h2, slot.reshape(b, N_EXPERTS, 1, n), aff.reshape(b, N_EXPERTS, 1, n), w1, w3, w2)


def _combine_body(sc_ref, ye_ref, x_ref, md_ref, o_ref, *, cap):
    sc = sc_ref[0]
    jj = lax.broadcasted_iota(jnp.int32, (sc.shape[0], cap), 1)
    acc = None
    for e in range(N_EXPERTS):
        put = (sc[:, e:e + 1] == jj).astype(BF16)
        t = _dot(put, ye_ref[e, 0])
        acc = t if acc is None else acc + t
    o_ref[0] = x_ref[0] + md_ref[0, 0, 5:6, :] * acc


def _moe_combine(slotc, ye, x1, md, n, blk_rows, is_ctx, cap, name):
    b, _, d = x1.shape
    tm = _pick_tile(n, (256, 128, 64, 32, 16, 8))
    off = blk_rows // tm
    return _call(
        functools.partial(_combine_body, cap=cap), (b, n // tm),
        [pl.BlockSpec((1, tm, 128), lambda b, i: (b, i, 0)),
         pl.BlockSpec((N_EXPERTS, 1, cap, d), lambda b, i: (0, b, 0, 0)),
         pl.BlockSpec((1, tm, d), lambda b, i: (b, off + i, 0)),
         pl.BlockSpec((1, 1, 6, d), lambda b, i: (b, 1 if is_ctx else 0, 0, 0))],
        pl.BlockSpec((1, tm, d), lambda b, i: (b, i, 0)),
        _sds((b, n, d), F32), name=name)(slotc, ye, x1, md)


def _moe(x1, h2, logits, md, w1, w3, w2, l_tok, m_tok, ctx_out):
    b = x1.shape[0]
    cap_l = CAPACITY_FACTOR * l_tok // N_EXPERTS
    slot, aff, slotc = _select(logits, l_tok, 0, cap_l, "moe_select_latent")
    ye = _moe_ffn(h2, slot, aff, w1, w3, w2, l_tok, 0, cap_l, 1, "moe_ffn_latent")
    x2 = _moe_combine(slotc, ye, x1, md, l_tok, 0, False, cap_l, "moe_combine_latent")
    if not ctx_out:
        return x2, None
    cap_c = CAPACITY_FACTOR * m_tok // N_EXPERTS
    blk = l_tok // m_tok
    slot, aff, slotc = _select(logits, m_tok, blk, cap_c, "moe_select_ctx")
    ye = _moe_ffn(h2, slot, aff, w1, w3, w2, m_tok, blk, cap_c, b, "moe_ffn_ctx")
    xc2 = _moe_combine(slotc, ye, x1, md, m_tok, l_tok, True, cap_c, "moe_combine_ctx")
    return x2, xc2


def _layer_params(l, w_in, na_q_norm, na_k_norm, na_rpb, mla_cq_norm, mla_ckv_norm, mla_w_uq, mla_w_ukv,
                  mla_q_norm, mla_k_norm, rw_mu_ks, rw_mu_qs, rw_w0, rw_w2, rw_a0, rw_a2, rw_g2, rw_k_k, rw_k_a,
                  rw_r_k, rw_ln_w, rw_ln_b, w_br, w_out, rows):
    w = w_in[l].astype(BF16)
    d = w.shape[0]
    kq = KEY_COLS
    kpe = w[:, 1280:1344]
    zpad = lambda n: jnp.zeros((d, n), w.dtype)
    wz = jnp.concatenate([
        w[:, kq:kq + 512], w[:, 0:512], w[:, 512:1024],
        w[:, kq + 512:kq + 1024],
        w[:, 1344:1856], w[:, 1856:2368], w[:, kq + 1024:kq + 1536],
        w[:, kq + 1632:kq + 2144],
        w[:, 1024:1280],
        kpe, _swap_halves(kpe),
        w[:, 2368:2496],
        w[:, kq + 1536:kq + 1632], zpad(32),
        zpad(NZ - 4736)], axis=1)
    wg = w[:, kq + 2144:]

    def head_gain(g):
        return jnp.concatenate([g[:128], g[128:], _swap_halves(g[128:])])[None, :]

    wq = mla_w_uq[l].reshape(MLA_QLORA, MLA_HEADS, MLA_NOPE + MLA_ROPE)
    wq = jnp.concatenate([wq, _swap_halves(wq[:, :, MLA_NOPE:])], axis=-1).reshape(MLA_QLORA, MLA_HEADS * 256)
    w2cat = jnp.zeros((128, 1024), F32).at[0:32, 0:512].set(rw_w2[l, 0]).at[32:64, 512:].set(rw_w2[l, 1])
    a2cat = jnp.zeros((128, 1024), F32).at[64:96, 0:512].set(rw_a2[l, 0]).at[96:128, 512:].set(rw_a2[l, 1])
    mu = jnp.concatenate([rw_mu_ks[l][:1024], rw_mu_qs[l][:512], rw_mu_ks[l][1024:], rw_mu_qs[l][512:],
                          jnp.zeros((32,), F32)])[None, :]
    hd = lambda a: a.reshape(RW_HEADS, 1, RW_DH)
    return {
        'wz': wz, 'wg': wg,
        'na_gq': na_q_norm[l][None, :], 'na_gk': na_k_norm[l][None, :],
        'na_bias': _na_bias_table(na_rpb[l], rows),
        'g_cq': mla_cq_norm[l][None, :], 'g_ckv': mla_ckv_norm[l][None, :],
        'w_uq': wq.astype(BF16), 'w_ukv': mla_w_ukv[l].astype(BF16),
        'g_q': head_gain(mla_q_norm[l]), 'g_k': head_gain(mla_k_norm[l]),
        'rw_mu': mu, 'rw_k_k': rw_k_k[l][None, :], 'rw_k_a': rw_k_a[l][None, :],
        'rw_w0': rw_w0[l].reshape(1, 1024), 'rw_a0': rw_a0[l].reshape(1, 1024), 'rw_w2': w2cat, 'rw_a2': a2cat,
        'rw_ln_w': hd(rw_ln_w[l]), 'rw_ln_b': hd(rw_ln_b[l]), 'rw_r_k': hd(rw_r_k[l]),
        'rw_g2': jnp.concatenate([rw_g2[l], jnp.zeros((32, RW_W), F32)], axis=0),
        'w_br': w_br[l].astype(BF16), 'w_out': w_out[l].astype(BF16),
    }


def _rw_halo(z3, tr, nl):
    b, t, _ = z3.shape
    nt = t // tr
    z4 = z3.reshape(b, nt, tr, z3.shape[-1])
    pick = lambda a: jnp.concatenate([a[..., ZC_RWK:ZC_RWK + 1536], a[..., ZC_L4:ZC_L4 + 256]], axis=-1)
    first, last = pick(z4[:, :, 0, :]), pick(z4[:, :, tr - 1, :])
    zero = jnp.zeros_like(first[:, :1])
    prev = jnp.concatenate([zero, last[:, :-1]], axis=1)
    nxt = jnp.concatenate([first[:, 1:], zero], axis=1)
    tile = jnp.arange(nt)[None, :, None]
    prev = jnp.where(tile == nl, 0.0, prev)
    nxt = jnp.where(tile == nl - 1, 0.0, nxt)
    return jnp.stack([prev, nxt], axis=2)


def _layer(xs, md, p, g1n, g2n, wr, w1, w3, w2, tabs, l_tok, m_tok, tr, ctx_out):
    b, t, d = xs.shape
    nl, nm = l_tok // tr, m_tok // tr
    nrow = (nl + nm) if ctx_out else nl
    blk = l_tok // m_tok

    h = _norm1(xs, g1n, md, tr, nl)
    z3 = _matmul(h.reshape(b * t, d), p['wz'], F32).reshape(b, t, NZ)

    nq, nk, nv = _na_prep(z3, p['na_gq'], p['na_gk'], tr)
    y_na = _na_latent(nq, nk, nv, p['na_bias'], l_tok, m_tok)
    mq, mk, mv = _mla_prep(z3, p, tabs['rope_c'], tabs['rope_s'], tr)
    tq = _pick_tile(l_tok, (256, 128, 64))
    y_mla = _attention([mq], [mk], mv, tq=tq, q0=0, nq=l_tok // tq, tk=t, kblk=0, name="mla_latent")
    rv, rr, rg, cl, kd, bb, kx = _rw_prep(z3, _rw_halo(z3, tr, nl), p, tr)
    yf, yb = _rw_scan(rr, rv, cl, kd, bb, kx, tr, nl, nm)
    y_rw = _rw_out(yf, yb, rr, rv, kd, rg, p, tr, nrow)
    y_fn, yc_fn = _fourier(z3, tabs, tr, l_tok, m_tok, ctx_out)

    if ctx_out:
        yc_na = _attention([nq], [nk], nv, tq=m_tok, q0=blk, nq=1, tk=m_tok, kblk=blk, name="na_ctx")
        yc_mla = _attention([mq], [mk], mv, tq=m_tok, q0=blk, nq=1, tk=m_tok, kblk=blk, name="mla_ctx")
        y_fn = jnp.concatenate([y_fn, yc_fn], axis=1)
        y_na = jnp.concatenate([y_na, yc_na], axis=1)
        y_mla = jnp.concatenate([y_mla, yc_mla], axis=1)

    acc = _merge(h, (y_fn, y_na, y_mla, y_rw), p['wg'], p['w_br'], tr, nrow)
    x1 = _outproj(acc, p['w_out'], xs, md, tr, nl, nrow)
    h2, logits = _norm2_router(x1, g2n, md, wr, tr, nl)
    return _moe(x1, h2, logits, md, w1, w3, w2, l_tok, m_tok, ctx_out)


def kernel(x, c, ctx, c_ctx, ada_w, ada_b, norm1_g, norm2_g, w_in, na_q_norm, na_k_norm, na_rpb, mla_cq_norm, mla_ckv_norm, mla_w_uq, mla_w_ukv, mla_q_norm, mla_k_norm, rw_mu_ks, rw_mu_qs, rw_w0, rw_w2, rw_a0, rw_a2, rw_g2, rw_k_k, rw_k_a, rw_r_k, rw_ln_w, rw_ln_b, w_br, w_out, moe_router, moe_w1, moe_w3, moe_w2):
    b, l_tok, d = x.shape
    m_tok = ctx.shape[1]
    depth = ada_w.shape[0]
    tr = min(m_tok, 256)
    assert l_tok % tr == 0 and m_tok % tr == 0 and l_tok % m_tok == 0 and tr % RW_CHUNK == 0
    assert l_tok % GRID_W == 0 and m_tok % 128 == 0

    nr = -(-(b + 1) // 8) * 8
    cc = jnp.concatenate([c, c_ctx[None, :], jnp.zeros((nr - b - 1, d), F32)], axis=0)
    mod = _modulation(cc, ada_w, ada_b)
    tabs = {}
    tabs['rope_c'], tabs['rope_s'] = _rope_tables(l_tok, m_tok)

    xs = jnp.concatenate([x, ctx], axis=1)
    for l in range(depth):
        ctx_out = l < depth - 1
        tabs.update(_fourier_tables(l_tok, m_tok, ctx_out))
        p = _layer_params(l, w_in, na_q_norm, na_k_norm, na_rpb, mla_cq_norm, mla_ckv_norm, mla_w_uq, mla_w_ukv,
                          mla_q_norm, mla_k_norm, rw_mu_ks, rw_mu_qs, rw_w0, rw_w2, rw_a0, rw_a2, rw_g2, rw_k_k,
                          rw_k_a, rw_r_k, rw_ln_w, rw_ln_b, w_br, w_out, l_tok // GRID_W)
        md = jnp.stack([mod[l, :b], jnp.broadcast_to(mod[l, b], (b, 6 * d))], axis=1).reshape(b, 2, 6, d)
        wr = jnp.concatenate([moe_router[l], jnp.zeros((d, 128 - N_EXPERTS), F32)], axis=1)
        x2, xc2 = _layer(xs, md, p, norm1_g[l][None, :], norm2_g[l][None, :], wr,
                         moe_w1[l].astype(BF16), moe_w3[l].astype(BF16), moe_w2[l].astype(BF16),
                         tabs, l_tok, m_tok, tr, ctx_out)
        xs = jnp.concatenate([x2, xc2], axis=1) if ctx_out else x2
    return xs
```

```python
import functools
import math

import jax
import jax.numpy as jnp
from jax import lax
from jax.experimental import pallas as pl
from jax.experimental.pallas import tpu as pltpu

F32 = jnp.float32
BF16 = jnp.bfloat16

GRID_W = 64
N_BRANCH = 4
BRANCH_W = 512
NA_HEADS, NA_DH, NA_KH, NA_KW = 8, 64, 8, 16
NA_SCALE = NA_DH ** -0.5
MLA_HEADS, MLA_NOPE, MLA_ROPE, MLA_V = 4, 128, 64, 128
MLA_QLORA, MLA_KVLORA = 512, 256
MLA_SCALE = (MLA_NOPE + MLA_ROPE) ** -0.5
RW_HEADS, RW_DH, RW_W = 8, 64, 512
RW_LORA, RW_GATE_LORA = 32, 96
RW_GN_EPS = 64e-5
N_EXPERTS = 16
CAPACITY_FACTOR = 2
ROPE_THETA = 10000.0
NORM_EPS = 1e-6
NEG_INF = -1e30
RW_CHUNK = 64
KEY_COLS = 2496
NZ = 4736
ZC_NA = 0
ZC_CQ = 1536
ZC_RWK = 2048
ZC_FN = 3584
ZC_CKV = 4096
ZC_KPE = 4352
ZC_L4 = 4480
ZC_G = 4608

VMEM_LIMIT = 56 * 2 ** 20


def _call(body, grid, in_specs, out_specs, out_shape, scratch=(), sem=None, name=None, aliases=None):
    return pl.pallas_call(
        body, grid=grid, in_specs=in_specs, out_specs=out_specs, out_shape=out_shape,
        scratch_shapes=list(scratch), name=name, input_output_aliases=aliases or {},
        compiler_params=pltpu.CompilerParams(
            dimension_semantics=sem or ("parallel",) * len(grid), vmem_limit_bytes=VMEM_LIMIT))


def _full(a):
    nd = a.ndim
    return pl.BlockSpec(a.shape, lambda *_: (0,) * nd)


def _sds(shape, dtype):
    return jax.ShapeDtypeStruct(shape, dtype)


def _dot(a, b):
    return jnp.dot(a, b, preferred_element_type=F32)


def _dot_nt(a, b):
    return lax.dot_general(a, b, (((1,), (1,)), ((), ())), preferred_element_type=F32)


def _split2(x):
    hi = x.astype(BF16)
    return hi, (x - hi.astype(F32)).astype(BF16)


def _split3(x):
    hi = x.astype(BF16)
    r = x - hi.astype(F32)
    mid = r.astype(BF16)
    return hi, mid, (r - mid.astype(F32)).astype(BF16)


def _dot3(a, b):
    ah, al = _split2(a)
    bh, bl = _split2(b)
    return _dot(ah, bh) + (_dot(ah, bl) + _dot(al, bh))


def _dot_sel(sel_bf16, x):
    h, m, l = _split3(x)
    return _dot(sel_bf16, h) + (_dot(sel_bf16, m) + _dot(sel_bf16, l))


def _bmm(spec, a, b, passes):
    e = functools.partial(jnp.einsum, spec, preferred_element_type=F32)
    if passes == 1:
        return e(a.astype(BF16), b.astype(BF16))
    ah, al = _split2(a)
    bh, bl = _split2(b)
    return e(ah, bh) + (e(ah, bl) + e(al, bh))


def _pick_tile(n, cands):
    for c in cands:
        if n % c == 0:
            return c
    raise ValueError(f"no tile for {n}")


def _mod_body(c_ref, w_ref, b_ref, o_ref):
    cc = c_ref[...]
    sh, sl = _split2(cc * jax.nn.sigmoid(cc))
    wb = w_ref[0].astype(BF16)
    o_ref[0] = (_dot(sh, wb) + _dot(sl, wb)) + b_ref[0]


def _modulation(cc, ada_w, ada_b):
    depth, d, n6 = ada_w.shape
    r = cc.shape[0]
    tn = 1024
    return _call(
        _mod_body, (depth, n6 // tn),
        [pl.BlockSpec((r, d), lambda l, n: (0, 0)),
         pl.BlockSpec((1, d, tn), lambda l, n: (l, 0, n)),
         pl.BlockSpec((1, 1, tn), lambda l, n: (l, 0, n))],
        pl.BlockSpec((1, r, tn), lambda l, n: (l, 0, n)),
        _sds((depth, r, n6), F32), name="adaln_mod")(cc, ada_w, ada_b.reshape(depth, 1, n6))


def _modnorm(x, g, shift, scale):
    y = x * lax.rsqrt(jnp.mean(x * x, axis=-1, keepdims=True) + NORM_EPS)
    return (y * g) * (1.0 + scale) + shift


def _norm1_body(x_ref, g_ref, md_ref, h_ref):
    h = _modnorm(x_ref[0], g_ref[...], md_ref[0, 0, 0:1, :], md_ref[0, 0, 1:2, :])
    h_ref[0] = h.astype(BF16)


def _norm2_body(x_ref, g_ref, md_ref, wr_ref, h_ref, lg_ref):
    h = _modnorm(x_ref[0], g_ref[...], md_ref[0, 0, 3:4, :], md_ref[0, 0, 4:5, :])
    h_ref[0] = h.astype(BF16)
    lg_ref[0] = _dot3(h, wr_ref[...])


def _md_spec(d, nl):
    return pl.BlockSpec((1, 1, 6, d), lambda b, i: (b, i // nl, 0, 0))


def _norm1(xs, g, md, tr, nl):
    b, t, d = xs.shape
    return _call(
        _norm1_body, (b, t // tr),
        [pl.BlockSpec((1, tr, d), lambda b, i: (b, i, 0)), _full(g), _md_spec(d, nl)],
        pl.BlockSpec((1, tr, d), lambda b, i: (b, i, 0)),
        _sds((b, t, d), BF16), name="norm1")(xs, g, md)


def _norm2_router(x1, g, md, wr, tr, nl):
    b, t, d = x1.shape
    return _call(
        _norm2_body, (b, t // tr),
        [pl.BlockSpec((1, tr, d), lambda b, i: (b, i, 0)), _full(g), _md_spec(d, nl), _full(wr)],
        [pl.BlockSpec((1, tr, d), lambda b, i: (b, i, 0)), pl.BlockSpec((1, tr, 128), lambda b, i: (b, i, 0))],
        [_sds((b, t, d), BF16), _sds((b, t, 128), F32)], name="norm2_router")(x1, g, md, wr)


def _mm_body(a_ref, w_ref, o_ref):
    o_ref[...] = _dot(a_ref[...], w_ref[...]).astype(o_ref.dtype)


def _matmul(a, w, out_dtype):
    r, k = a.shape
    n = w.shape[1]
    tm = _pick_tile(r, (512, 384, 256, 128, 64, 32, 16, 8))
    return _call(
        _mm_body, (r // tm,),
        [pl.BlockSpec((tm, k), lambda i: (i, 0)),
         pl.BlockSpec((k, n), lambda i: (0, 0), pipeline_mode=pl.Buffered(1))],
        pl.BlockSpec((tm, n), lambda i: (i, 0)),
        _sds((r, n), out_dtype), name="in_proj")(a, w)


def _head_sum(x, ones_bd):
    hi, lo = _split2(x)
    return _dot(hi, ones_bd) + _dot(lo, ones_bd)


def _head_sumsq(x, ones_bd):
    return _head_sum(x * x, ones_bd)


def _head_ones(width, dh):
    i = jnp.arange(width) // dh
    return (i[:, None] == i[None, :]).astype(BF16)


def _na_prep_body(z_ref, gq_ref, gk_ref, e_ref, q_ref, k_ref, v_ref):
    z = z_ref[0]
    q, k, v = z[:, :512], z[:, 512:1024], z[:, 1024:]
    e = e_ref[...]
    qn = (q * lax.rsqrt(_head_sumsq(q, e) / NA_DH + NORM_EPS) * (gq_ref[...] * NA_SCALE)).astype(BF16)
    kn = (k * lax.rsqrt(_head_sumsq(k, e) / NA_DH + NORM_EPS) * gk_ref[...]).astype(BF16)
    vb = v.astype(BF16)
    for h in range(NA_HEADS):
        lo = h * NA_DH
        q_ref[0, h] = qn[:, lo:lo + NA_DH]
        k_ref[0, h] = kn[:, lo:lo + NA_DH]
        v_ref[0, h] = vb[:, lo:lo + NA_DH]


def _na_prep(z3, gq, gk, tr):
    b, t, _ = z3.shape
    hm = pl.BlockSpec((1, NA_HEADS, tr, NA_DH), lambda b, i: (b, 0, i, 0))
    shp = _sds((b, NA_HEADS, t, NA_DH), BF16)
    e = _head_ones(NA_HEADS * NA_DH, NA_DH)
    gq, gk = jnp.tile(gq, (1, NA_HEADS)), jnp.tile(gk, (1, NA_HEADS))
    return _call(
        _na_prep_body, (b, t // tr),
        [pl.BlockSpec((1, tr, 1536), lambda b, i: (b, i, 0)), _full(gq), _full(gk), _full(e)],
        [hm, hm, hm], [shp, shp, shp], name="na_prep")(z3, gq, gk, e)


def _softmax_pv(s, vs):
    m = s[0].max(axis=-1, keepdims=True)
    for t in s[1:]:
        m = jnp.maximum(m, t.max(axis=-1, keepdims=True))
    den = None
    acc = None
    for t, v in zip(s, vs):
        p = jnp.exp(t - m)
        ps = p.sum(axis=-1, keepdims=True)
        o = jnp.einsum('hqk,hkd->hqd', p.astype(BF16), v, preferred_element_type=F32)
        den = ps if den is None else den + ps
        acc = o if acc is None else acc + o
    return acc / den


def _qk(q, k):
    return jnp.einsum('hqd,hkd->hqk', q, k, preferred_element_type=F32)


def _heads_to_lanes(o):
    return jnp.concatenate([o[h] for h in range(o.shape[0])], axis=-1)


NA_ROWS_PER_STEP = 4


def _na_lat_body(q_ref, k_ref, v_ref, bias_ref, o_ref, *, rows, kh, l_tok, m_tok, rps):
    for rr in range(rps):
        r = pl.program_id(1) * rps + rr
        rs = jnp.clip(r - kh // 2, 0, rows - kh)
        delta = r - rs
        start = pl.multiple_of(rs * GRID_W, GRID_W)
        q = q_ref[0, :, rr * GRID_W:(rr + 1) * GRID_W, :]
        kb = k_ref[0, :, pl.ds(start, kh * GRID_W), :]
        vb = v_ref[0, :, pl.ds(start, kh * GRID_W), :]
        kc = k_ref[0, :, l_tok:l_tok + m_tok, :]
        vc = v_ref[0, :, l_tok:l_tok + m_tok, :]
        o = _softmax_pv([_qk(q, kb) + bias_ref[delta], _qk(q, kc)], [vb, vc])
        o_ref[0, rr * GRID_W:(rr + 1) * GRID_W, :] = _heads_to_lanes(o).astype(BF16)


def _na_latent(q, k, v, bias, l_tok, m_tok):
    b, _, t, _ = q.shape
    rows = l_tok // GRID_W
    kh = min(NA_KH, rows)
    rps = math.gcd(rows, NA_ROWS_PER_STEP)
    kv = pl.BlockSpec((1, NA_HEADS, t, NA_DH), lambda b, r: (b, 0, 0, 0))
    return _call(
        functools.partial(_na_lat_body, rows=rows, kh=kh, l_tok=l_tok, m_tok=m_tok, rps=rps), (b, rows // rps),
        [pl.BlockSpec((1, NA_HEADS, rps * GRID_W, NA_DH), lambda b, r: (b, 0, r, 0)), kv, kv, _full(bias)],
        pl.BlockSpec((1, rps * GRID_W, BRANCH_W), lambda b, r: (b, r, 0)),
        _sds((b, l_tok, BRANCH_W), BF16), name="na_latent")(q, k, v, bias)


def _toeplitz_body(r_ref, oh_ref, valid_ref, o_ref):
    h, m, l = _split3(r_ref[...])
    oh = oh_ref[...]
    t = _dot(h, oh) + (_dot(m, oh) + _dot(l, oh))
    o_ref[...] = jnp.where(valid_ref[...] > 0.0, t, NEG_INF)


def _na_bias_table(rpb, rows):
    kh = min(NA_KH, rows)
    nh, ndr, ndc = rpb.shape
    col = jnp.arange(GRID_W)
    cs = jnp.clip(col - NA_KW // 2, 0, GRID_W - NA_KW)
    valid = (col[None, :] >= cs[:, None]) & (col[None, :] < cs[:, None] + NA_KW)
    dc = jnp.clip(col[None, :] - col[:, None] + NA_KW - 1, 0, 2 * NA_KW - 2)
    onehot = (jnp.arange(128)[:, None] == dc.reshape(1, -1)).astype(BF16)
    rp = jnp.pad(rpb.astype(F32).reshape(nh * ndr, ndc), ((0, 0), (0, 128 - ndc)))
    vmask = valid.reshape(1, -1).astype(F32)
    toep = pl.pallas_call(_toeplitz_body, out_shape=_sds((nh * ndr, GRID_W * GRID_W), F32),
                          name="na_bias")(rp, onehot, vmask)
    toep = toep.reshape(nh, ndr, GRID_W, GRID_W)
    per_delta = []
    for delta in range(kh):
        lo = NA_KH - 1 - delta
        t = toep[:, lo:lo + kh].transpose(0, 2, 1, 3)
        per_delta.append(t.reshape(nh, GRID_W, kh * GRID_W))
    return jnp.stack(per_delta, axis=0)


def _attn_body(*refs, nparts, heads):
    q_refs, k_refs, v_ref, o_ref = refs[:nparts], refs[nparts:2 * nparts], refs[2 * nparts], refs[-1]
    s = None
    for qr, kr in zip(q_refs, k_refs):
        t = _qk(qr[0], kr[0])
        s = t if s is None else s + t
    o_ref[0] = _heads_to_lanes(_softmax_pv([s], [v_ref[0]])).astype(BF16)


def _attention(qs, ks, v, *, tq, q0, nq, tk, kblk, name):
    b, heads, _, dv = v.shape
    qspec = [pl.BlockSpec((1, heads, tq, q.shape[-1]), lambda b, i: (b, 0, q0 + i, 0)) for q in qs]
    kspec = [pl.BlockSpec((1, heads, tk, k.shape[-1]), lambda b, i: (b, 0, kblk, 0)) for k in ks]
    vspec = pl.BlockSpec((1, heads, tk, dv), lambda b, i: (b, 0, kblk, 0))
    return _call(
        functools.partial(_attn_body, nparts=len(qs), heads=heads), (b, nq),
        qspec + kspec + [vspec],
        pl.BlockSpec((1, tq, heads * dv), lambda b, i: (b, i, 0)),
        _sds((b, nq * tq, heads * dv), BF16), name=name)(*qs, *ks, v)


def _mla_q_body(z_ref, gc_ref, w_ref, gh_ref, ct_ref, st_ref, q_ref):
    cq = z_ref[0]
    cqn = (cq * lax.rsqrt(jnp.mean(cq * cq, axis=-1, keepdims=True) + NORM_EPS) * gc_ref[...]).astype(BF16)
    q = _dot(cqn, w_ref[...])
    gh = gh_ref[...]
    ct, st = ct_ref[...], st_ref[...]
    for h in range(MLA_HEADS):
        qh = q[:, h * 256:(h + 1) * 256]
        nope, pe, sw = qh[:, :128], qh[:, 128:192], qh[:, 192:256]
        ms = (jnp.sum(nope * nope, axis=-1, keepdims=True) + jnp.sum(pe * pe, axis=-1, keepdims=True)) \
            / (MLA_NOPE + MLA_ROPE)
        rinv = lax.rsqrt(ms + NORM_EPS)
        rot = (pe * rinv * gh[:, 128:192]) * ct + (sw * rinv * gh[:, 192:256]) * st
        qh = jnp.concatenate([nope * rinv * gh[:, :128], rot, jnp.zeros_like(rot)], axis=-1)
        q_ref[0, h] = (qh * MLA_SCALE).astype(BF16)


def _mla_kv_body(zc_ref, zp_ref, gc_ref, w_ref, gh_ref, ct_ref, st_ref, k_ref, v_ref):
    ckv = zc_ref[0]
    cn = (ckv * lax.rsqrt(jnp.mean(ckv * ckv, axis=-1, keepdims=True) + NORM_EPS) * gc_ref[...]).astype(BF16)
    kv = _dot(cn, w_ref[...])
    zp = zp_ref[0]
    pe, sw = zp[:, :64], zp[:, 64:128]
    pe2 = jnp.sum(pe * pe, axis=-1, keepdims=True)
    gh = gh_ref[...]
    ct, st = ct_ref[...], st_ref[...]
    for h in range(MLA_HEADS):
        nope = kv[:, h * 256:h * 256 + 128]
        ms = (jnp.sum(nope * nope, axis=-1, keepdims=True) + pe2) / (MLA_NOPE + MLA_ROPE)
        rinv = lax.rsqrt(ms + NORM_EPS)
        rot = (pe * rinv * gh[:, 128:192]) * ct + (sw * rinv * gh[:, 192:256]) * st
        k_ref[0, h] = jnp.concatenate([nope * rinv * gh[:, :128], rot, jnp.zeros_like(rot)], axis=-1).astype(BF16)
        v_ref[0, h] = kv[:, h * 256 + 128:(h + 1) * 256].astype(BF16)


def _mla_prep(z3, p, ct, st, tr):
    b, t, _ = z3.shape
    rope = pl.BlockSpec((tr, MLA_ROPE), lambda b, i: (i, 0))

    def hm(d):
        return pl.BlockSpec((1, MLA_HEADS, tr, d), lambda b, i: (b, 0, i, 0))

    def shp(d):
        return _sds((b, MLA_HEADS, t, d), BF16)

    q = _call(
        _mla_q_body, (b, t // tr),
        [pl.BlockSpec((1, tr, 512), lambda b, i: (b, i, ZC_CQ // 512)), _full(p['g_cq']), _full(p['w_uq']),
         _full(p['g_q']), rope, rope],
        hm(256), shp(256), name="mla_q_prep")(z3, p['g_cq'], p['w_uq'], p['g_q'], ct, st)
    k, v = _call(
        _mla_kv_body, (b, t // tr),
        [pl.BlockSpec((1, tr, 256), lambda b, i: (b, i, ZC_CKV // 256)),
         pl.BlockSpec((1, tr, 128), lambda b, i: (b, i, ZC_KPE // 128)),
         _full(p['g_ckv']), _full(p['w_ukv']), _full(p['g_k']), rope, rope],
        [hm(256), hm(128)], [shp(256), shp(128)], name="mla_kv_prep")(
            z3, z3, p['g_ckv'], p['w_ukv'], p['g_k'], ct, st)
    return q, k, v


def _rope_tables(l_tok, m_tok):
    half = MLA_ROPE // 4
    freqs = ROPE_THETA ** (-jnp.arange(half, dtype=F32) / half)
    pos = jnp.arange(l_tok)
    ar = (pos // GRID_W).astype(F32)[:, None] * freqs[None, :]
    ac = (pos % GRID_W).astype(F32)[:, None] * freqs[None, :]
    ct = jnp.concatenate([jnp.cos(ar), jnp.cos(ar), jnp.cos(ac), jnp.cos(ac)], axis=-1)
    st = jnp.concatenate([-jnp.sin(ar), jnp.sin(ar), -jnp.sin(ac), jnp.sin(ac)], axis=-1)
    ct = jnp.concatenate([ct, jnp.ones((m_tok, MLA_ROPE), F32)], axis=0)
    st = jnp.concatenate([st, jnp.zeros((m_tok, MLA_ROPE), F32)], axis=0)
    return ct, st


def _swap_halves(a):
    return jnp.concatenate([a[..., 16:32], a[..., 0:16], a[..., 48:64], a[..., 32:48]], axis=-1)


def _dft1_body(x_ref, bh_ref, bl_ref, o_ref):
    xh, xl = _split2(x_ref[0])
    r = _dot(xh, bh_ref[...]) + (_dot(xh, bl_ref[...]) + _dot(xl, bh_ref[...]))
    o_ref[0] = r.astype(BF16)


def _dft2_body(c_ref, s_ref, xc_ref, xs_ref, o_ref):
    o_ref[0] = (_dot(c_ref[...], xc_ref[0]) - _dot(s_ref[...], xs_ref[0])).astype(BF16)


def _dft_mats(n):
    k = jnp.arange(n, dtype=jnp.int32)
    ang = ((k[:, None] * k[None, :]) % n).astype(F32) * (2.0 * math.pi / n)
    s = 1.0 / math.sqrt(n)
    return jnp.cos(ang) * s, jnp.sin(ang) * s


def _fourier_tables(l_tok, m_tok, ctx_out):
    cw, sw = _dft_mats(BRANCH_W // 4)
    eye = jnp.eye(4, dtype=F32)
    bd = jnp.concatenate([jnp.kron(eye, cw), jnp.kron(eye, sw)], axis=1)
    tabs = {'bd': _split2(bd), 'lat': tuple(m.astype(BF16) for m in _dft_mats(l_tok))}
    if ctx_out:
        tabs['ctx'] = tuple(m.astype(BF16) for m in _dft_mats(m_tok))
    return tabs


def _fourier(z3, tabs, tr, l_tok, m_tok, ctx_out):
    b, t, _ = z3.shape
    bh, bl = tabs['bd']
    row = pl.BlockSpec((1, tr, 1024), lambda b, i: (b, i, 0))
    nrow = (t if ctx_out else l_tok) // tr
    xw = _call(
        _dft1_body, (b, nrow),
        [pl.BlockSpec((1, tr, 512), lambda b, i: (b, i, ZC_FN // 512)), _full(bh), _full(bl)],
        row, _sds((b, nrow * tr, 1024), BF16), name="dft_channels")(z3, bh, bl)

    def seq_dft(n, blk, mats, name):
        tm = _pick_tile(n, (512, 256, 128, 64, 32, 16, 8))
        mspec = pl.BlockSpec((tm, n), lambda b, i: (i, 0))
        xc = pl.BlockSpec((1, n, 512), lambda b, i: (b, blk, 0))
        xs = pl.BlockSpec((1, n, 512), lambda b, i: (b, blk, 1))
        return _call(
            _dft2_body, (b, n // tm), [mspec, mspec, xc, xs],
            pl.BlockSpec((1, tm, 512), lambda b, i: (b, i, 0)),
            _sds((b, n, 512), BF16), name=name)(*mats, xw, xw)

    y = seq_dft(l_tok, 0, tabs['lat'], "dft_seq_latent")
    yc = seq_dft(m_tok, l_tok // m_tok, tabs['ctx'], "dft_seq_ctx") if ctx_out else None
    return y, yc


def _rw_prep_body(zk_ref, zv_ref, zr_ref, zl_ref, zg_ref, halo_ref, mu_ref, kk_ref, ka_ref, w0_ref, a0_ref,
                  w2_ref, a2_ref, trf_ref, trb_ref, e_ref,
                  v_ref, r_ref, g_ref, cl_ref, kd_ref, b_ref, kx_ref, *, tr):
    row = lax.broadcasted_iota(jnp.int32, (tr, 1), 0)
    halo = halo_ref[0, 0]
    mu = mu_ref[...]

    def mix(x, lo, hi):
        xp = jnp.where(row == 0, halo[0:1, lo:hi], pltpu.roll(x, 1, 0))
        xn = jnp.where(row == tr - 1, halo[1:2, lo:hi], pltpu.roll(x, tr - 1, 0))
        return x + (0.5 * (xp + xn) - x) * mu[:, lo:hi]

    k = mix(zk_ref[0], 0, 512)
    v = mix(zv_ref[0], 512, 1024)
    r = mix(zr_ref[0], 1024, 1536)
    l4 = mix(zl_ref[0], 1536, 1664)
    g_ref[0] = mix(zg_ref[0], 1664, 1792)

    wcat = w0_ref[...] + _dot3(jnp.tanh(l4), w2_ref[...])
    acat = a0_ref[...] + _dot3(l4, a2_ref[...])
    kk = k * kk_ref[...]
    kkn = kk / jnp.maximum(jnp.sqrt(_head_sumsq(kk, e_ref[...])), 1e-12)
    for d, tri_ref in enumerate((trf_ref, trb_ref)):
        w = wcat[:, d * 512:(d + 1) * 512]
        lw = -math.exp(-0.5) * jax.nn.sigmoid(w)
        a = jax.nn.sigmoid(acat[:, d * 512:(d + 1) * 512])
        kd = k * (1.0 + (a - 1.0) * ka_ref[...])
        cl = _dot_sel(tri_ref[...], lw)
        bb = kkn * a
        kx = kkn * jnp.exp(-lw)
        for h in range(RW_HEADS):
            lo = h * RW_DH
            cl_ref[0, d, h] = cl[:, lo:lo + RW_DH]
            kd_ref[0, d, h] = kd[:, lo:lo + RW_DH]
            b_ref[0, d, h] = bb[:, lo:lo + RW_DH]
            kx_ref[0, d, h] = kx[:, lo:lo + RW_DH]
    for h in range(RW_HEADS):
        lo = h * RW_DH
        v_ref[0, h] = v[:, lo:lo + RW_DH]
        r_ref[0, h] = r[:, lo:lo + RW_DH]


def _rw_prep(z3, halo, p, tr):
    b, t, _ = z3.shape
    c = RW_CHUNK
    ti = jnp.arange(tr)
    same = (ti[:, None] // c) == (ti[None, :] // c)
    trf = (same & (ti[None, :] <= ti[:, None])).astype(BF16)
    trb = (same & (ti[None, :] >= ti[:, None])).astype(BF16)

    def zcol(off, w):
        return pl.BlockSpec((1, tr, w), lambda b, i: (b, i, off // w))

    hm = pl.BlockSpec((1, RW_HEADS, tr, RW_DH), lambda b, i: (b, 0, i, 0))
    hmd = pl.BlockSpec((1, 2, RW_HEADS, tr, RW_DH), lambda b, i: (b, 0, 0, i, 0))
    s1 = _sds((b, RW_HEADS, t, RW_DH), F32)
    s2 = _sds((b, 2, RW_HEADS, t, RW_DH), F32)
    consts = [p['rw_mu'], p['rw_k_k'], p['rw_k_a'], p['rw_w0'], p['rw_a0'], p['rw_w2'], p['rw_a2'], trf, trb,
              _head_ones(RW_W, RW_DH)]
    return _call(
        functools.partial(_rw_prep_body, tr=tr), (b, t // tr),
        [zcol(ZC_RWK, 512), zcol(ZC_RWK + 512, 512), zcol(ZC_RWK + 1024, 512), zcol(ZC_L4, 128), zcol(ZC_G, 128),
         pl.BlockSpec((1, 1, 2, 1792), lambda b, i: (b, i, 0, 0))] + [_full(a) for a in consts],
        [hm, hm, pl.BlockSpec((1, tr, 128), lambda b, i: (b, i, 0)), hmd, hmd, hmd, hmd],
        [s1, s1, _sds((b, t, 128), F32), s2, s2, s2, s2], name="rw_prep")(z3, z3, z3, z3, z3, halo, *consts)


RW_P_HI = 1
RW_P_LO = 1


def _rw_scan_tile(r_ref, v_ref, cl_ref, kd_ref, b_ref, kx_ref, y_ref, s_ref, *, tr, rev):
    c = RW_CHUNK
    nc = tr // c
    g = RW_HEADS * nc

    def ld(x):
        return x.reshape(g, c, RW_DH)

    r, v = ld(r_ref[0]), ld(v_ref[0])
    cl, kd, bb, kx = ld(cl_ref[0, 0]), ld(kd_ref[0, 0]), ld(b_ref[0, 0]), ld(kx_ref[0, 0])
    last = 0 if rev else c - 1
    ctot = cl[:, last:last + 1, :]
    e = jnp.exp(cl)
    ei = jnp.exp(-cl)
    ec = jnp.exp(ctot - cl)
    kkt, rt = kx * e, r * e
    kw, bw = kd * ei, bb * ei
    kc, bc = kd * ec, bb * ec
    a_cat = jnp.concatenate([kkt, rt], axis=1)
    pp = _bmm('gtd,gsd->gts', a_cat, jnp.concatenate([kw, bw], axis=1), RW_P_HI)
    ti = lax.broadcasted_iota(jnp.int32, (c, c), 0)
    si = lax.broadcasted_iota(jnp.int32, (c, c), 1)
    strict = (si > ti) if rev else (si < ti)
    incl = (si >= ti) if rev else (si <= ti)
    nmat = jnp.where(strict, pp[:, :c, :c], 0.0)
    x = -jnp.where(strict, pp[:, :c, c:], 0.0)
    ark = jnp.where(incl, pp[:, c:, :c], 0.0)
    arb = jnp.where(incl, pp[:, c:, c:], 0.0)
    eye_c = (ti == si).astype(F32)
    tm = eye_c + x
    xp = _bmm('gts,gsu->gtu', x, x, RW_P_HI)
    nsq = int(math.log2(c)) - 1
    for i in range(nsq):
        if i < nsq - 1:
            both = _bmm('gts,gsu->gtu', xp, jnp.concatenate([tm, xp], axis=2), RW_P_LO)
            tm, xp = tm + both[:, :, :c], both[:, :, c:]
        else:
            tm = tm + _bmm('gts,gsu->gtu', xp, tm, RW_P_LO)
    nav = _bmm('gts,gsd->gtd', jnp.concatenate([nmat, ark], axis=1), v, RW_P_HI)
    nv, arkv = nav[:, :c], nav[:, c:]
    ta = _bmm('gts,gsd->gtd', tm, jnp.concatenate([kkt, nv], axis=2), RW_P_HI)
    ata = _bmm('gts,gsd->gtd', arb, ta, RW_P_HI)
    a2 = rt - ata[:, :, :RW_DH]
    y0 = arkv - ata[:, :, RW_DH:]
    tb = _bmm('gtk,gtd->gkd', bc, ta, RW_P_HI)
    di = lax.broadcasted_iota(jnp.int32, (RW_DH, RW_DH), 0)
    dj = lax.broadcasted_iota(jnp.int32, (RW_DH, RW_DH), 1)
    gmt = jnp.where(di == dj, jnp.exp(ctot), 0.0) - tb[:, :, :RW_DH]
    hmt = _bmm('gtk,gtv->gkv', kc, v, RW_P_HI) - tb[:, :, RW_DH:]

    def per_chunk(x):
        return x.reshape(RW_HEADS, nc, x.shape[1], x.shape[2])

    a2, y0, gmt, hmt = per_chunk(a2), per_chunk(y0), per_chunk(gmt), per_chunk(hmt)
    st = s_ref[...]
    ys = [None] * nc
    for ci in (range(nc - 1, -1, -1) if rev else range(nc)):
        ys[ci] = _bmm('htk,hkv->htv', a2[:, ci], st, RW_P_HI) + y0[:, ci]
        st = _bmm('hke,hev->hkv', gmt[:, ci], st, RW_P_HI) + hmt[:, ci]
    s_ref[...] = st
    y_ref[0] = jnp.concatenate(ys, axis=1)


def _rw_scan_body(*refs, tr):
    fwd, bwd, (yf_ref, yb_ref, sf_ref, sb_ref) = refs[0:6], refs[6:12], refs[12:16]

    @pl.when(pl.program_id(1) == 0)
    def _():
        sf_ref[...] = jnp.zeros_like(sf_ref)
        sb_ref[...] = jnp.zeros_like(sb_ref)

    _rw_scan_tile(*fwd, yf_ref, sf_ref, tr=tr, rev=False)
    _rw_scan_tile(*bwd, yb_ref, sb_ref, tr=tr, rev=True)


def _rw_scan(r, v, cl, kd, bb, kx, tr, nl, nm):
    b, _, t, _ = r.shape
    nt = nl + nm
    tile_f = lambda j: jnp.where(j < nm, nl + j, j - nm)
    tile_b = lambda j: nt - 1 - j

    def specs(tile, d):
        hm = pl.BlockSpec((1, RW_HEADS, tr, RW_DH), lambda b, j: (b, 0, tile(j), 0))
        hmd = pl.BlockSpec((1, 1, RW_HEADS, tr, RW_DH), lambda b, j: (b, d, 0, tile(j), 0))
        return hm, hmd

    hf, hfd = specs(tile_f, 0)
    hb, hbd = specs(tile_b, 1)
    shp = _sds((b, RW_HEADS, t, RW_DH), F32)
    state = pltpu.VMEM((RW_HEADS, RW_DH, RW_DH), F32)
    return _call(
        functools.partial(_rw_scan_body, tr=tr), (b, nt),
        [hf, hf, hfd, hfd, hfd, hfd, hb, hb, hbd, hbd, hbd, hbd], [hf, hb], [shp, shp],
        scratch=[state, state], sem=("parallel", "arbitrary"),
        name="rw_scan")(r, v, cl, kd, bb, kx, r, v, cl, kd, bb, kx)


def _rw_out_body(yf_ref, yb_ref, r_ref, v_ref, kd_ref, g_ref, lnw_ref, lnb_ref, rk_ref, g2_ref, o_ref):
    y = yf_ref[0] + yb_ref[0]
    mu = jnp.mean(y, axis=-1, keepdims=True)
    var = jnp.mean(jnp.square(y - mu), axis=-1, keepdims=True)
    yn = (y - mu) * lax.rsqrt(var + RW_GN_EPS) * lnw_ref[...] + lnb_ref[...]
    ksum = kd_ref[0, 0] + kd_ref[0, 1]
    bonus = jnp.sum(r_ref[0] * ksum * rk_ref[...], axis=-1, keepdims=True) * v_ref[0]
    o = yn + bonus
    o = jnp.concatenate([o[h] for h in range(RW_HEADS)], axis=-1)
    gate = _dot3(jax.nn.sigmoid(g_ref[0]), g2_ref[...])
    o_ref[0] = (o * gate).astype(BF16)


def _rw_out(yf, yb, r, v, kd, g, p, tr, nrow):
    b = r.shape[0]
    hm = pl.BlockSpec((1, RW_HEADS, tr, RW_DH), lambda b, i: (b, 0, i, 0))
    consts = [p['rw_ln_w'], p['rw_ln_b'], p['rw_r_k'], p['rw_g2']]
    return _call(
        _rw_out_body, (b, nrow),
        [hm, hm, hm, hm, pl.BlockSpec((1, 2, RW_HEADS, tr, RW_DH), lambda b, i: (b, 0, 0, i, 0)),
         pl.BlockSpec((1, tr, 128), lambda b, i: (b, i, 0))] + [_full(a) for a in consts],
        pl.BlockSpec((1, tr, RW_W), lambda b, i: (b, i, 0)),
        _sds((b, nrow * tr, RW_W), BF16), name="rw_out")(yf, yb, r, v, kd, g, *consts)


GATE_COL0 = KEY_COLS + 2144
GATE_SHIFT = GATE_COL0 % 128


def _merge_body(*refs, tn):
    h_ref, y_refs, w_refs, wb_ref, o_ref, wg_ref = refs[0], refs[1:5], refs[5:13], refs[13], refs[14], refs[15]

    @pl.when((pl.program_id(1) == 0) & (pl.program_id(2) == 0))
    def _():
        for k in range(N_BRANCH):
            win = jnp.concatenate([w_refs[2 * k][0], w_refs[2 * k + 1][0]], axis=1)
            wg_ref[k] = win[:, GATE_SHIFT:GATE_SHIFT + tn].astype(BF16)

    h = h_ref[0]
    acc = None
    for k in range(N_BRANCH):
        t = jax.nn.sigmoid(_dot(h, wg_ref[k])) * _dot(y_refs[k][0], wb_ref[k])
        acc = t if acc is None else acc + t
    o_ref[0] = acc.astype(BF16)


def _merge(h, ys, w_in, layer, wbr, rows):
    b, _, d = h.shape
    tn = 256
    base = GATE_COL0 - GATE_SHIFT
    assert base % tn == 0 and d % tn == 0
    tm = _pick_tile(rows, (768, 512, 256, 128, 64))
    row = lambda w: pl.BlockSpec((1, tm, w), lambda n, b, i: (b, i, 0))
    wspecs = []
    for k in range(N_BRANCH):
        col = base + k * d
        wspecs.append(pl.BlockSpec((1, d, tn), functools.partial(
            lambda n, b, i, col: (layer, 0, col // tn + n), col=col)))
        wspecs.append(pl.BlockSpec((1, d, 128), functools.partial(
            lambda n, b, i, col: (layer, 0, (col + tn) // 128 + n * (tn // 128)), col=col)))
    return _call(
        functools.partial(_merge_body, tn=tn), (d // tn, b, rows // tm),
        [row(d)] + [row(BRANCH_W)] * 4 + wspecs + [pl.BlockSpec((N_BRANCH, BRANCH_W, tn), lambda n, b, i: (0, 0, n))],
        pl.BlockSpec((1, tm, tn), lambda n, b, i: (b, i, n)),
        _sds((b, rows, d), BF16), scratch=[pltpu.VMEM((N_BRANCH, d, tn), BF16)],
        sem=("arbitrary", "arbitrary", "arbitrary"), name="merge")(h, *ys, *([w_in] * 8), wbr)


def _outproj_body(a_ref, w_ref, x_ref, md_ref, o_ref, *, tm, l_tok):
    row = pl.program_id(1) * tm + lax.broadcasted_iota(jnp.int32, (tm, 1), 0)
    gate = jnp.where(row < l_tok, md_ref[0, 0, 2:3, :], md_ref[0, 1, 2:3, :])
    o_ref[0] = x_ref[0] + gate * _dot(a_ref[0], w_ref[...])


def _outproj(acc, w_out, xs, md, l_tok):
    b, rows, d = acc.shape
    tm = _pick_tile(rows, (768, 512, 256, 128, 64))
    row = pl.BlockSpec((1, tm, d), lambda b, i: (b, i, 0))
    return _call(
        functools.partial(_outproj_body, tm=tm, l_tok=l_tok), (b, rows // tm),
        [row, _full(w_out), row, pl.BlockSpec((1, 2, 6, d), lambda b, i: (b, 0, 0, 0))], row,
        _sds((b, rows, d), F32), name="out_proj")(acc, w_out, xs, md)


def _select_body(lg_ref, slot_ref, aff_ref, slotc_ref, *, n, cap):
    lg = lg_ref[0]
    lane = lax.broadcasted_iota(jnp.int32, lg.shape, 1)
    lg = jnp.where(lane < N_EXPERTS, lg, NEG_INF)
    ex = jnp.exp(lg - lg.max(axis=-1, keepdims=True))
    aff = ex / ex.sum(axis=-1, keepdims=True)
    aff_t = aff.T[:N_EXPERTS]

    def count_ge(t):
        return jnp.sum((aff_t >= t).astype(F32), axis=1, keepdims=True)

    tiny = 2.0 ** -126
    lo = jnp.full((N_EXPERTS, 1), tiny, F32)
    for sh in (64, 32, 16, 8, 4, 2, 1):
        cand = lo * (2.0 ** sh)
        lo = jnp.where(count_ge(cand) >= cap, cand, lo)
    hi = lo * 2.0
    below = count_ge(tiny) < cap
    lo = jnp.where(below, 0.0, lo)
    hi = jnp.where(below, tiny, hi)
    for _ in range(40):
        mid = lo + (hi - lo) * 0.5
        ok = count_ge(mid) >= cap
        lo = jnp.where(ok, mid, lo)
        hi = jnp.where(ok, hi, mid)
    gt = aff_t >= hi
    eq = (aff_t >= lo) & (aff_t < hi)
    need = cap - jnp.sum(gt.astype(F32), axis=1, keepdims=True)
    both = jnp.concatenate([gt.astype(BF16), eq.astype(BF16)], axis=0)
    cw = min(n, 512)
    pre = []
    for cb in range(n // cw):
        tp = lax.broadcasted_iota(jnp.int32, (n, cw), 0)
        tt = lax.broadcasted_iota(jnp.int32, (n, cw), 1) + cb * cw
        pre.append(_dot(both, (tp < tt).astype(BF16)))
    pre = jnp.concatenate(pre, axis=1) if len(pre) > 1 else pre[0]
    pre_gt, pre_eq = pre[:N_EXPERTS], pre[N_EXPERTS:]
    sel = gt | (eq & (pre_eq < need))
    slot = jnp.where(sel, pre_gt + jnp.minimum(pre_eq, need), -1.0)
    slot_ref[0] = slot.astype(jnp.int32)
    aff_ref[0] = aff_t
    pad = jnp.full((128 - N_EXPERTS, n), -1.0, F32)
    slotc_ref[0] = jnp.concatenate([slot, pad], axis=0).T.astype(jnp.int32)


def _select(logits, n, blk, cap, name):
    b = logits.shape[0]
    er = pl.BlockSpec((1, N_EXPERTS, n), lambda b: (b, 0, 0))
    return _call(
        functools.partial(_select_body, n=n, cap=cap), (b,),
        [pl.BlockSpec((1, n, 128), lambda b: (b, blk, 0))],
        [er, er, pl.BlockSpec((1, n, 128), lambda b: (b, 0, 0))],
        [_sds((b, N_EXPERTS, n), jnp.int32), _sds((b, N_EXPERTS, n), F32), _sds((b, n, 128), jnp.int32)],
        name=name)(logits)


def _ffn_body(h_ref, slot_ref, aff_ref, w1_ref, w3_ref, w2_ref, o_ref, *, grp, n, cap):
    xs, gates = [], []
    jrow = lax.broadcasted_iota(jnp.int32, (cap, n), 0)
    for gi in range(grp):
        pick = jrow == slot_ref[gi, 0]
        xs.append(_dot(pick.astype(BF16), h_ref[gi]).astype(BF16))
        gates.append(jnp.sum(jnp.where(pick, aff_ref[gi, 0], 0.0), axis=1, keepdims=True))
    xe = jnp.concatenate(xs, axis=0) if grp > 1 else xs[0]
    gate = jnp.concatenate(gates, axis=0) if grp > 1 else gates[0]
    a = _dot(xe, w1_ref[0])
    u = _dot(xe, w3_ref[0])
    hm = (a * jax.nn.sigmoid(a) * u).astype(BF16)
    ye = (_dot(hm, w2_ref[0]) * gate).astype(BF16)
    for gi in range(grp):
        o_ref[0, gi] = ye[gi * cap:(gi + 1) * cap]


def _moe_ffn(h2, slot, aff, w1, w3, w2, n, blk, cap, grp, name):
    b, _, d = h2.shape
    ff = w1.shape[-1]
    sr = pl.BlockSpec((grp, 1, 1, n), lambda e, g: (g, e, 0, 0))
    return _call(
        functools.partial(_ffn_body, grp=grp, n=n, cap=cap), (N_EXPERTS, b // grp),
        [pl.BlockSpec((grp, n, d), lambda e, g: (g, blk, 0)), sr, sr,
         pl.BlockSpec((1, d, ff), lambda e, g: (e, 0, 0)), pl.BlockSpec((1, d, ff), lambda e, g: (e, 0, 0)),
         pl.BlockSpec((1, ff, d), lambda e, g: (e, 0, 0))],
        pl.BlockSpec((1, grp, cap, d), lambda e, g: (e, g, 0, 0)),
        _sds((N_EXPERTS, b, cap, d), BF16), name=name)(
            h2, slot.reshape(b, N_EXPERTS, 1, n), aff.reshape(b, N_EXPERTS, 1, n), w1, w3, w2)


def _combine_body(sc_ref, ye_ref, x_ref, md_ref, *rest, cap):
    o_ref = rest[-1]
    sc = sc_ref[0]
    jj = lax.broadcasted_iota(jnp.int32, (sc.shape[0], cap), 1)
    acc = None
    for e in range(N_EXPERTS):
        put = (sc[:, e:e + 1] == jj).astype(BF16)
        t = _dot(put, ye_ref[e, 0])
        acc = t if acc is None else acc + t
    o_ref[0] = x_ref[0] + md_ref[0, 0, 5:6, :] * acc


def _moe_combine(slotc, ye, x1, md, n, blk_rows, is_ctx, cap, name, out_rows, prev=None):
    b, _, d = x1.shape
    tm = _pick_tile(n, (512, 256, 128, 64, 32, 16, 8))
    off = blk_rows // tm
    extra_specs, extra_args, aliases = [], [], None
    if prev is not None:
        extra_specs, extra_args, aliases = [pl.BlockSpec(memory_space=pl.ANY)], [prev], {4: 0}
    return _call(
        functools.partial(_combine_body, cap=cap), (b, n // tm),
        [pl.BlockSpec((1, tm, 128), lambda b, i: (b, i, 0)),
         pl.BlockSpec((N_EXPERTS, 1, cap, d), lambda b, i: (0, b, 0, 0)),
         pl.BlockSpec((1, tm, d), lambda b, i: (b, off + i, 0)),
         pl.BlockSpec((1, 1, 6, d), lambda b, i: (b, 1 if is_ctx else 0, 0, 0))] + extra_specs,
        pl.BlockSpec((1, tm, d), lambda b, i: (b, off + i, 0)),
        _sds((b, out_rows, d), F32), name=name, aliases=aliases)(slotc, ye, x1, md, *extra_args)


def _moe(x1, h2, logits, md, w1, w3, w2, l_tok, m_tok, ctx_out):
    b = x1.shape[0]
    cap_l = CAPACITY_FACTOR * l_tok // N_EXPERTS
    slot, aff, slotc = _select(logits, l_tok, 0, cap_l, "moe_select_latent")
    ye = _moe_ffn(h2, slot, aff, w1, w3, w2, l_tok, 0, cap_l, 1, "moe_ffn_latent")
    out_rows = l_tok + m_tok if ctx_out else l_tok
    x2 = _moe_combine(slotc, ye, x1, md, l_tok, 0, False, cap_l, "moe_combine_latent", out_rows)
    if not ctx_out:
        return x2
    cap_c = CAPACITY_FACTOR * m_tok // N_EXPERTS
    blk = l_tok // m_tok
    slot, aff, slotc = _select(logits, m_tok, blk, cap_c, "moe_select_ctx")
    ye = _moe_ffn(h2, slot, aff, w1, w3, w2, m_tok, blk, cap_c, b, "moe_ffn_ctx")
    return _moe_combine(slotc, ye, x1, md, m_tok, l_tok, True, cap_c, "moe_combine_ctx", out_rows, prev=x2)


def _layer_params(l, w_in, na_q_norm, na_k_norm, na_rpb, mla_cq_norm, mla_ckv_norm, mla_w_uq, mla_w_ukv,
                  mla_q_norm, mla_k_norm, rw_mu_ks, rw_mu_qs, rw_w0, rw_w2, rw_a0, rw_a2, rw_g2, rw_k_k, rw_k_a,
                  rw_r_k, rw_ln_w, rw_ln_b, w_br, w_out, rows):
    w = w_in[l, :, :GATE_COL0].astype(BF16)
    d = w.shape[0]
    kq = KEY_COLS
    kpe = w[:, 1280:1344]
    zpad = lambda n: jnp.zeros((d, n), w.dtype)
    wz = jnp.concatenate([
        w[:, kq:kq + 512], w[:, 0:512], w[:, 512:1024],
        w[:, kq + 512:kq + 1024],
        w[:, 1344:1856], w[:, 1856:2368], w[:, kq + 1024:kq + 1536],
        w[:, kq + 1632:kq + 2144],
        w[:, 1024:1280],
        kpe, _swap_halves(kpe),
        w[:, 2368:2496],
        w[:, kq + 1536:kq + 1632], zpad(32)], axis=1)
    assert wz.shape[1] == NZ

    def head_gain(g):
        return jnp.concatenate([g[:128], g[128:], _swap_halves(g[128:])])[None, :]

    wq = mla_w_uq[l].reshape(MLA_QLORA, MLA_HEADS, MLA_NOPE + MLA_ROPE)
    wq = jnp.concatenate([wq, _swap_halves(wq[:, :, MLA_NOPE:])], axis=-1).reshape(MLA_QLORA, MLA_HEADS * 256)
    w2cat = jnp.zeros((128, 1024), F32).at[0:32, 0:512].set(rw_w2[l, 0]).at[32:64, 512:].set(rw_w2[l, 1])
    a2cat = jnp.zeros((128, 1024), F32).at[64:96, 0:512].set(rw_a2[l, 0]).at[96:128, 512:].set(rw_a2[l, 1])
    mu = jnp.concatenate([rw_mu_ks[l][:1024], rw_mu_qs[l][:512], rw_mu_ks[l][1024:], rw_mu_qs[l][512:],
                          jnp.zeros((32,), F32)])[None, :]
    hd = lambda a: a.reshape(RW_HEADS, 1, RW_DH)
    return {
        'wz': wz, 'w_in': w_in, 'layer': l,
        'na_gq': na_q_norm[l][None, :], 'na_gk': na_k_norm[l][None, :],
        'na_bias': _na_bias_table(na_rpb[l], rows),
        'g_cq': mla_cq_norm[l][None, :], 'g_ckv': mla_ckv_norm[l][None, :],
        'w_uq': wq.astype(BF16), 'w_ukv': mla_w_ukv[l].astype(BF16),
        'g_q': head_gain(mla_q_norm[l]), 'g_k': head_gain(mla_k_norm[l]),
        'rw_mu': mu, 'rw_k_k': rw_k_k[l][None, :], 'rw_k_a': rw_k_a[l][None, :],
        'rw_w0': rw_w0[l].reshape(1, 1024), 'rw_a0': rw_a0[l].reshape(1, 1024), 'rw_w2': w2cat, 'rw_a2': a2cat,
        'rw_ln_w': hd(rw_ln_w[l]), 'rw_ln_b': hd(rw_ln_b[l]), 'rw_r_k': hd(rw_r_k[l]),
        'rw_g2': jnp.concatenate([rw_g2[l], jnp.zeros((32, RW_W), F32)], axis=0),
        'w_br': w_br[l].astype(BF16), 'w_out': w_out[l].astype(BF16),
    }


def _rw_halo(z3, tr, nl):
    b, t, _ = z3.shape
    nt = t // tr
    z4 = z3.reshape(b, nt, tr, z3.shape[-1])
    pick = lambda a: jnp.concatenate([a[..., ZC_RWK:ZC_RWK + 1536], a[..., ZC_L4:ZC_L4 + 256]], axis=-1)
    first, last = pick(z4[:, :, 0, :]), pick(z4[:, :, tr - 1, :])
    zero = jnp.zeros_like(first[:, :1])
    prev = jnp.concatenate([zero, last[:, :-1]], axis=1)
    nxt = jnp.concatenate([first[:, 1:], zero], axis=1)
    tile = jnp.arange(nt)[None, :, None]
    prev = jnp.where(tile == nl, 0.0, prev)
    nxt = jnp.where(tile == nl - 1, 0.0, nxt)
    return jnp.stack([prev, nxt], axis=2)


def _layer(xs, md, p, g1n, g2n, wr, w1, w3, w2, tabs, l_tok, m_tok, tr, ctx_out):
    b, t, d = xs.shape
    nl, nm = l_tok // tr, m_tok // tr
    nrow = (nl + nm) if ctx_out else nl
    blk = l_tok // m_tok

    h = _norm1(xs, g1n, md, tr, nl)
    z3 = _matmul(h.reshape(b * t, d), p['wz'], F32).reshape(b, t, NZ)

    nq, nk, nv = _na_prep(z3, p['na_gq'], p['na_gk'], tr)
    y_na = _na_latent(nq, nk, nv, p['na_bias'], l_tok, m_tok)
    mq, mk, mv = _mla_prep(z3, p, tabs['rope_c'], tabs['rope_s'], tr)
    tq = _pick_tile(l_tok, (256, 128, 64))
    y_mla = _attention([mq], [mk], mv, tq=tq, q0=0, nq=l_tok // tq, tk=t, kblk=0, name="mla_latent")
    rv, rr, rg, cl, kd, bb, kx = _rw_prep(z3, _rw_halo(z3, tr, nl), p, tr)
    yf, yb = _rw_scan(rr, rv, cl, kd, bb, kx, tr, nl, nm)
    y_rw = _rw_out(yf, yb, rr, rv, kd, rg, p, tr, nrow)
    y_fn, yc_fn = _fourier(z3, tabs, tr, l_tok, m_tok, ctx_out)

    if ctx_out:
        yc_na = _attention([nq], [nk], nv, tq=m_tok, q0=blk, nq=1, tk=m_tok, kblk=blk, name="na_ctx")
        yc_mla = _attention([mq], [mk], mv, tq=m_tok, q0=blk, nq=1, tk=m_tok, kblk=blk, name="mla_ctx")
        y_fn = jnp.concatenate([y_fn, yc_fn], axis=1)
        y_na = jnp.concatenate([y_na, yc_na], axis=1)
        y_mla = jnp.concatenate([y_mla, yc_mla], axis=1)

    acc = _merge(h, (y_fn, y_na, y_mla, y_rw), p['w_in'], p['layer'], p['w_br'], nrow * tr)
    x1 = _outproj(acc, p['w_out'], xs, md, l_tok)
    h2, logits = _norm2_router(x1, g2n, md, wr, tr, nl)
    return _moe(x1, h2, logits, md, w1, w3, w2, l_tok, m_tok, ctx_out)


def kernel(x, c, ctx, c_ctx, ada_w, ada_b, norm1_g, norm2_g, w_in, na_q_norm, na_k_norm, na_rpb, mla_cq_norm, mla_ckv_norm, mla_w_uq, mla_w_ukv, mla_q_norm, mla_k_norm, rw_mu_ks, rw_mu_qs, rw_w0, rw_w2, rw_a0, rw_a2, rw_g2, rw_k_k, rw_k_a, rw_r_k, rw_ln_w, rw_ln_b, w_br, w_out, moe_router, moe_w1, moe_w3, moe_w2):
    b, l_tok, d = x.shape
    m_tok = ctx.shape[1]
    depth = ada_w.shape[0]
    tr = min(m_tok, 256)
    assert l_tok % tr == 0 and m_tok % tr == 0 and l_tok % m_tok == 0 and tr % RW_CHUNK == 0
    assert l_tok % GRID_W == 0 and m_tok % 128 == 0

    nr = -(-(b + 1) // 8) * 8
    cc = jnp.concatenate([c, c_ctx[None, :], jnp.zeros((nr - b - 1, d), F32)], axis=0)
    mod = _modulation(cc, ada_w, ada_b)
    tabs = {}
    tabs['rope_c'], tabs['rope_s'] = _rope_tables(l_tok, m_tok)

    xs = jnp.concatenate([x, ctx], axis=1)
    for l in range(depth):
        ctx_out = l < depth - 1
        tabs.update(_fourier_tables(l_tok, m_tok, ctx_out))
        p = _layer_params(l, w_in, na_q_norm, na_k_norm, na_rpb, mla_cq_norm, mla_ckv_norm, mla_w_uq, mla_w_ukv,
                          mla_q_norm, mla_k_norm, rw_mu_ks, rw_mu_qs, rw_w0, rw_w2, rw_a0, rw_a2, rw_g2, rw_k_k,
                          rw_k_a, rw_r_k, rw_ln_w, rw_ln_b, w_br, w_out, l_tok // GRID_W)
        md = jnp.stack([mod[l, :b], jnp.broadcast_to(mod[l, b], (b, 6 * d))], axis=1).reshape(b, 2, 6, d)
        wr = jnp.concatenate([moe_router[l], jnp.zeros((d, 128 - N_EXPERTS), F32)], axis=1)
        xs = _layer(xs, md, p, norm1_g[l][None, :], norm2_g[l][None, :], wr,
                         moe_w1[l].astype(BF16), moe_w3[l].astype(BF16), moe_w2[l].astype(BF16),
                         tabs, l_tok, m_tok, tr, ctx_out)
    return xs
```

```python
import functools
import math

import jax
import jax.numpy as jnp
from jax import lax
from jax.experimental import pallas as pl
from jax.experimental.pallas import tpu as pltpu

F32 = jnp.float32
BF16 = jnp.bfloat16

GRID_W = 64
N_BRANCH = 4
BRANCH_W = 512
NA_HEADS, NA_DH, NA_KH, NA_KW = 8, 64, 8, 16
NA_SCALE = NA_DH ** -0.5
MLA_HEADS, MLA_NOPE, MLA_ROPE, MLA_V = 4, 128, 64, 128
MLA_QLORA, MLA_KVLORA = 512, 256
MLA_SCALE = (MLA_NOPE + MLA_ROPE) ** -0.5
RW_HEADS, RW_DH, RW_W = 8, 64, 512
RW_LORA, RW_GATE_LORA = 32, 96
RW_GN_EPS = 64e-5
N_EXPERTS = 16
CAPACITY_FACTOR = 2
ROPE_THETA = 10000.0
NORM_EPS = 1e-6
NEG_INF = -1e30
RW_CHUNK = 64
KEY_COLS = 2496
NZ = 4736
ZC_NA = 0
ZC_CQ = 1536
ZC_RWK = 2048
ZC_FN = 3584
ZC_CKV = 4096
ZC_KPE = 4352
ZC_L4 = 4480
ZC_G = 4608

VMEM_LIMIT = 56 * 2 ** 20


def _call(body, grid, in_specs, out_specs, out_shape, scratch=(), sem=None, name=None, aliases=None):
    return pl.pallas_call(
        body, grid=grid, in_specs=in_specs, out_specs=out_specs, out_shape=out_shape,
        scratch_shapes=list(scratch), name=name, input_output_aliases=aliases or {},
        compiler_params=pltpu.CompilerParams(
            dimension_semantics=sem or ("parallel",) * len(grid), vmem_limit_bytes=VMEM_LIMIT))


def _full(a):
    nd = a.ndim
    return pl.BlockSpec(a.shape, lambda *_: (0,) * nd)


def _sds(shape, dtype):
    return jax.ShapeDtypeStruct(shape, dtype)


def _dot(a, b):
    return jnp.dot(a, b, preferred_element_type=F32)


def _dot_nt(a, b):
    return lax.dot_general(a, b, (((1,), (1,)), ((), ())), preferred_element_type=F32)


def _split2(x):
    hi = x.astype(BF16)
    return hi, (x - hi.astype(F32)).astype(BF16)


def _split3(x):
    hi = x.astype(BF16)
    r = x - hi.astype(F32)
    mid = r.astype(BF16)
    return hi, mid, (r - mid.astype(F32)).astype(BF16)


def _dot3(a, b):
    ah, al = _split2(a)
    bh, bl = _split2(b)
    return _dot(ah, bh) + (_dot(ah, bl) + _dot(al, bh))


def _dot_sel(sel_bf16, x):
    h, m, l = _split3(x)
    return _dot(sel_bf16, h) + (_dot(sel_bf16, m) + _dot(sel_bf16, l))


def _bmm(spec, a, b, passes):
    e = functools.partial(jnp.einsum, spec, preferred_element_type=F32)
    if passes == 1:
        return e(a.astype(BF16), b.astype(BF16))
    ah, al = _split2(a)
    bh, bl = _split2(b)
    return e(ah, bh) + (e(ah, bl) + e(al, bh))


def _pick_tile(n, cands):
    for c in cands:
        if n % c == 0:
            return c
    raise ValueError(f"no tile for {n}")


def _mod_body(c_ref, w_ref, b_ref, o_ref):
    cc = c_ref[...]
    sh, sl = _split2(cc * jax.nn.sigmoid(cc))
    wb = w_ref[0].astype(BF16)
    o_ref[0] = (_dot(sh, wb) + _dot(sl, wb)) + b_ref[0]


def _modulation(cc, ada_w, ada_b):
    depth, d, n6 = ada_w.shape
    r = cc.shape[0]
    tn = 1024
    return _call(
        _mod_body, (depth, n6 // tn),
        [pl.BlockSpec((r, d), lambda l, n: (0, 0)),
         pl.BlockSpec((1, d, tn), lambda l, n: (l, 0, n)),
         pl.BlockSpec((1, 1, tn), lambda l, n: (l, 0, n))],
        pl.BlockSpec((1, r, tn), lambda l, n: (l, 0, n)),
        _sds((depth, r, n6), F32), name="adaln_mod")(cc, ada_w, ada_b.reshape(depth, 1, n6))


def _modnorm(x, g, shift, scale):
    y = x * lax.rsqrt(jnp.mean(x * x, axis=-1, keepdims=True) + NORM_EPS)
    return (y * g) * (1.0 + scale) + shift


def _norm1_body(x_ref, g_ref, md_ref, h_ref):
    h = _modnorm(x_ref[0], g_ref[...], md_ref[0, 0, 0:1, :], md_ref[0, 0, 1:2, :])
    h_ref[0] = h.astype(BF16)


def _norm2_body(x_ref, g_ref, md_ref, wr_ref, h_ref, lg_ref):
    h = _modnorm(x_ref[0], g_ref[...], md_ref[0, 0, 3:4, :], md_ref[0, 0, 4:5, :])
    h_ref[0] = h.astype(BF16)
    lg_ref[0] = _dot3(h, wr_ref[...])


def _md_spec(d, nl):
    return pl.BlockSpec((1, 1, 6, d), lambda b, i: (b, i // nl, 0, 0))


def _norm1(xs, g, md, tr, nl):
    b, t, d = xs.shape
    return _call(
        _norm1_body, (b, t // tr),
        [pl.BlockSpec((1, tr, d), lambda b, i: (b, i, 0)), _full(g), _md_spec(d, nl)],
        pl.BlockSpec((1, tr, d), lambda b, i: (b, i, 0)),
        _sds((b, t, d), BF16), name="norm1")(xs, g, md)


def _norm2_router(x1, g, md, wr, tr, nl):
    b, t, d = x1.shape
    return _call(
        _norm2_body, (b, t // tr),
        [pl.BlockSpec((1, tr, d), lambda b, i: (b, i, 0)), _full(g), _md_spec(d, nl), _full(wr)],
        [pl.BlockSpec((1, tr, d), lambda b, i: (b, i, 0)), pl.BlockSpec((1, tr, 128), lambda b, i: (b, i, 0))],
        [_sds((b, t, d), BF16), _sds((b, t, 128), F32)], name="norm2_router")(x1, g, md, wr)


def _mm_body(a_ref, w_ref, o_ref):
    o_ref[...] = _dot(a_ref[...], w_ref[...]).astype(o_ref.dtype)


def _matmul(a, w, out_dtype):
    r, k = a.shape
    n = w.shape[1]
    tm = _pick_tile(r, (512, 384, 256, 128, 64, 32, 16, 8))
    return _call(
        _mm_body, (r // tm,),
        [pl.BlockSpec((tm, k), lambda i: (i, 0)),
         pl.BlockSpec((k, n), lambda i: (0, 0), pipeline_mode=pl.Buffered(1))],
        pl.BlockSpec((tm, n), lambda i: (i, 0)),
        _sds((r, n), out_dtype), name="in_proj")(a, w)


def _head_sum(x, ones_bd):
    hi, lo = _split2(x)
    return _dot(hi, ones_bd) + _dot(lo, ones_bd)


def _head_sumsq(x, ones_bd):
    return _head_sum(x * x, ones_bd)


def _head_ones(width, dh):
    i = jnp.arange(width) // dh
    return (i[:, None] == i[None, :]).astype(BF16)


def _na_prep_body(z_ref, gq_ref, gk_ref, e_ref, q_ref, k_ref, v_ref):
    z = z_ref[0]
    q, k, v = z[:, :512], z[:, 512:1024], z[:, 1024:]
    e = e_ref[...]
    qn = (q * lax.rsqrt(_head_sumsq(q, e) / NA_DH + NORM_EPS) * (gq_ref[...] * NA_SCALE)).astype(BF16)
    kn = (k * lax.rsqrt(_head_sumsq(k, e) / NA_DH + NORM_EPS) * gk_ref[...]).astype(BF16)
    vb = v.astype(BF16)
    for h in range(NA_HEADS):
        lo = h * NA_DH
        q_ref[0, h] = qn[:, lo:lo + NA_DH]
        k_ref[0, h] = kn[:, lo:lo + NA_DH]
        v_ref[0, h] = vb[:, lo:lo + NA_DH]


def _na_prep(z3, gq, gk, tr):
    b, t, _ = z3.shape
    hm = pl.BlockSpec((1, NA_HEADS, tr, NA_DH), lambda b, i: (b, 0, i, 0))
    shp = _sds((b, NA_HEADS, t, NA_DH), BF16)
    e = _head_ones(NA_HEADS * NA_DH, NA_DH)
    gq, gk = jnp.tile(gq, (1, NA_HEADS)), jnp.tile(gk, (1, NA_HEADS))
    return _call(
        _na_prep_body, (b, t // tr),
        [pl.BlockSpec((1, tr, 1536), lambda b, i: (b, i, 0)), _full(gq), _full(gk), _full(e)],
        [hm, hm, hm], [shp, shp, shp], name="na_prep")(z3, gq, gk, e)


def _softmax_pv(s, vs):
    m = s[0].max(axis=-1, keepdims=True)
    for t in s[1:]:
        m = jnp.maximum(m, t.max(axis=-1, keepdims=True))
    den = None
    acc = None
    for t, v in zip(s, vs):
        p = jnp.exp(t - m)
        ps = p.sum(axis=-1, keepdims=True)
        o = jnp.einsum('hqk,hkd->hqd', p.astype(BF16), v, preferred_element_type=F32)
        den = ps if den is None else den + ps
        acc = o if acc is None else acc + o
    return acc / den


def _qk(q, k):
    return jnp.einsum('hqd,hkd->hqk', q, k, preferred_element_type=F32)


def _heads_to_lanes(o):
    return jnp.concatenate([o[h] for h in range(o.shape[0])], axis=-1)


NA_ROWS_PER_STEP = 4


def _na_lat_body(q_ref, k_ref, v_ref, bias_ref, o_ref, *, rows, kh, l_tok, m_tok, rps):
    for rr in range(rps):
        r = pl.program_id(1) * rps + rr
        rs = jnp.clip(r - kh // 2, 0, rows - kh)
        delta = r - rs
        start = pl.multiple_of(rs * GRID_W, GRID_W)
        q = q_ref[0, :, rr * GRID_W:(rr + 1) * GRID_W, :]
        kb = k_ref[0, :, pl.ds(start, kh * GRID_W), :]
        vb = v_ref[0, :, pl.ds(start, kh * GRID_W), :]
        kc = k_ref[0, :, l_tok:l_tok + m_tok, :]
        vc = v_ref[0, :, l_tok:l_tok + m_tok, :]
        o = _softmax_pv([_qk(q, kb) + bias_ref[delta], _qk(q, kc)], [vb, vc])
        o_ref[0, rr * GRID_W:(rr + 1) * GRID_W, :] = _heads_to_lanes(o).astype(BF16)


def _na_latent(q, k, v, bias, l_tok, m_tok):
    b, _, t, _ = q.shape
    rows = l_tok // GRID_W
    kh = min(NA_KH, rows)
    rps = math.gcd(rows, NA_ROWS_PER_STEP)
    kv = pl.BlockSpec((1, NA_HEADS, t, NA_DH), lambda b, r: (b, 0, 0, 0))
    return _call(
        functools.partial(_na_lat_body, rows=rows, kh=kh, l_tok=l_tok, m_tok=m_tok, rps=rps), (b, rows // rps),
        [pl.BlockSpec((1, NA_HEADS, rps * GRID_W, NA_DH), lambda b, r: (b, 0, r, 0)), kv, kv, _full(bias)],
        pl.BlockSpec((1, rps * GRID_W, BRANCH_W), lambda b, r: (b, r, 0)),
        _sds((b, l_tok, BRANCH_W), BF16), name="na_latent")(q, k, v, bias)


def _toeplitz_body(r_ref, oh_ref, valid_ref, o_ref):
    h, m, l = _split3(r_ref[...])
    oh = oh_ref[...]
    t = _dot(h, oh) + (_dot(m, oh) + _dot(l, oh))
    o_ref[...] = jnp.where(valid_ref[...] > 0.0, t, NEG_INF)


def _na_bias_table(rpb, rows):
    kh = min(NA_KH, rows)
    nh, ndr, ndc = rpb.shape
    col = jnp.arange(GRID_W)
    cs = jnp.clip(col - NA_KW // 2, 0, GRID_W - NA_KW)
    valid = (col[None, :] >= cs[:, None]) & (col[None, :] < cs[:, None] + NA_KW)
    dc = jnp.clip(col[None, :] - col[:, None] + NA_KW - 1, 0, 2 * NA_KW - 2)
    onehot = (jnp.arange(128)[:, None] == dc.reshape(1, -1)).astype(BF16)
    rp = jnp.pad(rpb.astype(F32).reshape(nh * ndr, ndc), ((0, 0), (0, 128 - ndc)))
    vmask = valid.reshape(1, -1).astype(F32)
    toep = pl.pallas_call(_toeplitz_body, out_shape=_sds((nh * ndr, GRID_W * GRID_W), F32),
                          name="na_bias")(rp, onehot, vmask)
    toep = toep.reshape(nh, ndr, GRID_W, GRID_W)
    per_delta = []
    for delta in range(kh):
        lo = NA_KH - 1 - delta
        t = toep[:, lo:lo + kh].transpose(0, 2, 1, 3)
        per_delta.append(t.reshape(nh, GRID_W, kh * GRID_W))
    return jnp.stack(per_delta, axis=0)


def _attn_body(*refs, nparts, heads):
    q_refs, k_refs, v_ref, o_ref = refs[:nparts], refs[nparts:2 * nparts], refs[2 * nparts], refs[-1]
    s = None
    for qr, kr in zip(q_refs, k_refs):
        t = _qk(qr[0], kr[0])
        s = t if s is None else s + t
    o_ref[0] = _heads_to_lanes(_softmax_pv([s], [v_ref[0]])).astype(BF16)


def _attention(qs, ks, v, *, tq, q0, nq, tk, kblk, name):
    b, heads, _, dv = v.shape
    qspec = [pl.BlockSpec((1, heads, tq, q.shape[-1]), lambda b, i: (b, 0, q0 + i, 0)) for q in qs]
    kspec = [pl.BlockSpec((1, heads, tk, k.shape[-1]), lambda b, i: (b, 0, kblk, 0)) for k in ks]
    vspec = pl.BlockSpec((1, heads, tk, dv), lambda b, i: (b, 0, kblk, 0))
    return _call(
        functools.partial(_attn_body, nparts=len(qs), heads=heads), (b, nq),
        qspec + kspec + [vspec],
        pl.BlockSpec((1, tq, heads * dv), lambda b, i: (b, i, 0)),
        _sds((b, nq * tq, heads * dv), BF16), name=name)(*qs, *ks, v)


def _mla_q_body(z_ref, gc_ref, w_ref, gh_ref, ct_ref, st_ref, q_ref):
    cq = z_ref[0]
    cqn = (cq * lax.rsqrt(jnp.mean(cq * cq, axis=-1, keepdims=True) + NORM_EPS) * gc_ref[...]).astype(BF16)
    q = _dot(cqn, w_ref[...])
    gh = gh_ref[...]
    ct, st = ct_ref[...], st_ref[...]
    for h in range(MLA_HEADS):
        qh = q[:, h * 256:(h + 1) * 256]
        nope, pe, sw = qh[:, :128], qh[:, 128:192], qh[:, 192:256]
        ms = (jnp.sum(nope * nope, axis=-1, keepdims=True) + jnp.sum(pe * pe, axis=-1, keepdims=True)) \
            / (MLA_NOPE + MLA_ROPE)
        rinv = lax.rsqrt(ms + NORM_EPS)
        rot = (pe * rinv * gh[:, 128:192]) * ct + (sw * rinv * gh[:, 192:256]) * st
        qh = jnp.concatenate([nope * rinv * gh[:, :128], rot, jnp.zeros_like(rot)], axis=-1)
        q_ref[0, h] = (qh * MLA_SCALE).astype(BF16)


def _mla_kv_body(zc_ref, zp_ref, gc_ref, w_ref, gh_ref, ct_ref, st_ref, k_ref, v_ref):
    ckv = zc_ref[0]
    cn = (ckv * lax.rsqrt(jnp.mean(ckv * ckv, axis=-1, keepdims=True) + NORM_EPS) * gc_ref[...]).astype(BF16)
    kv = _dot(cn, w_ref[...])
    zp = zp_ref[0]
    pe, sw = zp[:, :64], zp[:, 64:128]
    pe2 = jnp.sum(pe * pe, axis=-1, keepdims=True)
    gh = gh_ref[...]
    ct, st = ct_ref[...], st_ref[...]
    for h in range(MLA_HEADS):
        nope = kv[:, h * 256:h * 256 + 128]
        ms = (jnp.sum(nope * nope, axis=-1, keepdims=True) + pe2) / (MLA_NOPE + MLA_ROPE)
        rinv = lax.rsqrt(ms + NORM_EPS)
        rot = (pe * rinv * gh[:, 128:192]) * ct + (sw * rinv * gh[:, 192:256]) * st
        k_ref[0, h] = jnp.concatenate([nope * rinv * gh[:, :128], rot, jnp.zeros_like(rot)], axis=-1).astype(BF16)
        v_ref[0, h] = kv[:, h * 256 + 128:(h + 1) * 256].astype(BF16)


def _mla_prep(z3, p, ct, st, tr):
    b, t, _ = z3.shape
    rope = pl.BlockSpec((tr, MLA_ROPE), lambda b, i: (i, 0))

    def hm(d):
        return pl.BlockSpec((1, MLA_HEADS, tr, d), lambda b, i: (b, 0, i, 0))

    def shp(d):
        return _sds((b, MLA_HEADS, t, d), BF16)

    q = _call(
        _mla_q_body, (b, t // tr),
        [pl.BlockSpec((1, tr, 512), lambda b, i: (b, i, ZC_CQ // 512)), _full(p['g_cq']), _full(p['w_uq']),
         _full(p['g_q']), rope, rope],
        hm(256), shp(256), name="mla_q_prep")(z3, p['g_cq'], p['w_uq'], p['g_q'], ct, st)
    k, v = _call(
        _mla_kv_body, (b, t // tr),
        [pl.BlockSpec((1, tr, 256), lambda b, i: (b, i, ZC_CKV // 256)),
         pl.BlockSpec((1, tr, 128), lambda b, i: (b, i, ZC_KPE // 128)),
         _full(p['g_ckv']), _full(p['w_ukv']), _full(p['g_k']), rope, rope],
        [hm(256), hm(128)], [shp(256), shp(128)], name="mla_kv_prep")(
            z3, z3, p['g_ckv'], p['w_ukv'], p['g_k'], ct, st)
    return q, k, v


def _rope_tables(l_tok, m_tok):
    half = MLA_ROPE // 4
    freqs = ROPE_THETA ** (-jnp.arange(half, dtype=F32) / half)
    pos = jnp.arange(l_tok)
    ar = (pos // GRID_W).astype(F32)[:, None] * freqs[None, :]
    ac = (pos % GRID_W).astype(F32)[:, None] * freqs[None, :]
    ct = jnp.concatenate([jnp.cos(ar), jnp.cos(ar), jnp.cos(ac), jnp.cos(ac)], axis=-1)
    st = jnp.concatenate([-jnp.sin(ar), jnp.sin(ar), -jnp.sin(ac), jnp.sin(ac)], axis=-1)
    ct = jnp.concatenate([ct, jnp.ones((m_tok, MLA_ROPE), F32)], axis=0)
    st = jnp.concatenate([st, jnp.zeros((m_tok, MLA_ROPE), F32)], axis=0)
    return ct, st


def _swap_halves(a):
    return jnp.concatenate([a[..., 16:32], a[..., 0:16], a[..., 48:64], a[..., 32:48]], axis=-1)


def _dft1_body(x_ref, bh_ref, bl_ref, o_ref):
    xh, xl = _split2(x_ref[0])
    r = _dot(xh, bh_ref[...]) + (_dot(xh, bl_ref[...]) + _dot(xl, bh_ref[...]))
    o_ref[0] = r.astype(BF16)


def _dft2_body(c_ref, s_ref, xc_ref, xs_ref, o_ref):
    o_ref[0] = (_dot(c_ref[...], xc_ref[0]) - _dot(s_ref[...], xs_ref[0])).astype(BF16)


def _dft_mats(n):
    k = jnp.arange(n, dtype=jnp.int32)
    ang = ((k[:, None] * k[None, :]) % n).astype(F32) * (2.0 * math.pi / n)
    s = 1.0 / math.sqrt(n)
    return jnp.cos(ang) * s, jnp.sin(ang) * s


def _fourier_tables(l_tok, m_tok, ctx_out):
    cw, sw = _dft_mats(BRANCH_W // 4)
    eye = jnp.eye(4, dtype=F32)
    bd = jnp.concatenate([jnp.kron(eye, cw), jnp.kron(eye, sw)], axis=1)
    tabs = {'bd': _split2(bd), 'lat': tuple(m.astype(BF16) for m in _dft_mats(l_tok))}
    if ctx_out:
        tabs['ctx'] = tuple(m.astype(BF16) for m in _dft_mats(m_tok))
    return tabs


def _fourier(z3, tabs, tr, l_tok, m_tok, ctx_out):
    b, t, _ = z3.shape
    bh, bl = tabs['bd']
    row = pl.BlockSpec((1, tr, 1024), lambda b, i: (b, i, 0))
    nrow = (t if ctx_out else l_tok) // tr
    xw = _call(
        _dft1_body, (b, nrow),
        [pl.BlockSpec((1, tr, 512), lambda b, i: (b, i, ZC_FN // 512)), _full(bh), _full(bl)],
        row, _sds((b, nrow * tr, 1024), BF16), name="dft_channels")(z3, bh, bl)

    def seq_dft(n, blk, mats, name):
        tm = _pick_tile(n, (512, 256, 128, 64, 32, 16, 8))
        mspec = pl.BlockSpec((tm, n), lambda b, i: (i, 0))
        xc = pl.BlockSpec((1, n, 512), lambda b, i: (b, blk, 0))
        xs = pl.BlockSpec((1, n, 512), lambda b, i: (b, blk, 1))
        return _call(
            _dft2_body, (b, n // tm), [mspec, mspec, xc, xs],
            pl.BlockSpec((1, tm, 512), lambda b, i: (b, i, 0)),
            _sds((b, n, 512), BF16), name=name)(*mats, xw, xw)

    y = seq_dft(l_tok, 0, tabs['lat'], "dft_seq_latent")
    yc = seq_dft(m_tok, l_tok // m_tok, tabs['ctx'], "dft_seq_ctx") if ctx_out else None
    return y, yc


def _rw_prep_body(zk_ref, zv_ref, zr_ref, zl_ref, zg_ref, halo_ref, mu_ref, kk_ref, ka_ref, w0_ref, a0_ref,
                  w2_ref, a2_ref, trf_ref, trb_ref, e_ref,
                  v_ref, r_ref, g_ref, cl_ref, kd_ref, b_ref, kx_ref, *, tr):
    row = lax.broadcasted_iota(jnp.int32, (tr, 1), 0)
    halo = halo_ref[0, 0]
    mu = mu_ref[...]

    def mix(x, lo, hi):
        xp = jnp.where(row == 0, halo[0:1, lo:hi], pltpu.roll(x, 1, 0))
        xn = jnp.where(row == tr - 1, halo[1:2, lo:hi], pltpu.roll(x, tr - 1, 0))
        return x + (0.5 * (xp + xn) - x) * mu[:, lo:hi]

    k = mix(zk_ref[0], 0, 512)
    v = mix(zv_ref[0], 512, 1024)
    r = mix(zr_ref[0], 1024, 1536)
    l4 = mix(zl_ref[0], 1536, 1664)
    g_ref[0] = mix(zg_ref[0], 1664, 1792)

    wcat = w0_ref[...] + _dot3(jnp.tanh(l4), w2_ref[...])
    acat = a0_ref[...] + _dot3(l4, a2_ref[...])
    kk = k * kk_ref[...]
    kkn = kk / jnp.maximum(jnp.sqrt(_head_sumsq(kk, e_ref[...])), 1e-12)
    for d, tri_ref in enumerate((trf_ref, trb_ref)):
        w = wcat[:, d * 512:(d + 1) * 512]
        lw = -math.exp(-0.5) * jax.nn.sigmoid(w)
        a = jax.nn.sigmoid(acat[:, d * 512:(d + 1) * 512])
        kd = k * (1.0 + (a - 1.0) * ka_ref[...])
        cl = _dot_sel(tri_ref[...], lw)
        bb = kkn * a
        kx = kkn * jnp.exp(-lw)
        for h in range(RW_HEADS):
            lo = h * RW_DH
            cl_ref[0, d, h] = cl[:, lo:lo + RW_DH]
            kd_ref[0, d, h] = kd[:, lo:lo + RW_DH]
            b_ref[0, d, h] = bb[:, lo:lo + RW_DH]
            kx_ref[0, d, h] = kx[:, lo:lo + RW_DH]
    for h in range(RW_HEADS):
        lo = h * RW_DH
        v_ref[0, h] = v[:, lo:lo + RW_DH]
        r_ref[0, h] = r[:, lo:lo + RW_DH]


def _rw_prep(z3, halo, p, tr):
    b, t, _ = z3.shape
    c = RW_CHUNK
    ti = jnp.arange(tr)
    same = (ti[:, None] // c) == (ti[None, :] // c)
    trf = (same & (ti[None, :] <= ti[:, None])).astype(BF16)
    trb = (same & (ti[None, :] >= ti[:, None])).astype(BF16)

    def zcol(off, w):
        return pl.BlockSpec((1, tr, w), lambda b, i: (b, i, off // w))

    hm = pl.BlockSpec((1, RW_HEADS, tr, RW_DH), lambda b, i: (b, 0, i, 0))
    hmd = pl.BlockSpec((1, 2, RW_HEADS, tr, RW_DH), lambda b, i: (b, 0, 0, i, 0))
    s1 = _sds((b, RW_HEADS, t, RW_DH), F32)
    s2 = _sds((b, 2, RW_HEADS, t, RW_DH), F32)
    consts = [p['rw_mu'], p['rw_k_k'], p['rw_k_a'], p['rw_w0'], p['rw_a0'], p['rw_w2'], p['rw_a2'], trf, trb,
              _head_ones(RW_W, RW_DH)]
    return _call(
        functools.partial(_rw_prep_body, tr=tr), (b, t // tr),
        [zcol(ZC_RWK, 512), zcol(ZC_RWK + 512, 512), zcol(ZC_RWK + 1024, 512), zcol(ZC_L4, 128), zcol(ZC_G, 128),
         pl.BlockSpec((1, 1, 2, 1792), lambda b, i: (b, i, 0, 0))] + [_full(a) for a in consts],
        [hm, hm, pl.BlockSpec((1, tr, 128), lambda b, i: (b, i, 0)), hmd, hmd, hmd, hmd],
        [s1, s1, _sds((b, t, 128), F32), s2, s2, s2, s2], name="rw_prep")(z3, z3, z3, z3, z3, halo, *consts)


RW_P_HI = 1
RW_P_LO = 1


def _rw_scan_tile(r_ref, v_ref, cl_ref, kd_ref, b_ref, kx_ref, y_ref, s_ref, *, tr, rev):
    c = RW_CHUNK
    nc = tr // c
    g = RW_HEADS * nc

    def ld(x):
        return x.reshape(g, c, RW_DH)

    r, v = ld(r_ref[0]), ld(v_ref[0])
    cl, kd, bb, kx = ld(cl_ref[0, 0]), ld(kd_ref[0, 0]), ld(b_ref[0, 0]), ld(kx_ref[0, 0])
    last = 0 if rev else c - 1
    ctot = cl[:, last:last + 1, :]
    e = jnp.exp(cl)
    ei = jnp.exp(-cl)
    ec = jnp.exp(ctot - cl)
    kkt, rt = kx * e, r * e
    kw, bw = kd * ei, bb * ei
    kc, bc = kd * ec, bb * ec
    a_cat = jnp.concatenate([kkt, rt], axis=1)
    pp = _bmm('gtd,gsd->gts', a_cat, jnp.concatenate([kw, bw], axis=1), RW_P_HI)
    ti = lax.broadcasted_iota(jnp.int32, (c, c), 0)
    si = lax.broadcasted_iota(jnp.int32, (c, c), 1)
    strict = (si > ti) if rev else (si < ti)
    incl = (si >= ti) if rev else (si <= ti)
    nmat = jnp.where(strict, pp[:, :c, :c], 0.0)
    x = -jnp.where(strict, pp[:, :c, c:], 0.0)
    ark = jnp.where(incl, pp[:, c:, :c], 0.0)
    arb = jnp.where(incl, pp[:, c:, c:], 0.0)
    eye_c = (ti == si).astype(F32)
    tm = eye_c + x
    xp = _bmm('gts,gsu->gtu', x, x, RW_P_HI)
    nsq = int(math.log2(c)) - 1
    for i in range(nsq):
        if i < nsq - 1:
            both = _bmm('gts,gsu->gtu', xp, jnp.concatenate([tm, xp], axis=2), RW_P_LO)
            tm, xp = tm + both[:, :, :c], both[:, :, c:]
        else:
            tm = tm + _bmm('gts,gsu->gtu', xp, tm, RW_P_LO)
    nav = _bmm('gts,gsd->gtd', jnp.concatenate([nmat, ark], axis=1), v, RW_P_HI)
    nv, arkv = nav[:, :c], nav[:, c:]
    ta = _bmm('gts,gsd->gtd', tm, jnp.concatenate([kkt, nv], axis=2), RW_P_HI)
    ata = _bmm('gts,gsd->gtd', arb, ta, RW_P_HI)
    a2 = rt - ata[:, :, :RW_DH]
    y0 = arkv - ata[:, :, RW_DH:]
    tb = _bmm('gtk,gtd->gkd', bc, ta, RW_P_HI)
    di = lax.broadcasted_iota(jnp.int32, (RW_DH, RW_DH), 0)
    dj = lax.broadcasted_iota(jnp.int32, (RW_DH, RW_DH), 1)
    gmt = jnp.where(di == dj, jnp.exp(ctot), 0.0) - tb[:, :, :RW_DH]
    hmt = _bmm('gtk,gtv->gkv', kc, v, RW_P_HI) - tb[:, :, RW_DH:]

    def per_chunk(x):
        return x.reshape(RW_HEADS, nc, x.shape[1], x.shape[2])

    a2, y0, gmt, hmt = per_chunk(a2), per_chunk(y0), per_chunk(gmt), per_chunk(hmt)
    st = s_ref[...]
    ys = [None] * nc
    for ci in (range(nc - 1, -1, -1) if rev else range(nc)):
        ys[ci] = _bmm('htk,hkv->htv', a2[:, ci], st, RW_P_HI) + y0[:, ci]
        st = _bmm('hke,hev->hkv', gmt[:, ci], st, RW_P_HI) + hmt[:, ci]
    s_ref[...] = st
    y_ref[0] = jnp.concatenate(ys, axis=1)


def _rw_scan_body(*refs, tr):
    fwd, bwd, (yf_ref, yb_ref, sf_ref, sb_ref) = refs[0:6], refs[6:12], refs[12:16]

    @pl.when(pl.program_id(1) == 0)
    def _():
        sf_ref[...] = jnp.zeros_like(sf_ref)
        sb_ref[...] = jnp.zeros_like(sb_ref)

    _rw_scan_tile(*fwd, yf_ref, sf_ref, tr=tr, rev=False)
    _rw_scan_tile(*bwd, yb_ref, sb_ref, tr=tr, rev=True)


def _rw_scan(r, v, cl, kd, bb, kx, tr, nl, nm):
    b, _, t, _ = r.shape
    nt = nl + nm
    tile_f = lambda j: jnp.where(j < nm, nl + j, j - nm)
    tile_b = lambda j: nt - 1 - j

    def specs(tile, d):
        hm = pl.BlockSpec((1, RW_HEADS, tr, RW_DH), lambda b, j: (b, 0, tile(j), 0))
        hmd = pl.BlockSpec((1, 1, RW_HEADS, tr, RW_DH), lambda b, j: (b, d, 0, tile(j), 0))
        return hm, hmd

    hf, hfd = specs(tile_f, 0)
    hb, hbd = specs(tile_b, 1)
    shp = _sds((b, RW_HEADS, t, RW_DH), F32)
    state = pltpu.VMEM((RW_HEADS, RW_DH, RW_DH), F32)
    return _call(
        functools.partial(_rw_scan_body, tr=tr), (b, nt),
        [hf, hf, hfd, hfd, hfd, hfd, hb, hb, hbd, hbd, hbd, hbd], [hf, hb], [shp, shp],
        scratch=[state, state], sem=("parallel", "arbitrary"),
        name="rw_scan")(r, v, cl, kd, bb, kx, r, v, cl, kd, bb, kx)


def _rw_out_body(yf_ref, yb_ref, r_ref, v_ref, kd_ref, g_ref, lnw_ref, lnb_ref, rk_ref, g2_ref, o_ref):
    y = yf_ref[0] + yb_ref[0]
    mu = jnp.mean(y, axis=-1, keepdims=True)
    var = jnp.mean(jnp.square(y - mu), axis=-1, keepdims=True)
    yn = (y - mu) * lax.rsqrt(var + RW_GN_EPS) * lnw_ref[...] + lnb_ref[...]
    ksum = kd_ref[0, 0] + kd_ref[0, 1]
    bonus = jnp.sum(r_ref[0] * ksum * rk_ref[...], axis=-1, keepdims=True) * v_ref[0]
    o = yn + bonus
    o = jnp.concatenate([o[h] for h in range(RW_HEADS)], axis=-1)
    gate = _dot3(jax.nn.sigmoid(g_ref[0]), g2_ref[...])
    o_ref[0] = (o * gate).astype(BF16)


def _rw_out(yf, yb, r, v, kd, g, p, tr, nrow):
    b = r.shape[0]
    hm = pl.BlockSpec((1, RW_HEADS, tr, RW_DH), lambda b, i: (b, 0, i, 0))
    consts = [p['rw_ln_w'], p['rw_ln_b'], p['rw_r_k'], p['rw_g2']]
    return _call(
        _rw_out_body, (b, nrow),
        [hm, hm, hm, hm, pl.BlockSpec((1, 2, RW_HEADS, tr, RW_DH), lambda b, i: (b, 0, 0, i, 0)),
         pl.BlockSpec((1, tr, 128), lambda b, i: (b, i, 0))] + [_full(a) for a in consts],
        pl.BlockSpec((1, tr, RW_W), lambda b, i: (b, i, 0)),
        _sds((b, nrow * tr, RW_W), BF16), name="rw_out")(yf, yb, r, v, kd, g, *consts)


GATE_COL0 = KEY_COLS + 2144


def _merge_body(h_ref, y0_ref, y1_ref, y2_ref, y3_ref, g0_ref, g1_ref, g2_ref, g3_ref, wb_ref, o_ref):
    h = h_ref[0]
    acc = None
    for i, (y_ref, wg_ref) in enumerate(zip((y0_ref, y1_ref, y2_ref, y3_ref), (g0_ref, g1_ref, g2_ref, g3_ref))):
        t = jax.nn.sigmoid(_dot(h, wg_ref[...])) * _dot(y_ref[0], wb_ref[i])
        acc = t if acc is None else acc + t
    o_ref[0] = acc.astype(BF16)


def _merge(h, ys, wg, wbr, rows):
    b, _, d = h.shape
    tn = 512
    nn = d // tn
    tm = _pick_tile(rows, (1024, 768, 512, 256, 128, 64))
    row = lambda w: pl.BlockSpec((1, tm, w), lambda n, b, i: (b, i, 0))
    gspecs = [pl.BlockSpec((d, tn), functools.partial(lambda n, b, i, k: (0, k * nn + n), k=k))
              for k in range(N_BRANCH)]
    return _call(
        _merge_body, (nn, b, rows // tm),
        [row(d)] + [row(BRANCH_W)] * 4 + gspecs + [pl.BlockSpec((N_BRANCH, BRANCH_W, tn), lambda n, b, i: (0, 0, n))],
        pl.BlockSpec((1, tm, tn), lambda n, b, i: (b, i, n)),
        _sds((b, rows, d), BF16), name="merge")(h, *ys, wg, wg, wg, wg, wbr)


def _outproj_body(a_ref, w_ref, x_ref, md_ref, o_ref, *, tm, l_tok):
    row = pl.program_id(1) * tm + lax.broadcasted_iota(jnp.int32, (tm, 1), 0)
    gate = jnp.where(row < l_tok, md_ref[0, 0, 2:3, :], md_ref[0, 1, 2:3, :])
    o_ref[0] = x_ref[0] + gate * _dot(a_ref[0], w_ref[...])


def _outproj(acc, w_out, xs, md, l_tok):
    b, rows, d = acc.shape
    tm = _pick_tile(rows, (768, 512, 256, 128, 64))
    row = pl.BlockSpec((1, tm, d), lambda b, i: (b, i, 0))
    return _call(
        functools.partial(_outproj_body, tm=tm, l_tok=l_tok), (b, rows // tm),
        [row, _full(w_out), row, pl.BlockSpec((1, 2, 6, d), lambda b, i: (b, 0, 0, 0))], row,
        _sds((b, rows, d), F32), name="out_proj")(acc, w_out, xs, md)


SLOT_CHUNKS = 4
TOKEN_BLOCK = 256


def _select_body(lg_ref, slot_ref, aff_ref, slotc_ref, kr_ref, *, n, cap):
    lg = lg_ref[0]
    lane = lax.broadcasted_iota(jnp.int32, lg.shape, 1)
    lg = jnp.where(lane < N_EXPERTS, lg, NEG_INF)
    ex = jnp.exp(lg - lg.max(axis=-1, keepdims=True))
    aff = ex / ex.sum(axis=-1, keepdims=True)
    aff_t = aff.T[:N_EXPERTS]

    def count_ge(t):
        return jnp.sum((aff_t >= t).astype(F32), axis=1, keepdims=True)

    tiny = 2.0 ** -126
    lo = jnp.full((N_EXPERTS, 1), tiny, F32)
    for sh in (64, 32, 16, 8, 4, 2, 1):
        cand = lo * (2.0 ** sh)
        lo = jnp.where(count_ge(cand) >= cap, cand, lo)
    hi = lo * 2.0
    below = count_ge(tiny) < cap
    lo = jnp.where(below, 0.0, lo)
    hi = jnp.where(below, tiny, hi)
    for _ in range(40):
        mid = lo + (hi - lo) * 0.5
        ok = count_ge(mid) >= cap
        lo = jnp.where(ok, mid, lo)
        hi = jnp.where(ok, hi, mid)
    gt = aff_t >= hi
    eq = (aff_t >= lo) & (aff_t < hi)
    need = cap - jnp.sum(gt.astype(F32), axis=1, keepdims=True)
    both = jnp.concatenate([gt.astype(BF16), eq.astype(BF16)], axis=0)
    cw = min(n, 512)
    pre = []
    for cb in range(n // cw):
        tp = lax.broadcasted_iota(jnp.int32, (n, cw), 0)
        tt = lax.broadcasted_iota(jnp.int32, (n, cw), 1) + cb * cw
        pre.append(_dot(both, (tp < tt).astype(BF16)))
    pre = jnp.concatenate(pre, axis=1) if len(pre) > 1 else pre[0]
    pre_gt, pre_eq = pre[:N_EXPERTS], pre[N_EXPERTS:]
    sel = gt | (eq & (pre_eq < need))
    slot = jnp.where(sel, pre_gt + jnp.minimum(pre_eq, need), -1.0)
    slot_ref[0] = slot.astype(jnp.int32)
    aff_ref[0] = aff_t
    pad = jnp.full((128 - N_EXPERTS, n), -1.0, F32)
    slotc_ref[0] = jnp.concatenate([slot, pad], axis=0).T.astype(jnp.int32)
    tok = lax.broadcasted_iota(jnp.int32, slot.shape, 1).astype(F32)
    sc = cap // SLOT_CHUNKS if cap % SLOT_CHUNKS == 0 else cap
    los, his = [], []
    for c in range(SLOT_CHUNKS):
        inc = (slot >= c * sc) & (slot < (c + 1) * sc)
        los.append(jnp.min(jnp.where(inc, tok, float(n - 1)), axis=1, keepdims=True))
        his.append(jnp.max(jnp.where(inc, tok, 0.0), axis=1, keepdims=True))
    kr = jnp.concatenate(los + his, axis=1) * (1.0 / min(n, TOKEN_BLOCK))
    kr_ref[0] = kr.astype(jnp.int32)


def _select(logits, n, blk, cap, name):
    b = logits.shape[0]
    er = pl.BlockSpec((1, N_EXPERTS, n), lambda b: (b, 0, 0))
    return _call(
        functools.partial(_select_body, n=n, cap=cap), (b,),
        [pl.BlockSpec((1, n, 128), lambda b: (b, blk, 0))],
        [er, er, pl.BlockSpec((1, n, 128), lambda b: (b, 0, 0)),
         pl.BlockSpec((1, N_EXPERTS, 2 * SLOT_CHUNKS), lambda b: (b, 0, 0))],
        [_sds((b, N_EXPERTS, n), jnp.int32), _sds((b, N_EXPERTS, n), F32), _sds((b, n, 128), jnp.int32),
         _sds((b, N_EXPERTS, 2 * SLOT_CHUNKS), jnp.int32)],
        name=name)(logits)


def _ffn_body(h_ref, slot_ref, aff_ref, w1_ref, w3_ref, w2_ref, o_ref, *, grp, n, cap):
    xs, gates = [], []
    jrow = lax.broadcasted_iota(jnp.int32, (cap, n), 0)
    for gi in range(grp):
        pick = jrow == slot_ref[gi, 0]
        xs.append(_dot(pick.astype(BF16), h_ref[gi]).astype(BF16))
        gates.append(jnp.sum(jnp.where(pick, aff_ref[gi, 0], 0.0), axis=1, keepdims=True))
    xe = jnp.concatenate(xs, axis=0) if grp > 1 else xs[0]
    gate = jnp.concatenate(gates, axis=0) if grp > 1 else gates[0]
    a = _dot(xe, w1_ref[0])
    u = _dot(xe, w3_ref[0])
    hm = (a * jax.nn.sigmoid(a) * u).astype(BF16)
    ye = (_dot(hm, w2_ref[0]) * gate).astype(BF16)
    for gi in range(grp):
        o_ref[0, gi] = ye[gi * cap:(gi + 1) * cap]


def _ffn_ranged_body(kr_ref, h_ref, slot_ref, aff_ref, w1_ref, w3_ref, w2_ref, o_ref, xe_ref, gate_ref, *, cap, tb):
    sc = cap // SLOT_CHUNKS
    base = (pl.program_id(1) * N_EXPERTS + pl.program_id(0)) * (2 * SLOT_CHUNKS)
    xe_ref[...] = jnp.zeros_like(xe_ref)
    gate_ref[...] = jnp.zeros_like(gate_ref)
    for c in range(SLOT_CHUNKS):
        jrow = lax.broadcasted_iota(jnp.int32, (sc, tb), 0) + c * sc

        def block(kb, carry, c=c, jrow=jrow):
            pick = jrow == slot_ref[0, 0, kb]
            rows = h_ref[0, pl.ds(pl.multiple_of(kb * tb, tb), tb), :]
            xe_ref[c * sc:(c + 1) * sc, :] += _dot(pick.astype(BF16), rows)
            gate_ref[c * sc:(c + 1) * sc, :] += jnp.sum(jnp.where(pick, aff_ref[0, 0, kb], 0.0), axis=1, keepdims=True)
            return carry

        lax.fori_loop(kr_ref[base + c], kr_ref[base + SLOT_CHUNKS + c] + 1, block, 0)
    xe = xe_ref[...].astype(BF16)
    a = _dot(xe, w1_ref[0])
    u = _dot(xe, w3_ref[0])
    hm = (a * jax.nn.sigmoid(a) * u).astype(BF16)
    o_ref[0, 0] = (_dot(hm, w2_ref[0]) * gate_ref[:, 0:1]).astype(BF16)


def _moe_ffn_ranged(h2, slot, aff, kr, w1, w3, w2, n, cap, name):
    b, _, d = h2.shape
    ff = w1.shape[-1]
    tb = TOKEN_BLOCK
    nb = n // tb
    sr = pl.BlockSpec((1, 1, nb, 1, tb), lambda e, g, kr: (g, e, 0, 0, 0))
    grid_spec = pltpu.PrefetchScalarGridSpec(
        num_scalar_prefetch=1, grid=(N_EXPERTS, b),
        in_specs=[pl.BlockSpec((1, n, d), lambda e, g, kr: (g, 0, 0)), sr, sr,
                  pl.BlockSpec((1, d, ff), lambda e, g, kr: (e, 0, 0)),
                  pl.BlockSpec((1, d, ff), lambda e, g, kr: (e, 0, 0)),
                  pl.BlockSpec((1, ff, d), lambda e, g, kr: (e, 0, 0))],
        out_specs=pl.BlockSpec((1, 1, cap, d), lambda e, g, kr: (e, g, 0, 0)),
        scratch_shapes=[pltpu.VMEM((cap, d), F32), pltpu.VMEM((cap, 128), F32)])
    return pl.pallas_call(
        functools.partial(_ffn_ranged_body, cap=cap, tb=tb), grid_spec=grid_spec,
        out_shape=_sds((N_EXPERTS, b, cap, d), BF16), name=name,
        compiler_params=pltpu.CompilerParams(dimension_semantics=("parallel", "parallel"),
                                             vmem_limit_bytes=VMEM_LIMIT))(
            kr.reshape(-1), h2, slot.reshape(b, N_EXPERTS, nb, 1, tb), aff.reshape(b, N_EXPERTS, nb, 1, tb),
            w1, w3, w2)


def _moe_ffn(h2, slot, aff, w1, w3, w2, n, blk, cap, grp, name):
    b, _, d = h2.shape
    ff = w1.shape[-1]
    sr = pl.BlockSpec((grp, 1, 1, n), lambda e, g: (g, e, 0, 0))
    return _call(
        functools.partial(_ffn_body, grp=grp, n=n, cap=cap), (N_EXPERTS, b // grp),
        [pl.BlockSpec((grp, n, d), lambda e, g: (g, blk, 0)), sr, sr,
         pl.BlockSpec((1, d, ff), lambda e, g: (e, 0, 0)), pl.BlockSpec((1, d, ff), lambda e, g: (e, 0, 0)),
         pl.BlockSpec((1, ff, d), lambda e, g: (e, 0, 0))],
        pl.BlockSpec((1, grp, cap, d), lambda e, g: (e, g, 0, 0)),
        _sds((N_EXPERTS, b, cap, d), BF16), name=name)(
            h2, slot.reshape(b, N_EXPERTS, 1, n), aff.reshape(b, N_EXPERTS, 1, n), w1, w3, w2)


def _combine_body(sc_ref, ye_ref, x_ref, md_ref, *rest, cap):
    o_ref = rest[-1]
    sc = sc_ref[0]
    jj = lax.broadcasted_iota(jnp.int32, (sc.shape[0], cap), 1)
    acc = None
    for e in range(N_EXPERTS):
        put = (sc[:, e:e + 1] == jj).astype(BF16)
        t = _dot(put, ye_ref[e, 0])
        acc = t if acc is None else acc + t
    o_ref[0] = x_ref[0] + md_ref[0, 0, 5:6, :] * acc


def _moe_combine(slotc, ye, x1, md, n, blk_rows, is_ctx, cap, name, out_rows, prev=None):
    b, _, d = x1.shape
    tm = _pick_tile(n, (512, 256, 128, 64, 32, 16, 8))
    off = blk_rows // tm
    extra_specs, extra_args, aliases = [], [], None
    if prev is not None:
        extra_specs, extra_args, aliases = [pl.BlockSpec(memory_space=pl.ANY)], [prev], {4: 0}
    return _call(
        functools.partial(_combine_body, cap=cap), (b, n // tm),
        [pl.BlockSpec((1, tm, 128), lambda b, i: (b, i, 0)),
         pl.BlockSpec((N_EXPERTS, 1, cap, d), lambda b, i: (0, b, 0, 0)),
         pl.BlockSpec((1, tm, d), lambda b, i: (b, off + i, 0)),
         pl.BlockSpec((1, 1, 6, d), lambda b, i: (b, 1 if is_ctx else 0, 0, 0))] + extra_specs,
        pl.BlockSpec((1, tm, d), lambda b, i: (b, off + i, 0)),
        _sds((b, out_rows, d), F32), name=name, aliases=aliases)(slotc, ye, x1, md, *extra_args)


def _moe(x1, h2, logits, md, w1, w3, w2, l_tok, m_tok, ctx_out):
    b = x1.shape[0]
    cap_l = CAPACITY_FACTOR * l_tok // N_EXPERTS
    slot, aff, slotc, kr = _select(logits, l_tok, 0, cap_l, "moe_select_latent")
    if l_tok % TOKEN_BLOCK == 0 and cap_l % (16 * SLOT_CHUNKS) == 0:
        ye = _moe_ffn_ranged(h2, slot, aff, kr, w1, w3, w2, l_tok, cap_l, "moe_ffn_latent")
    else:
        ye = _moe_ffn(h2, slot, aff, w1, w3, w2, l_tok, 0, cap_l, 1, "moe_ffn_latent")
    out_rows = l_tok + m_tok if ctx_out else l_tok
    x2 = _moe_combine(slotc, ye, x1, md, l_tok, 0, False, cap_l, "moe_combine_latent", out_rows)
    if not ctx_out:
        return x2
    cap_c = CAPACITY_FACTOR * m_tok // N_EXPERTS
    blk = l_tok // m_tok
    slot, aff, slotc, _ = _select(logits, m_tok, blk, cap_c, "moe_select_ctx")
    ye = _moe_ffn(h2, slot, aff, w1, w3, w2, m_tok, blk, cap_c, b, "moe_ffn_ctx")
    return _moe_combine(slotc, ye, x1, md, m_tok, l_tok, True, cap_c, "moe_combine_ctx", out_rows, prev=x2)


def _layer_params(l, w_in, na_q_norm, na_k_norm, na_rpb, mla_cq_norm, mla_ckv_norm, mla_w_uq, mla_w_ukv,
                  mla_q_norm, mla_k_norm, rw_mu_ks, rw_mu_qs, rw_w0, rw_w2, rw_a0, rw_a2, rw_g2, rw_k_k, rw_k_a,
                  rw_r_k, rw_ln_w, rw_ln_b, w_br, w_out, rows):
    w = w_in[l]
    d = w.shape[0]
    kq = KEY_COLS
    kpe = w[:, 1280:1344]
    zpad = lambda n: jnp.zeros((d, n), w.dtype)
    wz = jnp.concatenate([
        w[:, kq:kq + 512], w[:, 0:512], w[:, 512:1024],
        w[:, kq + 512:kq + 1024],
        w[:, 1344:1856], w[:, 1856:2368], w[:, kq + 1024:kq + 1536],
        w[:, kq + 1632:kq + 2144],
        w[:, 1024:1280],
        kpe, _swap_halves(kpe),
        w[:, 2368:2496],
        w[:, kq + 1536:kq + 1632], zpad(32)], axis=1).astype(BF16)
    assert wz.shape[1] == NZ
    wg = w[:, GATE_COL0:].astype(BF16)

    def head_gain(g):
        return jnp.concatenate([g[:128], g[128:], _swap_halves(g[128:])])[None, :]

    wq = mla_w_uq[l].reshape(MLA_QLORA, MLA_HEADS, MLA_NOPE + MLA_ROPE)
    wq = jnp.concatenate([wq, _swap_halves(wq[:, :, MLA_NOPE:])], axis=-1).reshape(MLA_QLORA, MLA_HEADS * 256)
    w2cat = jnp.zeros((128, 1024), F32).at[0:32, 0:512].set(rw_w2[l, 0]).at[32:64, 512:].set(rw_w2[l, 1])
    a2cat = jnp.zeros((128, 1024), F32).at[64:96, 0:512].set(rw_a2[l, 0]).at[96:128, 512:].set(rw_a2[l, 1])
    mu = jnp.concatenate([rw_mu_ks[l][:1024], rw_mu_qs[l][:512], rw_mu_ks[l][1024:], rw_mu_qs[l][512:],
                          jnp.zeros((32,), F32)])[None, :]
    hd = lambda a: a.reshape(RW_HEADS, 1, RW_DH)
    return {
        'wz': wz, 'wg': wg,
        'na_gq': na_q_norm[l][None, :], 'na_gk': na_k_norm[l][None, :],
        'na_bias': _na_bias_table(na_rpb[l], rows),
        'g_cq': mla_cq_norm[l][None, :], 'g_ckv': mla_ckv_norm[l][None, :],
        'w_uq': wq.astype(BF16), 'w_ukv': mla_w_ukv[l].astype(BF16),
        'g_q': head_gain(mla_q_norm[l]), 'g_k': head_gain(mla_k_norm[l]),
        'rw_mu': mu, 'rw_k_k': rw_k_k[l][None, :], 'rw_k_a': rw_k_a[l][None, :],
        'rw_w0': rw_w0[l].reshape(1, 1024), 'rw_a0': rw_a0[l].reshape(1, 1024), 'rw_w2': w2cat, 'rw_a2': a2cat,
        'rw_ln_w': hd(rw_ln_w[l]), 'rw_ln_b': hd(rw_ln_b[l]), 'rw_r_k': hd(rw_r_k[l]),
        'rw_g2': jnp.concatenate([rw_g2[l], jnp.zeros((32, RW_W), F32)], axis=0),
        'w_br': w_br[l].astype(BF16), 'w_out': w_out[l].astype(BF16),
    }


def _rw_halo(z3, tr, nl):
    b, t, _ = z3.shape
    nt = t // tr
    z4 = z3.reshape(b, nt, tr, z3.shape[-1])
    pick = lambda a: jnp.concatenate([a[..., ZC_RWK:ZC_RWK + 1536], a[..., ZC_L4:ZC_L4 + 256]], axis=-1)
    first, last = pick(z4[:, :, 0, :]), pick(z4[:, :, tr - 1, :])
    zero = jnp.zeros_like(first[:, :1])
    prev = jnp.concatenate([zero, last[:, :-1]], axis=1)
    nxt = jnp.concatenate([first[:, 1:], zero], axis=1)
    tile = jnp.arange(nt)[None, :, None]
    prev = jnp.where(tile == nl, 0.0, prev)
    nxt = jnp.where(tile == nl - 1, 0.0, nxt)
    return jnp.stack([prev, nxt], axis=2)


def _layer(xs, md, p, g1n, g2n, wr, w1, w3, w2, tabs, l_tok, m_tok, tr, ctx_out):
    b, t, d = xs.shape
    nl, nm = l_tok // tr, m_tok // tr
    nrow = (nl + nm) if ctx_out else nl
    blk = l_tok // m_tok

    h = _norm1(xs, g1n, md, tr, nl)
    z3 = _matmul(h.reshape(b * t, d), p['wz'], F32).reshape(b, t, NZ)

    nq, nk, nv = _na_prep(z3, p['na_gq'], p['na_gk'], tr)
    y_na = _na_latent(nq, nk, nv, p['na_bias'], l_tok, m_tok)
    mq, mk, mv = _mla_prep(z3, p, tabs['rope_c'], tabs['rope_s'], tr)
    tq = _pick_tile(l_tok, (256, 128, 64))
    y_mla = _attention([mq], [mk], mv, tq=tq, q0=0, nq=l_tok // tq, tk=t, kblk=0, name="mla_latent")
    rv, rr, rg, cl, kd, bb, kx = _rw_prep(z3, _rw_halo(z3, tr, nl), p, tr)
    yf, yb = _rw_scan(rr, rv, cl, kd, bb, kx, tr, nl, nm)
    y_rw = _rw_out(yf, yb, rr, rv, kd, rg, p, tr, nrow)
    y_fn, yc_fn = _fourier(z3, tabs, tr, l_tok, m_tok, ctx_out)

    if ctx_out:
        yc_na = _attention([nq], [nk], nv, tq=m_tok, q0=blk, nq=1, tk=m_tok, kblk=blk, name="na_ctx")
        yc_mla = _attention([mq], [mk], mv, tq=m_tok, q0=blk, nq=1, tk=m_tok, kblk=blk, name="mla_ctx")
        y_fn = jnp.concatenate([y_fn, yc_fn], axis=1)
        y_na = jnp.concatenate([y_na, yc_na], axis=1)
        y_mla = jnp.concatenate([y_mla, yc_mla], axis=1)

    acc = _merge(h, (y_fn, y_na, y_mla, y_rw), p['wg'], p['w_br'], nrow * tr)
    x1 = _outproj(acc, p['w_out'], xs, md, l_tok)
    h2, logits = _norm2_router(x1, g2n, md, wr, tr, nl)
    return _moe(x1, h2, logits, md, w1, w3, w2, l_tok, m_tok, ctx_out)


def kernel(x, c, ctx, c_ctx, ada_w, ada_b, norm1_g, norm2_g, w_in, na_q_norm, na_k_norm, na_rpb, mla_cq_norm, mla_ckv_norm, mla_w_uq, mla_w_ukv, mla_q_norm, mla_k_norm, rw_mu_ks, rw_mu_qs, rw_w0, rw_w2, rw_a0, rw_a2, rw_g2, rw_k_k, rw_k_a, rw_r_k, rw_ln_w, rw_ln_b, w_br, w_out, moe_router, moe_w1, moe_w3, moe_w2):
    b, l_tok, d = x.shape
    m_tok = ctx.shape[1]
    depth = ada_w.shape[0]
    tr = min(m_tok, 256)
    assert l_tok % tr == 0 and m_tok % tr == 0 and l_tok % m_tok == 0 and tr % RW_CHUNK == 0
    assert l_tok % GRID_W == 0 and m_tok % 128 == 0

    nr = -(-(b + 1) // 8) * 8
    cc = jnp.concatenate([c, c_ctx[None, :], jnp.zeros((nr - b - 1, d), F32)], axis=0)
    mod = _modulation(cc, ada_w, ada_b)
    tabs = {}
    tabs['rope_c'], tabs['rope_s'] = _rope_tables(l_tok, m_tok)

    xs = jnp.concatenate([x, ctx], axis=1)
    for l in range(depth):
        ctx_out = l < depth - 1
        tabs.update(_fourier_tables(l_tok, m_tok, ctx_out))
        p = _layer_params(l, w_in, na_q_norm, na_k_norm, na_rpb, mla_cq_norm, mla_ckv_norm, mla_w_uq, mla_w_ukv,
                          mla_q_norm, mla_k_norm, rw_mu_ks, rw_mu_qs, rw_w0, rw_w2, rw_a0, rw_a2, rw_g2, rw_k_k,
                          rw_k_a, rw_r_k, rw_ln_w, rw_ln_b, w_br, w_out, l_tok // GRID_W)
        md = jnp.stack([mod[l, :b], jnp.broadcast_to(mod[l, b], (b, 6 * d))], axis=1).reshape(b, 2, 6, d)
        wr = jnp.concatenate([moe_router[l], jnp.zeros((d, 128 - N_EXPERTS), F32)], axis=1)
        xs = _layer(xs, md, p, norm1_g[l][None, :], norm2_g[l][None, :], wr,
                         moe_w1[l].astype(BF16), moe_w3[l].astype(BF16), moe_w2[l].astype(BF16),
                         tabs, l_tok, m_tok, tr, ctx_out)
    return xs
```

```python
import functools
import math

import jax
import jax.numpy as jnp
from jax import lax
from jax.experimental import pallas as pl
from jax.experimental.pallas import tpu as pltpu

F32 = jnp.float32
BF16 = jnp.bfloat16

GRID_W = 64
N_BRANCH = 4
BRANCH_W = 512
NA_HEADS, NA_DH, NA_KH, NA_KW = 8, 64, 8, 16
NA_SCALE = NA_DH ** -0.5
MLA_HEADS, MLA_NOPE, MLA_ROPE, MLA_V = 4, 128, 64, 128
MLA_QLORA, MLA_KVLORA = 512, 256
MLA_SCALE = (MLA_NOPE + MLA_ROPE) ** -0.5
RW_HEADS, RW_DH, RW_W = 8, 64, 512
RW_LORA, RW_GATE_LORA = 32, 96
RW_GN_EPS = 64e-5
N_EXPERTS = 16
CAPACITY_FACTOR = 2
ROPE_THETA = 10000.0
NORM_EPS = 1e-6
NEG_INF = -1e30
RW_CHUNK = 64
KEY_COLS = 2496
NZ = 4736
ZC_NA = 0
ZC_CQ = 1536
ZC_RWK = 2048
ZC_FN = 3584
ZC_CKV = 4096
ZC_KPE = 4352
ZC_L4 = 4480
ZC_G = 4608

VMEM_LIMIT = 56 * 2 ** 20


def _call(body, grid, in_specs, out_specs, out_shape, scratch=(), sem=None, name=None, aliases=None):
    return pl.pallas_call(
        body, grid=grid, in_specs=in_specs, out_specs=out_specs, out_shape=out_shape,
        scratch_shapes=list(scratch), name=name, input_output_aliases=aliases or {},
        compiler_params=pltpu.CompilerParams(
            dimension_semantics=sem or ("parallel",) * len(grid), vmem_limit_bytes=VMEM_LIMIT))


def _full(a):
    nd = a.ndim
    return pl.BlockSpec(a.shape, lambda *_: (0,) * nd)


def _sds(shape, dtype):
    return jax.ShapeDtypeStruct(shape, dtype)


def _dot(a, b):
    return jnp.dot(a, b, preferred_element_type=F32)


def _dot_nt(a, b):
    return lax.dot_general(a, b, (((1,), (1,)), ((), ())), preferred_element_type=F32)


def _split2(x):
    hi = x.astype(BF16)
    return hi, (x - hi.astype(F32)).astype(BF16)


def _split3(x):
    hi = x.astype(BF16)
    r = x - hi.astype(F32)
    mid = r.astype(BF16)
    return hi, mid, (r - mid.astype(F32)).astype(BF16)


def _dot3(a, b):
    ah, al = _split2(a)
    bh, bl = _split2(b)
    return _dot(ah, bh) + (_dot(ah, bl) + _dot(al, bh))


def _dot_sel(sel_bf16, x):
    h, m, l = _split3(x)
    return _dot(sel_bf16, h) + (_dot(sel_bf16, m) + _dot(sel_bf16, l))


def _bmm(spec, a, b, passes):
    e = functools.partial(jnp.einsum, spec, preferred_element_type=F32)
    if passes == 1:
        return e(a.astype(BF16), b.astype(BF16))
    ah, al = _split2(a)
    bh, bl = _split2(b)
    return e(ah, bh) + (e(ah, bl) + e(al, bh))


def _pick_tile(n, cands):
    for c in cands:
        if n % c == 0:
            return c
    raise ValueError(f"no tile for {n}")


def _mod_body(c_ref, w_ref, b_ref, o_ref):
    cc = c_ref[...]
    s = cc * jax.nn.sigmoid(cc)
    o_ref[0] = _dot3(s, w_ref[0]) + b_ref[0]


def _modulation(cc, ada_w, ada_b):
    depth, d, n6 = ada_w.shape
    r = cc.shape[0]
    tn = 1024
    return _call(
        _mod_body, (depth, n6 // tn),
        [pl.BlockSpec((r, d), lambda l, n: (0, 0)),
         pl.BlockSpec((1, d, tn), lambda l, n: (l, 0, n)),
         pl.BlockSpec((1, 1, tn), lambda l, n: (l, 0, n))],
        pl.BlockSpec((1, r, tn), lambda l, n: (l, 0, n)),
        _sds((depth, r, n6), F32), name="adaln_mod")(cc, ada_w, ada_b.reshape(depth, 1, n6))


def _modnorm(x, g, shift, scale):
    y = x * lax.rsqrt(jnp.mean(x * x, axis=-1, keepdims=True) + NORM_EPS)
    return (y * g) * (1.0 + scale) + shift


def _norm1_body(x_ref, g_ref, md_ref, h_ref):
    h = _modnorm(x_ref[0], g_ref[...], md_ref[0, 0, 0:1, :], md_ref[0, 0, 1:2, :])
    h_ref[0] = h.astype(BF16)


def _norm2_body(x_ref, g_ref, md_ref, wr_ref, h_ref, lg_ref):
    h = _modnorm(x_ref[0], g_ref[...], md_ref[0, 0, 3:4, :], md_ref[0, 0, 4:5, :])
    h_ref[0] = h.astype(BF16)
    lg_ref[0] = _dot3(h, wr_ref[...])


def _md_spec(d, nl):
    return pl.BlockSpec((1, 1, 6, d), lambda b, i: (b, i // nl, 0, 0))


def _norm1(xs, g, md, tr, nl):
    b, t, d = xs.shape
    return _call(
        _norm1_body, (b, t // tr),
        [pl.BlockSpec((1, tr, d), lambda b, i: (b, i, 0)), _full(g), _md_spec(d, nl)],
        pl.BlockSpec((1, tr, d), lambda b, i: (b, i, 0)),
        _sds((b, t, d), BF16), name="norm1")(xs, g, md)


def _norm2_router(x1, g, md, wr, tr, nl):
    b, t, d = x1.shape
    return _call(
        _norm2_body, (b, t // tr),
        [pl.BlockSpec((1, tr, d), lambda b, i: (b, i, 0)), _full(g), _md_spec(d, nl), _full(wr)],
        [pl.BlockSpec((1, tr, d), lambda b, i: (b, i, 0)), pl.BlockSpec((1, tr, 128), lambda b, i: (b, i, 0))],
        [_sds((b, t, d), BF16), _sds((b, t, 128), F32)], name="norm2_router")(x1, g, md, wr)


def _mm_body(a_ref, w_ref, o_ref):
    o_ref[...] = _dot(a_ref[...], w_ref[...]).astype(o_ref.dtype)


def _matmul(a, w, out_dtype):
    r, k = a.shape
    n = w.shape[1]
    tm = _pick_tile(r, (512, 384, 256, 128, 64, 32, 16, 8))
    return _call(
        _mm_body, (r // tm,),
        [pl.BlockSpec((tm, k), lambda i: (i, 0)),
         pl.BlockSpec((k, n), lambda i: (0, 0), pipeline_mode=pl.Buffered(1))],
        pl.BlockSpec((tm, n), lambda i: (i, 0)),
        _sds((r, n), out_dtype), name="in_proj")(a, w)


def _head_sum(x, ones_bd):
    hi, lo = _split2(x)
    return _dot(hi, ones_bd) + _dot(lo, ones_bd)


def _head_sumsq(x, ones_bd):
    return _head_sum(x * x, ones_bd)


def _head_ones(width, dh):
    i = jnp.arange(width) // dh
    return (i[:, None] == i[None, :]).astype(BF16)


def _na_prep_body(z_ref, gq_ref, gk_ref, e_ref, q_ref, k_ref, v_ref):
    z = z_ref[0]
    q, k, v = z[:, :512], z[:, 512:1024], z[:, 1024:]
    e = e_ref[...]
    qn = (q * lax.rsqrt(_head_sumsq(q, e) / NA_DH + NORM_EPS) * (gq_ref[...] * NA_SCALE)).astype(BF16)
    kn = (k * lax.rsqrt(_head_sumsq(k, e) / NA_DH + NORM_EPS) * gk_ref[...]).astype(BF16)
    vb = v.astype(BF16)
    for h in range(NA_HEADS):
        lo = h * NA_DH
        q_ref[0, h] = qn[:, lo:lo + NA_DH]
        k_ref[0, h] = kn[:, lo:lo + NA_DH]
        v_ref[0, h] = vb[:, lo:lo + NA_DH]


def _na_prep(z3, gq, gk, tr):
    b, t, _ = z3.shape
    hm = pl.BlockSpec((1, NA_HEADS, tr, NA_DH), lambda b, i: (b, 0, i, 0))
    shp = _sds((b, NA_HEADS, t, NA_DH), BF16)
    e = _head_ones(NA_HEADS * NA_DH, NA_DH)
    gq, gk = jnp.tile(gq, (1, NA_HEADS)), jnp.tile(gk, (1, NA_HEADS))
    return _call(
        _na_prep_body, (b, t // tr),
        [pl.BlockSpec((1, tr, 1536), lambda b, i: (b, i, 0)), _full(gq), _full(gk), _full(e)],
        [hm, hm, hm], [shp, shp, shp], name="na_prep")(z3, gq, gk, e)


def _softmax_pv(s, vs):
    m = s[0].max(axis=-1, keepdims=True)
    for t in s[1:]:
        m = jnp.maximum(m, t.max(axis=-1, keepdims=True))
    den = None
    acc = None
    for t, v in zip(s, vs):
        p = jnp.exp(t - m)
        ps = p.sum(axis=-1, keepdims=True)
        o = jnp.einsum('hqk,hkd->hqd', p.astype(BF16), v, preferred_element_type=F32)
        den = ps if den is None else den + ps
        acc = o if acc is None else acc + o
    return acc / den


def _qk(q, k):
    return jnp.einsum('hqd,hkd->hqk', q, k, preferred_element_type=F32)


def _heads_to_lanes(o):
    return jnp.concatenate([o[h] for h in range(o.shape[0])], axis=-1)


NA_ROWS_PER_STEP = 4


def _na_lat_body(q_ref, k_ref, v_ref, bias_ref, o_ref, *, rows, kh, l_tok, m_tok, rps):
    for rr in range(rps):
        r = pl.program_id(1) * rps + rr
        rs = jnp.clip(r - kh // 2, 0, rows - kh)
        delta = r - rs
        start = pl.multiple_of(rs * GRID_W, GRID_W)
        q = q_ref[0, :, rr * GRID_W:(rr + 1) * GRID_W, :]
        kb = k_ref[0, :, pl.ds(start, kh * GRID_W), :]
        vb = v_ref[0, :, pl.ds(start, kh * GRID_W), :]
        kc = k_ref[0, :, l_tok:l_tok + m_tok, :]
        vc = v_ref[0, :, l_tok:l_tok + m_tok, :]
        o = _softmax_pv([_qk(q, kb) + bias_ref[delta], _qk(q, kc)], [vb, vc])
        o_ref[0, rr * GRID_W:(rr + 1) * GRID_W, :] = _heads_to_lanes(o).astype(BF16)


def _na_latent(q, k, v, bias, l_tok, m_tok, out_rows):
    b, _, t, _ = q.shape
    rows = l_tok // GRID_W
    kh = min(NA_KH, rows)
    rps = math.gcd(rows, NA_ROWS_PER_STEP)
    kv = pl.BlockSpec((1, NA_HEADS, t, NA_DH), lambda b, r: (b, 0, 0, 0))
    return _call(
        functools.partial(_na_lat_body, rows=rows, kh=kh, l_tok=l_tok, m_tok=m_tok, rps=rps), (b, rows // rps),
        [pl.BlockSpec((1, NA_HEADS, rps * GRID_W, NA_DH), lambda b, r: (b, 0, r, 0)), kv, kv, _full(bias)],
        pl.BlockSpec((1, rps * GRID_W, BRANCH_W), lambda b, r: (b, r, 0)),
        _sds((b, out_rows, BRANCH_W), BF16), name="na_latent")(q, k, v, bias)


def _toeplitz_body(r_ref, oh_ref, valid_ref, o_ref):
    h, m, l = _split3(r_ref[...])
    oh = oh_ref[...]
    t = _dot(h, oh) + (_dot(m, oh) + _dot(l, oh))
    o_ref[...] = jnp.where(valid_ref[...] > 0.0, t, NEG_INF)


def _na_bias_table(rpb, rows):
    kh = min(NA_KH, rows)
    nh, ndr, ndc = rpb.shape
    col = jnp.arange(GRID_W)
    cs = jnp.clip(col - NA_KW // 2, 0, GRID_W - NA_KW)
    valid = (col[None, :] >= cs[:, None]) & (col[None, :] < cs[:, None] + NA_KW)
    dc = jnp.clip(col[None, :] - col[:, None] + NA_KW - 1, 0, 2 * NA_KW - 2)
    onehot = (jnp.arange(128)[:, None] == dc.reshape(1, -1)).astype(BF16)
    rp = jnp.pad(rpb.astype(F32).reshape(nh * ndr, ndc), ((0, 0), (0, 128 - ndc)))
    vmask = valid.reshape(1, -1).astype(F32)
    toep = pl.pallas_call(_toeplitz_body, out_shape=_sds((nh * ndr, GRID_W * GRID_W), F32),
                          name="na_bias")(rp, onehot, vmask)
    toep = toep.reshape(nh, ndr, GRID_W, GRID_W)
    per_delta = []
    for delta in range(kh):
        lo = NA_KH - 1 - delta
        t = toep[:, lo:lo + kh].transpose(0, 2, 1, 3)
        per_delta.append(t.reshape(nh, GRID_W, kh * GRID_W))
    return jnp.stack(per_delta, axis=0)


def _attn_body(*refs, nparts, heads):
    q_refs, k_refs, v_ref, o_ref = refs[:nparts], refs[nparts:2 * nparts], refs[2 * nparts], refs[-1]
    s = None
    for qr, kr in zip(q_refs, k_refs):
        t = _qk(qr[0], kr[0])
        s = t if s is None else s + t
    o_ref[0] = _heads_to_lanes(_softmax_pv([s], [v_ref[0]])).astype(BF16)


def _into(prev, n_in):
    if prev is None:
        return [], [], None
    return [pl.BlockSpec(memory_space=pl.ANY)], [prev], {n_in: 0}


def _attention(qs, ks, v, *, tq, q0, nq, tk, kblk, name, out_rows, prev=None):
    b, heads, _, dv = v.shape
    qspec = [pl.BlockSpec((1, heads, tq, q.shape[-1]), lambda b, i: (b, 0, q0 + i, 0)) for q in qs]
    kspec = [pl.BlockSpec((1, heads, tk, k.shape[-1]), lambda b, i: (b, 0, kblk, 0)) for k in ks]
    vspec = pl.BlockSpec((1, heads, tk, dv), lambda b, i: (b, 0, kblk, 0))
    xs, xa, aliases = _into(prev, 2 * len(qs) + 1)
    return _call(
        functools.partial(_attn_body, nparts=len(qs), heads=heads), (b, nq),
        qspec + kspec + [vspec] + xs,
        pl.BlockSpec((1, tq, heads * dv), lambda b, i: (b, q0 + i, 0)),
        _sds((b, out_rows, heads * dv), BF16), name=name, aliases=aliases)(*qs, *ks, v, *xa)


def _mla_q_body(z_ref, gc_ref, w_ref, gh_ref, ct_ref, st_ref, q_ref):
    cq = z_ref[0]
    cqn = (cq * lax.rsqrt(jnp.mean(cq * cq, axis=-1, keepdims=True) + NORM_EPS) * gc_ref[...]).astype(BF16)
    q = _dot(cqn, w_ref[...])
    gh = gh_ref[...]
    ct, st = ct_ref[...], st_ref[...]
    for h in range(MLA_HEADS):
        qh = q[:, h * 256:(h + 1) * 256]
        nope, pe, sw = qh[:, :128], qh[:, 128:192], qh[:, 192:256]
        ms = (jnp.sum(nope * nope, axis=-1, keepdims=True) + jnp.sum(pe * pe, axis=-1, keepdims=True)) \
            / (MLA_NOPE + MLA_ROPE)
        rinv = lax.rsqrt(ms + NORM_EPS)
        rot = (pe * rinv * gh[:, 128:192]) * ct + (sw * rinv * gh[:, 192:256]) * st
        qh = jnp.concatenate([nope * rinv * gh[:, :128], rot, jnp.zeros_like(rot)], axis=-1)
        q_ref[0, h] = (qh * MLA_SCALE).astype(BF16)


def _mla_kv_body(zc_ref, zp_ref, gc_ref, w_ref, gh_ref, ct_ref, st_ref, k_ref, v_ref):
    ckv = zc_ref[0]
    cn = (ckv * lax.rsqrt(jnp.mean(ckv * ckv, axis=-1, keepdims=True) + NORM_EPS) * gc_ref[...]).astype(BF16)
    kv = _dot(cn, w_ref[...])
    zp = zp_ref[0]
    pe, sw = zp[:, :64], zp[:, 64:128]
    pe2 = jnp.sum(pe * pe, axis=-1, keepdims=True)
    gh = gh_ref[...]
    ct, st = ct_ref[...], st_ref[...]
    for h in range(MLA_HEADS):
        nope = kv[:, h * 256:h * 256 + 128]
        ms = (jnp.sum(nope * nope, axis=-1, keepdims=True) + pe2) / (MLA_NOPE + MLA_ROPE)
        rinv = lax.rsqrt(ms + NORM_EPS)
        rot = (pe * rinv * gh[:, 128:192]) * ct + (sw * rinv * gh[:, 192:256]) * st
        k_ref[0, h] = jnp.concatenate([nope * rinv * gh[:, :128], rot, jnp.zeros_like(rot)], axis=-1).astype(BF16)
        v_ref[0, h] = kv[:, h * 256 + 128:(h + 1) * 256].astype(BF16)


def _mla_prep(z3, p, ct, st, tr):
    b, t, _ = z3.shape
    rope = pl.BlockSpec((tr, MLA_ROPE), lambda b, i: (i, 0))

    def hm(d):
        return pl.BlockSpec((1, MLA_HEADS, tr, d), lambda b, i: (b, 0, i, 0))

    def shp(d):
        return _sds((b, MLA_HEADS, t, d), BF16)

    q = _call(
        _mla_q_body, (b, t // tr),
        [pl.BlockSpec((1, tr, 512), lambda b, i: (b, i, ZC_CQ // 512)), _full(p['g_cq']), _full(p['w_uq']),
         _full(p['g_q']), rope, rope],
        hm(256), shp(256), name="mla_q_prep")(z3, p['g_cq'], p['w_uq'], p['g_q'], ct, st)
    k, v = _call(
        _mla_kv_body, (b, t // tr),
        [pl.BlockSpec((1, tr, 256), lambda b, i: (b, i, ZC_CKV // 256)),
         pl.BlockSpec((1, tr, 128), lambda b, i: (b, i, ZC_KPE // 128)),
         _full(p['g_ckv']), _full(p['w_ukv']), _full(p['g_k']), rope, rope],
        [hm(256), hm(128)], [shp(256), shp(128)], name="mla_kv_prep")(
            z3, z3, p['g_ckv'], p['w_ukv'], p['g_k'], ct, st)
    return q, k, v


def _rope_tables(l_tok, m_tok):
    half = MLA_ROPE // 4
    freqs = ROPE_THETA ** (-jnp.arange(half, dtype=F32) / half)
    pos = jnp.arange(l_tok)
    ar = (pos // GRID_W).astype(F32)[:, None] * freqs[None, :]
    ac = (pos % GRID_W).astype(F32)[:, None] * freqs[None, :]
    ct = jnp.concatenate([jnp.cos(ar), jnp.cos(ar), jnp.cos(ac), jnp.cos(ac)], axis=-1)
    st = jnp.concatenate([-jnp.sin(ar), jnp.sin(ar), -jnp.sin(ac), jnp.sin(ac)], axis=-1)
    ct = jnp.concatenate([ct, jnp.ones((m_tok, MLA_ROPE), F32)], axis=0)
    st = jnp.concatenate([st, jnp.zeros((m_tok, MLA_ROPE), F32)], axis=0)
    return ct, st


def _swap_halves(a):
    return jnp.concatenate([a[..., 16:32], a[..., 0:16], a[..., 48:64], a[..., 32:48]], axis=-1)


def _dft1_body(x_ref, bh_ref, bl_ref, o_ref):
    xh, xl = _split2(x_ref[0])
    r = _dot(xh, bh_ref[...]) + (_dot(xh, bl_ref[...]) + _dot(xl, bh_ref[...]))
    o_ref[0] = r.astype(BF16)


def _dft2_body(c_ref, s_ref, xc_ref, xs_ref, *rest):
    rest[-1][0] = (_dot(c_ref[...], xc_ref[0]) - _dot(s_ref[...], xs_ref[0])).astype(BF16)


def _dft_mats(n):
    k = jnp.arange(n, dtype=jnp.int32)
    ang = ((k[:, None] * k[None, :]) % n).astype(F32) * (2.0 * math.pi / n)
    s = 1.0 / math.sqrt(n)
    return jnp.cos(ang) * s, jnp.sin(ang) * s


def _fourier_tables(l_tok, m_tok, ctx_out):
    cw, sw = _dft_mats(BRANCH_W // 4)
    eye = jnp.eye(4, dtype=F32)
    bd = jnp.concatenate([jnp.kron(eye, cw), jnp.kron(eye, sw)], axis=1)
    tabs = {'bd': _split2(bd), 'lat': tuple(m.astype(BF16) for m in _dft_mats(l_tok))}
    if ctx_out:
        tabs['ctx'] = tuple(m.astype(BF16) for m in _dft_mats(m_tok))
    return tabs


def _fourier(z3, tabs, tr, l_tok, m_tok, ctx_out):
    b, t, _ = z3.shape
    bh, bl = tabs['bd']
    row = pl.BlockSpec((1, tr, 1024), lambda b, i: (b, i, 0))
    nrow = (t if ctx_out else l_tok) // tr
    xw = _call(
        _dft1_body, (b, nrow),
        [pl.BlockSpec((1, tr, 512), lambda b, i: (b, i, ZC_FN // 512)), _full(bh), _full(bl)],
        row, _sds((b, nrow * tr, 1024), BF16), name="dft_channels")(z3, bh, bl)

    def seq_dft(n, blk, mats, name, prev=None):
        tm = _pick_tile(n, (512, 256, 128, 64, 32, 16, 8))
        mspec = pl.BlockSpec((tm, n), lambda b, i: (i, 0))
        xc = pl.BlockSpec((1, n, 512), lambda b, i: (b, blk, 0))
        xs = pl.BlockSpec((1, n, 512), lambda b, i: (b, blk, 1))
        off = blk * n // tm
        ps, pa, aliases = _into(prev, 4)
        return _call(
            _dft2_body, (b, n // tm), [mspec, mspec, xc, xs] + ps,
            pl.BlockSpec((1, tm, 512), lambda b, i: (b, off + i, 0)),
            _sds((b, nrow * tr, 512), BF16), name=name, aliases=aliases)(*mats, xw, xw, *pa)

    y = seq_dft(l_tok, 0, tabs['lat'], "dft_seq_latent")
    if ctx_out:
        y = seq_dft(m_tok, l_tok // m_tok, tabs['ctx'], "dft_seq_ctx", prev=y)
    return y


def _rw_prep_body(zk_ref, zv_ref, zr_ref, zl_ref, zg_ref, halo_ref, mu_ref, kk_ref, ka_ref, w0_ref, a0_ref,
                  w2_ref, a2_ref, trf_ref, trb_ref, e_ref,
                  v_ref, r_ref, g_ref, cl_ref, kd_ref, b_ref, kx_ref, *, tr):
    row = lax.broadcasted_iota(jnp.int32, (tr, 1), 0)
    halo = halo_ref[0, 0]
    mu = mu_ref[...]

    def mix(x, lo, hi):
        xp = jnp.where(row == 0, halo[0:1, lo:hi], pltpu.roll(x, 1, 0))
        xn = jnp.where(row == tr - 1, halo[1:2, lo:hi], pltpu.roll(x, tr - 1, 0))
        return x + (0.5 * (xp + xn) - x) * mu[:, lo:hi]

    k = mix(zk_ref[0], 0, 512)
    v = mix(zv_ref[0], 512, 1024)
    r = mix(zr_ref[0], 1024, 1536)
    l4 = mix(zl_ref[0], 1536, 1664)
    g_ref[0] = mix(zg_ref[0], 1664, 1792)

    wcat = w0_ref[...] + _dot3(jnp.tanh(l4), w2_ref[...])
    acat = a0_ref[...] + _dot3(l4, a2_ref[...])
    kk = k * kk_ref[...]
    kkn = kk / jnp.maximum(jnp.sqrt(_head_sumsq(kk, e_ref[...])), 1e-12)
    for d, tri_ref in enumerate((trf_ref, trb_ref)):
        w = wcat[:, d * 512:(d + 1) * 512]
        lw = -math.exp(-0.5) * jax.nn.sigmoid(w)
        a = jax.nn.sigmoid(acat[:, d * 512:(d + 1) * 512])
        kd = k * (1.0 + (a - 1.0) * ka_ref[...])
        cl = _dot_sel(tri_ref[...], lw)
        bb = kkn * a
        kx = kkn * jnp.exp(-lw)
        for h in range(RW_HEADS):
            lo = h * RW_DH
            cl_ref[0, d, h] = cl[:, lo:lo + RW_DH]
            kd_ref[0, d, h] = kd[:, lo:lo + RW_DH]
            b_ref[0, d, h] = bb[:, lo:lo + RW_DH]
            kx_ref[0, d, h] = kx[:, lo:lo + RW_DH]
    for h in range(RW_HEADS):
        lo = h * RW_DH
        v_ref[0, h] = v[:, lo:lo + RW_DH]
        r_ref[0, h] = r[:, lo:lo + RW_DH]


def _rw_prep(z3, halo, p, tr):
    b, t, _ = z3.shape
    c = RW_CHUNK
    ti = jnp.arange(tr)
    same = (ti[:, None] // c) == (ti[None, :] // c)
    trf = (same & (ti[None, :] <= ti[:, None])).astype(BF16)
    trb = (same & (ti[None, :] >= ti[:, None])).astype(BF16)

    def zcol(off, w):
        return pl.BlockSpec((1, tr, w), lambda b, i: (b, i, off // w))

    hm = pl.BlockSpec((1, RW_HEADS, tr, RW_DH), lambda b, i: (b, 0, i, 0))
    hmd = pl.BlockSpec((1, 2, RW_HEADS, tr, RW_DH), lambda b, i: (b, 0, 0, i, 0))
    s1 = _sds((b, RW_HEADS, t, RW_DH), F32)
    s2 = _sds((b, 2, RW_HEADS, t, RW_DH), F32)
    consts = [p['rw_mu'], p['rw_k_k'], p['rw_k_a'], p['rw_w0'], p['rw_a0'], p['rw_w2'], p['rw_a2'], trf, trb,
              _head_ones(RW_W, RW_DH)]
    return _call(
        functools.partial(_rw_prep_body, tr=tr), (b, t // tr),
        [zcol(ZC_RWK, 512), zcol(ZC_RWK + 512, 512), zcol(ZC_RWK + 1024, 512), zcol(ZC_L4, 128), zcol(ZC_G, 128),
         pl.BlockSpec((1, 1, 2, 1792), lambda b, i: (b, i, 0, 0))] + [_full(a) for a in consts],
        [hm, hm, pl.BlockSpec((1, tr, 128), lambda b, i: (b, i, 0)), hmd, hmd, hmd, hmd],
        [s1, s1, _sds((b, t, 128), F32), s2, s2, s2, s2], name="rw_prep")(z3, z3, z3, z3, z3, halo, *consts)


RW_P_HI = 1
RW_P_LO = 1


def _rw_scan_tile(r_ref, v_ref, cl_ref, kd_ref, b_ref, kx_ref, y_ref, s_ref, *, tr, rev):
    c = RW_CHUNK
    nc = tr // c
    g = RW_HEADS * nc

    def ld(x):
        return x.reshape(g, c, RW_DH)

    r, v = ld(r_ref[0]), ld(v_ref[0])
    cl, kd, bb, kx = ld(cl_ref[0, 0]), ld(kd_ref[0, 0]), ld(b_ref[0, 0]), ld(kx_ref[0, 0])
    last = 0 if rev else c - 1
    ctot = cl[:, last:last + 1, :]
    e = jnp.exp(cl)
    ei = jnp.exp(-cl)
    ec = jnp.exp(ctot - cl)
    kkt, rt = kx * e, r * e
    kw, bw = kd * ei, bb * ei
    kc, bc = kd * ec, bb * ec
    a_cat = jnp.concatenate([kkt, rt], axis=1)
    pp = _bmm('gtd,gsd->gts', a_cat, jnp.concatenate([kw, bw], axis=1), RW_P_HI)
    ti = lax.broadcasted_iota(jnp.int32, (c, c), 0)
    si = lax.broadcasted_iota(jnp.int32, (c, c), 1)
    strict = (si > ti) if rev else (si < ti)
    incl = (si >= ti) if rev else (si <= ti)
    nmat = jnp.where(strict, pp[:, :c, :c], 0.0)
    x = -jnp.where(strict, pp[:, :c, c:], 0.0)
    ark = jnp.where(incl, pp[:, c:, :c], 0.0)
    arb = jnp.where(incl, pp[:, c:, c:], 0.0)
    eye_c = (ti == si).astype(F32)
    tm = eye_c + x
    xp = _bmm('gts,gsu->gtu', x, x, RW_P_HI)
    nsq = int(math.log2(c)) - 1
    for i in range(nsq):
        if i < nsq - 1:
            both = _bmm('gts,gsu->gtu', xp, jnp.concatenate([tm, xp], axis=2), RW_P_LO)
            tm, xp = tm + both[:, :, :c], both[:, :, c:]
        else:
            tm = tm + _bmm('gts,gsu->gtu', xp, tm, RW_P_LO)
    nav = _bmm('gts,gsd->gtd', jnp.concatenate([nmat, ark], axis=1), v, RW_P_HI)
    nv, arkv = nav[:, :c], nav[:, c:]
    ta = _bmm('gts,gsd->gtd', tm, jnp.concatenate([kkt, nv], axis=2), RW_P_HI)
    ata = _bmm('gts,gsd->gtd', arb, ta, RW_P_HI)
    a2 = rt - ata[:, :, :RW_DH]
    y0 = arkv - ata[:, :, RW_DH:]
    tb = _bmm('gtk,gtd->gkd', bc, ta, RW_P_HI)
    di = lax.broadcasted_iota(jnp.int32, (RW_DH, RW_DH), 0)
    dj = lax.broadcasted_iota(jnp.int32, (RW_DH, RW_DH), 1)
    gmt = jnp.where(di == dj, jnp.exp(ctot), 0.0) - tb[:, :, :RW_DH]
    hmt = _bmm('gtk,gtv->gkv', kc, v, RW_P_HI) - tb[:, :, RW_DH:]

    def per_chunk(x):
        return x.reshape(RW_HEADS, nc, x.shape[1], x.shape[2])

    a2, y0, gmt, hmt = per_chunk(a2), per_chunk(y0), per_chunk(gmt), per_chunk(hmt)
    st = s_ref[...]
    ys = [None] * nc
    for ci in (range(nc - 1, -1, -1) if rev else range(nc)):
        ys[ci] = _bmm('htk,hkv->htv', a2[:, ci], st, RW_P_HI) + y0[:, ci]
        st = _bmm('hke,hev->hkv', gmt[:, ci], st, RW_P_HI) + hmt[:, ci]
    s_ref[...] = st
    y_ref[0] = jnp.concatenate(ys, axis=1)


def _rw_scan_body(*refs, tr):
    fwd, bwd, (yf_ref, yb_ref, sf_ref, sb_ref) = refs[0:6], refs[6:12], refs[12:16]

    @pl.when(pl.program_id(1) == 0)
    def _():
        sf_ref[...] = jnp.zeros_like(sf_ref)
        sb_ref[...] = jnp.zeros_like(sb_ref)

    _rw_scan_tile(*fwd, yf_ref, sf_ref, tr=tr, rev=False)
    _rw_scan_tile(*bwd, yb_ref, sb_ref, tr=tr, rev=True)


def _rw_scan(r, v, cl, kd, bb, kx, tr, nl, nm):
    b, _, t, _ = r.shape
    nt = nl + nm
    tile_f = lambda j: jnp.where(j < nm, nl + j, j - nm)
    tile_b = lambda j: nt - 1 - j

    def specs(tile, d):
        hm = pl.BlockSpec((1, RW_HEADS, tr, RW_DH), lambda b, j: (b, 0, tile(j), 0))
        hmd = pl.BlockSpec((1, 1, RW_HEADS, tr, RW_DH), lambda b, j: (b, d, 0, tile(j), 0))
        return hm, hmd

    hf, hfd = specs(tile_f, 0)
    hb, hbd = specs(tile_b, 1)
    shp = _sds((b, RW_HEADS, t, RW_DH), F32)
    state = pltpu.VMEM((RW_HEADS, RW_DH, RW_DH), F32)
    return _call(
        functools.partial(_rw_scan_body, tr=tr), (b, nt),
        [hf, hf, hfd, hfd, hfd, hfd, hb, hb, hbd, hbd, hbd, hbd], [hf, hb], [shp, shp],
        scratch=[state, state], sem=("parallel", "arbitrary"),
        name="rw_scan")(r, v, cl, kd, bb, kx, r, v, cl, kd, bb, kx)


def _rw_out_body(yf_ref, yb_ref, r_ref, v_ref, kd_ref, g_ref, lnw_ref, lnb_ref, rk_ref, g2_ref, o_ref):
    y = yf_ref[0] + yb_ref[0]
    mu = jnp.mean(y, axis=-1, keepdims=True)
    var = jnp.mean(jnp.square(y - mu), axis=-1, keepdims=True)
    yn = (y - mu) * lax.rsqrt(var + RW_GN_EPS) * lnw_ref[...] + lnb_ref[...]
    ksum = kd_ref[0, 0] + kd_ref[0, 1]
    bonus = jnp.sum(r_ref[0] * ksum * rk_ref[...], axis=-1, keepdims=True) * v_ref[0]
    o = yn + bonus
    o = jnp.concatenate([o[h] for h in range(RW_HEADS)], axis=-1)
    gate = _dot3(jax.nn.sigmoid(g_ref[0]), g2_ref[...])
    o_ref[0] = (o * gate).astype(BF16)


def _rw_out(yf, yb, r, v, kd, g, p, tr, nrow):
    b = r.shape[0]
    hm = pl.BlockSpec((1, RW_HEADS, tr, RW_DH), lambda b, i: (b, 0, i, 0))
    consts = [p['rw_ln_w'], p['rw_ln_b'], p['rw_r_k'], p['rw_g2']]
    return _call(
        _rw_out_body, (b, nrow),
        [hm, hm, hm, hm, pl.BlockSpec((1, 2, RW_HEADS, tr, RW_DH), lambda b, i: (b, 0, 0, i, 0)),
         pl.BlockSpec((1, tr, 128), lambda b, i: (b, i, 0))] + [_full(a) for a in consts],
        pl.BlockSpec((1, tr, RW_W), lambda b, i: (b, i, 0)),
        _sds((b, nrow * tr, RW_W), BF16), name="rw_out")(yf, yb, r, v, kd, g, *consts)


GATE_COL0 = KEY_COLS + 2144


def _merge_body(h_ref, y0_ref, y1_ref, y2_ref, y3_ref, g0_ref, g1_ref, g2_ref, g3_ref, wb_ref, o_ref):
    h = h_ref[0]
    acc = None
    for i, (y_ref, wg_ref) in enumerate(zip((y0_ref, y1_ref, y2_ref, y3_ref), (g0_ref, g1_ref, g2_ref, g3_ref))):
        t = jax.nn.sigmoid(_dot(h, wg_ref[...])) * _dot(y_ref[0], wb_ref[0, i])
        acc = t if acc is None else acc + t
    o_ref[0] = acc.astype(BF16)


def _merge(h, ys, wg, wbr, layer, rows):
    b, _, d = h.shape
    tn = 512
    nn = d // tn
    tm = _pick_tile(rows, (1024, 768, 512, 256, 128, 64))
    row = lambda w: pl.BlockSpec((1, tm, w), lambda n, b, i: (b, i, 0))
    gspecs = [pl.BlockSpec((d, tn), functools.partial(lambda n, b, i, k: (0, k * nn + n), k=k))
              for k in range(N_BRANCH)]
    return _call(
        _merge_body, (nn, b, rows // tm),
        [row(d)] + [row(BRANCH_W)] * 4 + gspecs
        + [pl.BlockSpec((1, N_BRANCH, BRANCH_W, tn), lambda n, b, i: (layer, 0, 0, n))],
        pl.BlockSpec((1, tm, tn), lambda n, b, i: (b, i, n)),
        _sds((b, rows, d), BF16), name="merge")(h, *ys, wg, wg, wg, wg, wbr)


def _outproj_body(a_ref, w_ref, x_ref, md_ref, o_ref, *, tm, l_tok):
    row = pl.program_id(1) * tm + lax.broadcasted_iota(jnp.int32, (tm, 1), 0)
    gate = jnp.where(row < l_tok, md_ref[0, 0, 2:3, :], md_ref[0, 1, 2:3, :])
    o_ref[0] = x_ref[0] + gate * _dot(a_ref[0], w_ref[0])


def _outproj(acc, w_out, layer, xs, md, l_tok):
    b, rows, d = acc.shape
    tm = _pick_tile(rows, (768, 512, 256, 128, 64))
    row = pl.BlockSpec((1, tm, d), lambda b, i: (b, i, 0))
    return _call(
        functools.partial(_outproj_body, tm=tm, l_tok=l_tok), (b, rows // tm),
        [row, pl.BlockSpec((1, d, d), lambda b, i: (layer, 0, 0)), row,
         pl.BlockSpec((1, 2, 6, d), lambda b, i: (b, 0, 0, 0))], row,
        _sds((b, rows, d), F32), name="out_proj")(acc, w_out, xs, md)


def _select_body(lg_ref, slot_ref, aff_ref, slotc_ref, *, n, cap):
    lg = lg_ref[0]
    lane = lax.broadcasted_iota(jnp.int32, lg.shape, 1)
    lg = jnp.where(lane < N_EXPERTS, lg, NEG_INF)
    ex = jnp.exp(lg - lg.max(axis=-1, keepdims=True))
    aff = ex / ex.sum(axis=-1, keepdims=True)
    aff_t = aff.T[:N_EXPERTS]

    def count_ge(t):
        return jnp.sum((aff_t >= t).astype(F32), axis=1, keepdims=True)

    tiny = 2.0 ** -126
    lo = jnp.full((N_EXPERTS, 1), tiny, F32)
    for sh in (64, 32, 16, 8, 4, 2, 1):
        cand = lo * (2.0 ** sh)
        lo = jnp.where(count_ge(cand) >= cap, cand, lo)
    hi = lo * 2.0
    below = count_ge(tiny) < cap
    lo = jnp.where(below, 0.0, lo)
    hi = jnp.where(below, tiny, hi)
    for _ in range(40):
        mid = lo + (hi - lo) * 0.5
        ok = count_ge(mid) >= cap
        lo = jnp.where(ok, mid, lo)
        hi = jnp.where(ok, hi, mid)
    gt = aff_t >= hi
    eq = (aff_t >= lo) & (aff_t < hi)
    need = cap - jnp.sum(gt.astype(F32), axis=1, keepdims=True)
    both = jnp.concatenate([gt.astype(BF16), eq.astype(BF16)], axis=0)
    cw = min(n, 512)
    pre = []
    for cb in range(n // cw):
        tp = lax.broadcasted_iota(jnp.int32, (n, cw), 0)
        tt = lax.broadcasted_iota(jnp.int32, (n, cw), 1) + cb * cw
        pre.append(_dot(both, (tp < tt).astype(BF16)))
    pre = jnp.concatenate(pre, axis=1) if len(pre) > 1 else pre[0]
    pre_gt, pre_eq = pre[:N_EXPERTS], pre[N_EXPERTS:]
    sel = gt | (eq & (pre_eq < need))
    slot = jnp.where(sel, pre_gt + jnp.minimum(pre_eq, need), -1.0)
    slot_ref[0] = slot.astype(jnp.int32)
    aff_ref[0] = aff_t
    pad = jnp.full((128 - N_EXPERTS, n), -1.0, F32)
    slotc_ref[0] = jnp.concatenate([slot, pad], axis=0).T.astype(jnp.int32)


def _select(logits, n, blk, cap, name):
    b = logits.shape[0]
    er = pl.BlockSpec((1, N_EXPERTS, n), lambda b: (b, 0, 0))
    return _call(
        functools.partial(_select_body, n=n, cap=cap), (b,),
        [pl.BlockSpec((1, n, 128), lambda b: (b, blk, 0))],
        [er, er, pl.BlockSpec((1, n, 128), lambda b: (b, 0, 0))],
        [_sds((b, N_EXPERTS, n), jnp.int32), _sds((b, N_EXPERTS, n), F32), _sds((b, n, 128), jnp.int32)],
        name=name)(logits)


def _ffn_body(h_ref, slot_ref, aff_ref, w1_ref, w3_ref, w2_ref, o_ref, *, grp, n, cap):
    xs, gates = [], []
    jrow = lax.broadcasted_iota(jnp.int32, (cap, n), 0)
    for gi in range(grp):
        pick = jrow == slot_ref[gi, 0]
        xs.append(_dot(pick.astype(BF16), h_ref[gi]).astype(BF16))
        gates.append(jnp.sum(jnp.where(pick, aff_ref[gi, 0], 0.0), axis=1, keepdims=True))
    xe = jnp.concatenate(xs, axis=0) if grp > 1 else xs[0]
    gate = jnp.concatenate(gates, axis=0) if grp > 1 else gates[0]
    a = _dot(xe, w1_ref[0, 0])
    u = _dot(xe, w3_ref[0, 0])
    hm = (a * jax.nn.sigmoid(a) * u).astype(BF16)
    ye = (_dot(hm, w2_ref[0, 0]) * gate).astype(BF16)
    for gi in range(grp):
        o_ref[0, gi] = ye[gi * cap:(gi + 1) * cap]


def _moe_ffn(h2, slot, aff, w1, w3, w2, layer, n, blk, cap, grp, name):
    b, _, d = h2.shape
    ff = w1.shape[-1]
    sr = pl.BlockSpec((grp, 1, 1, n), lambda e, g: (g, e, 0, 0))
    return _call(
        functools.partial(_ffn_body, grp=grp, n=n, cap=cap), (N_EXPERTS, b // grp),
        [pl.BlockSpec((grp, n, d), lambda e, g: (g, blk, 0)), sr, sr,
         pl.BlockSpec((1, 1, d, ff), lambda e, g: (layer, e, 0, 0)),
         pl.BlockSpec((1, 1, d, ff), lambda e, g: (layer, e, 0, 0)),
         pl.BlockSpec((1, 1, ff, d), lambda e, g: (layer, e, 0, 0))],
        pl.BlockSpec((1, grp, cap, d), lambda e, g: (e, g, 0, 0)),
        _sds((N_EXPERTS, b, cap, d), BF16), name=name)(
            h2, slot.reshape(b, N_EXPERTS, 1, n), aff.reshape(b, N_EXPERTS, 1, n), w1, w3, w2)


def _combine_body(sc_ref, ye_ref, x_ref, md_ref, *rest, cap):
    o_ref = rest[-1]
    sc = sc_ref[0]
    jj = lax.broadcasted_iota(jnp.int32, (sc.shape[0], cap), 1)
    acc = None
    for e in range(N_EXPERTS):
        put = (sc[:, e:e + 1] == jj).astype(BF16)
        t = _dot(put, ye_ref[e, 0])
        acc = t if acc is None else acc + t
    o_ref[0] = x_ref[0] + md_ref[0, 0, 5:6, :] * acc


def _moe_combine(slotc, ye, x1, md, n, blk_rows, is_ctx, cap, name, out_rows, prev=None):
    b, _, d = x1.shape
    tm = _pick_tile(n, (512, 256, 128, 64, 32, 16, 8))
    off = blk_rows // tm
    extra_specs, extra_args, aliases = [], [], None
    if prev is not None:
        extra_specs, extra_args, aliases = [pl.BlockSpec(memory_space=pl.ANY)], [prev], {4: 0}
    return _call(
        functools.partial(_combine_body, cap=cap), (b, n // tm),
        [pl.BlockSpec((1, tm, 128), lambda b, i: (b, i, 0)),
         pl.BlockSpec((N_EXPERTS, 1, cap, d), lambda b, i: (0, b, 0, 0)),
         pl.BlockSpec((1, tm, d), lambda b, i: (b, off + i, 0)),
         pl.BlockSpec((1, 1, 6, d), lambda b, i: (b, 1 if is_ctx else 0, 0, 0))] + extra_specs,
        pl.BlockSpec((1, tm, d), lambda b, i: (b, off + i, 0)),
        _sds((b, out_rows, d), F32), name=name, aliases=aliases)(slotc, ye, x1, md, *extra_args)


def _moe(x1, h2, logits, md, w1, w3, w2, layer, l_tok, m_tok, ctx_out):
    b = x1.shape[0]
    cap_l = CAPACITY_FACTOR * l_tok // N_EXPERTS
    slot, aff, slotc = _select(logits, l_tok, 0, cap_l, "moe_select_latent")
    ye = _moe_ffn(h2, slot, aff, w1, w3, w2, layer, l_tok, 0, cap_l, 1, "moe_ffn_latent")
    out_rows = l_tok + m_tok if ctx_out else l_tok
    x2 = _moe_combine(slotc, ye, x1, md, l_tok, 0, False, cap_l, "moe_combine_latent", out_rows)
    if not ctx_out:
        return x2
    cap_c = CAPACITY_FACTOR * m_tok // N_EXPERTS
    blk = l_tok // m_tok
    slot, aff, slotc = _select(logits, m_tok, blk, cap_c, "moe_select_ctx")
    ye = _moe_ffn(h2, slot, aff, w1, w3, w2, layer, m_tok, blk, cap_c, b, "moe_ffn_ctx")
    return _moe_combine(slotc, ye, x1, md, m_tok, l_tok, True, cap_c, "moe_combine_ctx", out_rows, prev=x2)


def _layer_params(l, w_in, na_q_norm, na_k_norm, na_rpb, mla_cq_norm, mla_ckv_norm, mla_w_uq, mla_w_ukv,
                  mla_q_norm, mla_k_norm, rw_mu_ks, rw_mu_qs, rw_w0, rw_w2, rw_a0, rw_a2, rw_g2, rw_k_k, rw_k_a,
                  rw_r_k, rw_ln_w, rw_ln_b, w_br, w_out, rows):
    w = w_in[l]
    d = w.shape[0]
    kq = KEY_COLS
    kpe = w[:, 1280:1344]
    zpad = lambda n: jnp.zeros((d, n), w.dtype)
    wz = jnp.concatenate([
        w[:, kq:kq + 512], w[:, 0:512], w[:, 512:1024],
        w[:, kq + 512:kq + 1024],
        w[:, 1344:1856], w[:, 1856:2368], w[:, kq + 1024:kq + 1536],
        w[:, kq + 1632:kq + 2144],
        w[:, 1024:1280],
        kpe, _swap_halves(kpe),
        w[:, 2368:2496],
        w[:, kq + 1536:kq + 1632], zpad(32)], axis=1).astype(BF16)
    assert wz.shape[1] == NZ
    wg = w[:, GATE_COL0:].astype(BF16)

    def head_gain(g):
        return jnp.concatenate([g[:128], g[128:], _swap_halves(g[128:])])[None, :]

    wq = mla_w_uq[l].reshape(MLA_QLORA, MLA_HEADS, MLA_NOPE + MLA_ROPE)
    wq = jnp.concatenate([wq, _swap_halves(wq[:, :, MLA_NOPE:])], axis=-1).reshape(MLA_QLORA, MLA_HEADS * 256)
    w2cat = jnp.zeros((128, 1024), F32).at[0:32, 0:512].set(rw_w2[l, 0]).at[32:64, 512:].set(rw_w2[l, 1])
    a2cat = jnp.zeros((128, 1024), F32).at[64:96, 0:512].set(rw_a2[l, 0]).at[96:128, 512:].set(rw_a2[l, 1])
    mu = jnp.concatenate([rw_mu_ks[l][:1024], rw_mu_qs[l][:512], rw_mu_ks[l][1024:], rw_mu_qs[l][512:],
                          jnp.zeros((32,), F32)])[None, :]
    hd = lambda a: a.reshape(RW_HEADS, 1, RW_DH)
    return {
        'wz': wz, 'wg': wg,
        'na_gq': na_q_norm[l][None, :], 'na_gk': na_k_norm[l][None, :],
        'na_bias': _na_bias_table(na_rpb[l], rows),
        'g_cq': mla_cq_norm[l][None, :], 'g_ckv': mla_ckv_norm[l][None, :],
        'w_uq': wq.astype(BF16), 'w_ukv': mla_w_ukv[l].astype(BF16),
        'g_q': head_gain(mla_q_norm[l]), 'g_k': head_gain(mla_k_norm[l]),
        'rw_mu': mu, 'rw_k_k': rw_k_k[l][None, :], 'rw_k_a': rw_k_a[l][None, :],
        'rw_w0': rw_w0[l].reshape(1, 1024), 'rw_a0': rw_a0[l].reshape(1, 1024), 'rw_w2': w2cat, 'rw_a2': a2cat,
        'rw_ln_w': hd(rw_ln_w[l]), 'rw_ln_b': hd(rw_ln_b[l]), 'rw_r_k': hd(rw_r_k[l]),
        'rw_g2': jnp.concatenate([rw_g2[l], jnp.zeros((32, RW_W), F32)], axis=0),
    }


def _rw_halo(z3, tr, nl):
    b, t, _ = z3.shape
    nt = t // tr
    z4 = z3.reshape(b, nt, tr, z3.shape[-1])
    pick = lambda a: jnp.concatenate([a[..., ZC_RWK:ZC_RWK + 1536], a[..., ZC_L4:ZC_L4 + 256]], axis=-1)
    first, last = pick(z4[:, :, 0, :]), pick(z4[:, :, tr - 1, :])
    zero = jnp.zeros_like(first[:, :1])
    prev = jnp.concatenate([zero, last[:, :-1]], axis=1)
    nxt = jnp.concatenate([first[:, 1:], zero], axis=1)
    tile = jnp.arange(nt)[None, :, None]
    prev = jnp.where(tile == nl, 0.0, prev)
    nxt = jnp.where(tile == nl - 1, 0.0, nxt)
    return jnp.stack([prev, nxt], axis=2)


def _layer(xs, md, p, g1n, g2n, wr, deep, layer, tabs, l_tok, m_tok, tr, ctx_out):
    b, t, d = xs.shape
    nl, nm = l_tok // tr, m_tok // tr
    nrow = (nl + nm) if ctx_out else nl
    blk = l_tok // m_tok

    h = _norm1(xs, g1n, md, tr, nl)
    z3 = _matmul(h.reshape(b * t, d), p['wz'], F32).reshape(b, t, NZ)

    nq, nk, nv = _na_prep(z3, p['na_gq'], p['na_gk'], tr)
    rows = nrow * tr
    y_na = _na_latent(nq, nk, nv, p['na_bias'], l_tok, m_tok, rows)
    mq, mk, mv = _mla_prep(z3, p, tabs['rope_c'], tabs['rope_s'], tr)
    tq = _pick_tile(l_tok, (256, 128, 64))
    y_mla = _attention([mq], [mk], mv, tq=tq, q0=0, nq=l_tok // tq, tk=t, kblk=0, name="mla_latent", out_rows=rows)
    rv, rr, rg, cl, kd, bb, kx = _rw_prep(z3, _rw_halo(z3, tr, nl), p, tr)
    yf, yb = _rw_scan(rr, rv, cl, kd, bb, kx, tr, nl, nm)
    y_rw = _rw_out(yf, yb, rr, rv, kd, rg, p, tr, nrow)
    y_fn = _fourier(z3, tabs, tr, l_tok, m_tok, ctx_out)

    if ctx_out:
        ctx_q = dict(tq=m_tok, q0=blk, nq=1, tk=m_tok, kblk=blk, out_rows=rows)
        y_na = _attention([nq], [nk], nv, name="na_ctx", prev=y_na, **ctx_q)
        y_mla = _attention([mq], [mk], mv, name="mla_ctx", prev=y_mla, **ctx_q)

    acc = _merge(h, (y_fn, y_na, y_mla, y_rw), p['wg'], deep['w_br'], layer, rows)
    x1 = _outproj(acc, deep['w_out'], layer, xs, md, l_tok)
    h2, logits = _norm2_router(x1, g2n, md, wr, tr, nl)
    return _moe(x1, h2, logits, md, deep['w1'], deep['w3'], deep['w2'], layer, l_tok, m_tok, ctx_out)


def kernel(x, c, ctx, c_ctx, ada_w, ada_b, norm1_g, norm2_g, w_in, na_q_norm, na_k_norm, na_rpb, mla_cq_norm, mla_ckv_norm, mla_w_uq, mla_w_ukv, mla_q_norm, mla_k_norm, rw_mu_ks, rw_mu_qs, rw_w0, rw_w2, rw_a0, rw_a2, rw_g2, rw_k_k, rw_k_a, rw_r_k, rw_ln_w, rw_ln_b, w_br, w_out, moe_router, moe_w1, moe_w3, moe_w2):
    b, l_tok, d = x.shape
    m_tok = ctx.shape[1]
    depth = ada_w.shape[0]
    tr = min(m_tok, 256)
    assert l_tok % tr == 0 and m_tok % tr == 0 and l_tok % m_tok == 0 and tr % RW_CHUNK == 0
    assert l_tok % GRID_W == 0 and m_tok % 128 == 0

    nr = -(-(b + 1) // 8) * 8
    cc = jnp.concatenate([c, c_ctx[None, :], jnp.zeros((nr - b - 1, d), F32)], axis=0)
    mod = _modulation(cc, ada_w, ada_b)
    tabs = {}
    tabs['rope_c'], tabs['rope_s'] = _rope_tables(l_tok, m_tok)

    deep = {'w1': moe_w1.astype(BF16), 'w3': moe_w3.astype(BF16), 'w2': moe_w2.astype(BF16),
            'w_br': w_br.astype(BF16), 'w_out': w_out.astype(BF16)}
    xs = jnp.concatenate([x, ctx], axis=1)
    for l in range(depth):
        ctx_out = l < depth - 1
        tabs.update(_fourier_tables(l_tok, m_tok, ctx_out))
        p = _layer_params(l, w_in, na_q_norm, na_k_norm, na_rpb, mla_cq_norm, mla_ckv_norm, mla_w_uq, mla_w_ukv,
                          mla_q_norm, mla_k_norm, rw_mu_ks, rw_mu_qs, rw_w0, rw_w2, rw_a0, rw_a2, rw_g2, rw_k_k,
                          rw_k_a, rw_r_k, rw_ln_w, rw_ln_b, w_br, w_out, l_tok // GRID_W)
        md = jnp.stack([mod[l, :b], jnp.broadcast_to(mod[l, b], (b, 6 * d))], axis=1).reshape(b, 2, 6, d)
        wr = jnp.concatenate([moe_router[l], jnp.zeros((d, 128 - N_EXPERTS), F32)], axis=1)
        xs = _layer(xs, md, p, norm1_g[l][None, :], norm2_g[l][None, :], wr, deep, l, tabs, l_tok, m_tok, tr, ctx_out)
    return xs
```

```python
import functools
import math

import jax
import jax.numpy as jnp
from jax import lax
from jax.experimental import pallas as pl
from jax.experimental.pallas import tpu as pltpu

F32 = jnp.float32
BF16 = jnp.bfloat16

GRID_W = 64
N_BRANCH = 4
BRANCH_W = 512
NA_HEADS, NA_DH, NA_KH, NA_KW = 8, 64, 8, 16
NA_SCALE = NA_DH ** -0.5
MLA_HEADS, MLA_NOPE, MLA_ROPE, MLA_V = 4, 128, 64, 128
MLA_QLORA, MLA_KVLORA = 512, 256
MLA_SCALE = (MLA_NOPE + MLA_ROPE) ** -0.5
RW_HEADS, RW_DH, RW_W = 8, 64, 512
RW_LORA, RW_GATE_LORA = 32, 96
RW_GN_EPS = 64e-5
N_EXPERTS = 16
CAPACITY_FACTOR = 2
ROPE_THETA = 10000.0
NORM_EPS = 1e-6
NEG_INF = -1e30
LOG2E = math.log2(math.e)
RW_CHUNK = 64
KEY_COLS = 2496
NZ = 4736
ZC_NA = 0
ZC_CQ = 1536
ZC_RWK = 2048
ZC_FN = 3584
ZC_CKV = 4096
ZC_KPE = 4352
ZC_L4 = 4480
ZC_G = 4608

VMEM_LIMIT = 56 * 2 ** 20


def _call(body, grid, in_specs, out_specs, out_shape, scratch=(), sem=None, name=None, aliases=None):
    return pl.pallas_call(
        body, grid=grid, in_specs=in_specs, out_specs=out_specs, out_shape=out_shape,
        scratch_shapes=list(scratch), name=name, input_output_aliases=aliases or {},
        compiler_params=pltpu.CompilerParams(
            dimension_semantics=sem or ("parallel",) * len(grid), vmem_limit_bytes=VMEM_LIMIT))


def _full(a):
    nd = a.ndim
    return pl.BlockSpec(a.shape, lambda *_: (0,) * nd)


def _sds(shape, dtype):
    return jax.ShapeDtypeStruct(shape, dtype)


def _dot(a, b):
    return jnp.dot(a, b, preferred_element_type=F32)


def _dot_nt(a, b):
    return lax.dot_general(a, b, (((1,), (1,)), ((), ())), preferred_element_type=F32)


def _split2(x):
    hi = x.astype(BF16)
    return hi, (x - hi.astype(F32)).astype(BF16)


def _split3(x):
    hi = x.astype(BF16)
    r = x - hi.astype(F32)
    mid = r.astype(BF16)
    return hi, mid, (r - mid.astype(F32)).astype(BF16)


def _dot3(a, b):
    ah, al = _split2(a)
    bh, bl = _split2(b)
    return _dot(ah, bh) + (_dot(ah, bl) + _dot(al, bh))


def _dot_sel(sel_bf16, x):
    h, m, l = _split3(x)
    return _dot(sel_bf16, h) + (_dot(sel_bf16, m) + _dot(sel_bf16, l))


def _bmm(spec, a, b, passes):
    e = functools.partial(jnp.einsum, spec, preferred_element_type=F32)
    if passes == 1:
        return e(a.astype(BF16), b.astype(BF16))
    ah, al = _split2(a)
    bh, bl = _split2(b)
    return e(ah, bh) + (e(ah, bl) + e(al, bh))


def _pick_tile(n, cands):
    for c in cands:
        if n % c == 0:
            return c
    raise ValueError(f"no tile for {n}")


def _mod_body(c_ref, w_ref, b_ref, o_ref):
    cc = c_ref[...]
    s = cc * jax.nn.sigmoid(cc)
    tn = o_ref.shape[-1]
    for j in range(0, tn, 1024):
        o_ref[0, :, j:j + 1024] = _dot3(s, w_ref[0, :, j:j + 1024]) + b_ref[0, :, j:j + 1024]


def _modulation(cc, ada_w, ada_b):
    depth, d, n6 = ada_w.shape
    r = cc.shape[0]
    tn = 2048
    return _call(
        _mod_body, (depth, n6 // tn),
        [pl.BlockSpec((r, d), lambda l, n: (0, 0)),
         pl.BlockSpec((1, d, tn), lambda l, n: (l, 0, n)),
         pl.BlockSpec((1, 1, tn), lambda l, n: (l, 0, n))],
        pl.BlockSpec((1, r, tn), lambda l, n: (l, 0, n)),
        _sds((depth, r, n6), F32), name="adaln_mod")(cc, ada_w, ada_b.reshape(depth, 1, n6))


def _modnorm(x, g, shift, scale):
    y = x * lax.rsqrt(jnp.mean(x * x, axis=-1, keepdims=True) + NORM_EPS)
    return (y * g) * (1.0 + scale) + shift


def _norm1_body(x_ref, g_ref, md_ref, h_ref):
    h = _modnorm(x_ref[0], g_ref[...], md_ref[0, 0, 0:1, :], md_ref[0, 0, 1:2, :])
    h_ref[0] = h.astype(BF16)


def _norm2_body(x_ref, g_ref, md_ref, wr_ref, h_ref, lg_ref):
    h = _modnorm(x_ref[0], g_ref[...], md_ref[0, 0, 3:4, :], md_ref[0, 0, 4:5, :])
    h_ref[0] = h.astype(BF16)
    lg_ref[0] = _dot3(h, wr_ref[...])


def _md_spec(d, nl):
    return pl.BlockSpec((1, 1, 6, d), lambda b, i: (b, i // nl, 0, 0))


def _norm1(xs, g, md, tr, nl):
    b, t, d = xs.shape
    return _call(
        _norm1_body, (b, t // tr),
        [pl.BlockSpec((1, tr, d), lambda b, i: (b, i, 0)), _full(g), _md_spec(d, nl)],
        pl.BlockSpec((1, tr, d), lambda b, i: (b, i, 0)),
        _sds((b, t, d), BF16), name="norm1")(xs, g, md)


def _norm2_router(x1, g, md, wr, tr, nl):
    b, t, d = x1.shape
    return _call(
        _norm2_body, (b, t // tr),
        [pl.BlockSpec((1, tr, d), lambda b, i: (b, i, 0)), _full(g), _md_spec(d, nl), _full(wr)],
        [pl.BlockSpec((1, tr, d), lambda b, i: (b, i, 0)), pl.BlockSpec((1, tr, 128), lambda b, i: (b, i, 0))],
        [_sds((b, t, d), BF16), _sds((b, t, 128), F32)], name="norm2_router")(x1, g, md, wr)


def _mm_body(a_ref, w_ref, o_ref):
    o_ref[...] = _dot(a_ref[...], w_ref[...]).astype(o_ref.dtype)


def _matmul(a, w, out_dtype):
    r, k = a.shape
    n = w.shape[1]
    tm = _pick_tile(r, (512, 384, 256, 128, 64, 32, 16, 8))
    return _call(
        _mm_body, (r // tm,),
        [pl.BlockSpec((tm, k), lambda i: (i, 0)),
         pl.BlockSpec((k, n), lambda i: (0, 0), pipeline_mode=pl.Buffered(1))],
        pl.BlockSpec((tm, n), lambda i: (i, 0)),
        _sds((r, n), out_dtype), name="in_proj")(a, w)


def _head_sum(x, ones_bd):
    hi, lo = _split2(x)
    return _dot(hi, ones_bd) + _dot(lo, ones_bd)


def _head_sumsq(x, ones_bd):
    return _head_sum(x * x, ones_bd)


def _head_ones(width, dh):
    i = jnp.arange(width) // dh
    return (i[:, None] == i[None, :]).astype(BF16)


def _na_prep_body(z_ref, gq_ref, gk_ref, e_ref, q_ref, k_ref, v_ref):
    z = z_ref[0]
    q, k, v = z[:, :512], z[:, 512:1024], z[:, 1024:]
    e = e_ref[...]
    qn = (q * lax.rsqrt(_head_sumsq(q, e) / NA_DH + NORM_EPS) * (gq_ref[...] * (NA_SCALE * LOG2E))).astype(BF16)
    kn = (k * lax.rsqrt(_head_sumsq(k, e) / NA_DH + NORM_EPS) * gk_ref[...]).astype(BF16)
    vb = v.astype(BF16)
    for h in range(NA_HEADS):
        lo = h * NA_DH
        q_ref[0, h] = qn[:, lo:lo + NA_DH]
        k_ref[0, h] = kn[:, lo:lo + NA_DH]
        v_ref[0, h] = vb[:, lo:lo + NA_DH]


def _na_prep(z3, gq, gk, tr):
    b, t, _ = z3.shape
    hm = pl.BlockSpec((1, NA_HEADS, tr, NA_DH), lambda b, i: (b, 0, i, 0))
    shp = _sds((b, NA_HEADS, t, NA_DH), BF16)
    e = _head_ones(NA_HEADS * NA_DH, NA_DH)
    gq, gk = jnp.tile(gq, (1, NA_HEADS)), jnp.tile(gk, (1, NA_HEADS))
    return _call(
        _na_prep_body, (b, t // tr),
        [pl.BlockSpec((1, tr, 1536), lambda b, i: (b, i, 0)), _full(gq), _full(gk), _full(e)],
        [hm, hm, hm], [shp, shp, shp], name="na_prep")(z3, gq, gk, e)


def _softmax_pv(s, vs):
    m = s[0].max(axis=-1, keepdims=True)
    for t in s[1:]:
        m = jnp.maximum(m, t.max(axis=-1, keepdims=True))
    den = None
    acc = None
    for t, v in zip(s, vs):
        p = jnp.exp2(t - m)
        ps = p.sum(axis=-1, keepdims=True)
        o = jnp.einsum('hqk,hkd->hqd', p.astype(BF16), v, preferred_element_type=F32)
        den = ps if den is None else den + ps
        acc = o if acc is None else acc + o
    return acc / den


def _qk(q, k):
    return jnp.einsum('hqd,hkd->hqk', q, k, preferred_element_type=F32)


def _heads_to_lanes(o):
    return jnp.concatenate([o[h] for h in range(o.shape[0])], axis=-1)


NA_ROWS_PER_STEP = 4


def _na_lat_body(q_ref, k_ref, v_ref, bias_ref, o_ref, *, rows, kh, l_tok, m_tok, rps):
    for rr in range(rps):
        r = pl.program_id(1) * rps + rr
        rs = jnp.clip(r - kh // 2, 0, rows - kh)
        delta = r - rs
        start = pl.multiple_of(rs * GRID_W, GRID_W)
        q = q_ref[0, :, rr * GRID_W:(rr + 1) * GRID_W, :]
        kb = k_ref[0, :, pl.ds(start, kh * GRID_W), :]
        vb = v_ref[0, :, pl.ds(start, kh * GRID_W), :]
        kc = k_ref[0, :, l_tok:l_tok + m_tok, :]
        vc = v_ref[0, :, l_tok:l_tok + m_tok, :]
        o = _softmax_pv([_qk(q, kb) + bias_ref[delta], _qk(q, kc)], [vb, vc])
        o_ref[0, rr * GRID_W:(rr + 1) * GRID_W, :] = _heads_to_lanes(o).astype(BF16)


def _na_latent(q, k, v, bias, l_tok, m_tok, out_rows):
    b, _, t, _ = q.shape
    rows = l_tok // GRID_W
    kh = min(NA_KH, rows)
    rps = math.gcd(rows, NA_ROWS_PER_STEP)
    kv = pl.BlockSpec((1, NA_HEADS, t, NA_DH), lambda b, r: (b, 0, 0, 0))
    return _call(
        functools.partial(_na_lat_body, rows=rows, kh=kh, l_tok=l_tok, m_tok=m_tok, rps=rps), (b, rows // rps),
        [pl.BlockSpec((1, NA_HEADS, rps * GRID_W, NA_DH), lambda b, r: (b, 0, r, 0)), kv, kv, _full(bias)],
        pl.BlockSpec((1, rps * GRID_W, BRANCH_W), lambda b, r: (b, r, 0)),
        _sds((b, out_rows, BRANCH_W), BF16), name="na_latent")(q, k, v, bias)


def _toeplitz_body(r_ref, oh_ref, valid_ref, o_ref):
    h, m, l = _split3(r_ref[...])
    oh = oh_ref[...]
    t = _dot(h, oh) + (_dot(m, oh) + _dot(l, oh))
    o_ref[...] = jnp.where(valid_ref[...] > 0.0, t * LOG2E, NEG_INF)


def _na_bias_table(rpb, rows):
    kh = min(NA_KH, rows)
    nh, ndr, ndc = rpb.shape
    col = jnp.arange(GRID_W)
    cs = jnp.clip(col - NA_KW // 2, 0, GRID_W - NA_KW)
    valid = (col[None, :] >= cs[:, None]) & (col[None, :] < cs[:, None] + NA_KW)
    dc = jnp.clip(col[None, :] - col[:, None] + NA_KW - 1, 0, 2 * NA_KW - 2)
    onehot = (jnp.arange(128)[:, None] == dc.reshape(1, -1)).astype(BF16)
    rp = jnp.pad(rpb.astype(F32).reshape(nh * ndr, ndc), ((0, 0), (0, 128 - ndc)))
    vmask = valid.reshape(1, -1).astype(F32)
    toep = pl.pallas_call(_toeplitz_body, out_shape=_sds((nh * ndr, GRID_W * GRID_W), F32),
                          name="na_bias")(rp, onehot, vmask)
    toep = toep.reshape(nh, ndr, GRID_W, GRID_W)
    per_delta = []
    for delta in range(kh):
        lo = NA_KH - 1 - delta
        t = toep[:, lo:lo + kh].transpose(0, 2, 1, 3)
        per_delta.append(t.reshape(nh, GRID_W, kh * GRID_W))
    return jnp.stack(per_delta, axis=0)


def _attn_body(*refs, nparts, heads):
    q_refs, k_refs, v_ref, o_ref = refs[:nparts], refs[nparts:2 * nparts], refs[2 * nparts], refs[-1]
    s = None
    for qr, kr in zip(q_refs, k_refs):
        t = _qk(qr[0], kr[0])
        s = t if s is None else s + t
    o_ref[0] = _heads_to_lanes(_softmax_pv([s], [v_ref[0]])).astype(BF16)


def _into(prev, n_in):
    if prev is None:
        return [], [], None
    return [pl.BlockSpec(memory_space=pl.ANY)], [prev], {n_in: 0}


def _attention(qs, ks, v, *, tq, q0, nq, tk, kblk, name, out_rows, prev=None):
    b, heads, _, dv = v.shape
    qspec = [pl.BlockSpec((1, heads, tq, q.shape[-1]), lambda b, i: (b, 0, q0 + i, 0)) for q in qs]
    kspec = [pl.BlockSpec((1, heads, tk, k.shape[-1]), lambda b, i: (b, 0, kblk, 0)) for k in ks]
    vspec = pl.BlockSpec((1, heads, tk, dv), lambda b, i: (b, 0, kblk, 0))
    xs, xa, aliases = _into(prev, 2 * len(qs) + 1)
    return _call(
        functools.partial(_attn_body, nparts=len(qs), heads=heads), (b, nq),
        qspec + kspec + [vspec] + xs,
        pl.BlockSpec((1, tq, heads * dv), lambda b, i: (b, q0 + i, 0)),
        _sds((b, out_rows, heads * dv), BF16), name=name, aliases=aliases)(*qs, *ks, v, *xa)


def _mla_q_body(z_ref, gc_ref, w_ref, gh_ref, ct_ref, st_ref, q_ref):
    cq = z_ref[0]
    cqn = (cq * lax.rsqrt(jnp.mean(cq * cq, axis=-1, keepdims=True) + NORM_EPS) * gc_ref[...]).astype(BF16)
    q = _dot(cqn, w_ref[...])
    gh = gh_ref[...]
    ct, st = ct_ref[...], st_ref[...]
    for h in range(MLA_HEADS):
        qh = q[:, h * 256:(h + 1) * 256]
        nope, pe, sw = qh[:, :128], qh[:, 128:192], qh[:, 192:256]
        ms = (jnp.sum(nope * nope, axis=-1, keepdims=True) + jnp.sum(pe * pe, axis=-1, keepdims=True)) \
            / (MLA_NOPE + MLA_ROPE)
        rinv = lax.rsqrt(ms + NORM_EPS)
        rot = (pe * rinv * gh[:, 128:192]) * ct + (sw * rinv * gh[:, 192:256]) * st
        qh = jnp.concatenate([nope * rinv * gh[:, :128], rot, jnp.zeros_like(rot)], axis=-1)
        q_ref[0, h] = (qh * (MLA_SCALE * LOG2E)).astype(BF16)


def _mla_kv_body(zc_ref, zp_ref, gc_ref, w_ref, gh_ref, ct_ref, st_ref, k_ref, v_ref):
    ckv = zc_ref[0]
    cn = (ckv * lax.rsqrt(jnp.mean(ckv * ckv, axis=-1, keepdims=True) + NORM_EPS) * gc_ref[...]).astype(BF16)
    kv = _dot(cn, w_ref[...])
    zp = zp_ref[0]
    pe, sw = zp[:, :64], zp[:, 64:128]
    pe2 = jnp.sum(pe * pe, axis=-1, keepdims=True)
    gh = gh_ref[...]
    ct, st = ct_ref[...], st_ref[...]
    for h in range(MLA_HEADS):
        nope = kv[:, h * 256:h * 256 + 128]
        ms = (jnp.sum(nope * nope, axis=-1, keepdims=True) + pe2) / (MLA_NOPE + MLA_ROPE)
        rinv = lax.rsqrt(ms + NORM_EPS)
        rot = (pe * rinv * gh[:, 128:192]) * ct + (sw * rinv * gh[:, 192:256]) * st
        k_ref[0, h] = jnp.concatenate([nope * rinv * gh[:, :128], rot, jnp.zeros_like(rot)], axis=-1).astype(BF16)
        v_ref[0, h] = kv[:, h * 256 + 128:(h + 1) * 256].astype(BF16)


def _mla_prep(z3, p, ct, st, tr):
    b, t, _ = z3.shape
    rope = pl.BlockSpec((tr, MLA_ROPE), lambda b, i: (i, 0))

    def hm(d):
        return pl.BlockSpec((1, MLA_HEADS, tr, d), lambda b, i: (b, 0, i, 0))

    def shp(d):
        return _sds((b, MLA_HEADS, t, d), BF16)

    q = _call(
        _mla_q_body, (b, t // tr),
        [pl.BlockSpec((1, tr, 512), lambda b, i: (b, i, ZC_CQ // 512)), _full(p['g_cq']), _full(p['w_uq']),
         _full(p['g_q']), rope, rope],
        hm(256), shp(256), name="mla_q_prep")(z3, p['g_cq'], p['w_uq'], p['g_q'], ct, st)
    k, v = _call(
        _mla_kv_body, (b, t // tr),
        [pl.BlockSpec((1, tr, 256), lambda b, i: (b, i, ZC_CKV // 256)),
         pl.BlockSpec((1, tr, 128), lambda b, i: (b, i, ZC_KPE // 128)),
         _full(p['g_ckv']), _full(p['w_ukv']), _full(p['g_k']), rope, rope],
        [hm(256), hm(128)], [shp(256), shp(128)], name="mla_kv_prep")(
            z3, z3, p['g_ckv'], p['w_ukv'], p['g_k'], ct, st)
    return q, k, v


def _rope_tables(l_tok, m_tok):
    half = MLA_ROPE // 4
    freqs = ROPE_THETA ** (-jnp.arange(half, dtype=F32) / half)
    pos = jnp.arange(l_tok)
    ar = (pos // GRID_W).astype(F32)[:, None] * freqs[None, :]
    ac = (pos % GRID_W).astype(F32)[:, None] * freqs[None, :]
    ct = jnp.concatenate([jnp.cos(ar), jnp.cos(ar), jnp.cos(ac), jnp.cos(ac)], axis=-1)
    st = jnp.concatenate([-jnp.sin(ar), jnp.sin(ar), -jnp.sin(ac), jnp.sin(ac)], axis=-1)
    ct = jnp.concatenate([ct, jnp.ones((m_tok, MLA_ROPE), F32)], axis=0)
    st = jnp.concatenate([st, jnp.zeros((m_tok, MLA_ROPE), F32)], axis=0)
    return ct, st


def _swap_halves(a):
    return jnp.concatenate([a[..., 16:32], a[..., 0:16], a[..., 48:64], a[..., 32:48]], axis=-1)


def _dft1_body(x_ref, bh_ref, bl_ref, o_ref):
    xh, xl = _split2(x_ref[0])
    r = _dot(xh, bh_ref[...]) + (_dot(xh, bl_ref[...]) + _dot(xl, bh_ref[...]))
    o_ref[0] = r.astype(BF16)


def _dft2_body(c_ref, s_ref, xc_ref, xs_ref, *rest):
    rest[-1][0] = (_dot(c_ref[...], xc_ref[0]) - _dot(s_ref[...], xs_ref[0])).astype(BF16)


def _dft_mats(n):
    k = jnp.arange(n, dtype=jnp.int32)
    ang = ((k[:, None] * k[None, :]) % n).astype(F32) * (2.0 * math.pi / n)
    s = 1.0 / math.sqrt(n)
    return jnp.cos(ang) * s, jnp.sin(ang) * s


def _fourier_tables(l_tok, m_tok, ctx_out):
    cw, sw = _dft_mats(BRANCH_W // 4)
    eye = jnp.eye(4, dtype=F32)
    bd = jnp.concatenate([jnp.kron(eye, cw), jnp.kron(eye, sw)], axis=1)
    tabs = {'bd': _split2(bd), 'lat': tuple(m.astype(BF16) for m in _dft_mats(l_tok))}
    if ctx_out:
        tabs['ctx'] = tuple(m.astype(BF16) for m in _dft_mats(m_tok))
    return tabs


def _fourier(z3, tabs, tr, l_tok, m_tok, ctx_out):
    b, t, _ = z3.shape
    bh, bl = tabs['bd']
    row = pl.BlockSpec((1, tr, 1024), lambda b, i: (b, i, 0))
    nrow = (t if ctx_out else l_tok) // tr
    xw = _call(
        _dft1_body, (b, nrow),
        [pl.BlockSpec((1, tr, 512), lambda b, i: (b, i, ZC_FN // 512)), _full(bh), _full(bl)],
        row, _sds((b, nrow * tr, 1024), BF16), name="dft_channels")(z3, bh, bl)

    def seq_dft(n, blk, mats, name, prev=None):
        tm = _pick_tile(n, (512, 256, 128, 64, 32, 16, 8))
        mspec = pl.BlockSpec((tm, n), lambda b, i: (i, 0))
        xc = pl.BlockSpec((1, n, 512), lambda b, i: (b, blk, 0))
        xs = pl.BlockSpec((1, n, 512), lambda b, i: (b, blk, 1))
        off = blk * n // tm
        ps, pa, aliases = _into(prev, 4)
        return _call(
            _dft2_body, (b, n // tm), [mspec, mspec, xc, xs] + ps,
            pl.BlockSpec((1, tm, 512), lambda b, i: (b, off + i, 0)),
            _sds((b, nrow * tr, 512), BF16), name=name, aliases=aliases)(*mats, xw, xw, *pa)

    y = seq_dft(l_tok, 0, tabs['lat'], "dft_seq_latent")
    if ctx_out:
        y = seq_dft(m_tok, l_tok // m_tok, tabs['ctx'], "dft_seq_ctx", prev=y)
    return y


def _rw_prep_body(zk_ref, zv_ref, zr_ref, zl_ref, zg_ref, halo_ref, mu_ref, kk_ref, ka_ref, w0_ref, a0_ref,
                  w2_ref, a2_ref, trf_ref, trb_ref, e_ref,
                  v_ref, r_ref, g_ref, cl_ref, kd_ref, b_ref, kx_ref, *, tr):
    row = lax.broadcasted_iota(jnp.int32, (tr, 1), 0)
    halo = halo_ref[0, 0]
    mu = mu_ref[...]

    def mix(x, lo, hi):
        xp = jnp.where(row == 0, halo[0:1, lo:hi], pltpu.roll(x, 1, 0))
        xn = jnp.where(row == tr - 1, halo[1:2, lo:hi], pltpu.roll(x, tr - 1, 0))
        return x + (0.5 * (xp + xn) - x) * mu[:, lo:hi]

    k = mix(zk_ref[0], 0, 512)
    v = mix(zv_ref[0], 512, 1024)
    r = mix(zr_ref[0], 1024, 1536)
    l4 = mix(zl_ref[0], 1536, 1664)
    g_ref[0] = mix(zg_ref[0], 1664, 1792)

    wcat = w0_ref[...] + _dot3(jnp.tanh(l4), w2_ref[...])
    acat = a0_ref[...] + _dot3(l4, a2_ref[...])
    kk = k * kk_ref[...]
    kkn = kk / jnp.maximum(jnp.sqrt(_head_sumsq(kk, e_ref[...])), 1e-12)
    for d, tri_ref in enumerate((trf_ref, trb_ref)):
        w = wcat[:, d * 512:(d + 1) * 512]
        lw = -math.exp(-0.5) * jax.nn.sigmoid(w)
        a = jax.nn.sigmoid(acat[:, d * 512:(d + 1) * 512])
        kd = k * (1.0 + (a - 1.0) * ka_ref[...])
        cl = _dot_sel(tri_ref[...], lw)
        bb = kkn * a
        kx = kkn * jnp.exp(-lw)
        for h in range(RW_HEADS):
            lo = h * RW_DH
            cl_ref[0, d, h] = cl[:, lo:lo + RW_DH]
            kd_ref[0, d, h] = kd[:, lo:lo + RW_DH]
            b_ref[0, d, h] = bb[:, lo:lo + RW_DH]
            kx_ref[0, d, h] = kx[:, lo:lo + RW_DH]
    for h in range(RW_HEADS):
        lo = h * RW_DH
        v_ref[0, h] = v[:, lo:lo + RW_DH]
        r_ref[0, h] = r[:, lo:lo + RW_DH]


def _rw_prep(z3, halo, p, tr):
    b, t, _ = z3.shape
    c = RW_CHUNK
    ti = jnp.arange(tr)
    same = (ti[:, None] // c) == (ti[None, :] // c)
    trf = (same & (ti[None, :] <= ti[:, None])).astype(BF16)
    trb = (same & (ti[None, :] >= ti[:, None])).astype(BF16)

    def zcol(off, w):
        return pl.BlockSpec((1, tr, w), lambda b, i: (b, i, off // w))

    hm = pl.BlockSpec((1, RW_HEADS, tr, RW_DH), lambda b, i: (b, 0, i, 0))
    hmd = pl.BlockSpec((1, 2, RW_HEADS, tr, RW_DH), lambda b, i: (b, 0, 0, i, 0))
    s1 = _sds((b, RW_HEADS, t, RW_DH), F32)
    s2 = _sds((b, 2, RW_HEADS, t, RW_DH), F32)
    consts = [p['rw_mu'], p['rw_k_k'], p['rw_k_a'], p['rw_w0'], p['rw_a0'], p['rw_w2'], p['rw_a2'], trf, trb,
              _head_ones(RW_W, RW_DH)]
    return _call(
        functools.partial(_rw_prep_body, tr=tr), (b, t // tr),
        [zcol(ZC_RWK, 512), zcol(ZC_RWK + 512, 512), zcol(ZC_RWK + 1024, 512), zcol(ZC_L4, 128), zcol(ZC_G, 128),
         pl.BlockSpec((1, 1, 2, 1792), lambda b, i: (b, i, 0, 0))] + [_full(a) for a in consts],
        [hm, hm, pl.BlockSpec((1, tr, 128), lambda b, i: (b, i, 0)), hmd, hmd, hmd, hmd],
        [s1, s1, _sds((b, t, 128), F32), s2, s2, s2, s2], name="rw_prep")(z3, z3, z3, z3, z3, halo, *consts)


RW_P_HI = 1
RW_P_LO = 1


def _rw_scan_tile(r_ref, v_ref, cl_ref, kd_ref, b_ref, kx_ref, y_ref, s_ref, *, tr, rev):
    c = RW_CHUNK
    nc = tr // c
    g = RW_HEADS * nc

    def ld(x):
        return x.reshape(g, c, RW_DH)

    r, v = ld(r_ref[0]), ld(v_ref[0])
    cl, kd, bb, kx = ld(cl_ref[0, 0]), ld(kd_ref[0, 0]), ld(b_ref[0, 0]), ld(kx_ref[0, 0])
    last = 0 if rev else c - 1
    ctot = cl[:, last:last + 1, :]
    e = jnp.exp(cl)
    ei = jnp.exp(-cl)
    ec = jnp.exp(ctot - cl)
    kkt, rt = kx * e, r * e
    kw, bw = kd * ei, bb * ei
    kc, bc = kd * ec, bb * ec
    a_cat = jnp.concatenate([kkt, rt], axis=1)
    pp = _bmm('gtd,gsd->gts', a_cat, jnp.concatenate([kw, bw], axis=1), RW_P_HI)
    ti = lax.broadcasted_iota(jnp.int32, (c, c), 0)
    si = lax.broadcasted_iota(jnp.int32, (c, c), 1)
    strict = (si > ti) if rev else (si < ti)
    incl = (si >= ti) if rev else (si <= ti)
    nmat = jnp.where(strict, pp[:, :c, :c], 0.0)
    x = -jnp.where(strict, pp[:, :c, c:], 0.0)
    ark = jnp.where(incl, pp[:, c:, :c], 0.0)
    arb = jnp.where(incl, pp[:, c:, c:], 0.0)
    eye_c = (ti == si).astype(F32)
    tm = eye_c + x
    xp = _bmm('gts,gsu->gtu', x, x, RW_P_HI)
    nsq = int(math.log2(c)) - 1
    for i in range(nsq):
        if i < nsq - 1:
            both = _bmm('gts,gsu->gtu', xp, jnp.concatenate([tm, xp], axis=2), RW_P_LO)
            tm, xp = tm + both[:, :, :c], both[:, :, c:]
        else:
            tm = tm + _bmm('gts,gsu->gtu', xp, tm, RW_P_LO)
    nav = _bmm('gts,gsd->gtd', jnp.concatenate([nmat, ark], axis=1), v, RW_P_HI)
    nv, arkv = nav[:, :c], nav[:, c:]
    ta = _bmm('gts,gsd->gtd', tm, jnp.concatenate([kkt, nv], axis=2), RW_P_HI)
    ata = _bmm('gts,gsd->gtd', arb, ta, RW_P_HI)
    a2 = rt - ata[:, :, :RW_DH]
    y0 = arkv - ata[:, :, RW_DH:]
    tb = _bmm('gtk,gtd->gkd', bc, ta, RW_P_HI)
    di = lax.broadcasted_iota(jnp.int32, (RW_DH, RW_DH), 0)
    dj = lax.broadcasted_iota(jnp.int32, (RW_DH, RW_DH), 1)
    gmt = jnp.where(di == dj, jnp.exp(ctot), 0.0) - tb[:, :, :RW_DH]
    hmt = _bmm('gtk,gtv->gkv', kc, v, RW_P_HI) - tb[:, :, RW_DH:]

    def per_chunk(x):
        return x.reshape(RW_HEADS, nc, x.shape[1], x.shape[2])

    a2, y0, gmt, hmt = per_chunk(a2), per_chunk(y0), per_chunk(gmt), per_chunk(hmt)
    st = s_ref[...]
    ys = [None] * nc
    for ci in (range(nc - 1, -1, -1) if rev else range(nc)):
        ys[ci] = _bmm('htk,hkv->htv', a2[:, ci], st, RW_P_HI) + y0[:, ci]
        st = _bmm('hke,hev->hkv', gmt[:, ci], st, RW_P_HI) + hmt[:, ci]
    s_ref[...] = st
    y_ref[0] = jnp.concatenate(ys, axis=1)


def _rw_scan_body(*refs, tr):
    fwd, bwd, (yf_ref, yb_ref, sf_ref, sb_ref) = refs[0:6], refs[6:12], refs[12:16]

    @pl.when(pl.program_id(1) == 0)
    def _():
        sf_ref[...] = jnp.zeros_like(sf_ref)
        sb_ref[...] = jnp.zeros_like(sb_ref)

    _rw_scan_tile(*fwd, yf_ref, sf_ref, tr=tr, rev=False)
    _rw_scan_tile(*bwd, yb_ref, sb_ref, tr=tr, rev=True)


def _rw_scan(r, v, cl, kd, bb, kx, tr, nl, nm):
    b, _, t, _ = r.shape
    nt = nl + nm
    tile_f = lambda j: jnp.where(j < nm, nl + j, j - nm)
    tile_b = lambda j: nt - 1 - j

    def specs(tile, d):
        hm = pl.BlockSpec((1, RW_HEADS, tr, RW_DH), lambda b, j: (b, 0, tile(j), 0))
        hmd = pl.BlockSpec((1, 1, RW_HEADS, tr, RW_DH), lambda b, j: (b, d, 0, tile(j), 0))
        return hm, hmd

    hf, hfd = specs(tile_f, 0)
    hb, hbd = specs(tile_b, 1)
    shp = _sds((b, RW_HEADS, t, RW_DH), F32)
    state = pltpu.VMEM((RW_HEADS, RW_DH, RW_DH), F32)
    return _call(
        functools.partial(_rw_scan_body, tr=tr), (b, nt),
        [hf, hf, hfd, hfd, hfd, hfd, hb, hb, hbd, hbd, hbd, hbd], [hf, hb], [shp, shp],
        scratch=[state, state], sem=("parallel", "arbitrary"),
        name="rw_scan")(r, v, cl, kd, bb, kx, r, v, cl, kd, bb, kx)


def _rw_out_body(yf_ref, yb_ref, r_ref, v_ref, kd_ref, g_ref, lnw_ref, lnb_ref, rk_ref, g2_ref, o_ref):
    y = yf_ref[0] + yb_ref[0]
    mu = jnp.mean(y, axis=-1, keepdims=True)
    var = jnp.mean(jnp.square(y - mu), axis=-1, keepdims=True)
    yn = (y - mu) * lax.rsqrt(var + RW_GN_EPS) * lnw_ref[...] + lnb_ref[...]
    ksum = kd_ref[0, 0] + kd_ref[0, 1]
    bonus = jnp.sum(r_ref[0] * ksum * rk_ref[...], axis=-1, keepdims=True) * v_ref[0]
    o = yn + bonus
    o = jnp.concatenate([o[h] for h in range(RW_HEADS)], axis=-1)
    gate = _dot3(jax.nn.sigmoid(g_ref[0]), g2_ref[...])
    o_ref[0] = (o * gate).astype(BF16)


def _rw_out(yf, yb, r, v, kd, g, p, tr, nrow):
    b = r.shape[0]
    hm = pl.BlockSpec((1, RW_HEADS, tr, RW_DH), lambda b, i: (b, 0, i, 0))
    consts = [p['rw_ln_w'], p['rw_ln_b'], p['rw_r_k'], p['rw_g2']]
    return _call(
        _rw_out_body, (b, nrow),
        [hm, hm, hm, hm, pl.BlockSpec((1, 2, RW_HEADS, tr, RW_DH), lambda b, i: (b, 0, 0, i, 0)),
         pl.BlockSpec((1, tr, 128), lambda b, i: (b, i, 0))] + [_full(a) for a in consts],
        pl.BlockSpec((1, tr, RW_W), lambda b, i: (b, i, 0)),
        _sds((b, nrow * tr, RW_W), BF16), name="rw_out")(yf, yb, r, v, kd, g, *consts)


GATE_COL0 = KEY_COLS + 2144


def _merge_body(h_ref, y0_ref, y1_ref, y2_ref, y3_ref, g0_ref, g1_ref, g2_ref, g3_ref, wb_ref, o_ref):
    h = h_ref[0]
    acc = None
    for i, (y_ref, wg_ref) in enumerate(zip((y0_ref, y1_ref, y2_ref, y3_ref), (g0_ref, g1_ref, g2_ref, g3_ref))):
        t = jax.nn.sigmoid(_dot(h, wg_ref[...])) * _dot(y_ref[0], wb_ref[0, i])
        acc = t if acc is None else acc + t
    o_ref[0] = acc.astype(BF16)


def _merge(h, ys, wg, wbr, layer, rows):
    b, _, d = h.shape
    tn = 512
    nn = d // tn
    tm = _pick_tile(rows, (1024, 768, 512, 256, 128, 64))
    row = lambda w: pl.BlockSpec((1, tm, w), lambda n, b, i: (b, i, 0))
    gspecs = [pl.BlockSpec((d, tn), functools.partial(lambda n, b, i, k: (0, k * nn + n), k=k))
              for k in range(N_BRANCH)]
    return _call(
        _merge_body, (nn, b, rows // tm),
        [row(d)] + [row(BRANCH_W)] * 4 + gspecs
        + [pl.BlockSpec((1, N_BRANCH, BRANCH_W, tn), lambda n, b, i: (layer, 0, 0, n))],
        pl.BlockSpec((1, tm, tn), lambda n, b, i: (b, i, n)),
        _sds((b, rows, d), BF16), name="merge")(h, *ys, wg, wg, wg, wg, wbr)


def _outproj_body(a_ref, w_ref, x_ref, md_ref, o_ref, *, tm, l_tok):
    row = pl.program_id(1) * tm + lax.broadcasted_iota(jnp.int32, (tm, 1), 0)
    gate = jnp.where(row < l_tok, md_ref[0, 0, 2:3, :], md_ref[0, 1, 2:3, :])
    o_ref[0] = x_ref[0] + gate * _dot(a_ref[0], w_ref[0])


def _outproj(acc, w_out, layer, xs, md, l_tok):
    b, rows, d = acc.shape
    tm = _pick_tile(rows, (768, 512, 256, 128, 64))
    row = pl.BlockSpec((1, tm, d), lambda b, i: (b, i, 0))
    return _call(
        functools.partial(_outproj_body, tm=tm, l_tok=l_tok), (b, rows // tm),
        [row, pl.BlockSpec((1, d, d), lambda b, i: (layer, 0, 0)), row,
         pl.BlockSpec((1, 2, 6, d), lambda b, i: (b, 0, 0, 0))], row,
        _sds((b, rows, d), F32), name="out_proj")(acc, w_out, xs, md)


def _select_body(lg_ref, slot_ref, aff_ref, slotc_ref, *, n, cap):
    lg = lg_ref[0]
    lane = lax.broadcasted_iota(jnp.int32, lg.shape, 1)
    lg = jnp.where(lane < N_EXPERTS, lg, NEG_INF)
    ex = jnp.exp(lg - lg.max(axis=-1, keepdims=True))
    aff = ex / ex.sum(axis=-1, keepdims=True)
    aff_t = aff.T[:N_EXPERTS]

    def count_ge(t):
        return jnp.sum((aff_t >= t).astype(F32), axis=1, keepdims=True)

    tiny = 2.0 ** -126
    lo = jnp.full((N_EXPERTS, 1), tiny, F32)
    for sh in (64, 32, 16, 8, 4, 2, 1):
        cand = lo * (2.0 ** sh)
        lo = jnp.where(count_ge(cand) >= cap, cand, lo)
    hi = lo * 2.0
    below = count_ge(tiny) < cap
    lo = jnp.where(below, 0.0, lo)
    hi = jnp.where(below, tiny, hi)
    for _ in range(40):
        mid = lo + (hi - lo) * 0.5
        ok = count_ge(mid) >= cap
        lo = jnp.where(ok, mid, lo)
        hi = jnp.where(ok, hi, mid)
    gt = aff_t >= hi
    eq = (aff_t >= lo) & (aff_t < hi)
    need = cap - jnp.sum(gt.astype(F32), axis=1, keepdims=True)
    both = jnp.concatenate([gt.astype(BF16), eq.astype(BF16)], axis=0)
    cw = min(n, 512)
    pre = []
    for cb in range(n // cw):
        tp = lax.broadcasted_iota(jnp.int32, (n, cw), 0)
        tt = lax.broadcasted_iota(jnp.int32, (n, cw), 1) + cb * cw
        pre.append(_dot(both, (tp < tt).astype(BF16)))
    pre = jnp.concatenate(pre, axis=1) if len(pre) > 1 else pre[0]
    pre_gt, pre_eq = pre[:N_EXPERTS], pre[N_EXPERTS:]
    sel = gt | (eq & (pre_eq < need))
    slot = jnp.where(sel, pre_gt + jnp.minimum(pre_eq, need), -1.0)
    slot_ref[0] = slot.astype(jnp.int32)
    aff_ref[0] = aff_t
    pad = jnp.full((128 - N_EXPERTS, n), -1.0, F32)
    slotc_ref[0] = jnp.concatenate([slot, pad], axis=0).T.astype(jnp.int32)


def _select(logits, n, blk, cap, name):
    b = logits.shape[0]
    er = pl.BlockSpec((1, N_EXPERTS, n), lambda b: (b, 0, 0))
    return _call(
        functools.partial(_select_body, n=n, cap=cap), (b,),
        [pl.BlockSpec((1, n, 128), lambda b: (b, blk, 0))],
        [er, er, pl.BlockSpec((1, n, 128), lambda b: (b, 0, 0))],
        [_sds((b, N_EXPERTS, n), jnp.int32), _sds((b, N_EXPERTS, n), F32), _sds((b, n, 128), jnp.int32)],
        name=name)(logits)


def _ffn_body(h_ref, slot_ref, aff_ref, w1_ref, w3_ref, w2_ref, o_ref, *, grp, n, cap):
    xs, gates = [], []
    jrow = lax.broadcasted_iota(jnp.int32, (cap, n), 0)
    for gi in range(grp):
        pick = jrow == slot_ref[gi, 0]
        xs.append(_dot(pick.astype(BF16), h_ref[gi]).astype(BF16))
        gates.append(jnp.sum(jnp.where(pick, aff_ref[gi, 0], 0.0), axis=1, keepdims=True))
    xe = jnp.concatenate(xs, axis=0) if grp > 1 else xs[0]
    gate = jnp.concatenate(gates, axis=0) if grp > 1 else gates[0]
    a = _dot(xe, w1_ref[0, 0])
    u = _dot(xe, w3_ref[0, 0])
    hm = (a * jax.nn.sigmoid(a) * u).astype(BF16)
    ye = (_dot(hm, w2_ref[0, 0]) * gate).astype(BF16)
    for gi in range(grp):
        o_ref[0, gi] = ye[gi * cap:(gi + 1) * cap]


def _moe_ffn(h2, slot, aff, w1, w3, w2, layer, n, blk, cap, grp, name):
    b, _, d = h2.shape
    ff = w1.shape[-1]
    sr = pl.BlockSpec((grp, 1, 1, n), lambda e, g: (g, e, 0, 0))
    return _call(
        functools.partial(_ffn_body, grp=grp, n=n, cap=cap), (N_EXPERTS, b // grp),
        [pl.BlockSpec((grp, n, d), lambda e, g: (g, blk, 0)), sr, sr,
         pl.BlockSpec((1, 1, d, ff), lambda e, g: (layer, e, 0, 0)),
         pl.BlockSpec((1, 1, d, ff), lambda e, g: (layer, e, 0, 0)),
         pl.BlockSpec((1, 1, ff, d), lambda e, g: (layer, e, 0, 0))],
        pl.BlockSpec((1, grp, cap, d), lambda e, g: (e, g, 0, 0)),
        _sds((N_EXPERTS, b, cap, d), BF16), name=name)(
            h2, slot.reshape(b, N_EXPERTS, 1, n), aff.reshape(b, N_EXPERTS, 1, n), w1, w3, w2)


def _combine_body(sc_ref, ye_ref, x_ref, md_ref, *rest, cap):
    o_ref = rest[-1]
    sc = sc_ref[0]
    jj = lax.broadcasted_iota(jnp.int32, (sc.shape[0], cap), 1)
    acc = None
    for e in range(N_EXPERTS):
        put = (sc[:, e:e + 1] == jj).astype(BF16)
        t = _dot(put, ye_ref[e, 0])
        acc = t if acc is None else acc + t
    o_ref[0] = x_ref[0] + md_ref[0, 0, 5:6, :] * acc


def _moe_combine(slotc, ye, x1, md, n, blk_rows, is_ctx, cap, name, out_rows, prev=None):
    b, _, d = x1.shape
    tm = _pick_tile(n, (512, 256, 128, 64, 32, 16, 8))
    off = blk_rows // tm
    extra_specs, extra_args, aliases = [], [], None
    if prev is not None:
        extra_specs, extra_args, aliases = [pl.BlockSpec(memory_space=pl.ANY)], [prev], {4: 0}
    return _call(
        functools.partial(_combine_body, cap=cap), (b, n // tm),
        [pl.BlockSpec((1, tm, 128), lambda b, i: (b, i, 0)),
         pl.BlockSpec((N_EXPERTS, 1, cap, d), lambda b, i: (0, b, 0, 0)),
         pl.BlockSpec((1, tm, d), lambda b, i: (b, off + i, 0)),
         pl.BlockSpec((1, 1, 6, d), lambda b, i: (b, 1 if is_ctx else 0, 0, 0))] + extra_specs,
        pl.BlockSpec((1, tm, d), lambda b, i: (b, off + i, 0)),
        _sds((b, out_rows, d), F32), name=name, aliases=aliases)(slotc, ye, x1, md, *extra_args)


def _moe(x1, h2, logits, md, w1, w3, w2, layer, l_tok, m_tok, ctx_out):
    b = x1.shape[0]
    cap_l = CAPACITY_FACTOR * l_tok // N_EXPERTS
    slot, aff, slotc = _select(logits, l_tok, 0, cap_l, "moe_select_latent")
    ye = _moe_ffn(h2, slot, aff, w1, w3, w2, layer, l_tok, 0, cap_l, 1, "moe_ffn_latent")
    out_rows = l_tok + m_tok if ctx_out else l_tok
    x2 = _moe_combine(slotc, ye, x1, md, l_tok, 0, False, cap_l, "moe_combine_latent", out_rows)
    if not ctx_out:
        return x2
    cap_c = CAPACITY_FACTOR * m_tok // N_EXPERTS
    blk = l_tok // m_tok
    slot, aff, slotc = _select(logits, m_tok, blk, cap_c, "moe_select_ctx")
    ye = _moe_ffn(h2, slot, aff, w1, w3, w2, layer, m_tok, blk, cap_c, b, "moe_ffn_ctx")
    return _moe_combine(slotc, ye, x1, md, m_tok, l_tok, True, cap_c, "moe_combine_ctx", out_rows, prev=x2)


def _layer_params(l, w_in, na_q_norm, na_k_norm, na_rpb, mla_cq_norm, mla_ckv_norm, mla_w_uq, mla_w_ukv,
                  mla_q_norm, mla_k_norm, rw_mu_ks, rw_mu_qs, rw_w0, rw_w2, rw_a0, rw_a2, rw_g2, rw_k_k, rw_k_a,
                  rw_r_k, rw_ln_w, rw_ln_b, w_br, w_out, rows):
    w = w_in[l]
    d = w.shape[0]
    kq = KEY_COLS
    kpe = w[:, 1280:1344]
    zpad = lambda n: jnp.zeros((d, n), w.dtype)
    wz = jnp.concatenate([
        w[:, kq:kq + 512], w[:, 0:512], w[:, 512:1024],
        w[:, kq + 512:kq + 1024],
        w[:, 1344:1856], w[:, 1856:2368], w[:, kq + 1024:kq + 1536],
        w[:, kq + 1632:kq + 2144],
        w[:, 1024:1280],
        kpe, _swap_halves(kpe),
        w[:, 2368:2496],
        w[:, kq + 1536:kq + 1632], zpad(32)], axis=1).astype(BF16)
    assert wz.shape[1] == NZ
    wg = w[:, GATE_COL0:].astype(BF16)

    def head_gain(g):
        return jnp.concatenate([g[:128], g[128:], _swap_halves(g[128:])])[None, :]

    wq = mla_w_uq[l].reshape(MLA_QLORA, MLA_HEADS, MLA_NOPE + MLA_ROPE)
    wq = jnp.concatenate([wq, _swap_halves(wq[:, :, MLA_NOPE:])], axis=-1).reshape(MLA_QLORA, MLA_HEADS * 256)
    w2cat = jnp.zeros((128, 1024), F32).at[0:32, 0:512].set(rw_w2[l, 0]).at[32:64, 512:].set(rw_w2[l, 1])
    a2cat = jnp.zeros((128, 1024), F32).at[64:96, 0:512].set(rw_a2[l, 0]).at[96:128, 512:].set(rw_a2[l, 1])
    mu = jnp.concatenate([rw_mu_ks[l][:1024], rw_mu_qs[l][:512], rw_mu_ks[l][1024:], rw_mu_qs[l][512:],
                          jnp.zeros((32,), F32)])[None, :]
    hd = lambda a: a.reshape(RW_HEADS, 1, RW_DH)
    return {
        'wz': wz, 'wg': wg,
        'na_gq': na_q_norm[l][None, :], 'na_gk': na_k_norm[l][None, :],
        'na_bias': _na_bias_table(na_rpb[l], rows),
        'g_cq': mla_cq_norm[l][None, :], 'g_ckv': mla_ckv_norm[l][None, :],
        'w_uq': wq.astype(BF16), 'w_ukv': mla_w_ukv[l].astype(BF16),
        'g_q': head_gain(mla_q_norm[l]), 'g_k': head_gain(mla_k_norm[l]),
        'rw_mu': mu, 'rw_k_k': rw_k_k[l][None, :], 'rw_k_a': rw_k_a[l][None, :],
        'rw_w0': rw_w0[l].reshape(1, 1024), 'rw_a0': rw_a0[l].reshape(1, 1024), 'rw_w2': w2cat, 'rw_a2': a2cat,
        'rw_ln_w': hd(rw_ln_w[l]), 'rw_ln_b': hd(rw_ln_b[l]), 'rw_r_k': hd(rw_r_k[l]),
        'rw_g2': jnp.concatenate([rw_g2[l], jnp.zeros((32, RW_W), F32)], axis=0),
    }


def _rw_halo(z3, tr, nl):
    b, t, _ = z3.shape
    nt = t // tr
    z4 = z3.reshape(b, nt, tr, z3.shape[-1])
    pick = lambda a: jnp.concatenate([a[..., ZC_RWK:ZC_RWK + 1536], a[..., ZC_L4:ZC_L4 + 256]], axis=-1)
    first, last = pick(z4[:, :, 0, :]), pick(z4[:, :, tr - 1, :])
    zero = jnp.zeros_like(first[:, :1])
    prev = jnp.concatenate([zero, last[:, :-1]], axis=1)
    nxt = jnp.concatenate([first[:, 1:], zero], axis=1)
    tile = jnp.arange(nt)[None, :, None]
    prev = jnp.where(tile == nl, 0.0, prev)
    nxt = jnp.where(tile == nl - 1, 0.0, nxt)
    return jnp.stack([prev, nxt], axis=2)


def _layer(xs, md, p, g1n, g2n, wr, deep, layer, tabs, l_tok, m_tok, tr, ctx_out):
    b, t, d = xs.shape
    nl, nm = l_tok // tr, m_tok // tr
    nrow = (nl + nm) if ctx_out else nl
    blk = l_tok // m_tok

    h = _norm1(xs, g1n, md, tr, nl)
    z3 = _matmul(h.reshape(b * t, d), p['wz'], F32).reshape(b, t, NZ)

    nq, nk, nv = _na_prep(z3, p['na_gq'], p['na_gk'], tr)
    rows = nrow * tr
    y_na = _na_latent(nq, nk, nv, p['na_bias'], l_tok, m_tok, rows)
    mq, mk, mv = _mla_prep(z3, p, tabs['rope_c'], tabs['rope_s'], tr)
    tq = _pick_tile(l_tok, (256, 128, 64))
    y_mla = _attention([mq], [mk], mv, tq=tq, q0=0, nq=l_tok // tq, tk=t, kblk=0, name="mla_latent", out_rows=rows)
    rv, rr, rg, cl, kd, bb, kx = _rw_prep(z3, _rw_halo(z3, tr, nl), p, tr)
    yf, yb = _rw_scan(rr, rv, cl, kd, bb, kx, tr, nl, nm)
    y_rw = _rw_out(yf, yb, rr, rv, kd, rg, p, tr, nrow)
    y_fn = _fourier(z3, tabs, tr, l_tok, m_tok, ctx_out)

    if ctx_out:
        ctx_q = dict(tq=m_tok, q0=blk, nq=1, tk=m_tok, kblk=blk, out_rows=rows)
        y_na = _attention([nq], [nk], nv, name="na_ctx", prev=y_na, **ctx_q)
        y_mla = _attention([mq], [mk], mv, name="mla_ctx", prev=y_mla, **ctx_q)

    acc = _merge(h, (y_fn, y_na, y_mla, y_rw), p['wg'], deep['w_br'], layer, rows)
    x1 = _outproj(acc, deep['w_out'], layer, xs, md, l_tok)
    h2, logits = _norm2_router(x1, g2n, md, wr, tr, nl)
    return _moe(x1, h2, logits, md, deep['w1'], deep['w3'], deep['w2'], layer, l_tok, m_tok, ctx_out)


def kernel(x, c, ctx, c_ctx, ada_w, ada_b, norm1_g, norm2_g, w_in, na_q_norm, na_k_norm, na_rpb, mla_cq_norm, mla_ckv_norm, mla_w_uq, mla_w_ukv, mla_q_norm, mla_k_norm, rw_mu_ks, rw_mu_qs, rw_w0, rw_w2, rw_a0, rw_a2, rw_g2, rw_k_k, rw_k_a, rw_r_k, rw_ln_w, rw_ln_b, w_br, w_out, moe_router, moe_w1, moe_w3, moe_w2):
    b, l_tok, d = x.shape
    m_tok = ctx.shape[1]
    depth = ada_w.shape[0]
    tr = min(m_tok, 256)
    assert l_tok % tr == 0 and m_tok % tr == 0 and l_tok % m_tok == 0 and tr % RW_CHUNK == 0
    assert l_tok % GRID_W == 0 and m_tok % 128 == 0

    nr = -(-(b + 1) // 8) * 8
    cc = jnp.concatenate([c, c_ctx[None, :], jnp.zeros((nr - b - 1, d), F32)], axis=0)
    mod = _modulation(cc, ada_w, ada_b)
    tabs = {}
    tabs['rope_c'], tabs['rope_s'] = _rope_tables(l_tok, m_tok)

    deep = {'w1': moe_w1.astype(BF16), 'w3': moe_w3.astype(BF16), 'w2': moe_w2.astype(BF16),
            'w_br': w_br.astype(BF16), 'w_out': w_out.astype(BF16)}
    tabs.update(_fourier_tables(l_tok, m_tok, depth > 1))
    xs = jnp.concatenate([x, ctx], axis=1)
    for l in range(depth):
        ctx_out = l < depth - 1
        p = _layer_params(l, w_in, na_q_norm, na_k_norm, na_rpb, mla_cq_norm, mla_ckv_norm, mla_w_uq, mla_w_ukv,
                          mla_q_norm, mla_k_norm, rw_mu_ks, rw_mu_qs, rw_w0, rw_w2, rw_a0, rw_a2, rw_g2, rw_k_k,
                          rw_k_a, rw_r_k, rw_ln_w, rw_ln_b, w_br, w_out, l_tok // GRID_W)
        md = jnp.stack([mod[l, :b], jnp.broadcast_to(mod[l, b], (b, 6 * d))], axis=1).reshape(b, 2, 6, d)
        wr = jnp.concatenate([moe_router[l], jnp.zeros((d, 128 - N_EXPERTS), F32)], axis=1)
        xs = _layer(xs, md, p, norm1_g[l][None, :], norm2_g[l][None, :], wr, deep, l, tabs, l_tok, m_tok, tr, ctx_out)
    return xs
```

```python
import functools
import math

import jax
import jax.numpy as jnp
from jax import lax
from jax.experimental import pallas as pl
from jax.experimental.pallas import tpu as pltpu

F32 = jnp.float32
BF16 = jnp.bfloat16

GRID_W = 64
N_BRANCH = 4
BRANCH_W = 512
NA_HEADS, NA_DH, NA_KH, NA_KW = 8, 64, 8, 16
NA_SCALE = NA_DH ** -0.5
MLA_HEADS, MLA_NOPE, MLA_ROPE = 4, 128, 64
MLA_QLORA = 512
MLA_SCALE = (MLA_NOPE + MLA_ROPE) ** -0.5
RW_HEADS, RW_DH, RW_W = 8, 64, 512
RW_GN_EPS = 64e-5
N_EXPERTS = 16
CAPACITY_FACTOR = 2
ROPE_THETA = 10000.0
NORM_EPS = 1e-6
NEG_INF = -1e30
LOG2E = math.log2(math.e)
RW_CHUNK = 64
KEY_COLS = 2496
NZ = 4736
ZC_NA = 0
ZC_CQ = 1536
ZC_RWK = 2048
ZC_FN = 3584
ZC_CKV = 4096
ZC_KPE = 4352
ZC_L4 = 4480
ZC_G = 4608

VMEM_LIMIT = 56 * 2 ** 20


def _call(body, grid, in_specs, out_specs, out_shape, scratch=(), sem=None, name=None, aliases=None):
    return pl.pallas_call(
        body, grid=grid, in_specs=in_specs, out_specs=out_specs, out_shape=out_shape,
        scratch_shapes=list(scratch), name=name, input_output_aliases=aliases or {},
        compiler_params=pltpu.CompilerParams(
            dimension_semantics=sem or ("parallel",) * len(grid), vmem_limit_bytes=VMEM_LIMIT))


def _full(a):
    nd = a.ndim
    return pl.BlockSpec(a.shape, lambda *_: (0,) * nd)


def _sds(shape, dtype):
    return jax.ShapeDtypeStruct(shape, dtype)


def _dot(a, b):
    return jnp.dot(a, b, preferred_element_type=F32)


def _split2(x):
    hi = x.astype(BF16)
    return hi, (x - hi.astype(F32)).astype(BF16)


def _split3(x):
    hi = x.astype(BF16)
    r = x - hi.astype(F32)
    mid = r.astype(BF16)
    return hi, mid, (r - mid.astype(F32)).astype(BF16)


def _dot3(a, b):
    ah, al = _split2(a)
    bh, bl = _split2(b)
    return _dot(ah, bh) + (_dot(ah, bl) + _dot(al, bh))


def _dot_sel(sel_bf16, x):
    h, m, l = _split3(x)
    return _dot(sel_bf16, h) + (_dot(sel_bf16, m) + _dot(sel_bf16, l))


def _bmm(spec, a, b, passes):
    e = functools.partial(jnp.einsum, spec, preferred_element_type=F32)
    if passes == 1:
        return e(a.astype(BF16), b.astype(BF16))
    ah, al = _split2(a)
    bh, bl = _split2(b)
    return e(ah, bh) + (e(ah, bl) + e(al, bh))


def _pick_tile(n, cands):
    for c in cands:
        if n % c == 0:
            return c
    raise ValueError(f"no tile for {n}")


def _mod_body(c_ref, w_ref, b_ref, o_ref):
    cc = c_ref[...]
    s = cc * jax.nn.sigmoid(cc)
    tn = o_ref.shape[-1]
    for j in range(0, tn, 1024):
        o_ref[0, :, j:j + 1024] = _dot3(s, w_ref[0, :, j:j + 1024]) + b_ref[0, :, j:j + 1024]


def _modulation(cc, ada_w, ada_b):
    depth, d, n6 = ada_w.shape
    r = cc.shape[0]
    tn = 2048
    return _call(
        _mod_body, (depth, n6 // tn),
        [pl.BlockSpec((r, d), lambda l, n: (0, 0)),
         pl.BlockSpec((1, d, tn), lambda l, n: (l, 0, n)),
         pl.BlockSpec((1, 1, tn), lambda l, n: (l, 0, n))],
        pl.BlockSpec((1, r, tn), lambda l, n: (l, 0, n)),
        _sds((depth, r, n6), F32), name="adaln_mod")(cc, ada_w, ada_b.reshape(depth, 1, n6))


def _modnorm(x, g, shift, scale):
    y = x * lax.rsqrt(jnp.mean(x * x, axis=-1, keepdims=True) + NORM_EPS)
    return (y * g) * (1.0 + scale) + shift


def _norm1_body(x_ref, g_ref, md_ref, h_ref):
    h = _modnorm(x_ref[0], g_ref[...], md_ref[0, 0, 0:1, :], md_ref[0, 0, 1:2, :])
    h_ref[0] = h.astype(BF16)


def _norm2_body(x_ref, g_ref, md_ref, wr_ref, h_ref, lg_ref):
    h = _modnorm(x_ref[0], g_ref[...], md_ref[0, 0, 3:4, :], md_ref[0, 0, 4:5, :])
    h_ref[0] = h.astype(BF16)
    lg_ref[0] = _dot3(h, wr_ref[...])


def _md_spec(d, nl):
    return pl.BlockSpec((1, 1, 6, d), lambda b, i: (b, i // nl, 0, 0))


def _norm1(xs, g, md, tr, nl):
    b, t, d = xs.shape
    return _call(
        _norm1_body, (b, t // tr),
        [pl.BlockSpec((1, tr, d), lambda b, i: (b, i, 0)), _full(g), _md_spec(d, nl)],
        pl.BlockSpec((1, tr, d), lambda b, i: (b, i, 0)),
        _sds((b, t, d), BF16), name="norm1")(xs, g, md)


def _norm2_router(x1, g, md, wr, tr, nl):
    b, t, d = x1.shape
    return _call(
        _norm2_body, (b, t // tr),
        [pl.BlockSpec((1, tr, d), lambda b, i: (b, i, 0)), _full(g), _md_spec(d, nl), _full(wr)],
        [pl.BlockSpec((1, tr, d), lambda b, i: (b, i, 0)), pl.BlockSpec((1, tr, 128), lambda b, i: (b, i, 0))],
        [_sds((b, t, d), BF16), _sds((b, t, 128), F32)], name="norm2_router")(x1, g, md, wr)


def _mm_body(a_ref, w_ref, o_ref):
    o_ref[...] = _dot(a_ref[...], w_ref[...]).astype(o_ref.dtype)


def _matmul(a, w, out_dtype):
    r, k = a.shape
    n = w.shape[1]
    tm = _pick_tile(r, (512, 384, 256, 128, 64, 32, 16, 8))
    return _call(
        _mm_body, (r // tm,),
        [pl.BlockSpec((tm, k), lambda i: (i, 0)),
         pl.BlockSpec((k, n), lambda i: (0, 0), pipeline_mode=pl.Buffered(1))],
        pl.BlockSpec((tm, n), lambda i: (i, 0)),
        _sds((r, n), out_dtype), name="in_proj")(a, w)


def _head_sum(x, ones_bd):
    hi, lo = _split2(x)
    return _dot(hi, ones_bd) + _dot(lo, ones_bd)


def _head_sumsq(x, ones_bd):
    return _head_sum(x * x, ones_bd)


def _head_ones(width, dh):
    i = jnp.arange(width) // dh
    return (i[:, None] == i[None, :]).astype(BF16)


def _na_prep_body(z_ref, gq_ref, gk_ref, e_ref, q_ref, k_ref, v_ref):
    z = z_ref[0]
    q, k, v = z[:, :512], z[:, 512:1024], z[:, 1024:]
    e = e_ref[...]
    qn = (q * lax.rsqrt(_head_sumsq(q, e) / NA_DH + NORM_EPS) * (gq_ref[...] * (NA_SCALE * LOG2E))).astype(BF16)
    kn = (k * lax.rsqrt(_head_sumsq(k, e) / NA_DH + NORM_EPS) * gk_ref[...]).astype(BF16)
    vb = v.astype(BF16)
    for h in range(NA_HEADS):
        lo = h * NA_DH
        q_ref[0, h] = qn[:, lo:lo + NA_DH]
        k_ref[0, h] = kn[:, lo:lo + NA_DH]
        v_ref[0, h] = vb[:, lo:lo + NA_DH]


def _softmax_pv(s, vs):
    m = s[0].max(axis=-1, keepdims=True)
    for t in s[1:]:
        m = jnp.maximum(m, t.max(axis=-1, keepdims=True))
    den = None
    acc = None
    for t, v in zip(s, vs):
        p = jnp.exp2(t - m)
        ps = p.sum(axis=-1, keepdims=True)
        o = jnp.einsum('hqk,hkd->hqd', p.astype(BF16), v, preferred_element_type=F32)
        den = ps if den is None else den + ps
        acc = o if acc is None else acc + o
    return acc / den


def _qk(q, k):
    return jnp.einsum('hqd,hkd->hqk', q, k, preferred_element_type=F32)


def _heads_to_lanes(o):
    return jnp.concatenate([o[h] for h in range(o.shape[0])], axis=-1)


NA_ROWS_PER_STEP = 4


def _na_body(q_ref, k_ref, v_ref, bias_ref, o_ref, *, rows, kh, l_tok, m_tok, rps):
    kc = k_ref[0, :, l_tok:l_tok + m_tok, :]
    vc = v_ref[0, :, l_tok:l_tok + m_tok, :]

    @pl.when(pl.program_id(1) < rows // rps)
    def _():
        for rr in range(rps):
            r = pl.program_id(1) * rps + rr
            rs = jnp.clip(r - kh // 2, 0, rows - kh)
            delta = r - rs
            start = pl.multiple_of(rs * GRID_W, GRID_W)
            q = q_ref[0, :, rr * GRID_W:(rr + 1) * GRID_W, :]
            kb = k_ref[0, :, pl.ds(start, kh * GRID_W), :]
            vb = v_ref[0, :, pl.ds(start, kh * GRID_W), :]
            o = _softmax_pv([_qk(q, kb) + bias_ref[delta], _qk(q, kc)], [vb, vc])
            o_ref[0, rr * GRID_W:(rr + 1) * GRID_W, :] = _heads_to_lanes(o).astype(BF16)

    @pl.when(pl.program_id(1) >= rows // rps)
    def _():
        o_ref[0] = _heads_to_lanes(_softmax_pv([_qk(q_ref[0], kc)], [vc])).astype(BF16)


def _na_attention(q, k, v, bias, l_tok, m_tok, out_rows):
    b, _, t, _ = q.shape
    rows = l_tok // GRID_W
    kh = min(NA_KH, rows)
    rps = math.gcd(math.gcd(rows, NA_ROWS_PER_STEP), (out_rows - l_tok) // GRID_W)
    kv = pl.BlockSpec((1, NA_HEADS, t, NA_DH), lambda b, r: (b, 0, 0, 0))
    return _call(
        functools.partial(_na_body, rows=rows, kh=kh, l_tok=l_tok, m_tok=m_tok, rps=rps),
        (b, out_rows // (rps * GRID_W)),
        [pl.BlockSpec((1, NA_HEADS, rps * GRID_W, NA_DH), lambda b, r: (b, 0, r, 0)), kv, kv, _full(bias)],
        pl.BlockSpec((1, rps * GRID_W, BRANCH_W), lambda b, r: (b, r, 0)),
        _sds((b, out_rows, BRANCH_W), BF16), name="na_attention")(q, k, v, bias)


def _toeplitz_body(r_ref, oh_ref, valid_ref, o_ref):
    h, m, l = _split3(r_ref[...])
    oh = oh_ref[...]
    t = _dot(h, oh) + (_dot(m, oh) + _dot(l, oh))
    o_ref[...] = jnp.where(valid_ref[...] > 0.0, t * LOG2E, NEG_INF)


def _na_bias_table(rpb, rows):
    kh = min(NA_KH, rows)
    nh, ndr, ndc = rpb.shape
    col = jnp.arange(GRID_W)
    cs = jnp.clip(col - NA_KW // 2, 0, GRID_W - NA_KW)
    valid = (col[None, :] >= cs[:, None]) & (col[None, :] < cs[:, None] + NA_KW)
    dc = jnp.clip(col[None, :] - col[:, None] + NA_KW - 1, 0, 2 * NA_KW - 2)
    onehot = (jnp.arange(128)[:, None] == dc.reshape(1, -1)).astype(BF16)
    rp = jnp.pad(rpb.astype(F32).reshape(nh * ndr, ndc), ((0, 0), (0, 128 - ndc)))
    vmask = valid.reshape(1, -1).astype(F32)
    toep = pl.pallas_call(_toeplitz_body, out_shape=_sds((nh * ndr, GRID_W * GRID_W), F32),
                          name="na_bias")(rp, onehot, vmask)
    toep = toep.reshape(nh, ndr, GRID_W, GRID_W)
    per_delta = []
    for delta in range(kh):
        lo = NA_KH - 1 - delta
        t = toep[:, lo:lo + kh].transpose(0, 2, 1, 3)
        per_delta.append(t.reshape(nh, GRID_W, kh * GRID_W))
    return jnp.stack(per_delta, axis=0)


def _mla_body(q_ref, k_ref, v_ref, o_ref, *, nl, l_tok):
    @pl.when(pl.program_id(1) < nl)
    def _():
        o_ref[0] = _heads_to_lanes(_softmax_pv([_qk(q_ref[0], k_ref[0])], [v_ref[0]])).astype(BF16)

    @pl.when(pl.program_id(1) >= nl)
    def _():
        kc, vc = k_ref[0, :, l_tok:, :], v_ref[0, :, l_tok:, :]
        o_ref[0] = _heads_to_lanes(_softmax_pv([_qk(q_ref[0], kc)], [vc])).astype(BF16)


def _zero_tail(body, nvalid):
    def wrapped(*refs):
        @pl.when(pl.program_id(1) < nvalid)
        def _():
            body(*refs)

        @pl.when(pl.program_id(1) >= nvalid)
        def _():
            refs[-1][...] = jnp.zeros_like(refs[-1])

    return wrapped


def _into(prev, n_in):
    if prev is None:
        return [], [], None
    return [pl.BlockSpec(memory_space=pl.ANY)], [prev], {n_in: 0}


def _mla_attention(q, k, v, tq, l_tok, out_rows):
    b, heads, t, dv = v.shape
    kv = lambda a: pl.BlockSpec((1, heads, t, a.shape[-1]), lambda b, i: (b, 0, 0, 0))
    return _call(
        functools.partial(_mla_body, nl=l_tok // tq, l_tok=l_tok), (b, out_rows // tq),
        [pl.BlockSpec((1, heads, tq, q.shape[-1]), lambda b, i: (b, 0, i, 0)), kv(k), kv(v)],
        pl.BlockSpec((1, tq, heads * dv), lambda b, i: (b, i, 0)),
        _sds((b, out_rows, heads * dv), BF16), name="mla_attention")(q, k, v)


def _mla_q_body(z_ref, gc_ref, w_ref, gh_ref, ct_ref, st_ref, q_ref):
    cq = z_ref[0]
    cqn = (cq * lax.rsqrt(jnp.mean(cq * cq, axis=-1, keepdims=True) + NORM_EPS) * gc_ref[...]).astype(BF16)
    q = _dot(cqn, w_ref[...])
    gh = gh_ref[...]
    ct, st = ct_ref[...], st_ref[...]
    for h in range(MLA_HEADS):
        qh = q[:, h * 256:(h + 1) * 256]
        nope, pe, sw = qh[:, :128], qh[:, 128:192], qh[:, 192:256]
        ms = (jnp.sum(nope * nope, axis=-1, keepdims=True) + jnp.sum(pe * pe, axis=-1, keepdims=True)) \
            / (MLA_NOPE + MLA_ROPE)
        rinv = lax.rsqrt(ms + NORM_EPS)
        rot = (pe * rinv * gh[:, 128:192]) * ct + (sw * rinv * gh[:, 192:256]) * st
        qh = jnp.concatenate([nope * rinv * gh[:, :128], rot, jnp.zeros_like(rot)], axis=-1)
        q_ref[0, h] = (qh * (MLA_SCALE * LOG2E)).astype(BF16)


def _mla_kv_body(zc_ref, zp_ref, gc_ref, w_ref, gh_ref, ct_ref, st_ref, k_ref, v_ref):
    ckv = zc_ref[0]
    cn = (ckv * lax.rsqrt(jnp.mean(ckv * ckv, axis=-1, keepdims=True) + NORM_EPS) * gc_ref[...]).astype(BF16)
    kv = _dot(cn, w_ref[...])
    zp = zp_ref[0]
    pe, sw = zp[:, :64], zp[:, 64:128]
    pe2 = jnp.sum(pe * pe, axis=-1, keepdims=True)
    gh = gh_ref[...]
    ct, st = ct_ref[...], st_ref[...]
    for h in range(MLA_HEADS):
        nope = kv[:, h * 256:h * 256 + 128]
        ms = (jnp.sum(nope * nope, axis=-1, keepdims=True) + pe2) / (MLA_NOPE + MLA_ROPE)
        rinv = lax.rsqrt(ms + NORM_EPS)
        rot = (pe * rinv * gh[:, 128:192]) * ct + (sw * rinv * gh[:, 192:256]) * st
        k_ref[0, h] = jnp.concatenate([nope * rinv * gh[:, :128], rot, jnp.zeros_like(rot)], axis=-1).astype(BF16)
        v_ref[0, h] = kv[:, h * 256 + 128:(h + 1) * 256].astype(BF16)


def _rope_tables(l_tok, m_tok):
    half = MLA_ROPE // 4
    freqs = ROPE_THETA ** (-jnp.arange(half, dtype=F32) / half)
    pos = jnp.arange(l_tok)
    ar = (pos // GRID_W).astype(F32)[:, None] * freqs[None, :]
    ac = (pos % GRID_W).astype(F32)[:, None] * freqs[None, :]
    ct = jnp.concatenate([jnp.cos(ar), jnp.cos(ar), jnp.cos(ac), jnp.cos(ac)], axis=-1)
    st = jnp.concatenate([-jnp.sin(ar), jnp.sin(ar), -jnp.sin(ac), jnp.sin(ac)], axis=-1)
    ct = jnp.concatenate([ct, jnp.ones((m_tok, MLA_ROPE), F32)], axis=0)
    st = jnp.concatenate([st, jnp.zeros((m_tok, MLA_ROPE), F32)], axis=0)
    return ct, st


def _swap_halves(a):
    return jnp.concatenate([a[..., 16:32], a[..., 0:16], a[..., 48:64], a[..., 32:48]], axis=-1)


def _dft1_body(x_ref, bh_ref, bl_ref, o_ref):
    xh, xl = _split2(x_ref[0])
    r = _dot(xh, bh_ref[...]) + (_dot(xh, bl_ref[...]) + _dot(xl, bh_ref[...]))
    o_ref[0] = r.astype(BF16)


def _dft2_body(c_ref, s_ref, xc_ref, xs_ref, *rest):
    rest[-1][0] = (_dot(c_ref[...], xc_ref[0]) - _dot(s_ref[...], xs_ref[0])).astype(BF16)


def _dft_mats(n):
    k = jnp.arange(n, dtype=jnp.int32)
    ang = ((k[:, None] * k[None, :]) % n).astype(F32) * (2.0 * math.pi / n)
    s = 1.0 / math.sqrt(n)
    return jnp.cos(ang) * s, jnp.sin(ang) * s


def _fourier_tables(l_tok, m_tok, ctx_out):
    cw, sw = _dft_mats(BRANCH_W // 4)
    eye = jnp.eye(4, dtype=F32)
    bd = jnp.concatenate([jnp.kron(eye, cw), jnp.kron(eye, sw)], axis=1)
    tabs = {'bd': _split2(bd), 'lat': tuple(m.astype(BF16) for m in _dft_mats(l_tok))}
    if ctx_out:
        tabs['ctx'] = tuple(m.astype(BF16) for m in _dft_mats(m_tok))
    return tabs


def _mixer_prep_body(zna, zcq, zckv, zkpe, zfn, gq, gk, e, gcq, wuq, ghq, ct, st, gckv, wukv, ghk, bh, bl,
                     nq, nk, nv, mq, mk, mv, xw):
    _na_prep_body(zna, gq, gk, e, nq, nk, nv)
    _mla_q_body(zcq, gcq, wuq, ghq, ct, st, mq)
    _mla_kv_body(zckv, zkpe, gckv, wukv, ghk, ct, st, mk, mv)
    _dft1_body(zfn, bh, bl, xw)


def _mixer_prep(z3, p, tabs, tr):
    b, t, _ = z3.shape
    bh, bl = tabs['bd']
    e = _head_ones(NA_HEADS * NA_DH, NA_DH)
    gq, gk = jnp.tile(p['na_gq'], (1, NA_HEADS)), jnp.tile(p['na_gk'], (1, NA_HEADS))
    consts = [gq, gk, e, p['g_cq'], p['w_uq'], p['g_q'], None, None, p['g_ckv'], p['w_ukv'], p['g_k'], bh, bl]
    rope = pl.BlockSpec((tr, MLA_ROPE), lambda b, i: (i, 0))

    def zcol(off, w):
        return pl.BlockSpec((1, tr, w), lambda b, i: (b, i, off // w))

    def hm(heads, d):
        return pl.BlockSpec((1, heads, tr, d), lambda b, i: (b, 0, i, 0)), _sds((b, heads, t, d), BF16)

    outs = [hm(NA_HEADS, NA_DH)] * 3 + [hm(MLA_HEADS, 256), hm(MLA_HEADS, 256), hm(MLA_HEADS, 128),
                                        (pl.BlockSpec((1, tr, 1024), lambda b, i: (b, i, 0)), _sds((b, t, 1024), BF16))]
    args = [tabs['rope_c'] if c is None else c for c in consts]
    args[7] = tabs['rope_s']
    return _call(
        _mixer_prep_body, (b, t // tr),
        [zcol(ZC_NA, 1536), zcol(ZC_CQ, 512), zcol(ZC_CKV, 256), zcol(ZC_KPE, 128), zcol(ZC_FN, 512)]
        + [rope if c is None else _full(c) for c in consts],
        [o[0] for o in outs], [o[1] for o in outs], name="mixer_prep")(z3, z3, z3, z3, z3, *args)


def _fourier(xw, tabs, tr, l_tok, m_tok, ctx_out):
    b = xw.shape[0]
    nrow = (l_tok + m_tok if ctx_out else l_tok) // tr

    def seq_dft(n, blk, mats, name, prev=None):
        rest = 0 if prev is not None else nrow * tr - n
        tm = _pick_tile(math.gcd(n, rest), (512, 256, 128, 64, 32, 16, 8))
        nq, nfill = n // tm, rest // tm
        mspec = pl.BlockSpec((tm, n), lambda b, i: (jnp.minimum(i, nq - 1), 0))
        xc = pl.BlockSpec((1, n, 512), lambda b, i: (b, blk, 0))
        xs = pl.BlockSpec((1, n, 512), lambda b, i: (b, blk, 1))
        off = blk * n // tm
        ps, pa, aliases = _into(prev, 4)
        return _call(
            _zero_tail(_dft2_body, nq) if nfill else _dft2_body, (b, nq + nfill), [mspec, mspec, xc, xs] + ps,
            pl.BlockSpec((1, tm, 512), lambda b, i: (b, off + i, 0)),
            _sds((b, nrow * tr, 512), BF16), name=name, aliases=aliases)(*mats, xw, xw, *pa)

    y = seq_dft(l_tok, 0, tabs['lat'], "dft_seq_latent")
    if ctx_out:
        y = seq_dft(m_tok, l_tok // m_tok, tabs['ctx'], "dft_seq_ctx", prev=y)
    return y


def _rw_prep_body(zk_ref, zv_ref, zr_ref, zl_ref, zg_ref, halo_ref, mu_ref, kk_ref, ka_ref, w0_ref, a0_ref,
                  w2_ref, a2_ref, trf_ref, trb_ref, e_ref,
                  v_ref, r_ref, g_ref, cl_ref, kd_ref, b_ref, kx_ref, *, tr):
    row = lax.broadcasted_iota(jnp.int32, (tr, 1), 0)
    halo = halo_ref[0, 0]
    mu = mu_ref[...]

    def mix(x, lo, hi):
        xp = jnp.where(row == 0, halo[0:1, lo:hi], pltpu.roll(x, 1, 0))
        xn = jnp.where(row == tr - 1, halo[1:2, lo:hi], pltpu.roll(x, tr - 1, 0))
        return x + (0.5 * (xp + xn) - x) * mu[:, lo:hi]

    k = mix(zk_ref[0], 0, 512)
    v = mix(zv_ref[0], 512, 1024)
    r = mix(zr_ref[0], 1024, 1536)
    l4 = mix(zl_ref[0], 1536, 1664)
    g_ref[0] = mix(zg_ref[0], 1664, 1792)

    wcat = w0_ref[...] + _dot3(jnp.tanh(l4), w2_ref[...])
    acat = a0_ref[...] + _dot3(l4, a2_ref[...])
    kk = k * kk_ref[...]
    kkn = kk / jnp.maximum(jnp.sqrt(_head_sumsq(kk, e_ref[...])), 1e-12)
    for d, tri_ref in enumerate((trf_ref, trb_ref)):
        w = wcat[:, d * 512:(d + 1) * 512]
        lw = -math.exp(-0.5) * jax.nn.sigmoid(w)
        a = jax.nn.sigmoid(acat[:, d * 512:(d + 1) * 512])
        kd = k * (1.0 + (a - 1.0) * ka_ref[...])
        cl = _dot_sel(tri_ref[...], lw)
        bb = kkn * a
        kx = kkn * jnp.exp(-lw)
        for h in range(RW_HEADS):
            lo = h * RW_DH
            cl_ref[0, d, h] = cl[:, lo:lo + RW_DH]
            kd_ref[0, d, h] = kd[:, lo:lo + RW_DH]
            b_ref[0, d, h] = bb[:, lo:lo + RW_DH]
            kx_ref[0, d, h] = kx[:, lo:lo + RW_DH]
    for h in range(RW_HEADS):
        lo = h * RW_DH
        v_ref[0, h] = v[:, lo:lo + RW_DH]
        r_ref[0, h] = r[:, lo:lo + RW_DH]


def _rw_prep(z3, halo, p, tr):
    b, t, _ = z3.shape
    c = RW_CHUNK
    ti = jnp.arange(tr)
    same = (ti[:, None] // c) == (ti[None, :] // c)
    trf = (same & (ti[None, :] <= ti[:, None])).astype(BF16)
    trb = (same & (ti[None, :] >= ti[:, None])).astype(BF16)

    def zcol(off, w):
        return pl.BlockSpec((1, tr, w), lambda b, i: (b, i, off // w))

    hm = pl.BlockSpec((1, RW_HEADS, tr, RW_DH), lambda b, i: (b, 0, i, 0))
    hmd = pl.BlockSpec((1, 2, RW_HEADS, tr, RW_DH), lambda b, i: (b, 0, 0, i, 0))
    s1 = _sds((b, RW_HEADS, t, RW_DH), F32)
    s2 = _sds((b, 2, RW_HEADS, t, RW_DH), F32)
    consts = [p['rw_mu'], p['rw_k_k'], p['rw_k_a'], p['rw_w0'], p['rw_a0'], p['rw_w2'], p['rw_a2'], trf, trb,
              _head_ones(RW_W, RW_DH)]
    return _call(
        functools.partial(_rw_prep_body, tr=tr), (b, t // tr),
        [zcol(ZC_RWK, 512), zcol(ZC_RWK + 512, 512), zcol(ZC_RWK + 1024, 512), zcol(ZC_L4, 128), zcol(ZC_G, 128),
         pl.BlockSpec((1, 1, 2, 1792), lambda b, i: (b, i, 0, 0))] + [_full(a) for a in consts],
        [hm, hm, pl.BlockSpec((1, tr, 128), lambda b, i: (b, i, 0)), hmd, hmd, hmd, hmd],
        [s1, s1, _sds((b, t, 128), F32), s2, s2, s2, s2], name="rw_prep")(z3, z3, z3, z3, z3, halo, *consts)


RW_P_HI = 1
RW_P_LO = 1


RW_INV_BASE = 16


def _unit_tri_inverse(x, upper):
    g, c, _ = x.shape
    nb = c // RW_INV_BASE
    d = jnp.concatenate([x[:, i * RW_INV_BASE:(i + 1) * RW_INV_BASE, i * RW_INV_BASE:(i + 1) * RW_INV_BASE]
                         for i in range(nb)], axis=0)
    ii = lax.broadcasted_iota(jnp.int32, (RW_INV_BASE, RW_INV_BASE), 0)
    jj = lax.broadcasted_iota(jnp.int32, (RW_INV_BASE, RW_INV_BASE), 1)
    t = (ii == jj).astype(F32) + d
    xp = d
    for _ in range(int(math.log2(RW_INV_BASE)) - 1):
        xp = _bmm('gts,gsu->gtu', xp, xp, RW_P_HI)
        t = t + _bmm('gts,gsu->gtu', xp, t, RW_P_HI)
    size = RW_INV_BASE
    while size < c:
        n = c // size
        ta = jnp.concatenate([t[(2 * p) * g:(2 * p + 1) * g] for p in range(n // 2)], axis=0)
        tb = jnp.concatenate([t[(2 * p + 1) * g:(2 * p + 2) * g] for p in range(n // 2)], axis=0)
        if upper:
            off = jnp.concatenate([x[:, 2 * p * size:(2 * p + 1) * size, (2 * p + 1) * size:(2 * p + 2) * size]
                                   for p in range(n // 2)], axis=0)
            off = _bmm('gts,gsu->gtu', ta, _bmm('gts,gsu->gtu', off, tb, RW_P_HI), RW_P_HI)
            t = jnp.concatenate([jnp.concatenate([ta, off], axis=2),
                                 jnp.concatenate([jnp.zeros_like(off), tb], axis=2)], axis=1)
        else:
            off = jnp.concatenate([x[:, (2 * p + 1) * size:(2 * p + 2) * size, 2 * p * size:(2 * p + 1) * size]
                                   for p in range(n // 2)], axis=0)
            off = _bmm('gts,gsu->gtu', tb, _bmm('gts,gsu->gtu', off, ta, RW_P_HI), RW_P_HI)
            t = jnp.concatenate([jnp.concatenate([ta, jnp.zeros_like(off)], axis=2),
                                 jnp.concatenate([off, tb], axis=2)], axis=1)
        size *= 2
    return t


def _rw_scan_tile(r_ref, v_ref, cl_ref, kd_ref, b_ref, kx_ref, y_ref, s_ref, *, tr, rev):
    c = RW_CHUNK
    nc = tr // c
    g = RW_HEADS * nc

    def ld(x):
        return x.reshape(g, c, RW_DH)

    r, v = ld(r_ref[0]), ld(v_ref[0])
    cl, kd, bb, kx = ld(cl_ref[0, 0]), ld(kd_ref[0, 0]), ld(b_ref[0, 0]), ld(kx_ref[0, 0])
    last = 0 if rev else c - 1
    ctot = cl[:, last:last + 1, :]
    e = jnp.exp(cl)
    ei = jnp.exp(-cl)
    ec = jnp.exp(ctot - cl)
    kkt, rt = kx * e, r * e
    kw, bw = kd * ei, bb * ei
    kc, bc = kd * ec, bb * ec
    a_cat = jnp.concatenate([kkt, rt], axis=1)
    pp = _bmm('gtd,gsd->gts', a_cat, jnp.concatenate([kw, bw], axis=1), RW_P_HI)
    ti = lax.broadcasted_iota(jnp.int32, (c, c), 0)
    si = lax.broadcasted_iota(jnp.int32, (c, c), 1)
    strict = (si > ti) if rev else (si < ti)
    incl = (si >= ti) if rev else (si <= ti)
    nmat = jnp.where(strict, pp[:, :c, :c], 0.0)
    x = -jnp.where(strict, pp[:, :c, c:], 0.0)
    ark = jnp.where(incl, pp[:, c:, :c], 0.0)
    arb = jnp.where(incl, pp[:, c:, c:], 0.0)
    tm = _unit_tri_inverse(x, rev)
    nav = _bmm('gts,gsd->gtd', jnp.concatenate([nmat, ark], axis=1), v, RW_P_HI)
    nv, arkv = nav[:, :c], nav[:, c:]
    ta = _bmm('gts,gsd->gtd', tm, jnp.concatenate([kkt, nv], axis=2), RW_P_HI)
    ata = _bmm('gts,gsd->gtd', arb, ta, RW_P_HI)
    a2 = rt - ata[:, :, :RW_DH]
    y0 = arkv - ata[:, :, RW_DH:]
    tb = _bmm('gtk,gtd->gkd', bc, ta, RW_P_HI)
    di = lax.broadcasted_iota(jnp.int32, (RW_DH, RW_DH), 0)
    dj = lax.broadcasted_iota(jnp.int32, (RW_DH, RW_DH), 1)
    gmt = jnp.where(di == dj, jnp.exp(ctot), 0.0) - tb[:, :, :RW_DH]
    hmt = _bmm('gtk,gtv->gkv', kc, v, RW_P_HI) - tb[:, :, RW_DH:]

    def per_chunk(x):
        return x.reshape(RW_HEADS, nc, x.shape[1], x.shape[2])

    a2, y0, gmt, hmt = per_chunk(a2), per_chunk(y0), per_chunk(gmt), per_chunk(hmt)
    st = s_ref[...]
    ys = [None] * nc
    for ci in (range(nc - 1, -1, -1) if rev else range(nc)):
        ys[ci] = _bmm('htk,hkv->htv', a2[:, ci], st, RW_P_HI) + y0[:, ci]
        st = _bmm('hke,hev->hkv', gmt[:, ci], st, RW_P_HI) + hmt[:, ci]
    s_ref[...] = st
    y_ref[0] = jnp.concatenate(ys, axis=1)


def _rw_scan_body(*refs, tr):
    fwd, bwd, (yf_ref, yb_ref, sf_ref, sb_ref) = refs[0:6], refs[6:12], refs[12:16]

    @pl.when(pl.program_id(1) == 0)
    def _():
        sf_ref[...] = jnp.zeros_like(sf_ref)
        sb_ref[...] = jnp.zeros_like(sb_ref)

    _rw_scan_tile(*fwd, yf_ref, sf_ref, tr=tr, rev=False)
    _rw_scan_tile(*bwd, yb_ref, sb_ref, tr=tr, rev=True)


def _rw_scan(r, v, cl, kd, bb, kx, tr, nl, nm):
    b, _, t, _ = r.shape
    nt = nl + nm
    tile_f = lambda j: jnp.where(j < nm, nl + j, j - nm)
    tile_b = lambda j: nt - 1 - j

    def specs(tile, d):
        hm = pl.BlockSpec((1, RW_HEADS, tr, RW_DH), lambda b, j: (b, 0, tile(j), 0))
        hmd = pl.BlockSpec((1, 1, RW_HEADS, tr, RW_DH), lambda b, j: (b, d, 0, tile(j), 0))
        return hm, hmd

    hf, hfd = specs(tile_f, 0)
    hb, hbd = specs(tile_b, 1)
    shp = _sds((b, RW_HEADS, t, RW_DH), F32)
    state = pltpu.VMEM((RW_HEADS, RW_DH, RW_DH), F32)
    return _call(
        functools.partial(_rw_scan_body, tr=tr), (b, nt),
        [hf, hf, hfd, hfd, hfd, hfd, hb, hb, hbd, hbd, hbd, hbd], [hf, hb], [shp, shp],
        scratch=[state, state], sem=("parallel", "arbitrary"),
        name="rw_scan")(r, v, cl, kd, bb, kx, r, v, cl, kd, bb, kx)


def _rw_out_body(yf_ref, yb_ref, r_ref, v_ref, kd_ref, g_ref, lnw_ref, lnb_ref, rk_ref, g2_ref, o_ref):
    y = yf_ref[0] + yb_ref[0]
    mu = jnp.mean(y, axis=-1, keepdims=True)
    var = jnp.mean(jnp.square(y - mu), axis=-1, keepdims=True)
    yn = (y - mu) * lax.rsqrt(var + RW_GN_EPS) * lnw_ref[...] + lnb_ref[...]
    ksum = kd_ref[0, 0] + kd_ref[0, 1]
    bonus = jnp.sum(r_ref[0] * ksum * rk_ref[...], axis=-1, keepdims=True) * v_ref[0]
    o = yn + bonus
    o = jnp.concatenate([o[h] for h in range(RW_HEADS)], axis=-1)
    gate = _dot3(jax.nn.sigmoid(g_ref[0]), g2_ref[...])
    o_ref[0] = (o * gate).astype(BF16)


def _rw_out(yf, yb, r, v, kd, g, p, tr, nrow):
    b = r.shape[0]
    hm = pl.BlockSpec((1, RW_HEADS, tr, RW_DH), lambda b, i: (b, 0, i, 0))
    consts = [p['rw_ln_w'], p['rw_ln_b'], p['rw_r_k'], p['rw_g2']]
    return _call(
        _rw_out_body, (b, nrow),
        [hm, hm, hm, hm, pl.BlockSpec((1, 2, RW_HEADS, tr, RW_DH), lambda b, i: (b, 0, 0, i, 0)),
         pl.BlockSpec((1, tr, 128), lambda b, i: (b, i, 0))] + [_full(a) for a in consts],
        pl.BlockSpec((1, tr, RW_W), lambda b, i: (b, i, 0)),
        _sds((b, nrow * tr, RW_W), BF16), name="rw_out")(yf, yb, r, v, kd, g, *consts)


GATE_COL0 = KEY_COLS + 2144


def _merge_body(h_ref, y0_ref, y1_ref, y2_ref, y3_ref, g0_ref, g1_ref, g2_ref, g3_ref, wb_ref, o_ref):
    h = h_ref[0]
    acc = None
    for i, (y_ref, wg_ref) in enumerate(zip((y0_ref, y1_ref, y2_ref, y3_ref), (g0_ref, g1_ref, g2_ref, g3_ref))):
        t = jax.nn.sigmoid(_dot(h, wg_ref[...])) * _dot(y_ref[0], wb_ref[0, i])
        acc = t if acc is None else acc + t
    o_ref[0] = acc.astype(BF16)


def _merge(h, ys, wg, wbr, layer, rows):
    b, _, d = h.shape
    tn = 512
    nn = d // tn
    tm = _pick_tile(rows, (1024, 768, 512, 256, 128, 64))
    row = lambda w: pl.BlockSpec((1, tm, w), lambda n, b, i: (b, i, 0))
    gspecs = [pl.BlockSpec((d, tn), functools.partial(lambda n, b, i, k: (0, k * nn + n), k=k))
              for k in range(N_BRANCH)]
    return _call(
        _merge_body, (nn, b, rows // tm),
        [row(d)] + [row(BRANCH_W)] * 4 + gspecs
        + [pl.BlockSpec((1, N_BRANCH, BRANCH_W, tn), lambda n, b, i: (layer, 0, 0, n))],
        pl.BlockSpec((1, tm, tn), lambda n, b, i: (b, i, n)),
        _sds((b, rows, d), BF16), name="merge")(h, *ys, wg, wg, wg, wg, wbr)


def _outproj_body(a_ref, w_ref, x_ref, md_ref, o_ref, *, tm, l_tok):
    row = pl.program_id(1) * tm + lax.broadcasted_iota(jnp.int32, (tm, 1), 0)
    gate = jnp.where(row < l_tok, md_ref[0, 0, 2:3, :], md_ref[0, 1, 2:3, :])
    o_ref[0] = x_ref[0] + gate * _dot(a_ref[0], w_ref[0])


def _outproj(acc, w_out, layer, xs, md, l_tok):
    b, rows, d = acc.shape
    tm = _pick_tile(rows, (768, 512, 256, 128, 64))
    row = pl.BlockSpec((1, tm, d), lambda b, i: (b, i, 0))
    return _call(
        functools.partial(_outproj_body, tm=tm, l_tok=l_tok), (b, rows // tm),
        [row, pl.BlockSpec((1, d, d), lambda b, i: (layer, 0, 0)), row,
         pl.BlockSpec((1, 2, 6, d), lambda b, i: (b, 0, 0, 0))], row,
        _sds((b, rows, d), F32), name="out_proj")(acc, w_out, xs, md)


def _select_body(lg_ref, slot_ref, aff_ref, slotc_ref, *, n, cap):
    lg = lg_ref[0]
    lane = lax.broadcasted_iota(jnp.int32, lg.shape, 1)
    lg = jnp.where(lane < N_EXPERTS, lg, NEG_INF)
    ex = jnp.exp(lg - lg.max(axis=-1, keepdims=True))
    aff = ex / ex.sum(axis=-1, keepdims=True)
    aff_t = aff.T[:N_EXPERTS]

    def count_ge(t):
        return jnp.sum((aff_t >= t).astype(F32), axis=1, keepdims=True)

    tiny = 2.0 ** -126
    lo = jnp.full((N_EXPERTS, 1), tiny, F32)
    for sh in (64, 32, 16, 8, 4, 2, 1):
        cand = lo * (2.0 ** sh)
        lo = jnp.where(count_ge(cand) >= cap, cand, lo)
    hi = lo * 2.0
    below = count_ge(tiny) < cap
    lo = jnp.where(below, 0.0, lo)
    hi = jnp.where(below, tiny, hi)
    for _ in range(40):
        mid = lo + (hi - lo) * 0.5
        ok = count_ge(mid) >= cap
        lo = jnp.where(ok, mid, lo)
        hi = jnp.where(ok, hi, mid)
    gt = aff_t >= hi
    eq = (aff_t >= lo) & (aff_t < hi)
    need = cap - jnp.sum(gt.astype(F32), axis=1, keepdims=True)
    both = jnp.concatenate([gt.astype(BF16), eq.astype(BF16)], axis=0)
    cw = min(n, 512)
    pre = []
    for cb in range(n // cw):
        tp = lax.broadcasted_iota(jnp.int32, (n, cw), 0)
        tt = lax.broadcasted_iota(jnp.int32, (n, cw), 1) + cb * cw
        pre.append(_dot(both, (tp < tt).astype(BF16)))
    pre = jnp.concatenate(pre, axis=1) if len(pre) > 1 else pre[0]
    pre_gt, pre_eq = pre[:N_EXPERTS], pre[N_EXPERTS:]
    sel = gt | (eq & (pre_eq < need))
    slot = jnp.where(sel, pre_gt + jnp.minimum(pre_eq, need), -1.0)
    slot_ref[0] = slot.astype(jnp.int32)
    aff_ref[0] = aff_t
    pad = jnp.full((128 - N_EXPERTS, n), -1.0, F32)
    slotc_ref[0] = jnp.concatenate([slot, pad], axis=0).T.astype(jnp.int32)


def _select(logits, n, blk, cap, name):
    b = logits.shape[0]
    er = pl.BlockSpec((1, N_EXPERTS, n), lambda b: (b, 0, 0))
    return _call(
        functools.partial(_select_body, n=n, cap=cap), (b,),
        [pl.BlockSpec((1, n, 128), lambda b: (b, blk, 0))],
        [er, er, pl.BlockSpec((1, n, 128), lambda b: (b, 0, 0))],
        [_sds((b, N_EXPERTS, n), jnp.int32), _sds((b, N_EXPERTS, n), F32), _sds((b, n, 128), jnp.int32)],
        name=name)(logits)


def _ffn_body(h_ref, slot_ref, aff_ref, w1_ref, w3_ref, w2_ref, o_ref, *, grp, n, cap):
    xs, gates = [], []
    jrow = lax.broadcasted_iota(jnp.int32, (cap, n), 0)
    for gi in range(grp):
        pick = jrow == slot_ref[gi, 0]
        xs.append(_dot(pick.astype(BF16), h_ref[gi]).astype(BF16))
        gates.append(jnp.sum(jnp.where(pick, aff_ref[gi, 0], 0.0), axis=1, keepdims=True))
    xe = jnp.concatenate(xs, axis=0) if grp > 1 else xs[0]
    gate = jnp.concatenate(gates, axis=0) if grp > 1 else gates[0]
    a = _dot(xe, w1_ref[0, 0])
    u = _dot(xe, w3_ref[0, 0])
    hm = (a * jax.nn.sigmoid(a) * u).astype(BF16)
    ye = (_dot(hm, w2_ref[0, 0]) * gate).astype(BF16)
    for gi in range(grp):
        o_ref[0, gi] = ye[gi * cap:(gi + 1) * cap]


def _moe_ffn(h2, slot, aff, w1, w3, w2, layer, n, blk, cap, grp, name):
    b, _, d = h2.shape
    ff = w1.shape[-1]
    sr = pl.BlockSpec((grp, 1, 1, n), lambda e, g: (g, e, 0, 0))
    return _call(
        functools.partial(_ffn_body, grp=grp, n=n, cap=cap), (N_EXPERTS, b // grp),
        [pl.BlockSpec((grp, n, d), lambda e, g: (g, blk, 0)), sr, sr,
         pl.BlockSpec((1, 1, d, ff), lambda e, g: (layer, e, 0, 0)),
         pl.BlockSpec((1, 1, d, ff), lambda e, g: (layer, e, 0, 0)),
         pl.BlockSpec((1, 1, ff, d), lambda e, g: (layer, e, 0, 0))],
        pl.BlockSpec((1, grp, cap, d), lambda e, g: (e, g, 0, 0)),
        _sds((N_EXPERTS, b, cap, d), BF16), name=name)(
            h2, slot.reshape(b, N_EXPERTS, 1, n), aff.reshape(b, N_EXPERTS, 1, n), w1, w3, w2)


def _combine_body(sc_ref, ye_ref, x_ref, md_ref, *rest, cap):
    o_ref = rest[-1]
    sc = sc_ref[0]
    jj = lax.broadcasted_iota(jnp.int32, (sc.shape[0], cap), 1)
    acc = None
    for e in range(N_EXPERTS):
        put = (sc[:, e:e + 1] == jj).astype(BF16)
        t = _dot(put, ye_ref[e, 0])
        acc = t if acc is None else acc + t
    o_ref[0] = x_ref[0] + md_ref[0, 0, 5:6, :] * acc


def _moe_combine(slotc, ye, x1, md, n, blk_rows, is_ctx, cap, name, out_rows, prev=None):
    b, _, d = x1.shape
    rest = 0 if prev is not None else out_rows - blk_rows - n
    tm = _pick_tile(math.gcd(n, rest), (512, 256, 128, 64, 32, 16, 8))
    nq, nfill = n // tm, rest // tm
    off = blk_rows // tm
    extra_specs, extra_args, aliases = _into(prev, 4)
    body = functools.partial(_combine_body, cap=cap)
    clamp = lambda i: jnp.minimum(i, nq - 1)
    return _call(
        _zero_tail(body, nq) if nfill else body, (b, nq + nfill),
        [pl.BlockSpec((1, tm, 128), lambda b, i: (b, clamp(i), 0)),
         pl.BlockSpec((N_EXPERTS, 1, cap, d), lambda b, i: (0, b, 0, 0)),
         pl.BlockSpec((1, tm, d), lambda b, i: (b, off + clamp(i), 0)),
         pl.BlockSpec((1, 1, 6, d), lambda b, i: (b, 1 if is_ctx else 0, 0, 0))] + extra_specs,
        pl.BlockSpec((1, tm, d), lambda b, i: (b, off + i, 0)),
        _sds((b, out_rows, d), F32), name=name, aliases=aliases)(slotc, ye, x1, md, *extra_args)


def _moe(x1, h2, logits, md, w1, w3, w2, layer, l_tok, m_tok, ctx_out):
    b = x1.shape[0]
    cap_l = CAPACITY_FACTOR * l_tok // N_EXPERTS
    slot, aff, slotc = _select(logits, l_tok, 0, cap_l, "moe_select_latent")
    ye = _moe_ffn(h2, slot, aff, w1, w3, w2, layer, l_tok, 0, cap_l, 1, "moe_ffn_latent")
    out_rows = l_tok + m_tok if ctx_out else l_tok
    x2 = _moe_combine(slotc, ye, x1, md, l_tok, 0, False, cap_l, "moe_combine_latent", out_rows)
    if not ctx_out:
        return x2
    cap_c = CAPACITY_FACTOR * m_tok // N_EXPERTS
    blk = l_tok // m_tok
    slot, aff, slotc = _select(logits, m_tok, blk, cap_c, "moe_select_ctx")
    ye = _moe_ffn(h2, slot, aff, w1, w3, w2, layer, m_tok, blk, cap_c, b, "moe_ffn_ctx")
    return _moe_combine(slotc, ye, x1, md, m_tok, l_tok, True, cap_c, "moe_combine_ctx", out_rows, prev=x2)


def _layer_params(l, w_in, na_q_norm, na_k_norm, na_rpb, mla_cq_norm, mla_ckv_norm, mla_w_uq, mla_w_ukv,
                  mla_q_norm, mla_k_norm, rw_mu_ks, rw_mu_qs, rw_w0, rw_w2, rw_a0, rw_a2, rw_g2, rw_k_k, rw_k_a,
                  rw_r_k, rw_ln_w, rw_ln_b, w_br, w_out, rows):
    w = w_in[l]
    d = w.shape[0]
    kq = KEY_COLS
    kpe = w[:, 1280:1344]
    zpad = lambda n: jnp.zeros((d, n), w.dtype)
    wz = jnp.concatenate([
        w[:, kq:kq + 512], w[:, 0:512], w[:, 512:1024],
        w[:, kq + 512:kq + 1024],
        w[:, 1344:1856], w[:, 1856:2368], w[:, kq + 1024:kq + 1536],
        w[:, kq + 1632:kq + 2144],
        w[:, 1024:1280],
        kpe, _swap_halves(kpe),
        w[:, 2368:2496],
        w[:, kq + 1536:kq + 1632], zpad(32)], axis=1).astype(BF16)
    assert wz.shape[1] == NZ
    wg = w[:, GATE_COL0:].astype(BF16)

    def head_gain(g):
        return jnp.concatenate([g[:128], g[128:], _swap_halves(g[128:])])[None, :]

    wq = mla_w_uq[l].reshape(MLA_QLORA, MLA_HEADS, MLA_NOPE + MLA_ROPE)
    wq = jnp.concatenate([wq, _swap_halves(wq[:, :, MLA_NOPE:])], axis=-1).reshape(MLA_QLORA, MLA_HEADS * 256)
    w2cat = jnp.zeros((128, 1024), F32).at[0:32, 0:512].set(rw_w2[l, 0]).at[32:64, 512:].set(rw_w2[l, 1])
    a2cat = jnp.zeros((128, 1024), F32).at[64:96, 0:512].set(rw_a2[l, 0]).at[96:128, 512:].set(rw_a2[l, 1])
    mu = jnp.concatenate([rw_mu_ks[l][:1024], rw_mu_qs[l][:512], rw_mu_ks[l][1024:], rw_mu_qs[l][512:],
                          jnp.zeros((32,), F32)])[None, :]
    hd = lambda a: a.reshape(RW_HEADS, 1, RW_DH)
    return {
        'wz': wz, 'wg': wg,
        'na_gq': na_q_norm[l][None, :], 'na_gk': na_k_norm[l][None, :],
        'na_bias': _na_bias_table(na_rpb[l], rows),
        'g_cq': mla_cq_norm[l][None, :], 'g_ckv': mla_ckv_norm[l][None, :],
        'w_uq': wq.astype(BF16), 'w_ukv': mla_w_ukv[l].astype(BF16),
        'g_q': head_gain(mla_q_norm[l]), 'g_k': head_gain(mla_k_norm[l]),
        'rw_mu': mu, 'rw_k_k': rw_k_k[l][None, :], 'rw_k_a': rw_k_a[l][None, :],
        'rw_w0': rw_w0[l].reshape(1, 1024), 'rw_a0': rw_a0[l].reshape(1, 1024), 'rw_w2': w2cat, 'rw_a2': a2cat,
        'rw_ln_w': hd(rw_ln_w[l]), 'rw_ln_b': hd(rw_ln_b[l]), 'rw_r_k': hd(rw_r_k[l]),
        'rw_g2': jnp.concatenate([rw_g2[l], jnp.zeros((32, RW_W), F32)], axis=0),
    }


def _rw_halo(z3, tr, nl):
    b, t, _ = z3.shape
    nt = t // tr
    z4 = z3.reshape(b, nt, tr, z3.shape[-1])
    pick = lambda a: jnp.concatenate([a[..., ZC_RWK:ZC_RWK + 1536], a[..., ZC_L4:ZC_L4 + 256]], axis=-1)
    first, last = pick(z4[:, :, 0, :]), pick(z4[:, :, tr - 1, :])
    zero = jnp.zeros_like(first[:, :1])
    prev = jnp.concatenate([zero, last[:, :-1]], axis=1)
    nxt = jnp.concatenate([first[:, 1:], zero], axis=1)
    tile = jnp.arange(nt)[None, :, None]
    prev = jnp.where(tile == nl, 0.0, prev)
    nxt = jnp.where(tile == nl - 1, 0.0, nxt)
    return jnp.stack([prev, nxt], axis=2)


def _layer(xs, md, p, g1n, g2n, wr, deep, layer, tabs, l_tok, m_tok, tr, ctx_out):
    b, t, d = xs.shape
    nl, nm = l_tok // tr, m_tok // tr
    nrow = (nl + nm) if ctx_out else nl

    h = _norm1(xs, g1n, md, tr, nl)
    z3 = _matmul(h.reshape(b * t, d), p['wz'], F32).reshape(b, t, NZ)

    nq, nk, nv, mq, mk, mv, xw = _mixer_prep(z3, p, tabs, tr)
    rows = nrow * tr
    y_na = _na_attention(nq, nk, nv, p['na_bias'], l_tok, m_tok, rows)
    y_mla = _mla_attention(mq, mk, mv, tr, l_tok, rows)
    rv, rr, rg, cl, kd, bb, kx = _rw_prep(z3, _rw_halo(z3, tr, nl), p, tr)
    yf, yb = _rw_scan(rr, rv, cl, kd, bb, kx, tr, nl, nm)
    y_rw = _rw_out(yf, yb, rr, rv, kd, rg, p, tr, nrow)
    y_fn = _fourier(xw, tabs, tr, l_tok, m_tok, ctx_out)

    acc = _merge(h, (y_fn, y_na, y_mla, y_rw), p['wg'], deep['w_br'], layer, rows)
    x1 = _outproj(acc, deep['w_out'], layer, xs, md, l_tok)
    h2, logits = _norm2_router(x1, g2n, md, wr, tr, nl)
    return _moe(x1, h2, logits, md, deep['w1'], deep['w3'], deep['w2'], layer, l_tok, m_tok, ctx_out)


def kernel(x, c, ctx, c_ctx, ada_w, ada_b, norm1_g, norm2_g, w_in, na_q_norm, na_k_norm, na_rpb, mla_cq_norm, mla_ckv_norm, mla_w_uq, mla_w_ukv, mla_q_norm, mla_k_norm, rw_mu_ks, rw_mu_qs, rw_w0, rw_w2, rw_a0, rw_a2, rw_g2, rw_k_k, rw_k_a, rw_r_k, rw_ln_w, rw_ln_b, w_br, w_out, moe_router, moe_w1, moe_w3, moe_w2):
    b, l_tok, d = x.shape
    m_tok = ctx.shape[1]
    depth = ada_w.shape[0]
    tr = min(m_tok, 256)
    assert l_tok % tr == 0 and m_tok % tr == 0 and l_tok % m_tok == 0 and tr % RW_CHUNK == 0
    assert l_tok % GRID_W == 0 and m_tok % 128 == 0

    nr = -(-(b + 1) // 8) * 8
    cc = jnp.concatenate([c, c_ctx[None, :], jnp.zeros((nr - b - 1, d), F32)], axis=0)
    mod = _modulation(cc, ada_w, ada_b)
    tabs = {}
    tabs['rope_c'], tabs['rope_s'] = _rope_tables(l_tok, m_tok)

    deep = {'w1': moe_w1.astype(BF16), 'w3': moe_w3.astype(BF16), 'w2': moe_w2.astype(BF16),
            'w_br': w_br.astype(BF16), 'w_out': w_out.astype(BF16)}
    tabs.update(_fourier_tables(l_tok, m_tok, depth > 1))
    xs = jnp.concatenate([x, ctx], axis=1)
    for l in range(depth):
        ctx_out = l < depth - 1
        p = _layer_params(l, w_in, na_q_norm, na_k_norm, na_rpb, mla_cq_norm, mla_ckv_norm, mla_w_uq, mla_w_ukv,
                          mla_q_norm, mla_k_norm, rw_mu_ks, rw_mu_qs, rw_w0, rw_w2, rw_a0, rw_a2, rw_g2, rw_k_k,
                          rw_k_a, rw_r_k, rw_ln_w, rw_ln_b, w_br, w_out, l_tok // GRID_W)
        md = jnp.stack([mod[l, :b], jnp.broadcast_to(mod[l, b], (b, 6 * d))], axis=1).reshape(b, 2, 6, d)
        wr = jnp.concatenate([moe_router[l], jnp.zeros((d, 128 - N_EXPERTS), F32)], axis=1)
        xs = _layer(xs, md, p, norm1_g[l][None, :], norm2_g[l][None, :], wr, deep, l, tabs, l_tok, m_tok, tr, ctx_out)
    return xs
```

```python
import functools
import math

import jax
import jax.numpy as jnp
from jax import lax
from jax.experimental import pallas as pl
from jax.experimental.pallas import tpu as pltpu

F32 = jnp.float32
BF16 = jnp.bfloat16

GRID_W = 64
N_BRANCH = 4
BRANCH_W = 512
NA_HEADS, NA_DH, NA_KH, NA_KW = 8, 64, 8, 16
NA_SCALE = NA_DH ** -0.5
MLA_HEADS, MLA_NOPE, MLA_ROPE = 4, 128, 64
MLA_QLORA = 512
MLA_SCALE = (MLA_NOPE + MLA_ROPE) ** -0.5
RW_HEADS, RW_DH, RW_W = 8, 64, 512
RW_GN_EPS = 64e-5
N_EXPERTS = 16
CAPACITY_FACTOR = 2
ROPE_THETA = 10000.0
NORM_EPS = 1e-6
NEG_INF = -1e30
LOG2E = math.log2(math.e)
RW_CHUNK = 64
KEY_COLS = 2496
NZ = 4736
ZC_NA = 0
ZC_CQ = 1536
ZC_RWK = 2048
ZC_FN = 3584
ZC_CKV = 4096
ZC_KPE = 4352
ZC_L4 = 4480
ZC_G = 4608

VMEM_LIMIT = 56 * 2 ** 20


def _call(body, grid, in_specs, out_specs, out_shape, scratch=(), sem=None, name=None, aliases=None):
    return pl.pallas_call(
        body, grid=grid, in_specs=in_specs, out_specs=out_specs, out_shape=out_shape,
        scratch_shapes=list(scratch), name=name, input_output_aliases=aliases or {},
        compiler_params=pltpu.CompilerParams(
            dimension_semantics=sem or ("parallel",) * len(grid), vmem_limit_bytes=VMEM_LIMIT))


def _full(a):
    nd = a.ndim
    return pl.BlockSpec(a.shape, lambda *_: (0,) * nd)


def _sds(shape, dtype):
    return jax.ShapeDtypeStruct(shape, dtype)


def _dot(a, b):
    return jnp.dot(a, b, preferred_element_type=F32)


def _split2(x):
    hi = x.astype(BF16)
    return hi, (x - hi.astype(F32)).astype(BF16)


def _split3(x):
    hi = x.astype(BF16)
    r = x - hi.astype(F32)
    mid = r.astype(BF16)
    return hi, mid, (r - mid.astype(F32)).astype(BF16)


def _dot3(a, b):
    ah, al = _split2(a)
    bh, bl = _split2(b)
    return _dot(ah, bh) + (_dot(ah, bl) + _dot(al, bh))


def _dot_sel(sel_bf16, x):
    h, m, l = _split3(x)
    return _dot(sel_bf16, h) + (_dot(sel_bf16, m) + _dot(sel_bf16, l))


def _bmm(spec, a, b, passes):
    e = functools.partial(jnp.einsum, spec, preferred_element_type=F32)
    if passes == 1:
        return e(a.astype(BF16), b.astype(BF16))
    ah, al = _split2(a)
    bh, bl = _split2(b)
    return e(ah, bh) + (e(ah, bl) + e(al, bh))


def _pick_tile(n, cands):
    for c in cands:
        if n % c == 0:
            return c
    raise ValueError(f"no tile for {n}")


def _mod_body(c_ref, w_ref, b_ref, o_ref):
    cc = c_ref[...]
    s = cc * jax.nn.sigmoid(cc)
    tn = o_ref.shape[-1]
    for j in range(0, tn, 1024):
        o_ref[0, :, j:j + 1024] = _dot3(s, w_ref[0, :, j:j + 1024]) + b_ref[0, :, j:j + 1024]


def _modulation(cc, ada_w, ada_b):
    depth, d, n6 = ada_w.shape
    r = cc.shape[0]
    tn = 2048
    return _call(
        _mod_body, (depth, n6 // tn),
        [pl.BlockSpec((r, d), lambda l, n: (0, 0)),
         pl.BlockSpec((1, d, tn), lambda l, n: (l, 0, n)),
         pl.BlockSpec((1, 1, tn), lambda l, n: (l, 0, n))],
        pl.BlockSpec((1, r, tn), lambda l, n: (l, 0, n)),
        _sds((depth, r, n6), F32), name="adaln_mod")(cc, ada_w, ada_b.reshape(depth, 1, n6))


def _modnorm(x, g, shift, scale):
    y = x * lax.rsqrt(jnp.mean(x * x, axis=-1, keepdims=True) + NORM_EPS)
    return (y * g) * (1.0 + scale) + shift


def _norm1_body(x_ref, g_ref, md_ref, h_ref):
    h = _modnorm(x_ref[0], g_ref[...], md_ref[0, 0, 0:1, :], md_ref[0, 0, 1:2, :])
    h_ref[0] = h.astype(BF16)


def _norm2_body(x_ref, g_ref, md_ref, wr_ref, h_ref, lg_ref):
    h = _modnorm(x_ref[0], g_ref[...], md_ref[0, 0, 3:4, :], md_ref[0, 0, 4:5, :])
    h_ref[0] = h.astype(BF16)
    lg_ref[0] = _dot3(h, wr_ref[...])


def _md_spec(d, nl):
    return pl.BlockSpec((1, 1, 6, d), lambda b, i: (b, i // nl, 0, 0))


def _norm1(xs, g, md, tr, nl):
    b, t, d = xs.shape
    return _call(
        _norm1_body, (b, t // tr),
        [pl.BlockSpec((1, tr, d), lambda b, i: (b, i, 0)), _full(g), _md_spec(d, nl)],
        pl.BlockSpec((1, tr, d), lambda b, i: (b, i, 0)),
        _sds((b, t, d), BF16), name="norm1")(xs, g, md)


def _norm2_router(x1, g, md, wr, tr, nl):
    b, t, d = x1.shape
    return _call(
        _norm2_body, (b, t // tr),
        [pl.BlockSpec((1, tr, d), lambda b, i: (b, i, 0)), _full(g), _md_spec(d, nl), _full(wr)],
        [pl.BlockSpec((1, tr, d), lambda b, i: (b, i, 0)), pl.BlockSpec((1, tr, 128), lambda b, i: (b, i, 0))],
        [_sds((b, t, d), BF16), _sds((b, t, 128), F32)], name="norm2_router")(x1, g, md, wr)


def _mm_body(a_ref, w_ref, o_ref):
    o_ref[...] = _dot(a_ref[...], w_ref[...]).astype(o_ref.dtype)


def _matmul(a, w, out_dtype):
    r, k = a.shape
    n = w.shape[1]
    tm = _pick_tile(r, (512, 384, 256, 128, 64, 32, 16, 8))
    return _call(
        _mm_body, (r // tm,),
        [pl.BlockSpec((tm, k), lambda i: (i, 0)),
         pl.BlockSpec((k, n), lambda i: (0, 0), pipeline_mode=pl.Buffered(1))],
        pl.BlockSpec((tm, n), lambda i: (i, 0)),
        _sds((r, n), out_dtype), name="in_proj")(a, w)


def _head_sum(x, ones_bd):
    hi, lo = _split2(x)
    return _dot(hi, ones_bd) + _dot(lo, ones_bd)


def _head_sumsq(x, ones_bd):
    return _head_sum(x * x, ones_bd)


def _head_ones(width, dh):
    i = jnp.arange(width) // dh
    return (i[:, None] == i[None, :]).astype(BF16)


def _na_prep_body(z_ref, gq_ref, gk_ref, e_ref, q_ref, k_ref, v_ref):
    z = z_ref[0]
    q, k, v = z[:, :512], z[:, 512:1024], z[:, 1024:]
    e = e_ref[...]
    qn = (q * lax.rsqrt(_head_sumsq(q, e) / NA_DH + NORM_EPS) * (gq_ref[...] * (NA_SCALE * LOG2E))).astype(BF16)
    kn = (k * lax.rsqrt(_head_sumsq(k, e) / NA_DH + NORM_EPS) * gk_ref[...]).astype(BF16)
    vb = v.astype(BF16)
    for h in range(NA_HEADS):
        lo = h * NA_DH
        q_ref[0, h] = qn[:, lo:lo + NA_DH]
        k_ref[0, h] = kn[:, lo:lo + NA_DH]
        v_ref[0, h] = vb[:, lo:lo + NA_DH]


def _softmax_pv(s, vs):
    m = s[0].max(axis=-1, keepdims=True)
    for t in s[1:]:
        m = jnp.maximum(m, t.max(axis=-1, keepdims=True))
    den = None
    acc = None
    for t, v in zip(s, vs):
        p = jnp.exp2(t - m)
        ps = p.sum(axis=-1, keepdims=True)
        o = jnp.einsum('hqk,hkd->hqd', p.astype(BF16), v, preferred_element_type=F32)
        den = ps if den is None else den + ps
        acc = o if acc is None else acc + o
    return acc / den


def _qk(q, k):
    return jnp.einsum('hqd,hkd->hqk', q, k, preferred_element_type=F32)


def _heads_to_lanes(o):
    return jnp.concatenate([o[h] for h in range(o.shape[0])], axis=-1)


NA_ROWS_PER_STEP = 4


def _na_body(q_ref, k_ref, v_ref, bias_ref, o_ref, *, rows, kh, l_tok, m_tok, rps):
    kc = k_ref[0, :, l_tok:l_tok + m_tok, :]
    vc = v_ref[0, :, l_tok:l_tok + m_tok, :]

    @pl.when(pl.program_id(1) < rows // rps)
    def _():
        for rr in range(rps):
            r = pl.program_id(1) * rps + rr
            rs = jnp.clip(r - kh // 2, 0, rows - kh)
            delta = r - rs
            start = pl.multiple_of(rs * GRID_W, GRID_W)
            q = q_ref[0, :, rr * GRID_W:(rr + 1) * GRID_W, :]
            kb = k_ref[0, :, pl.ds(start, kh * GRID_W), :]
            vb = v_ref[0, :, pl.ds(start, kh * GRID_W), :]
            o = _softmax_pv([_qk(q, kb) + bias_ref[delta], _qk(q, kc)], [vb, vc])
            o_ref[0, rr * GRID_W:(rr + 1) * GRID_W, :] = _heads_to_lanes(o).astype(BF16)

    @pl.when(pl.program_id(1) >= rows // rps)
    def _():
        o_ref[0] = _heads_to_lanes(_softmax_pv([_qk(q_ref[0], kc)], [vc])).astype(BF16)


def _na_attention(q, k, v, bias, l_tok, m_tok, out_rows):
    b, _, t, _ = q.shape
    rows = l_tok // GRID_W
    kh = min(NA_KH, rows)
    rps = math.gcd(math.gcd(rows, NA_ROWS_PER_STEP), (out_rows - l_tok) // GRID_W)
    kv = pl.BlockSpec((1, NA_HEADS, t, NA_DH), lambda b, r: (b, 0, 0, 0))
    return _call(
        functools.partial(_na_body, rows=rows, kh=kh, l_tok=l_tok, m_tok=m_tok, rps=rps),
        (b, out_rows // (rps * GRID_W)),
        [pl.BlockSpec((1, NA_HEADS, rps * GRID_W, NA_DH), lambda b, r: (b, 0, r, 0)), kv, kv, _full(bias)],
        pl.BlockSpec((1, rps * GRID_W, BRANCH_W), lambda b, r: (b, r, 0)),
        _sds((b, out_rows, BRANCH_W), BF16), name="na_attention")(q, k, v, bias)


def _toeplitz_body(r_ref, oh_ref, valid_ref, o_ref):
    h, m, l = _split3(r_ref[...])
    oh = oh_ref[...]
    t = _dot(h, oh) + (_dot(m, oh) + _dot(l, oh))
    o_ref[...] = jnp.where(valid_ref[...] > 0.0, t * LOG2E, NEG_INF)


def _na_bias_table(rpb, rows):
    kh = min(NA_KH, rows)
    nh, ndr, ndc = rpb.shape
    col = jnp.arange(GRID_W)
    cs = jnp.clip(col - NA_KW // 2, 0, GRID_W - NA_KW)
    valid = (col[None, :] >= cs[:, None]) & (col[None, :] < cs[:, None] + NA_KW)
    dc = jnp.clip(col[None, :] - col[:, None] + NA_KW - 1, 0, 2 * NA_KW - 2)
    onehot = (jnp.arange(128)[:, None] == dc.reshape(1, -1)).astype(BF16)
    rp = jnp.pad(rpb.astype(F32).reshape(nh * ndr, ndc), ((0, 0), (0, 128 - ndc)))
    vmask = valid.reshape(1, -1).astype(F32)
    toep = pl.pallas_call(_toeplitz_body, out_shape=_sds((nh * ndr, GRID_W * GRID_W), F32),
                          name="na_bias")(rp, onehot, vmask)
    toep = toep.reshape(nh, ndr, GRID_W, GRID_W)
    per_delta = []
    for delta in range(kh):
        lo = NA_KH - 1 - delta
        t = toep[:, lo:lo + kh].transpose(0, 2, 1, 3)
        per_delta.append(t.reshape(nh, GRID_W, kh * GRID_W))
    return jnp.stack(per_delta, axis=0)


def _mla_body(q_ref, k_ref, v_ref, o_ref, *, nl, l_tok):
    @pl.when(pl.program_id(1) < nl)
    def _():
        o_ref[0] = _heads_to_lanes(_softmax_pv([_qk(q_ref[0], k_ref[0])], [v_ref[0]])).astype(BF16)

    @pl.when(pl.program_id(1) >= nl)
    def _():
        kc, vc = k_ref[0, :, l_tok:, :], v_ref[0, :, l_tok:, :]
        o_ref[0] = _heads_to_lanes(_softmax_pv([_qk(q_ref[0], kc)], [vc])).astype(BF16)


def _zero_tail(body, nvalid):
    def wrapped(*refs):
        @pl.when(pl.program_id(1) < nvalid)
        def _():
            body(*refs)

        @pl.when(pl.program_id(1) >= nvalid)
        def _():
            refs[-1][...] = jnp.zeros_like(refs[-1])

    return wrapped


def _into(prev, n_in):
    if prev is None:
        return [], [], None
    return [pl.BlockSpec(memory_space=pl.ANY)], [prev], {n_in: 0}


def _mla_attention(q, k, v, tq, l_tok, out_rows):
    b, heads, t, dv = v.shape
    kv = lambda a: pl.BlockSpec((1, heads, t, a.shape[-1]), lambda b, i: (b, 0, 0, 0))
    return _call(
        functools.partial(_mla_body, nl=l_tok // tq, l_tok=l_tok), (b, out_rows // tq),
        [pl.BlockSpec((1, heads, tq, q.shape[-1]), lambda b, i: (b, 0, i, 0)), kv(k), kv(v)],
        pl.BlockSpec((1, tq, heads * dv), lambda b, i: (b, i, 0)),
        _sds((b, out_rows, heads * dv), BF16), name="mla_attention")(q, k, v)


def _mla_q_body(z_ref, gc_ref, w_ref, gh_ref, ct_ref, st_ref, q_ref):
    cq = z_ref[0]
    cqn = (cq * lax.rsqrt(jnp.mean(cq * cq, axis=-1, keepdims=True) + NORM_EPS) * gc_ref[...]).astype(BF16)
    q = _dot(cqn, w_ref[...])
    gh = gh_ref[...]
    ct, st = ct_ref[...], st_ref[...]
    for h in range(MLA_HEADS):
        qh = q[:, h * 256:(h + 1) * 256]
        nope, pe, sw = qh[:, :128], qh[:, 128:192], qh[:, 192:256]
        ms = (jnp.sum(nope * nope, axis=-1, keepdims=True) + jnp.sum(pe * pe, axis=-1, keepdims=True)) \
            / (MLA_NOPE + MLA_ROPE)
        rinv = lax.rsqrt(ms + NORM_EPS)
        rot = (pe * rinv * gh[:, 128:192]) * ct + (sw * rinv * gh[:, 192:256]) * st
        qh = jnp.concatenate([nope * rinv * gh[:, :128], rot, jnp.zeros_like(rot)], axis=-1)
        q_ref[0, h] = (qh * (MLA_SCALE * LOG2E)).astype(BF16)


def _mla_kv_body(zc_ref, zp_ref, gc_ref, w_ref, gh_ref, ct_ref, st_ref, k_ref, v_ref):
    ckv = zc_ref[0]
    cn = (ckv * lax.rsqrt(jnp.mean(ckv * ckv, axis=-1, keepdims=True) + NORM_EPS) * gc_ref[...]).astype(BF16)
    kv = _dot(cn, w_ref[...])
    zp = zp_ref[0]
    pe, sw = zp[:, :64], zp[:, 64:128]
    pe2 = jnp.sum(pe * pe, axis=-1, keepdims=True)
    gh = gh_ref[...]
    ct, st = ct_ref[...], st_ref[...]
    for h in range(MLA_HEADS):
        nope = kv[:, h * 256:h * 256 + 128]
        ms = (jnp.sum(nope * nope, axis=-1, keepdims=True) + pe2) / (MLA_NOPE + MLA_ROPE)
        rinv = lax.rsqrt(ms + NORM_EPS)
        rot = (pe * rinv * gh[:, 128:192]) * ct + (sw * rinv * gh[:, 192:256]) * st
        k_ref[0, h] = jnp.concatenate([nope * rinv * gh[:, :128], rot, jnp.zeros_like(rot)], axis=-1).astype(BF16)
        v_ref[0, h] = kv[:, h * 256 + 128:(h + 1) * 256].astype(BF16)


def _rope_tables(l_tok, m_tok):
    half = MLA_ROPE // 4
    freqs = ROPE_THETA ** (-jnp.arange(half, dtype=F32) / half)
    pos = jnp.arange(l_tok)
    ar = (pos // GRID_W).astype(F32)[:, None] * freqs[None, :]
    ac = (pos % GRID_W).astype(F32)[:, None] * freqs[None, :]
    ct = jnp.concatenate([jnp.cos(ar), jnp.cos(ar), jnp.cos(ac), jnp.cos(ac)], axis=-1)
    st = jnp.concatenate([-jnp.sin(ar), jnp.sin(ar), -jnp.sin(ac), jnp.sin(ac)], axis=-1)
    ct = jnp.concatenate([ct, jnp.ones((m_tok, MLA_ROPE), F32)], axis=0)
    st = jnp.concatenate([st, jnp.zeros((m_tok, MLA_ROPE), F32)], axis=0)
    return ct, st


def _swap_halves(a):
    return jnp.concatenate([a[..., 16:32], a[..., 0:16], a[..., 48:64], a[..., 32:48]], axis=-1)


def _dft1_body(x_ref, bh_ref, bl_ref, o_ref):
    xh, xl = _split2(x_ref[0])
    r = _dot(xh, bh_ref[...]) + (_dot(xh, bl_ref[...]) + _dot(xl, bh_ref[...]))
    o_ref[0] = r.astype(BF16)


def _dft2_body(c_ref, s_ref, xc_ref, xs_ref, *rest):
    rest[-1][0] = (_dot(c_ref[...], xc_ref[0]) - _dot(s_ref[...], xs_ref[0])).astype(BF16)


def _dft_mats(n):
    k = jnp.arange(n, dtype=jnp.int32)
    ang = ((k[:, None] * k[None, :]) % n).astype(F32) * (2.0 * math.pi / n)
    s = 1.0 / math.sqrt(n)
    return jnp.cos(ang) * s, jnp.sin(ang) * s


def _fourier_tables(l_tok, m_tok, ctx_out):
    cw, sw = _dft_mats(BRANCH_W // 4)
    eye = jnp.eye(4, dtype=F32)
    bd = jnp.concatenate([jnp.kron(eye, cw), jnp.kron(eye, sw)], axis=1)
    tabs = {'bd': _split2(bd), 'lat': tuple(m.astype(BF16) for m in _dft_mats(l_tok))}
    if ctx_out:
        tabs['ctx'] = tuple(m.astype(BF16) for m in _dft_mats(m_tok))
    return tabs


def _mixer_prep_body(zna, zcq, zckv, zkpe, zfn, gq, gk, e, gcq, wuq, ghq, ct, st, gckv, wukv, ghk, bh, bl,
                     nq, nk, nv, mq, mk, mv, xw):
    _na_prep_body(zna, gq, gk, e, nq, nk, nv)
    _mla_q_body(zcq, gcq, wuq, ghq, ct, st, mq)
    _mla_kv_body(zckv, zkpe, gckv, wukv, ghk, ct, st, mk, mv)
    _dft1_body(zfn, bh, bl, xw)


def _mixer_prep(z3, p, tabs, tr):
    b, t, _ = z3.shape
    bh, bl = tabs['bd']
    e = _head_ones(NA_HEADS * NA_DH, NA_DH)
    gq, gk = jnp.tile(p['na_gq'], (1, NA_HEADS)), jnp.tile(p['na_gk'], (1, NA_HEADS))
    consts = [gq, gk, e, p['g_cq'], p['w_uq'], p['g_q'], None, None, p['g_ckv'], p['w_ukv'], p['g_k'], bh, bl]
    rope = pl.BlockSpec((tr, MLA_ROPE), lambda b, i: (i, 0))

    def zcol(off, w):
        return pl.BlockSpec((1, tr, w), lambda b, i: (b, i, off // w))

    def hm(heads, d):
        return pl.BlockSpec((1, heads, tr, d), lambda b, i: (b, 0, i, 0)), _sds((b, heads, t, d), BF16)

    outs = [hm(NA_HEADS, NA_DH)] * 3 + [hm(MLA_HEADS, 256), hm(MLA_HEADS, 256), hm(MLA_HEADS, 128),
                                        (pl.BlockSpec((1, tr, 1024), lambda b, i: (b, i, 0)), _sds((b, t, 1024), BF16))]
    args = [tabs['rope_c'] if c is None else c for c in consts]
    args[7] = tabs['rope_s']
    return _call(
        _mixer_prep_body, (b, t // tr),
        [zcol(ZC_NA, 1536), zcol(ZC_CQ, 512), zcol(ZC_CKV, 256), zcol(ZC_KPE, 128), zcol(ZC_FN, 512)]
        + [rope if c is None else _full(c) for c in consts],
        [o[0] for o in outs], [o[1] for o in outs], name="mixer_prep")(z3, z3, z3, z3, z3, *args)


def _fourier(xw, tabs, tr, l_tok, m_tok, ctx_out):
    b = xw.shape[0]
    nrow = (l_tok + m_tok if ctx_out else l_tok) // tr

    def seq_dft(n, blk, mats, name, prev=None):
        rest = 0 if prev is not None else nrow * tr - n
        tm = _pick_tile(math.gcd(n, rest), (512, 256, 128, 64, 32, 16, 8))
        nq, nfill = n // tm, rest // tm
        mspec = pl.BlockSpec((tm, n), lambda b, i: (jnp.minimum(i, nq - 1), 0))
        xc = pl.BlockSpec((1, n, 512), lambda b, i: (b, blk, 0))
        xs = pl.BlockSpec((1, n, 512), lambda b, i: (b, blk, 1))
        off = blk * n // tm
        ps, pa, aliases = _into(prev, 4)
        return _call(
            _zero_tail(_dft2_body, nq) if nfill else _dft2_body, (b, nq + nfill), [mspec, mspec, xc, xs] + ps,
            pl.BlockSpec((1, tm, 512), lambda b, i: (b, off + i, 0)),
            _sds((b, nrow * tr, 512), BF16), name=name, aliases=aliases)(*mats, xw, xw, *pa)

    y = seq_dft(l_tok, 0, tabs['lat'], "dft_seq_latent")
    if ctx_out:
        y = seq_dft(m_tok, l_tok // m_tok, tabs['ctx'], "dft_seq_ctx", prev=y)
    return y


def _rw_prep_body(zk_ref, zv_ref, zr_ref, zl_ref, zg_ref, halo_ref, mu_ref, kk_ref, ka_ref, w0_ref, a0_ref,
                  w2_ref, a2_ref, trf_ref, trb_ref, e_ref,
                  v_ref, r_ref, g_ref, cl_ref, kd_ref, b_ref, kx_ref, *, tr):
    row = lax.broadcasted_iota(jnp.int32, (tr, 1), 0)
    halo = halo_ref[0, 0]
    mu = mu_ref[...]

    def mix(x, lo, hi):
        xp = jnp.where(row == 0, halo[0:1, lo:hi], pltpu.roll(x, 1, 0))
        xn = jnp.where(row == tr - 1, halo[1:2, lo:hi], pltpu.roll(x, tr - 1, 0))
        return x + (0.5 * (xp + xn) - x) * mu[:, lo:hi]

    k = mix(zk_ref[0], 0, 512)
    v = mix(zv_ref[0], 512, 1024)
    r = mix(zr_ref[0], 1024, 1536)
    l4 = mix(zl_ref[0], 1536, 1664)
    g_ref[0] = mix(zg_ref[0], 1664, 1792)

    wcat = w0_ref[...] + _dot3(jnp.tanh(l4), w2_ref[...])
    acat = a0_ref[...] + _dot3(l4, a2_ref[...])
    kk = k * kk_ref[...]
    kkn = kk / jnp.maximum(jnp.sqrt(_head_sumsq(kk, e_ref[...])), 1e-12)
    for d, tri_ref in enumerate((trf_ref, trb_ref)):
        w = wcat[:, d * 512:(d + 1) * 512]
        lw = -math.exp(-0.5) * jax.nn.sigmoid(w)
        a = jax.nn.sigmoid(acat[:, d * 512:(d + 1) * 512])
        kd = k * (1.0 + (a - 1.0) * ka_ref[...])
        cl = _dot_sel(tri_ref[...], lw)
        bb = kkn * a
        kx = kkn * jnp.exp(-lw)
        for h in range(RW_HEADS):
            lo = h * RW_DH
            cl_ref[0, d, h] = cl[:, lo:lo + RW_DH]
            kd_ref[0, d, h] = kd[:, lo:lo + RW_DH]
            b_ref[0, d, h] = bb[:, lo:lo + RW_DH]
            kx_ref[0, d, h] = kx[:, lo:lo + RW_DH]
    for h in range(RW_HEADS):
        lo = h * RW_DH
        v_ref[0, h] = v[:, lo:lo + RW_DH]
        r_ref[0, h] = r[:, lo:lo + RW_DH]


def _rw_prep(z3, halo, p, tr):
    b, t, _ = z3.shape
    c = RW_CHUNK
    ti = jnp.arange(tr)
    same = (ti[:, None] // c) == (ti[None, :] // c)
    trf = (same & (ti[None, :] <= ti[:, None])).astype(BF16)
    trb = (same & (ti[None, :] >= ti[:, None])).astype(BF16)

    def zcol(off, w):
        return pl.BlockSpec((1, tr, w), lambda b, i: (b, i, off // w))

    hm = pl.BlockSpec((1, RW_HEADS, tr, RW_DH), lambda b, i: (b, 0, i, 0))
    hmd = pl.BlockSpec((1, 2, RW_HEADS, tr, RW_DH), lambda b, i: (b, 0, 0, i, 0))
    s1 = _sds((b, RW_HEADS, t, RW_DH), F32)
    s2 = _sds((b, 2, RW_HEADS, t, RW_DH), F32)
    consts = [p['rw_mu'], p['rw_k_k'], p['rw_k_a'], p['rw_w0'], p['rw_a0'], p['rw_w2'], p['rw_a2'], trf, trb,
              _head_ones(RW_W, RW_DH)]
    return _call(
        functools.partial(_rw_prep_body, tr=tr), (b, t // tr),
        [zcol(ZC_RWK, 512), zcol(ZC_RWK + 512, 512), zcol(ZC_RWK + 1024, 512), zcol(ZC_L4, 128), zcol(ZC_G, 128),
         pl.BlockSpec((1, 1, 2, 1792), lambda b, i: (b, i, 0, 0))] + [_full(a) for a in consts],
        [hm, hm, pl.BlockSpec((1, tr, 128), lambda b, i: (b, i, 0)), hmd, hmd, hmd, hmd],
        [s1, s1, _sds((b, t, 128), F32), s2, s2, s2, s2], name="rw_prep")(z3, z3, z3, z3, z3, halo, *consts)


RW_P_HI = 1
RW_P_LO = 1


RW_INV_BASE = 16


def _unit_tri_inverse(x, upper):
    g, c, _ = x.shape
    nb = c // RW_INV_BASE
    d = jnp.concatenate([x[:, i * RW_INV_BASE:(i + 1) * RW_INV_BASE, i * RW_INV_BASE:(i + 1) * RW_INV_BASE]
                         for i in range(nb)], axis=0)
    ii = lax.broadcasted_iota(jnp.int32, (RW_INV_BASE, RW_INV_BASE), 0)
    jj = lax.broadcasted_iota(jnp.int32, (RW_INV_BASE, RW_INV_BASE), 1)
    t = (ii == jj).astype(F32) + d
    xp = _bmm('gts,gsu->gtu', d, d, RW_P_HI)
    nsq = int(math.log2(RW_INV_BASE)) - 1
    for i in range(nsq):
        if i < nsq - 1:
            both = _bmm('gts,gsu->gtu', xp, jnp.concatenate([t, xp], axis=2), RW_P_HI)
            t, xp = t + both[:, :, :RW_INV_BASE], both[:, :, RW_INV_BASE:]
        else:
            t = t + _bmm('gts,gsu->gtu', xp, t, RW_P_HI)
    size = RW_INV_BASE
    while size < c:
        n = c // size
        ta = jnp.concatenate([t[(2 * p) * g:(2 * p + 1) * g] for p in range(n // 2)], axis=0)
        tb = jnp.concatenate([t[(2 * p + 1) * g:(2 * p + 2) * g] for p in range(n // 2)], axis=0)
        if upper:
            off = jnp.concatenate([x[:, 2 * p * size:(2 * p + 1) * size, (2 * p + 1) * size:(2 * p + 2) * size]
                                   for p in range(n // 2)], axis=0)
            off = _bmm('gts,gsu->gtu', ta, _bmm('gts,gsu->gtu', off, tb, RW_P_HI), RW_P_HI)
            t = jnp.concatenate([jnp.concatenate([ta, off], axis=2),
                                 jnp.concatenate([jnp.zeros_like(off), tb], axis=2)], axis=1)
        else:
            off = jnp.concatenate([x[:, (2 * p + 1) * size:(2 * p + 2) * size, 2 * p * size:(2 * p + 1) * size]
                                   for p in range(n // 2)], axis=0)
            off = _bmm('gts,gsu->gtu', tb, _bmm('gts,gsu->gtu', off, ta, RW_P_HI), RW_P_HI)
            t = jnp.concatenate([jnp.concatenate([ta, jnp.zeros_like(off)], axis=2),
                                 jnp.concatenate([off, tb], axis=2)], axis=1)
        size *= 2
    return t


def _rw_scan_tile(r_ref, v_ref, cl_ref, kd_ref, b_ref, kx_ref, y_ref, s_ref, *, tr, rev):
    c = RW_CHUNK
    nc = tr // c
    g = RW_HEADS * nc

    def ld(x):
        return x.reshape(g, c, RW_DH)

    r, v = ld(r_ref[0]), ld(v_ref[0])
    cl, kd, bb, kx = ld(cl_ref[0, 0]), ld(kd_ref[0, 0]), ld(b_ref[0, 0]), ld(kx_ref[0, 0])
    last = 0 if rev else c - 1
    ctot = cl[:, last:last + 1, :]
    e = jnp.exp(cl)
    ei = jnp.exp(-cl)
    ec = jnp.exp(ctot - cl)
    kkt, rt = kx * e, r * e
    kw, bw = kd * ei, bb * ei
    kc, bc = kd * ec, bb * ec
    a_cat = jnp.concatenate([kkt, rt], axis=1)
    pp = _bmm('gtd,gsd->gts', a_cat, jnp.concatenate([kw, bw], axis=1), RW_P_HI)
    ti = lax.broadcasted_iota(jnp.int32, (c, c), 0)
    si = lax.broadcasted_iota(jnp.int32, (c, c), 1)
    strict = (si > ti) if rev else (si < ti)
    incl = (si >= ti) if rev else (si <= ti)
    nmat = jnp.where(strict, pp[:, :c, :c], 0.0)
    x = -jnp.where(strict, pp[:, :c, c:], 0.0)
    ark = jnp.where(incl, pp[:, c:, :c], 0.0)
    arb = jnp.where(incl, pp[:, c:, c:], 0.0)
    tm = _unit_tri_inverse(x, rev)
    nav = _bmm('gts,gsd->gtd', jnp.concatenate([nmat, ark], axis=1), v, RW_P_HI)
    nv, arkv = nav[:, :c], nav[:, c:]
    ta = _bmm('gts,gsd->gtd', tm, jnp.concatenate([kkt, nv], axis=2), RW_P_HI)
    ata = _bmm('gts,gsd->gtd', arb, ta, RW_P_HI)
    a2 = rt - ata[:, :, :RW_DH]
    y0 = arkv - ata[:, :, RW_DH:]
    tb = _bmm('gtk,gtd->gkd', bc, ta, RW_P_HI)
    di = lax.broadcasted_iota(jnp.int32, (RW_DH, RW_DH), 0)
    dj = lax.broadcasted_iota(jnp.int32, (RW_DH, RW_DH), 1)
    gmt = jnp.where(di == dj, jnp.exp(ctot), 0.0) - tb[:, :, :RW_DH]
    hmt = _bmm('gtk,gtv->gkv', kc, v, RW_P_HI) - tb[:, :, RW_DH:]

    def per_chunk(x):
        return x.reshape(RW_HEADS, nc, x.shape[1], x.shape[2])

    a2, y0, gmt, hmt = per_chunk(a2), per_chunk(y0), per_chunk(gmt), per_chunk(hmt)
    st = s_ref[...]
    ys = [None] * nc
    for ci in (range(nc - 1, -1, -1) if rev else range(nc)):
        ys[ci] = _bmm('htk,hkv->htv', a2[:, ci], st, RW_P_HI) + y0[:, ci]
        st = _bmm('hke,hev->hkv', gmt[:, ci], st, RW_P_HI) + hmt[:, ci]
    s_ref[...] = st
    y_ref[0] = jnp.concatenate(ys, axis=1)


def _rw_scan_body(*refs, tr):
    fwd, bwd, (yf_ref, yb_ref, sf_ref, sb_ref) = refs[0:6], refs[6:12], refs[12:16]

    @pl.when(pl.program_id(1) == 0)
    def _():
        sf_ref[...] = jnp.zeros_like(sf_ref)
        sb_ref[...] = jnp.zeros_like(sb_ref)

    _rw_scan_tile(*fwd, yf_ref, sf_ref, tr=tr, rev=False)
    _rw_scan_tile(*bwd, yb_ref, sb_ref, tr=tr, rev=True)


def _rw_scan(r, v, cl, kd, bb, kx, tr, nl, nm):
    b, _, t, _ = r.shape
    nt = nl + nm
    tile_f = lambda j: jnp.where(j < nm, nl + j, j - nm)
    tile_b = lambda j: nt - 1 - j

    def specs(tile, d):
        hm = pl.BlockSpec((1, RW_HEADS, tr, RW_DH), lambda b, j: (b, 0, tile(j), 0))
        hmd = pl.BlockSpec((1, 1, RW_HEADS, tr, RW_DH), lambda b, j: (b, d, 0, tile(j), 0))
        return hm, hmd

    hf, hfd = specs(tile_f, 0)
    hb, hbd = specs(tile_b, 1)
    shp = _sds((b, RW_HEADS, t, RW_DH), F32)
    state = pltpu.VMEM((RW_HEADS, RW_DH, RW_DH), F32)
    return _call(
        functools.partial(_rw_scan_body, tr=tr), (b, nt),
        [hf, hf, hfd, hfd, hfd, hfd, hb, hb, hbd, hbd, hbd, hbd], [hf, hb], [shp, shp],
        scratch=[state, state], sem=("parallel", "arbitrary"),
        name="rw_scan")(r, v, cl, kd, bb, kx, r, v, cl, kd, bb, kx)


def _rw_out_body(yf_ref, yb_ref, r_ref, v_ref, kd_ref, g_ref, lnw_ref, lnb_ref, rk_ref, g2_ref, o_ref):
    y = yf_ref[0] + yb_ref[0]
    mu = jnp.mean(y, axis=-1, keepdims=True)
    var = jnp.mean(jnp.square(y - mu), axis=-1, keepdims=True)
    yn = (y - mu) * lax.rsqrt(var + RW_GN_EPS) * lnw_ref[...] + lnb_ref[...]
    ksum = kd_ref[0, 0] + kd_ref[0, 1]
    bonus = jnp.sum(r_ref[0] * ksum * rk_ref[...], axis=-1, keepdims=True) * v_ref[0]
    o = yn + bonus
    o = jnp.concatenate([o[h] for h in range(RW_HEADS)], axis=-1)
    gate = _dot3(jax.nn.sigmoid(g_ref[0]), g2_ref[...])
    o_ref[0] = (o * gate).astype(BF16)


def _rw_out(yf, yb, r, v, kd, g, p, tr, nrow):
    b = r.shape[0]
    hm = pl.BlockSpec((1, RW_HEADS, tr, RW_DH), lambda b, i: (b, 0, i, 0))
    consts = [p['rw_ln_w'], p['rw_ln_b'], p['rw_r_k'], p['rw_g2']]
    return _call(
        _rw_out_body, (b, nrow),
        [hm, hm, hm, hm, pl.BlockSpec((1, 2, RW_HEADS, tr, RW_DH), lambda b, i: (b, 0, 0, i, 0)),
         pl.BlockSpec((1, tr, 128), lambda b, i: (b, i, 0))] + [_full(a) for a in consts],
        pl.BlockSpec((1, tr, RW_W), lambda b, i: (b, i, 0)),
        _sds((b, nrow * tr, RW_W), BF16), name="rw_out")(yf, yb, r, v, kd, g, *consts)


GATE_COL0 = KEY_COLS + 2144


def _merge_body(h_ref, y0_ref, y1_ref, y2_ref, y3_ref, g0_ref, g1_ref, g2_ref, g3_ref, wb_ref, o_ref):
    h = h_ref[0]
    acc = None
    for i, (y_ref, wg_ref) in enumerate(zip((y0_ref, y1_ref, y2_ref, y3_ref), (g0_ref, g1_ref, g2_ref, g3_ref))):
        t = jax.nn.sigmoid(_dot(h, wg_ref[...])) * _dot(y_ref[0], wb_ref[0, i])
        acc = t if acc is None else acc + t
    o_ref[0] = acc.astype(BF16)


def _merge(h, ys, wg, wbr, layer, rows):
    b, _, d = h.shape
    tn = 512
    nn = d // tn
    tm = _pick_tile(rows, (1024, 768, 512, 256, 128, 64))
    row = lambda w: pl.BlockSpec((1, tm, w), lambda n, b, i: (b, i, 0))
    gspecs = [pl.BlockSpec((d, tn), functools.partial(lambda n, b, i, k: (0, k * nn + n), k=k))
              for k in range(N_BRANCH)]
    return _call(
        _merge_body, (nn, b, rows // tm),
        [row(d)] + [row(BRANCH_W)] * 4 + gspecs
        + [pl.BlockSpec((1, N_BRANCH, BRANCH_W, tn), lambda n, b, i: (layer, 0, 0, n))],
        pl.BlockSpec((1, tm, tn), lambda n, b, i: (b, i, n)),
        _sds((b, rows, d), BF16), name="merge")(h, *ys, wg, wg, wg, wg, wbr)


def _outproj_body(a_ref, w_ref, x_ref, md_ref, o_ref, *, tm, l_tok):
    row = pl.program_id(1) * tm + lax.broadcasted_iota(jnp.int32, (tm, 1), 0)
    gate = jnp.where(row < l_tok, md_ref[0, 0, 2:3, :], md_ref[0, 1, 2:3, :])
    o_ref[0] = x_ref[0] + gate * _dot(a_ref[0], w_ref[0])


def _outproj(acc, w_out, layer, xs, md, l_tok):
    b, rows, d = acc.shape
    tm = _pick_tile(rows, (768, 512, 256, 128, 64))
    row = pl.BlockSpec((1, tm, d), lambda b, i: (b, i, 0))
    return _call(
        functools.partial(_outproj_body, tm=tm, l_tok=l_tok), (b, rows // tm),
        [row, pl.BlockSpec((1, d, d), lambda b, i: (layer, 0, 0)), row,
         pl.BlockSpec((1, 2, 6, d), lambda b, i: (b, 0, 0, 0))], row,
        _sds((b, rows, d), F32), name="out_proj")(acc, w_out, xs, md)


def _select_body(lg_ref, slot_ref, aff_ref, slotc_ref, *, n, cap):
    lg = lg_ref[0]
    lane = lax.broadcasted_iota(jnp.int32, lg.shape, 1)
    lg = jnp.where(lane < N_EXPERTS, lg, NEG_INF)
    ex = jnp.exp(lg - lg.max(axis=-1, keepdims=True))
    aff = ex / ex.sum(axis=-1, keepdims=True)
    aff_t = aff.T[:N_EXPERTS]

    def count_ge(t):
        return jnp.sum((aff_t >= t).astype(F32), axis=1, keepdims=True)

    tiny = 2.0 ** -126
    lo = jnp.full((N_EXPERTS, 1), tiny, F32)
    for sh in (64, 32, 16, 8, 4, 2, 1):
        cand = lo * (2.0 ** sh)
        lo = jnp.where(count_ge(cand) >= cap, cand, lo)
    hi = lo * 2.0
    below = count_ge(tiny) < cap
    lo = jnp.where(below, 0.0, lo)
    hi = jnp.where(below, tiny, hi)
    for _ in range(40):
        mid = lo + (hi - lo) * 0.5
        ok = count_ge(mid) >= cap
        lo = jnp.where(ok, mid, lo)
        hi = jnp.where(ok, hi, mid)
    gt = aff_t >= hi
    eq = (aff_t >= lo) & (aff_t < hi)
    need = cap - jnp.sum(gt.astype(F32), axis=1, keepdims=True)
    both = jnp.concatenate([gt.astype(BF16), eq.astype(BF16)], axis=0)
    cw = min(n, 512)
    pre = []
    for cb in range(n // cw):
        tp = lax.broadcasted_iota(jnp.int32, (n, cw), 0)
        tt = lax.broadcasted_iota(jnp.int32, (n, cw), 1) + cb * cw
        pre.append(_dot(both, (tp < tt).astype(BF16)))
    pre = jnp.concatenate(pre, axis=1) if len(pre) > 1 else pre[0]
    pre_gt, pre_eq = pre[:N_EXPERTS], pre[N_EXPERTS:]
    sel = gt | (eq & (pre_eq < need))
    slot = jnp.where(sel, pre_gt + jnp.minimum(pre_eq, need), -1.0)
    slot_ref[0] = slot.astype(jnp.int32)
    aff_ref[0] = aff_t
    pad = jnp.full((128 - N_EXPERTS, n), -1.0, F32)
    slotc_ref[0] = jnp.concatenate([slot, pad], axis=0).T.astype(jnp.int32)


def _select(logits, n, blk, cap, name):
    b = logits.shape[0]
    er = pl.BlockSpec((1, N_EXPERTS, n), lambda b: (b, 0, 0))
    return _call(
        functools.partial(_select_body, n=n, cap=cap), (b,),
        [pl.BlockSpec((1, n, 128), lambda b: (b, blk, 0))],
        [er, er, pl.BlockSpec((1, n, 128), lambda b: (b, 0, 0))],
        [_sds((b, N_EXPERTS, n), jnp.int32), _sds((b, N_EXPERTS, n), F32), _sds((b, n, 128), jnp.int32)],
        name=name)(logits)


def _ffn_body(h_ref, slot_ref, aff_ref, w1_ref, w3_ref, w2_ref, o_ref, *, grp, n, cap):
    xs, gates = [], []
    jrow = lax.broadcasted_iota(jnp.int32, (cap, n), 0)
    for gi in range(grp):
        pick = jrow == slot_ref[gi, 0]
        xs.append(_dot(pick.astype(BF16), h_ref[gi]).astype(BF16))
        gates.append(jnp.sum(jnp.where(pick, aff_ref[gi, 0], 0.0), axis=1, keepdims=True))
    xe = jnp.concatenate(xs, axis=0) if grp > 1 else xs[0]
    gate = jnp.concatenate(gates, axis=0) if grp > 1 else gates[0]
    a = _dot(xe, w1_ref[0, 0])
    u = _dot(xe, w3_ref[0, 0])
    hm = (a * jax.nn.sigmoid(a) * u).astype(BF16)
    ye = (_dot(hm, w2_ref[0, 0]) * gate).astype(BF16)
    for gi in range(grp):
        o_ref[0, gi] = ye[gi * cap:(gi + 1) * cap]


def _moe_ffn(h2, slot, aff, w1, w3, w2, layer, n, blk, cap, grp, name):
    b, _, d = h2.shape
    ff = w1.shape[-1]
    sr = pl.BlockSpec((grp, 1, 1, n), lambda e, g: (g, e, 0, 0))
    return _call(
        functools.partial(_ffn_body, grp=grp, n=n, cap=cap), (N_EXPERTS, b // grp),
        [pl.BlockSpec((grp, n, d), lambda e, g: (g, blk, 0)), sr, sr,
         pl.BlockSpec((1, 1, d, ff), lambda e, g: (layer, e, 0, 0)),
         pl.BlockSpec((1, 1, d, ff), lambda e, g: (layer, e, 0, 0)),
         pl.BlockSpec((1, 1, ff, d), lambda e, g: (layer, e, 0, 0))],
        pl.BlockSpec((1, grp, cap, d), lambda e, g: (e, g, 0, 0)),
        _sds((N_EXPERTS, b, cap, d), BF16), name=name)(
            h2, slot.reshape(b, N_EXPERTS, 1, n), aff.reshape(b, N_EXPERTS, 1, n), w1, w3, w2)


def _combine_body(sc_ref, ye_ref, x_ref, md_ref, *rest, cap):
    o_ref = rest[-1]
    sc = sc_ref[0]
    jj = lax.broadcasted_iota(jnp.int32, (sc.shape[0], cap), 1)
    acc = None
    for e in range(N_EXPERTS):
        put = (sc[:, e:e + 1] == jj).astype(BF16)
        t = _dot(put, ye_ref[e, 0])
        acc = t if acc is None else acc + t
    o_ref[0] = x_ref[0] + md_ref[0, 0, 5:6, :] * acc


def _moe_combine(slotc, ye, x1, md, n, blk_rows, is_ctx, cap, name, out_rows, prev=None):
    b, _, d = x1.shape
    rest = 0 if prev is not None else out_rows - blk_rows - n
    tm = _pick_tile(math.gcd(n, rest), (512, 256, 128, 64, 32, 16, 8))
    nq, nfill = n // tm, rest // tm
    off = blk_rows // tm
    extra_specs, extra_args, aliases = _into(prev, 4)
    body = functools.partial(_combine_body, cap=cap)
    clamp = lambda i: jnp.minimum(i, nq - 1)
    return _call(
        _zero_tail(body, nq) if nfill else body, (b, nq + nfill),
        [pl.BlockSpec((1, tm, 128), lambda b, i: (b, clamp(i), 0)),
         pl.BlockSpec((N_EXPERTS, 1, cap, d), lambda b, i: (0, b, 0, 0)),
         pl.BlockSpec((1, tm, d), lambda b, i: (b, off + clamp(i), 0)),
         pl.BlockSpec((1, 1, 6, d), lambda b, i: (b, 1 if is_ctx else 0, 0, 0))] + extra_specs,
        pl.BlockSpec((1, tm, d), lambda b, i: (b, off + i, 0)),
        _sds((b, out_rows, d), F32), name=name, aliases=aliases)(slotc, ye, x1, md, *extra_args)


def _moe(x1, h2, logits, md, w1, w3, w2, layer, l_tok, m_tok, ctx_out):
    b = x1.shape[0]
    cap_l = CAPACITY_FACTOR * l_tok // N_EXPERTS
    slot, aff, slotc = _select(logits, l_tok, 0, cap_l, "moe_select_latent")
    ye = _moe_ffn(h2, slot, aff, w1, w3, w2, layer, l_tok, 0, cap_l, 1, "moe_ffn_latent")
    out_rows = l_tok + m_tok if ctx_out else l_tok
    x2 = _moe_combine(slotc, ye, x1, md, l_tok, 0, False, cap_l, "moe_combine_latent", out_rows)
    if not ctx_out:
        return x2
    cap_c = CAPACITY_FACTOR * m_tok // N_EXPERTS
    blk = l_tok // m_tok
    slot, aff, slotc = _select(logits, m_tok, blk, cap_c, "moe_select_ctx")
    ye = _moe_ffn(h2, slot, aff, w1, w3, w2, layer, m_tok, blk, cap_c, b, "moe_ffn_ctx")
    return _moe_combine(slotc, ye, x1, md, m_tok, l_tok, True, cap_c, "moe_combine_ctx", out_rows, prev=x2)


def _layer_params(l, w_in, na_q_norm, na_k_norm, na_rpb, mla_cq_norm, mla_ckv_norm, mla_w_uq, mla_w_ukv,
                  mla_q_norm, mla_k_norm, rw_mu_ks, rw_mu_qs, rw_w0, rw_w2, rw_a0, rw_a2, rw_g2, rw_k_k, rw_k_a,
                  rw_r_k, rw_ln_w, rw_ln_b, w_br, w_out, rows):
    w = w_in[l]
    d = w.shape[0]
    kq = KEY_COLS
    kpe = w[:, 1280:1344]
    zpad = lambda n: jnp.zeros((d, n), w.dtype)
    wz = jnp.concatenate([
        w[:, kq:kq + 512], w[:, 0:512], w[:, 512:1024],
        w[:, kq + 512:kq + 1024],
        w[:, 1344:1856], w[:, 1856:2368], w[:, kq + 1024:kq + 1536],
        w[:, kq + 1632:kq + 2144],
        w[:, 1024:1280],
        kpe, _swap_halves(kpe),
        w[:, 2368:2496],
        w[:, kq + 1536:kq + 1632], zpad(32)], axis=1).astype(BF16)
    assert wz.shape[1] == NZ
    wg = w[:, GATE_COL0:].astype(BF16)

    def head_gain(g):
        return jnp.concatenate([g[:128], g[128:], _swap_halves(g[128:])])[None, :]

    wq = mla_w_uq[l].reshape(MLA_QLORA, MLA_HEADS, MLA_NOPE + MLA_ROPE)
    wq = jnp.concatenate([wq, _swap_halves(wq[:, :, MLA_NOPE:])], axis=-1).reshape(MLA_QLORA, MLA_HEADS * 256)
    w2cat = jnp.zeros((128, 1024), F32).at[0:32, 0:512].set(rw_w2[l, 0]).at[32:64, 512:].set(rw_w2[l, 1])
    a2cat = jnp.zeros((128, 1024), F32).at[64:96, 0:512].set(rw_a2[l, 0]).at[96:128, 512:].set(rw_a2[l, 1])
    mu = jnp.concatenate([rw_mu_ks[l][:1024], rw_mu_qs[l][:512], rw_mu_ks[l][1024:], rw_mu_qs[l][512:],
                          jnp.zeros((32,), F32)])[None, :]
    hd = lambda a: a.reshape(RW_HEADS, 1, RW_DH)
    return {
        'wz': wz, 'wg': wg,
        'na_gq': na_q_norm[l][None, :], 'na_gk': na_k_norm[l][None, :],
        'na_bias': _na_bias_table(na_rpb[l], rows),
        'g_cq': mla_cq_norm[l][None, :], 'g_ckv': mla_ckv_norm[l][None, :],
        'w_uq': wq.astype(BF16), 'w_ukv': mla_w_ukv[l].astype(BF16),
        'g_q': head_gain(mla_q_norm[l]), 'g_k': head_gain(mla_k_norm[l]),
        'rw_mu': mu, 'rw_k_k': rw_k_k[l][None, :], 'rw_k_a': rw_k_a[l][None, :],
        'rw_w0': rw_w0[l].reshape(1, 1024), 'rw_a0': rw_a0[l].reshape(1, 1024), 'rw_w2': w2cat, 'rw_a2': a2cat,
        'rw_ln_w': hd(rw_ln_w[l]), 'rw_ln_b': hd(rw_ln_b[l]), 'rw_r_k': hd(rw_r_k[l]),
        'rw_g2': jnp.concatenate([rw_g2[l], jnp.zeros((32, RW_W), F32)], axis=0),
    }


def _rw_halo(z3, tr, nl):
    b, t, _ = z3.shape
    nt = t // tr
    z4 = z3.reshape(b, nt, tr, z3.shape[-1])
    pick = lambda a: jnp.concatenate([a[..., ZC_RWK:ZC_RWK + 1536], a[..., ZC_L4:ZC_L4 + 256]], axis=-1)
    first, last = pick(z4[:, :, 0, :]), pick(z4[:, :, tr - 1, :])
    zero = jnp.zeros_like(first[:, :1])
    prev = jnp.concatenate([zero, last[:, :-1]], axis=1)
    nxt = jnp.concatenate([first[:, 1:], zero], axis=1)
    tile = jnp.arange(nt)[None, :, None]
    prev = jnp.where(tile == nl, 0.0, prev)
    nxt = jnp.where(tile == nl - 1, 0.0, nxt)
    return jnp.stack([prev, nxt], axis=2)


def _layer(xs, md, p, g1n, g2n, wr, deep, layer, tabs, l_tok, m_tok, tr, ctx_out):
    b, t, d = xs.shape
    nl, nm = l_tok // tr, m_tok // tr
    nrow = (nl + nm) if ctx_out else nl

    h = _norm1(xs, g1n, md, tr, nl)
    z3 = _matmul(h.reshape(b * t, d), p['wz'], F32).reshape(b, t, NZ)

    nq, nk, nv, mq, mk, mv, xw = _mixer_prep(z3, p, tabs, tr)
    rows = nrow * tr
    y_na = _na_attention(nq, nk, nv, p['na_bias'], l_tok, m_tok, rows)
    y_mla = _mla_attention(mq, mk, mv, tr, l_tok, rows)
    rv, rr, rg, cl, kd, bb, kx = _rw_prep(z3, _rw_halo(z3, tr, nl), p, tr)
    yf, yb = _rw_scan(rr, rv, cl, kd, bb, kx, tr, nl, nm)
    y_rw = _rw_out(yf, yb, rr, rv, kd, rg, p, tr, nrow)
    y_fn = _fourier(xw, tabs, tr, l_tok, m_tok, ctx_out)

    acc = _merge(h, (y_fn, y_na, y_mla, y_rw), p['wg'], deep['w_br'], layer, rows)
    x1 = _outproj(acc, deep['w_out'], layer, xs, md, l_tok)
    h2, logits = _norm2_router(x1, g2n, md, wr, tr, nl)
    return _moe(x1, h2, logits, md, deep['w1'], deep['w3'], deep['w2'], layer, l_tok, m_tok, ctx_out)


def kernel(x, c, ctx, c_ctx, ada_w, ada_b, norm1_g, norm2_g, w_in, na_q_norm, na_k_norm, na_rpb, mla_cq_norm, mla_ckv_norm, mla_w_uq, mla_w_ukv, mla_q_norm, mla_k_norm, rw_mu_ks, rw_mu_qs, rw_w0, rw_w2, rw_a0, rw_a2, rw_g2, rw_k_k, rw_k_a, rw_r_k, rw_ln_w, rw_ln_b, w_br, w_out, moe_router, moe_w1, moe_w3, moe_w2):
    b, l_tok, d = x.shape
    m_tok = ctx.shape[1]
    depth = ada_w.shape[0]
    tr = min(m_tok, 256)
    assert l_tok % tr == 0 and m_tok % tr == 0 and l_tok % m_tok == 0 and tr % RW_CHUNK == 0
    assert l_tok % GRID_W == 0 and m_tok % 128 == 0

    nr = -(-(b + 1) // 8) * 8
    cc = jnp.concatenate([c, c_ctx[None, :], jnp.zeros((nr - b - 1, d), F32)], axis=0)
    mod = _modulation(cc, ada_w, ada_b)
    tabs = {}
    tabs['rope_c'], tabs['rope_s'] = _rope_tables(l_tok, m_tok)

    deep = {'w1': moe_w1.astype(BF16), 'w3': moe_w3.astype(BF16), 'w2': moe_w2.astype(BF16),
            'w_br': w_br.astype(BF16), 'w_out': w_out.astype(BF16)}
    tabs.update(_fourier_tables(l_tok, m_tok, depth > 1))
    xs = jnp.concatenate([x, ctx], axis=1)
    for l in range(depth):
        ctx_out = l < depth - 1
        p = _layer_params(l, w_in, na_q_norm, na_k_norm, na_rpb, mla_cq_norm, mla_ckv_norm, mla_w_uq, mla_w_ukv,
                          mla_q_norm, mla_k_norm, rw_mu_ks, rw_mu_qs, rw_w0, rw_w2, rw_a0, rw_a2, rw_g2, rw_k_k,
                          rw_k_a, rw_r_k, rw_ln_w, rw_ln_b, w_br, w_out, l_tok // GRID_W)
        md = jnp.stack([mod[l, :b], jnp.broadcast_to(mod[l, b], (b, 6 * d))], axis=1).reshape(b, 2, 6, d)
        wr = jnp.concatenate([moe_router[l], jnp.zeros((d, 128 - N_EXPERTS), F32)], axis=1)
        xs = _layer(xs, md, p, norm1_g[l][None, :], norm2_g[l][None, :], wr, deep, l, tabs, l_tok, m_tok, tr, ctx_out)
    return xs
```

```python
import functools
import math

import jax
import jax.numpy as jnp
from jax import lax
from jax.experimental import pallas as pl
from jax.experimental.pallas import tpu as pltpu

F32 = jnp.float32
BF16 = jnp.bfloat16

GRID_W = 64
N_BRANCH = 4
BRANCH_W = 512
NA_HEADS, NA_DH, NA_KH, NA_KW = 8, 64, 8, 16
NA_SCALE = NA_DH ** -0.5
MLA_HEADS, MLA_NOPE, MLA_ROPE = 4, 128, 64
MLA_QLORA = 512
MLA_SCALE = (MLA_NOPE + MLA_ROPE) ** -0.5
RW_HEADS, RW_DH, RW_W = 8, 64, 512
RW_GN_EPS = 64e-5
N_EXPERTS = 16
CAPACITY_FACTOR = 2
ROPE_THETA = 10000.0
NORM_EPS = 1e-6
NEG_INF = -1e30
LOG2E = math.log2(math.e)
RW_CHUNK = 64
KEY_COLS = 2496
NZ = 4736
ZC_NA = 0
ZC_CQ = 1536
ZC_RWK = 2048
ZC_FN = 3584
ZC_CKV = 4096
ZC_KPE = 4352
ZC_L4 = 4480
ZC_G = 4608

VMEM_LIMIT = 56 * 2 ** 20


def _call(body, grid, in_specs, out_specs, out_shape, scratch=(), sem=None, name=None, aliases=None):
    return pl.pallas_call(
        body, grid=grid, in_specs=in_specs, out_specs=out_specs, out_shape=out_shape,
        scratch_shapes=list(scratch), name=name, input_output_aliases=aliases or {},
        compiler_params=pltpu.CompilerParams(
            dimension_semantics=sem or ("parallel",) * len(grid), vmem_limit_bytes=VMEM_LIMIT))


def _full(a):
    nd = a.ndim
    return pl.BlockSpec(a.shape, lambda *_: (0,) * nd)


def _sds(shape, dtype):
    return jax.ShapeDtypeStruct(shape, dtype)


def _dot(a, b):
    return jnp.dot(a, b, preferred_element_type=F32)


def _split2(x):
    hi = x.astype(BF16)
    return hi, (x - hi.astype(F32)).astype(BF16)


def _split3(x):
    hi = x.astype(BF16)
    r = x - hi.astype(F32)
    mid = r.astype(BF16)
    return hi, mid, (r - mid.astype(F32)).astype(BF16)


def _dot3(a, b):
    ah, al = _split2(a)
    bh, bl = _split2(b)
    return _dot(ah, bh) + (_dot(ah, bl) + _dot(al, bh))


def _dot_sel(sel_bf16, x):
    h, m, l = _split3(x)
    return _dot(sel_bf16, h) + (_dot(sel_bf16, m) + _dot(sel_bf16, l))


def _bmm(spec, a, b, passes):
    e = functools.partial(jnp.einsum, spec, preferred_element_type=F32)
    if passes == 1:
        return e(a.astype(BF16), b.astype(BF16))
    ah, al = _split2(a)
    bh, bl = _split2(b)
    return e(ah, bh) + (e(ah, bl) + e(al, bh))


def _pick_tile(n, cands):
    for c in cands:
        if n % c == 0:
            return c
    raise ValueError(f"no tile for {n}")


def _mod_body(c_ref, w_ref, b_ref, o_ref):
    cc = c_ref[...]
    s = cc * jax.nn.sigmoid(cc)
    tn = o_ref.shape[-1]
    for j in range(0, tn, 1024):
        o_ref[0, :, j:j + 1024] = _dot3(s, w_ref[0, :, j:j + 1024]) + b_ref[0, :, j:j + 1024]


def _modulation(cc, ada_w, ada_b):
    depth, d, n6 = ada_w.shape
    r = cc.shape[0]
    tn = 2048
    return _call(
        _mod_body, (depth, n6 // tn),
        [pl.BlockSpec((r, d), lambda l, n: (0, 0)),
         pl.BlockSpec((1, d, tn), lambda l, n: (l, 0, n)),
         pl.BlockSpec((1, 1, tn), lambda l, n: (l, 0, n))],
        pl.BlockSpec((1, r, tn), lambda l, n: (l, 0, n)),
        _sds((depth, r, n6), F32), name="adaln_mod")(cc, ada_w, ada_b.reshape(depth, 1, n6))


def _modnorm(x, g, shift, scale):
    y = x * lax.rsqrt(jnp.mean(x * x, axis=-1, keepdims=True) + NORM_EPS)
    return (y * g) * (1.0 + scale) + shift


def _norm1_body(x_ref, g_ref, md_ref, h_ref):
    h = _modnorm(x_ref[0], g_ref[...], md_ref[0, 0, 0:1, :], md_ref[0, 0, 1:2, :])
    h_ref[0] = h.astype(BF16)


def _norm2_body(x_ref, g_ref, md_ref, wr_ref, h_ref, lg_ref):
    h = _modnorm(x_ref[0], g_ref[...], md_ref[0, 0, 3:4, :], md_ref[0, 0, 4:5, :])
    h_ref[0] = h.astype(BF16)
    lg_ref[0] = _dot3(h, wr_ref[...])


def _md_spec(d, nl):
    return pl.BlockSpec((1, 1, 6, d), lambda b, i: (b, i // nl, 0, 0))


def _norm1(xs, g, md, tr, nl):
    b, t, d = xs.shape
    return _call(
        _norm1_body, (b, t // tr),
        [pl.BlockSpec((1, tr, d), lambda b, i: (b, i, 0)), _full(g), _md_spec(d, nl)],
        pl.BlockSpec((1, tr, d), lambda b, i: (b, i, 0)),
        _sds((b, t, d), BF16), name="norm1")(xs, g, md)


def _norm2_router(x1, g, md, wr, tr, nl):
    b, t, d = x1.shape
    return _call(
        _norm2_body, (b, t // tr),
        [pl.BlockSpec((1, tr, d), lambda b, i: (b, i, 0)), _full(g), _md_spec(d, nl), _full(wr)],
        [pl.BlockSpec((1, tr, d), lambda b, i: (b, i, 0)), pl.BlockSpec((1, tr, 128), lambda b, i: (b, i, 0))],
        [_sds((b, t, d), BF16), _sds((b, t, 128), F32)], name="norm2_router")(x1, g, md, wr)


def _mm_body(a_ref, w_ref, o_ref):
    o_ref[...] = _dot(a_ref[...], w_ref[...]).astype(o_ref.dtype)


def _matmul(a, w, out_dtype):
    r, k = a.shape
    n = w.shape[1]
    tm = _pick_tile(r, (512, 384, 256, 128, 64, 32, 16, 8))
    return _call(
        _mm_body, (r // tm,),
        [pl.BlockSpec((tm, k), lambda i: (i, 0)),
         pl.BlockSpec((k, n), lambda i: (0, 0), pipeline_mode=pl.Buffered(1))],
        pl.BlockSpec((tm, n), lambda i: (i, 0)),
        _sds((r, n), out_dtype), name="in_proj")(a, w)


def _head_sum(x, ones_bd):
    hi, lo = _split2(x)
    return _dot(hi, ones_bd) + _dot(lo, ones_bd)


def _head_sumsq(x, ones_bd):
    return _head_sum(x * x, ones_bd)


def _head_ones(width, dh):
    i = jnp.arange(width) // dh
    return (i[:, None] == i[None, :]).astype(BF16)


def _na_prep_body(z_ref, gq_ref, gk_ref, e_ref, q_ref, k_ref, v_ref):
    z = z_ref[0]
    q, k, v = z[:, :512], z[:, 512:1024], z[:, 1024:]
    e = e_ref[...]
    qn = (q * lax.rsqrt(_head_sumsq(q, e) / NA_DH + NORM_EPS) * (gq_ref[...] * (NA_SCALE * LOG2E))).astype(BF16)
    kn = (k * lax.rsqrt(_head_sumsq(k, e) / NA_DH + NORM_EPS) * gk_ref[...]).astype(BF16)
    vb = v.astype(BF16)
    for h in range(NA_HEADS):
        lo = h * NA_DH
        q_ref[0, h] = qn[:, lo:lo + NA_DH]
        k_ref[0, h] = kn[:, lo:lo + NA_DH]
        v_ref[0, h] = vb[:, lo:lo + NA_DH]


def _softmax_pv(s, vs):
    m = s[0].max(axis=-1, keepdims=True)
    for t in s[1:]:
        m = jnp.maximum(m, t.max(axis=-1, keepdims=True))
    den = None
    acc = None
    for t, v in zip(s, vs):
        p = jnp.exp2(t - m)
        ps = p.sum(axis=-1, keepdims=True)
        o = jnp.einsum('hqk,hkd->hqd', p.astype(BF16), v, preferred_element_type=F32)
        den = ps if den is None else den + ps
        acc = o if acc is None else acc + o
    return acc / den


def _qk(q, k):
    return jnp.einsum('hqd,hkd->hqk', q, k, preferred_element_type=F32)


def _heads_to_lanes(o):
    return jnp.concatenate([o[h] for h in range(o.shape[0])], axis=-1)


NA_ROWS_PER_STEP = 4


def _na_body(q_ref, k_ref, v_ref, bias_ref, o_ref, *, rows, kh, l_tok, m_tok, rps):
    kc = k_ref[0, :, l_tok:l_tok + m_tok, :]
    vc = v_ref[0, :, l_tok:l_tok + m_tok, :]

    @pl.when(pl.program_id(1) < rows // rps)
    def _():
        for rr in range(rps):
            r = pl.program_id(1) * rps + rr
            rs = jnp.clip(r - kh // 2, 0, rows - kh)
            delta = r - rs
            start = pl.multiple_of(rs * GRID_W, GRID_W)
            q = q_ref[0, :, rr * GRID_W:(rr + 1) * GRID_W, :]
            kb = k_ref[0, :, pl.ds(start, kh * GRID_W), :]
            vb = v_ref[0, :, pl.ds(start, kh * GRID_W), :]
            o = _softmax_pv([_qk(q, kb) + bias_ref[delta], _qk(q, kc)], [vb, vc])
            o_ref[0, rr * GRID_W:(rr + 1) * GRID_W, :] = _heads_to_lanes(o).astype(BF16)

    @pl.when(pl.program_id(1) >= rows // rps)
    def _():
        o_ref[0] = _heads_to_lanes(_softmax_pv([_qk(q_ref[0], kc)], [vc])).astype(BF16)


def _na_attention(q, k, v, bias, l_tok, m_tok, out_rows):
    b, _, t, _ = q.shape
    rows = l_tok // GRID_W
    kh = min(NA_KH, rows)
    rps = math.gcd(math.gcd(rows, NA_ROWS_PER_STEP), (out_rows - l_tok) // GRID_W)
    kv = pl.BlockSpec((1, NA_HEADS, t, NA_DH), lambda b, r: (b, 0, 0, 0))
    return _call(
        functools.partial(_na_body, rows=rows, kh=kh, l_tok=l_tok, m_tok=m_tok, rps=rps),
        (b, out_rows // (rps * GRID_W)),
        [pl.BlockSpec((1, NA_HEADS, rps * GRID_W, NA_DH), lambda b, r: (b, 0, r, 0)), kv, kv, _full(bias)],
        pl.BlockSpec((1, rps * GRID_W, BRANCH_W), lambda b, r: (b, r, 0)),
        _sds((b, out_rows, BRANCH_W), BF16), name="na_attention")(q, k, v, bias)


def _toeplitz_body(r_ref, oh_ref, valid_ref, o_ref):
    h, m, l = _split3(r_ref[...])
    oh = oh_ref[...]
    t = _dot(h, oh) + (_dot(m, oh) + _dot(l, oh))
    o_ref[...] = jnp.where(valid_ref[...] > 0.0, t * LOG2E, NEG_INF)


def _na_bias_table(rpb, rows):
    kh = min(NA_KH, rows)
    nh, ndr, ndc = rpb.shape
    col = jnp.arange(GRID_W)
    cs = jnp.clip(col - NA_KW // 2, 0, GRID_W - NA_KW)
    valid = (col[None, :] >= cs[:, None]) & (col[None, :] < cs[:, None] + NA_KW)
    dc = jnp.clip(col[None, :] - col[:, None] + NA_KW - 1, 0, 2 * NA_KW - 2)
    onehot = (jnp.arange(128)[:, None] == dc.reshape(1, -1)).astype(BF16)
    rp = jnp.pad(rpb.astype(F32).reshape(nh * ndr, ndc), ((0, 0), (0, 128 - ndc)))
    vmask = valid.reshape(1, -1).astype(F32)
    toep = pl.pallas_call(_toeplitz_body, out_shape=_sds((nh * ndr, GRID_W * GRID_W), F32),
                          name="na_bias")(rp, onehot, vmask)
    toep = toep.reshape(nh, ndr, GRID_W, GRID_W)
    per_delta = []
    for delta in range(kh):
        lo = NA_KH - 1 - delta
        t = toep[:, lo:lo + kh].transpose(0, 2, 1, 3)
        per_delta.append(t.reshape(nh, GRID_W, kh * GRID_W))
    return jnp.stack(per_delta, axis=0)


def _mla_body(q_ref, k_ref, v_ref, o_ref, *, nl, l_tok):
    @pl.when(pl.program_id(1) < nl)
    def _():
        o_ref[0] = _heads_to_lanes(_softmax_pv([_qk(q_ref[0], k_ref[0])], [v_ref[0]])).astype(BF16)

    @pl.when(pl.program_id(1) >= nl)
    def _():
        kc, vc = k_ref[0, :, l_tok:, :], v_ref[0, :, l_tok:, :]
        o_ref[0] = _heads_to_lanes(_softmax_pv([_qk(q_ref[0], kc)], [vc])).astype(BF16)


def _zero_tail(body, nvalid):
    def wrapped(*refs):
        @pl.when(pl.program_id(1) < nvalid)
        def _():
            body(*refs)

        @pl.when(pl.program_id(1) >= nvalid)
        def _():
            refs[-1][...] = jnp.zeros_like(refs[-1])

    return wrapped


def _into(prev, n_in):
    if prev is None:
        return [], [], None
    return [pl.BlockSpec(memory_space=pl.ANY)], [prev], {n_in: 0}


def _mla_attention(q, k, v, tq, l_tok, out_rows):
    b, heads, t, dv = v.shape
    kv = lambda a: pl.BlockSpec((1, heads, t, a.shape[-1]), lambda b, i: (b, 0, 0, 0))
    return _call(
        functools.partial(_mla_body, nl=l_tok // tq, l_tok=l_tok), (b, out_rows // tq),
        [pl.BlockSpec((1, heads, tq, q.shape[-1]), lambda b, i: (b, 0, i, 0)), kv(k), kv(v)],
        pl.BlockSpec((1, tq, heads * dv), lambda b, i: (b, i, 0)),
        _sds((b, out_rows, heads * dv), BF16), name="mla_attention")(q, k, v)


def _mla_q_body(z_ref, gc_ref, w_ref, gh_ref, ct_ref, st_ref, q_ref):
    cq = z_ref[0]
    cqn = (cq * lax.rsqrt(jnp.mean(cq * cq, axis=-1, keepdims=True) + NORM_EPS) * gc_ref[...]).astype(BF16)
    q = _dot(cqn, w_ref[...])
    gh = gh_ref[...]
    ct, st = ct_ref[...], st_ref[...]
    for h in range(MLA_HEADS):
        qh = q[:, h * 256:(h + 1) * 256]
        nope, pe, sw = qh[:, :128], qh[:, 128:192], qh[:, 192:256]
        ms = (jnp.sum(nope * nope, axis=-1, keepdims=True) + jnp.sum(pe * pe, axis=-1, keepdims=True)) \
            / (MLA_NOPE + MLA_ROPE)
        rinv = lax.rsqrt(ms + NORM_EPS)
        rot = (pe * rinv * gh[:, 128:192]) * ct + (sw * rinv * gh[:, 192:256]) * st
        qh = jnp.concatenate([nope * rinv * gh[:, :128], rot, jnp.zeros_like(rot)], axis=-1)
        q_ref[0, h] = (qh * (MLA_SCALE * LOG2E)).astype(BF16)


def _mla_kv_body(zc_ref, zp_ref, gc_ref, w_ref, gh_ref, ct_ref, st_ref, k_ref, v_ref):
    ckv = zc_ref[0]
    cn = (ckv * lax.rsqrt(jnp.mean(ckv * ckv, axis=-1, keepdims=True) + NORM_EPS) * gc_ref[...]).astype(BF16)
    kv = _dot(cn, w_ref[...])
    zp = zp_ref[0]
    pe, sw = zp[:, :64], zp[:, 64:128]
    pe2 = jnp.sum(pe * pe, axis=-1, keepdims=True)
    gh = gh_ref[...]
    ct, st = ct_ref[...], st_ref[...]
    for h in range(MLA_HEADS):
        nope = kv[:, h * 256:h * 256 + 128]
        ms = (jnp.sum(nope * nope, axis=-1, keepdims=True) + pe2) / (MLA_NOPE + MLA_ROPE)
        rinv = lax.rsqrt(ms + NORM_EPS)
        rot = (pe * rinv * gh[:, 128:192]) * ct + (sw * rinv * gh[:, 192:256]) * st
        k_ref[0, h] = jnp.concatenate([nope * rinv * gh[:, :128], rot, jnp.zeros_like(rot)], axis=-1).astype(BF16)
        v_ref[0, h] = kv[:, h * 256 + 128:(h + 1) * 256].astype(BF16)


def _rope_tables(l_tok, m_tok):
    half = MLA_ROPE // 4
    freqs = ROPE_THETA ** (-jnp.arange(half, dtype=F32) / half)
    pos = jnp.arange(l_tok)
    ar = (pos // GRID_W).astype(F32)[:, None] * freqs[None, :]
    ac = (pos % GRID_W).astype(F32)[:, None] * freqs[None, :]
    ct = jnp.concatenate([jnp.cos(ar), jnp.cos(ar), jnp.cos(ac), jnp.cos(ac)], axis=-1)
    st = jnp.concatenate([-jnp.sin(ar), jnp.sin(ar), -jnp.sin(ac), jnp.sin(ac)], axis=-1)
    ct = jnp.concatenate([ct, jnp.ones((m_tok, MLA_ROPE), F32)], axis=0)
    st = jnp.concatenate([st, jnp.zeros((m_tok, MLA_ROPE), F32)], axis=0)
    return ct, st


def _swap_halves(a):
    return jnp.concatenate([a[..., 16:32], a[..., 0:16], a[..., 48:64], a[..., 32:48]], axis=-1)


def _dft1_body(x_ref, bh_ref, bl_ref, o_ref):
    xh, xl = _split2(x_ref[0])
    r = _dot(xh, bh_ref[...]) + (_dot(xh, bl_ref[...]) + _dot(xl, bh_ref[...]))
    o_ref[0] = r.astype(BF16)


def _dft2_body(c_ref, s_ref, xc_ref, xs_ref, *rest):
    rest[-1][0] = (_dot(c_ref[...], xc_ref[0]) - _dot(s_ref[...], xs_ref[0])).astype(BF16)


def _dft_mats(n):
    k = jnp.arange(n, dtype=jnp.int32)
    ang = ((k[:, None] * k[None, :]) % n).astype(F32) * (2.0 * math.pi / n)
    s = 1.0 / math.sqrt(n)
    return jnp.cos(ang) * s, jnp.sin(ang) * s


def _fourier_tables(l_tok, m_tok, ctx_out):
    cw, sw = _dft_mats(BRANCH_W // 4)
    eye = jnp.eye(4, dtype=F32)
    bd = jnp.concatenate([jnp.kron(eye, cw), jnp.kron(eye, sw)], axis=1)
    tabs = {'bd': _split2(bd), 'lat': tuple(m.astype(BF16) for m in _dft_mats(l_tok))}
    if ctx_out:
        tabs['ctx'] = tuple(m.astype(BF16) for m in _dft_mats(m_tok))
    return tabs


def _mixer_prep_body(zna, zcq, zckv, zkpe, zfn, gq, gk, e, gcq, wuq, ghq, ct, st, gckv, wukv, ghk, bh, bl,
                     nq, nk, nv, mq, mk, mv, xw):
    _na_prep_body(zna, gq, gk, e, nq, nk, nv)
    _mla_q_body(zcq, gcq, wuq, ghq, ct, st, mq)
    _mla_kv_body(zckv, zkpe, gckv, wukv, ghk, ct, st, mk, mv)
    _dft1_body(zfn, bh, bl, xw)


def _mixer_prep(z3, p, tabs, tr):
    b, t, _ = z3.shape
    bh, bl = tabs['bd']
    e = _head_ones(NA_HEADS * NA_DH, NA_DH)
    gq, gk = jnp.tile(p['na_gq'], (1, NA_HEADS)), jnp.tile(p['na_gk'], (1, NA_HEADS))
    consts = [gq, gk, e, p['g_cq'], p['w_uq'], p['g_q'], None, None, p['g_ckv'], p['w_ukv'], p['g_k'], bh, bl]
    rope = pl.BlockSpec((tr, MLA_ROPE), lambda b, i: (i, 0))

    def zcol(off, w):
        return pl.BlockSpec((1, tr, w), lambda b, i: (b, i, off // w))

    def hm(heads, d):
        return pl.BlockSpec((1, heads, tr, d), lambda b, i: (b, 0, i, 0)), _sds((b, heads, t, d), BF16)

    outs = [hm(NA_HEADS, NA_DH)] * 3 + [hm(MLA_HEADS, 256), hm(MLA_HEADS, 256), hm(MLA_HEADS, 128),
                                        (pl.BlockSpec((1, tr, 1024), lambda b, i: (b, i, 0)), _sds((b, t, 1024), BF16))]
    args = [tabs['rope_c'] if c is None else c for c in consts]
    args[7] = tabs['rope_s']
    return _call(
        _mixer_prep_body, (b, t // tr),
        [zcol(ZC_NA, 1536), zcol(ZC_CQ, 512), zcol(ZC_CKV, 256), zcol(ZC_KPE, 128), zcol(ZC_FN, 512)]
        + [rope if c is None else _full(c) for c in consts],
        [o[0] for o in outs], [o[1] for o in outs], name="mixer_prep")(z3, z3, z3, z3, z3, *args)


def _fourier(xw, tabs, tr, l_tok, m_tok, ctx_out):
    b = xw.shape[0]
    nrow = (l_tok + m_tok if ctx_out else l_tok) // tr

    def seq_dft(n, blk, mats, name, prev=None):
        rest = 0 if prev is not None else nrow * tr - n
        tm = _pick_tile(math.gcd(n, rest), (512, 256, 128, 64, 32, 16, 8))
        nq, nfill = n // tm, rest // tm
        mspec = pl.BlockSpec((tm, n), lambda b, i: (jnp.minimum(i, nq - 1), 0))
        xc = pl.BlockSpec((1, n, 512), lambda b, i: (b, blk, 0))
        xs = pl.BlockSpec((1, n, 512), lambda b, i: (b, blk, 1))
        off = blk * n // tm
        ps, pa, aliases = _into(prev, 4)
        return _call(
            _zero_tail(_dft2_body, nq) if nfill else _dft2_body, (b, nq + nfill), [mspec, mspec, xc, xs] + ps,
            pl.BlockSpec((1, tm, 512), lambda b, i: (b, off + i, 0)),
            _sds((b, nrow * tr, 512), BF16), name=name, aliases=aliases)(*mats, xw, xw, *pa)

    y = seq_dft(l_tok, 0, tabs['lat'], "dft_seq_latent")
    if ctx_out:
        y = seq_dft(m_tok, l_tok // m_tok, tabs['ctx'], "dft_seq_ctx", prev=y)
    return y


def _rw_prep_body(zk_ref, zv_ref, zr_ref, zl_ref, zg_ref, halo_ref, mu_ref, kk_ref, ka_ref, w0_ref, a0_ref,
                  w2_ref, a2_ref, trf_ref, trb_ref, e_ref,
                  v_ref, r_ref, g_ref, cl_ref, kd_ref, b_ref, kx_ref, *, tr):
    row = lax.broadcasted_iota(jnp.int32, (tr, 1), 0)
    halo = halo_ref[0, 0]
    mu = mu_ref[...]

    def mix(x, lo, hi):
        xp = jnp.where(row == 0, halo[0:1, lo:hi], pltpu.roll(x, 1, 0))
        xn = jnp.where(row == tr - 1, halo[1:2, lo:hi], pltpu.roll(x, tr - 1, 0))
        return x + (0.5 * (xp + xn) - x) * mu[:, lo:hi]

    k = mix(zk_ref[0], 0, 512)
    v = mix(zv_ref[0], 512, 1024)
    r = mix(zr_ref[0], 1024, 1536)
    l4 = mix(zl_ref[0], 1536, 1664)
    g_ref[0] = mix(zg_ref[0], 1664, 1792)

    wcat = w0_ref[...] + _dot3(jnp.tanh(l4), w2_ref[...])
    acat = a0_ref[...] + _dot3(l4, a2_ref[...])
    kk = k * kk_ref[...]
    kkn = kk / jnp.maximum(jnp.sqrt(_head_sumsq(kk, e_ref[...])), 1e-12)
    for d, tri_ref in enumerate((trf_ref, trb_ref)):
        w = wcat[:, d * 512:(d + 1) * 512]
        lw = -math.exp(-0.5) * jax.nn.sigmoid(w)
        a = jax.nn.sigmoid(acat[:, d * 512:(d + 1) * 512])
        kd = k * (1.0 + (a - 1.0) * ka_ref[...])
        cl = _dot_sel(tri_ref[...], lw)
        bb = kkn * a
        kx = kkn * jnp.exp(-lw)
        for h in range(RW_HEADS):
            lo = h * RW_DH
            cl_ref[0, d, h] = cl[:, lo:lo + RW_DH]
            kd_ref[0, d, h] = kd[:, lo:lo + RW_DH]
            b_ref[0, d, h] = bb[:, lo:lo + RW_DH]
            kx_ref[0, d, h] = kx[:, lo:lo + RW_DH]
    for h in range(RW_HEADS):
        lo = h * RW_DH
        v_ref[0, h] = v[:, lo:lo + RW_DH]
        r_ref[0, h] = r[:, lo:lo + RW_DH]


def _rw_prep(z3, halo, p, tr):
    b, t, _ = z3.shape
    c = RW_CHUNK
    ti = jnp.arange(tr)
    same = (ti[:, None] // c) == (ti[None, :] // c)
    trf = (same & (ti[None, :] <= ti[:, None])).astype(BF16)
    trb = (same & (ti[None, :] >= ti[:, None])).astype(BF16)

    def zcol(off, w):
        return pl.BlockSpec((1, tr, w), lambda b, i: (b, i, off // w))

    hm = pl.BlockSpec((1, RW_HEADS, tr, RW_DH), lambda b, i: (b, 0, i, 0))
    hmd = pl.BlockSpec((1, 2, RW_HEADS, tr, RW_DH), lambda b, i: (b, 0, 0, i, 0))
    s1 = _sds((b, RW_HEADS, t, RW_DH), F32)
    s2 = _sds((b, 2, RW_HEADS, t, RW_DH), F32)
    consts = [p['rw_mu'], p['rw_k_k'], p['rw_k_a'], p['rw_w0'], p['rw_a0'], p['rw_w2'], p['rw_a2'], trf, trb,
              _head_ones(RW_W, RW_DH)]
    return _call(
        functools.partial(_rw_prep_body, tr=tr), (b, t // tr),
        [zcol(ZC_RWK, 512), zcol(ZC_RWK + 512, 512), zcol(ZC_RWK + 1024, 512), zcol(ZC_L4, 128), zcol(ZC_G, 128),
         pl.BlockSpec((1, 1, 2, 1792), lambda b, i: (b, i, 0, 0))] + [_full(a) for a in consts],
        [hm, hm, pl.BlockSpec((1, tr, 128), lambda b, i: (b, i, 0)), hmd, hmd, hmd, hmd],
        [s1, s1, _sds((b, t, 128), F32), s2, s2, s2, s2], name="rw_prep")(z3, z3, z3, z3, z3, halo, *consts)


RW_P_HI = 1
RW_P_LO = 1


RW_INV_BASE = 8


def _unit_tri_inverse(x, upper):
    g, c, _ = x.shape
    nb = c // RW_INV_BASE
    d = jnp.concatenate([x[:, i * RW_INV_BASE:(i + 1) * RW_INV_BASE, i * RW_INV_BASE:(i + 1) * RW_INV_BASE]
                         for i in range(nb)], axis=0)
    ii = lax.broadcasted_iota(jnp.int32, (RW_INV_BASE, RW_INV_BASE), 0)
    jj = lax.broadcasted_iota(jnp.int32, (RW_INV_BASE, RW_INV_BASE), 1)
    t = (ii == jj).astype(F32) + d
    xp = _bmm('gts,gsu->gtu', d, d, RW_P_HI)
    nsq = int(math.log2(RW_INV_BASE)) - 1
    for i in range(nsq):
        if i < nsq - 1:
            both = _bmm('gts,gsu->gtu', xp, jnp.concatenate([t, xp], axis=2), RW_P_HI)
            t, xp = t + both[:, :, :RW_INV_BASE], both[:, :, RW_INV_BASE:]
        else:
            t = t + _bmm('gts,gsu->gtu', xp, t, RW_P_HI)
    size = RW_INV_BASE
    while size < c:
        n = c // size
        ta = jnp.concatenate([t[(2 * p) * g:(2 * p + 1) * g] for p in range(n // 2)], axis=0)
        tb = jnp.concatenate([t[(2 * p + 1) * g:(2 * p + 2) * g] for p in range(n // 2)], axis=0)
        if upper:
            off = jnp.concatenate([x[:, 2 * p * size:(2 * p + 1) * size, (2 * p + 1) * size:(2 * p + 2) * size]
                                   for p in range(n // 2)], axis=0)
            off = _bmm('gts,gsu->gtu', ta, _bmm('gts,gsu->gtu', off, tb, RW_P_HI), RW_P_HI)
            t = jnp.concatenate([jnp.concatenate([ta, off], axis=2),
                                 jnp.concatenate([jnp.zeros_like(off), tb], axis=2)], axis=1)
        else:
            off = jnp.concatenate([x[:, (2 * p + 1) * size:(2 * p + 2) * size, 2 * p * size:(2 * p + 1) * size]
                                   for p in range(n // 2)], axis=0)
            off = _bmm('gts,gsu->gtu', tb, _bmm('gts,gsu->gtu', off, ta, RW_P_HI), RW_P_HI)
            t = jnp.concatenate([jnp.concatenate([ta, jnp.zeros_like(off)], axis=2),
                                 jnp.concatenate([off, tb], axis=2)], axis=1)
        size *= 2
    return t


def _rw_scan_tile(r_ref, v_ref, cl_ref, kd_ref, b_ref, kx_ref, y_ref, s_ref, *, tr, rev):
    c = RW_CHUNK
    nc = tr // c
    g = RW_HEADS * nc

    def ld(x):
        return x.reshape(g, c, RW_DH)

    r, v = ld(r_ref[0]), ld(v_ref[0])
    cl, kd, bb, kx = ld(cl_ref[0, 0]), ld(kd_ref[0, 0]), ld(b_ref[0, 0]), ld(kx_ref[0, 0])
    last = 0 if rev else c - 1
    ctot = cl[:, last:last + 1, :]
    e = jnp.exp(cl)
    ei = jnp.exp(-cl)
    ec = jnp.exp(ctot - cl)
    kkt, rt = kx * e, r * e
    kw, bw = kd * ei, bb * ei
    kc, bc = kd * ec, bb * ec
    a_cat = jnp.concatenate([kkt, rt], axis=1)
    pp = _bmm('gtd,gsd->gts', a_cat, jnp.concatenate([kw, bw], axis=1), RW_P_HI)
    ti = lax.broadcasted_iota(jnp.int32, (c, c), 0)
    si = lax.broadcasted_iota(jnp.int32, (c, c), 1)
    strict = (si > ti) if rev else (si < ti)
    incl = (si >= ti) if rev else (si <= ti)
    nmat = jnp.where(strict, pp[:, :c, :c], 0.0)
    x = -jnp.where(strict, pp[:, :c, c:], 0.0)
    ark = jnp.where(incl, pp[:, c:, :c], 0.0)
    arb = jnp.where(incl, pp[:, c:, c:], 0.0)
    tm = _unit_tri_inverse(x, rev)
    nav = _bmm('gts,gsd->gtd', jnp.concatenate([nmat, ark], axis=1), v, RW_P_HI)
    nv, arkv = nav[:, :c], nav[:, c:]
    ta = _bmm('gts,gsd->gtd', tm, jnp.concatenate([kkt, nv], axis=2), RW_P_HI)
    ata = _bmm('gts,gsd->gtd', arb, ta, RW_P_HI)
    a2 = rt - ata[:, :, :RW_DH]
    y0 = arkv - ata[:, :, RW_DH:]
    tb = _bmm('gtk,gtd->gkd', bc, ta, RW_P_HI)
    di = lax.broadcasted_iota(jnp.int32, (RW_DH, RW_DH), 0)
    dj = lax.broadcasted_iota(jnp.int32, (RW_DH, RW_DH), 1)
    gmt = jnp.where(di == dj, jnp.exp(ctot), 0.0) - tb[:, :, :RW_DH]
    hmt = _bmm('gtk,gtv->gkv', kc, v, RW_P_HI) - tb[:, :, RW_DH:]

    def per_chunk(x):
        return x.reshape(RW_HEADS, nc, x.shape[1], x.shape[2])

    a2, y0, gmt, hmt = per_chunk(a2), per_chunk(y0), per_chunk(gmt), per_chunk(hmt)
    st = s_ref[...]
    ys = [None] * nc
    for ci in (range(nc - 1, -1, -1) if rev else range(nc)):
        ys[ci] = _bmm('htk,hkv->htv', a2[:, ci], st, RW_P_HI) + y0[:, ci]
        st = _bmm('hke,hev->hkv', gmt[:, ci], st, RW_P_HI) + hmt[:, ci]
    s_ref[...] = st
    y_ref[0] = jnp.concatenate(ys, axis=1)


def _rw_scan_body(*refs, tr):
    fwd, bwd, (yf_ref, yb_ref, sf_ref, sb_ref) = refs[0:6], refs[6:12], refs[12:16]

    @pl.when(pl.program_id(1) == 0)
    def _():
        sf_ref[...] = jnp.zeros_like(sf_ref)
        sb_ref[...] = jnp.zeros_like(sb_ref)

    _rw_scan_tile(*fwd, yf_ref, sf_ref, tr=tr, rev=False)
    _rw_scan_tile(*bwd, yb_ref, sb_ref, tr=tr, rev=True)


def _rw_scan(r, v, cl, kd, bb, kx, tr, nl, nm):
    b, _, t, _ = r.shape
    nt = nl + nm
    tile_f = lambda j: jnp.where(j < nm, nl + j, j - nm)
    tile_b = lambda j: nt - 1 - j

    def specs(tile, d):
        hm = pl.BlockSpec((1, RW_HEADS, tr, RW_DH), lambda b, j: (b, 0, tile(j), 0))
        hmd = pl.BlockSpec((1, 1, RW_HEADS, tr, RW_DH), lambda b, j: (b, d, 0, tile(j), 0))
        return hm, hmd

    hf, hfd = specs(tile_f, 0)
    hb, hbd = specs(tile_b, 1)
    shp = _sds((b, RW_HEADS, t, RW_DH), F32)
    state = pltpu.VMEM((RW_HEADS, RW_DH, RW_DH), F32)
    return _call(
        functools.partial(_rw_scan_body, tr=tr), (b, nt),
        [hf, hf, hfd, hfd, hfd, hfd, hb, hb, hbd, hbd, hbd, hbd], [hf, hb], [shp, shp],
        scratch=[state, state], sem=("parallel", "arbitrary"),
        name="rw_scan")(r, v, cl, kd, bb, kx, r, v, cl, kd, bb, kx)


def _rw_out_body(yf_ref, yb_ref, r_ref, v_ref, kd_ref, g_ref, lnw_ref, lnb_ref, rk_ref, g2_ref, o_ref):
    y = yf_ref[0] + yb_ref[0]
    mu = jnp.mean(y, axis=-1, keepdims=True)
    var = jnp.mean(jnp.square(y - mu), axis=-1, keepdims=True)
    yn = (y - mu) * lax.rsqrt(var + RW_GN_EPS) * lnw_ref[...] + lnb_ref[...]
    ksum = kd_ref[0, 0] + kd_ref[0, 1]
    bonus = jnp.sum(r_ref[0] * ksum * rk_ref[...], axis=-1, keepdims=True) * v_ref[0]
    o = yn + bonus
    o = jnp.concatenate([o[h] for h in range(RW_HEADS)], axis=-1)
    gate = _dot3(jax.nn.sigmoid(g_ref[0]), g2_ref[...])
    o_ref[0] = (o * gate).astype(BF16)


def _rw_out(yf, yb, r, v, kd, g, p, tr, nrow):
    b = r.shape[0]
    hm = pl.BlockSpec((1, RW_HEADS, tr, RW_DH), lambda b, i: (b, 0, i, 0))
    consts = [p['rw_ln_w'], p['rw_ln_b'], p['rw_r_k'], p['rw_g2']]
    return _call(
        _rw_out_body, (b, nrow),
        [hm, hm, hm, hm, pl.BlockSpec((1, 2, RW_HEADS, tr, RW_DH), lambda b, i: (b, 0, 0, i, 0)),
         pl.BlockSpec((1, tr, 128), lambda b, i: (b, i, 0))] + [_full(a) for a in consts],
        pl.BlockSpec((1, tr, RW_W), lambda b, i: (b, i, 0)),
        _sds((b, nrow * tr, RW_W), BF16), name="rw_out")(yf, yb, r, v, kd, g, *consts)


GATE_COL0 = KEY_COLS + 2144


def _merge_body(h_ref, y0_ref, y1_ref, y2_ref, y3_ref, g0_ref, g1_ref, g2_ref, g3_ref, wb_ref, o_ref):
    h = h_ref[0]
    acc = None
    for i, (y_ref, wg_ref) in enumerate(zip((y0_ref, y1_ref, y2_ref, y3_ref), (g0_ref, g1_ref, g2_ref, g3_ref))):
        t = jax.nn.sigmoid(_dot(h, wg_ref[...])) * _dot(y_ref[0], wb_ref[0, i])
        acc = t if acc is None else acc + t
    o_ref[0] = acc.astype(BF16)


def _merge(h, ys, wg, wbr, layer, rows):
    b, _, d = h.shape
    tn = 512
    nn = d // tn
    tm = _pick_tile(rows, (1024, 768, 512, 256, 128, 64))
    row = lambda w: pl.BlockSpec((1, tm, w), lambda n, b, i: (b, i, 0))
    gspecs = [pl.BlockSpec((d, tn), functools.partial(lambda n, b, i, k: (0, k * nn + n), k=k))
              for k in range(N_BRANCH)]
    return _call(
        _merge_body, (nn, b, rows // tm),
        [row(d)] + [row(BRANCH_W)] * 4 + gspecs
        + [pl.BlockSpec((1, N_BRANCH, BRANCH_W, tn), lambda n, b, i: (layer, 0, 0, n))],
        pl.BlockSpec((1, tm, tn), lambda n, b, i: (b, i, n)),
        _sds((b, rows, d), BF16), name="merge")(h, *ys, wg, wg, wg, wg, wbr)


def _outproj_body(a_ref, w_ref, x_ref, md_ref, o_ref, *, tm, l_tok):
    row = pl.program_id(1) * tm + lax.broadcasted_iota(jnp.int32, (tm, 1), 0)
    gate = jnp.where(row < l_tok, md_ref[0, 0, 2:3, :], md_ref[0, 1, 2:3, :])
    o_ref[0] = x_ref[0] + gate * _dot(a_ref[0], w_ref[0])


def _outproj(acc, w_out, layer, xs, md, l_tok):
    b, rows, d = acc.shape
    tm = _pick_tile(rows, (768, 512, 256, 128, 64))
    row = pl.BlockSpec((1, tm, d), lambda b, i: (b, i, 0))
    return _call(
        functools.partial(_outproj_body, tm=tm, l_tok=l_tok), (b, rows // tm),
        [row, pl.BlockSpec((1, d, d), lambda b, i: (layer, 0, 0)), row,
         pl.BlockSpec((1, 2, 6, d), lambda b, i: (b, 0, 0, 0))], row,
        _sds((b, rows, d), F32), name="out_proj")(acc, w_out, xs, md)


def _select_body(lg_ref, slot_ref, aff_ref, slotc_ref, *, n, cap):
    lg = lg_ref[0]
    lane = lax.broadcasted_iota(jnp.int32, lg.shape, 1)
    lg = jnp.where(lane < N_EXPERTS, lg, NEG_INF)
    ex = jnp.exp(lg - lg.max(axis=-1, keepdims=True))
    aff = ex / ex.sum(axis=-1, keepdims=True)
    aff_t = aff.T[:N_EXPERTS]

    def count_ge(t):
        return jnp.sum((aff_t >= t).astype(F32), axis=1, keepdims=True)

    tiny = 2.0 ** -126
    lo = jnp.full((N_EXPERTS, 1), tiny, F32)
    for sh in (64, 32, 16, 8, 4, 2, 1):
        cand = lo * (2.0 ** sh)
        lo = jnp.where(count_ge(cand) >= cap, cand, lo)
    hi = lo * 2.0
    below = count_ge(tiny) < cap
    lo = jnp.where(below, 0.0, lo)
    hi = jnp.where(below, tiny, hi)
    for _ in range(40):
        mid = lo + (hi - lo) * 0.5
        ok = count_ge(mid) >= cap
        lo = jnp.where(ok, mid, lo)
        hi = jnp.where(ok, hi, mid)
    gt = aff_t >= hi
    eq = (aff_t >= lo) & (aff_t < hi)
    need = cap - jnp.sum(gt.astype(F32), axis=1, keepdims=True)
    both = jnp.concatenate([gt.astype(BF16), eq.astype(BF16)], axis=0)
    cw = min(n, 512)
    pre = []
    for cb in range(n // cw):
        tp = lax.broadcasted_iota(jnp.int32, (n, cw), 0)
        tt = lax.broadcasted_iota(jnp.int32, (n, cw), 1) + cb * cw
        pre.append(_dot(both, (tp < tt).astype(BF16)))
    pre = jnp.concatenate(pre, axis=1) if len(pre) > 1 else pre[0]
    pre_gt, pre_eq = pre[:N_EXPERTS], pre[N_EXPERTS:]
    sel = gt | (eq & (pre_eq < need))
    slot = jnp.where(sel, pre_gt + jnp.minimum(pre_eq, need), -1.0)
    slot_ref[0] = slot.astype(jnp.int32)
    aff_ref[0] = aff_t
    pad = jnp.full((128 - N_EXPERTS, n), -1.0, F32)
    slotc_ref[0] = jnp.concatenate([slot, pad], axis=0).T.astype(jnp.int32)


def _select(logits, n, blk, cap, name):
    b = logits.shape[0]
    er = pl.BlockSpec((1, N_EXPERTS, n), lambda b: (b, 0, 0))
    return _call(
        functools.partial(_select_body, n=n, cap=cap), (b,),
        [pl.BlockSpec((1, n, 128), lambda b: (b, blk, 0))],
        [er, er, pl.BlockSpec((1, n, 128), lambda b: (b, 0, 0))],
        [_sds((b, N_EXPERTS, n), jnp.int32), _sds((b, N_EXPERTS, n), F32), _sds((b, n, 128), jnp.int32)],
        name=name)(logits)


def _ffn_body(h_ref, slot_ref, aff_ref, w1_ref, w3_ref, w2_ref, o_ref, *, grp, n, cap):
    xs, gates = [], []
    jrow = lax.broadcasted_iota(jnp.int32, (cap, n), 0)
    for gi in range(grp):
        pick = jrow == slot_ref[gi, 0]
        xs.append(_dot(pick.astype(BF16), h_ref[gi]).astype(BF16))
        gates.append(jnp.sum(jnp.where(pick, aff_ref[gi, 0], 0.0), axis=1, keepdims=True))
    xe = jnp.concatenate(xs, axis=0) if grp > 1 else xs[0]
    gate = jnp.concatenate(gates, axis=0) if grp > 1 else gates[0]
    a = _dot(xe, w1_ref[0, 0])
    u = _dot(xe, w3_ref[0, 0])
    hm = (a * jax.nn.sigmoid(a) * u).astype(BF16)
    ye = (_dot(hm, w2_ref[0, 0]) * gate).astype(BF16)
    for gi in range(grp):
        o_ref[0, gi] = ye[gi * cap:(gi + 1) * cap]


def _moe_ffn(h2, slot, aff, w1, w3, w2, layer, n, blk, cap, grp, name):
    b, _, d = h2.shape
    ff = w1.shape[-1]
    sr = pl.BlockSpec((grp, 1, 1, n), lambda e, g: (g, e, 0, 0))
    return _call(
        functools.partial(_ffn_body, grp=grp, n=n, cap=cap), (N_EXPERTS, b // grp),
        [pl.BlockSpec((grp, n, d), lambda e, g: (g, blk, 0)), sr, sr,
         pl.BlockSpec((1, 1, d, ff), lambda e, g: (layer, e, 0, 0)),
         pl.BlockSpec((1, 1, d, ff), lambda e, g: (layer, e, 0, 0)),
         pl.BlockSpec((1, 1, ff, d), lambda e, g: (layer, e, 0, 0))],
        pl.BlockSpec((1, grp, cap, d), lambda e, g: (e, g, 0, 0)),
        _sds((N_EXPERTS, b, cap, d), BF16), name=name)(
            h2, slot.reshape(b, N_EXPERTS, 1, n), aff.reshape(b, N_EXPERTS, 1, n), w1, w3, w2)


def _combine_body(sc_ref, ye_ref, x_ref, md_ref, *rest, cap):
    o_ref = rest[-1]
    sc = sc_ref[0]
    jj = lax.broadcasted_iota(jnp.int32, (sc.shape[0], cap), 1)
    acc = None
    for e in range(N_EXPERTS):
        put = (sc[:, e:e + 1] == jj).astype(BF16)
        t = _dot(put, ye_ref[e, 0])
        acc = t if acc is None else acc + t
    o_ref[0] = x_ref[0] + md_ref[0, 0, 5:6, :] * acc


def _moe_combine(slotc, ye, x1, md, n, blk_rows, is_ctx, cap, name, out_rows, prev=None):
    b, _, d = x1.shape
    rest = 0 if prev is not None else out_rows - blk_rows - n
    tm = _pick_tile(math.gcd(n, rest), (512, 256, 128, 64, 32, 16, 8))
    nq, nfill = n // tm, rest // tm
    off = blk_rows // tm
    extra_specs, extra_args, aliases = _into(prev, 4)
    body = functools.partial(_combine_body, cap=cap)
    clamp = lambda i: jnp.minimum(i, nq - 1)
    return _call(
        _zero_tail(body, nq) if nfill else body, (b, nq + nfill),
        [pl.BlockSpec((1, tm, 128), lambda b, i: (b, clamp(i), 0)),
         pl.BlockSpec((N_EXPERTS, 1, cap, d), lambda b, i: (0, b, 0, 0)),
         pl.BlockSpec((1, tm, d), lambda b, i: (b, off + clamp(i), 0)),
         pl.BlockSpec((1, 1, 6, d), lambda b, i: (b, 1 if is_ctx else 0, 0, 0))] + extra_specs,
        pl.BlockSpec((1, tm, d), lambda b, i: (b, off + i, 0)),
        _sds((b, out_rows, d), F32), name=name, aliases=aliases)(slotc, ye, x1, md, *extra_args)


def _moe(x1, h2, logits, md, w1, w3, w2, layer, l_tok, m_tok, ctx_out):
    b = x1.shape[0]
    cap_l = CAPACITY_FACTOR * l_tok // N_EXPERTS
    slot, aff, slotc = _select(logits, l_tok, 0, cap_l, "moe_select_latent")
    ye = _moe_ffn(h2, slot, aff, w1, w3, w2, layer, l_tok, 0, cap_l, 1, "moe_ffn_latent")
    out_rows = l_tok + m_tok if ctx_out else l_tok
    x2 = _moe_combine(slotc, ye, x1, md, l_tok, 0, False, cap_l, "moe_combine_latent", out_rows)
    if not ctx_out:
        return x2
    cap_c = CAPACITY_FACTOR * m_tok // N_EXPERTS
    blk = l_tok // m_tok
    slot, aff, slotc = _select(logits, m_tok, blk, cap_c, "moe_select_ctx")
    ye = _moe_ffn(h2, slot, aff, w1, w3, w2, layer, m_tok, blk, cap_c, b, "moe_ffn_ctx")
    return _moe_combine(slotc, ye, x1, md, m_tok, l_tok, True, cap_c, "moe_combine_ctx", out_rows, prev=x2)


def _layer_params(l, w_in, na_q_norm, na_k_norm, na_rpb, mla_cq_norm, mla_ckv_norm, mla_w_uq, mla_w_ukv,
                  mla_q_norm, mla_k_norm, rw_mu_ks, rw_mu_qs, rw_w0, rw_w2, rw_a0, rw_a2, rw_g2, rw_k_k, rw_k_a,
                  rw_r_k, rw_ln_w, rw_ln_b, w_br, w_out, rows):
    w = w_in[l]
    d = w.shape[0]
    kq = KEY_COLS
    kpe = w[:, 1280:1344]
    zpad = lambda n: jnp.zeros((d, n), w.dtype)
    wz = jnp.concatenate([
        w[:, kq:kq + 512], w[:, 0:512], w[:, 512:1024],
        w[:, kq + 512:kq + 1024],
        w[:, 1344:1856], w[:, 1856:2368], w[:, kq + 1024:kq + 1536],
        w[:, kq + 1632:kq + 2144],
        w[:, 1024:1280],
        kpe, _swap_halves(kpe),
        w[:, 2368:2496],
        w[:, kq + 1536:kq + 1632], zpad(32)], axis=1).astype(BF16)
    assert wz.shape[1] == NZ
    wg = w[:, GATE_COL0:].astype(BF16)

    def head_gain(g):
        return jnp.concatenate([g[:128], g[128:], _swap_halves(g[128:])])[None, :]

    wq = mla_w_uq[l].reshape(MLA_QLORA, MLA_HEADS, MLA_NOPE + MLA_ROPE)
    wq = jnp.concatenate([wq, _swap_halves(wq[:, :, MLA_NOPE:])], axis=-1).reshape(MLA_QLORA, MLA_HEADS * 256)
    w2cat = jnp.zeros((128, 1024), F32).at[0:32, 0:512].set(rw_w2[l, 0]).at[32:64, 512:].set(rw_w2[l, 1])
    a2cat = jnp.zeros((128, 1024), F32).at[64:96, 0:512].set(rw_a2[l, 0]).at[96:128, 512:].set(rw_a2[l, 1])
    mu = jnp.concatenate([rw_mu_ks[l][:1024], rw_mu_qs[l][:512], rw_mu_ks[l][1024:], rw_mu_qs[l][512:],
                          jnp.zeros((32,), F32)])[None, :]
    hd = lambda a: a.reshape(RW_HEADS, 1, RW_DH)
    return {
        'wz': wz, 'wg': wg,
        'na_gq': na_q_norm[l][None, :], 'na_gk': na_k_norm[l][None, :],
        'na_bias': _na_bias_table(na_rpb[l], rows),
        'g_cq': mla_cq_norm[l][None, :], 'g_ckv': mla_ckv_norm[l][None, :],
        'w_uq': wq.astype(BF16), 'w_ukv': mla_w_ukv[l].astype(BF16),
        'g_q': head_gain(mla_q_norm[l]), 'g_k': head_gain(mla_k_norm[l]),
        'rw_mu': mu, 'rw_k_k': rw_k_k[l][None, :], 'rw_k_a': rw_k_a[l][None, :],
        'rw_w0': rw_w0[l].reshape(1, 1024), 'rw_a0': rw_a0[l].reshape(1, 1024), 'rw_w2': w2cat, 'rw_a2': a2cat,
        'rw_ln_w': hd(rw_ln_w[l]), 'rw_ln_b': hd(rw_ln_b[l]), 'rw_r_k': hd(rw_r_k[l]),
        'rw_g2': jnp.concatenate([rw_g2[l], jnp.zeros((32, RW_W), F32)], axis=0),
    }


def _rw_halo(z3, tr, nl):
    b, t, _ = z3.shape
    nt = t // tr
    z4 = z3.reshape(b, nt, tr, z3.shape[-1])
    pick = lambda a: jnp.concatenate([a[..., ZC_RWK:ZC_RWK + 1536], a[..., ZC_L4:ZC_L4 + 256]], axis=-1)
    first, last = pick(z4[:, :, 0, :]), pick(z4[:, :, tr - 1, :])
    zero = jnp.zeros_like(first[:, :1])
    prev = jnp.concatenate([zero, last[:, :-1]], axis=1)
    nxt = jnp.concatenate([first[:, 1:], zero], axis=1)
    tile = jnp.arange(nt)[None, :, None]
    prev = jnp.where(tile == nl, 0.0, prev)
    nxt = jnp.where(tile == nl - 1, 0.0, nxt)
    return jnp.stack([prev, nxt], axis=2)


def _layer(xs, md, p, g1n, g2n, wr, deep, layer, tabs, l_tok, m_tok, tr, ctx_out):
    b, t, d = xs.shape
    nl, nm = l_tok // tr, m_tok // tr
    nrow = (nl + nm) if ctx_out else nl

    h = _norm1(xs, g1n, md, tr, nl)
    z3 = _matmul(h.reshape(b * t, d), p['wz'], F32).reshape(b, t, NZ)

    nq, nk, nv, mq, mk, mv, xw = _mixer_prep(z3, p, tabs, tr)
    rows = nrow * tr
    y_na = _na_attention(nq, nk, nv, p['na_bias'], l_tok, m_tok, rows)
    y_mla = _mla_attention(mq, mk, mv, tr, l_tok, rows)
    rv, rr, rg, cl, kd, bb, kx = _rw_prep(z3, _rw_halo(z3, tr, nl), p, tr)
    yf, yb = _rw_scan(rr, rv, cl, kd, bb, kx, tr, nl, nm)
    y_rw = _rw_out(yf, yb, rr, rv, kd, rg, p, tr, nrow)
    y_fn = _fourier(xw, tabs, tr, l_tok, m_tok, ctx_out)

    acc = _merge(h, (y_fn, y_na, y_mla, y_rw), p['wg'], deep['w_br'], layer, rows)
    x1 = _outproj(acc, deep['w_out'], layer, xs, md, l_tok)
    h2, logits = _norm2_router(x1, g2n, md, wr, tr, nl)
    return _moe(x1, h2, logits, md, deep['w1'], deep['w3'], deep['w2'], layer, l_tok, m_tok, ctx_out)


def kernel(x, c, ctx, c_ctx, ada_w, ada_b, norm1_g, norm2_g, w_in, na_q_norm, na_k_norm, na_rpb, mla_cq_norm, mla_ckv_norm, mla_w_uq, mla_w_ukv, mla_q_norm, mla_k_norm, rw_mu_ks, rw_mu_qs, rw_w0, rw_w2, rw_a0, rw_a2, rw_g2, rw_k_k, rw_k_a, rw_r_k, rw_ln_w, rw_ln_b, w_br, w_out, moe_router, moe_w1, moe_w3, moe_w2):
    b, l_tok, d = x.shape
    m_tok = ctx.shape[1]
    depth = ada_w.shape[0]
    tr = min(m_tok, 256)
    assert l_tok % tr == 0 and m_tok % tr == 0 and l_tok % m_tok == 0 and tr % RW_CHUNK == 0
    assert l_tok % GRID_W == 0 and m_tok % 128 == 0

    nr = -(-(b + 1) // 8) * 8
    cc = jnp.concatenate([c, c_ctx[None, :], jnp.zeros((nr - b - 1, d), F32)], axis=0)
    mod = _modulation(cc, ada_w, ada_b)
    tabs = {}
    tabs['rope_c'], tabs['rope_s'] = _rope_tables(l_tok, m_tok)

    deep = {'w1': moe_w1.astype(BF16), 'w3': moe_w3.astype(BF16), 'w2': moe_w2.astype(BF16),
            'w_br': w_br.astype(BF16), 'w_out': w_out.astype(BF16)}
    tabs.update(_fourier_tables(l_tok, m_tok, depth > 1))
    xs = jnp.concatenate([x, ctx], axis=1)
    for l in range(depth):
        ctx_out = l < depth - 1
        p = _layer_params(l, w_in, na_q_norm, na_k_norm, na_rpb, mla_cq_norm, mla_ckv_norm, mla_w_uq, mla_w_ukv,
                          mla_q_norm, mla_k_norm, rw_mu_ks, rw_mu_qs, rw_w0, rw_w2, rw_a0, rw_a2, rw_g2, rw_k_k,
                          rw_k_a, rw_r_k, rw_ln_w, rw_ln_b, w_br, w_out, l_tok // GRID_W)
        md = jnp.stack([mod[l, :b], jnp.broadcast_to(mod[l, b], (b, 6 * d))], axis=1).reshape(b, 2, 6, d)
        wr = jnp.concatenate([moe_router[l], jnp.zeros((d, 128 - N_EXPERTS), F32)], axis=1)
        xs = _layer(xs, md, p, norm1_g[l][None, :], norm2_g[l][None, :], wr, deep, l, tabs, l_tok, m_tok, tr, ctx_out)
    return xs
```

```python
import functools
import math

import jax
import jax.numpy as jnp
from jax import lax
from jax.experimental import pallas as pl
from jax.experimental.pallas import tpu as pltpu

F32 = jnp.float32
BF16 = jnp.bfloat16

GRID_W = 64
N_BRANCH = 4
BRANCH_W = 512
NA_HEADS, NA_DH, NA_KH, NA_KW = 8, 64, 8, 16
NA_SCALE = NA_DH ** -0.5
MLA_HEADS, MLA_NOPE, MLA_ROPE = 4, 128, 64
MLA_QLORA = 512
MLA_SCALE = (MLA_NOPE + MLA_ROPE) ** -0.5
RW_HEADS, RW_DH, RW_W = 8, 64, 512
RW_GN_EPS = 64e-5
N_EXPERTS = 16
CAPACITY_FACTOR = 2
ROPE_THETA = 10000.0
NORM_EPS = 1e-6
NEG_INF = -1e30
LOG2E = math.log2(math.e)
RW_CHUNK = 64
KEY_COLS = 2496
NZ = 4736
ZC_NA = 0
ZC_CQ = 1536
ZC_RWK = 2048
ZC_FN = 3584
ZC_CKV = 4096
ZC_KPE = 4352
ZC_L4 = 4480
ZC_G = 4608

VMEM_LIMIT = 56 * 2 ** 20


def _call(body, grid, in_specs, out_specs, out_shape, scratch=(), sem=None, name=None, aliases=None):
    return pl.pallas_call(
        body, grid=grid, in_specs=in_specs, out_specs=out_specs, out_shape=out_shape,
        scratch_shapes=list(scratch), name=name, input_output_aliases=aliases or {},
        compiler_params=pltpu.CompilerParams(
            dimension_semantics=sem or ("parallel",) * len(grid), vmem_limit_bytes=VMEM_LIMIT))


def _full(a):
    nd = a.ndim
    return pl.BlockSpec(a.shape, lambda *_: (0,) * nd)


def _sds(shape, dtype):
    return jax.ShapeDtypeStruct(shape, dtype)


def _dot(a, b):
    return jnp.dot(a, b, preferred_element_type=F32)


def _split2(x):
    hi = x.astype(BF16)
    return hi, (x - hi.astype(F32)).astype(BF16)


def _split3(x):
    hi = x.astype(BF16)
    r = x - hi.astype(F32)
    mid = r.astype(BF16)
    return hi, mid, (r - mid.astype(F32)).astype(BF16)


def _dot3(a, b):
    ah, al = _split2(a)
    bh, bl = _split2(b)
    return _dot(ah, bh) + (_dot(ah, bl) + _dot(al, bh))


def _dot_sel(sel_bf16, x):
    h, m, l = _split3(x)
    return _dot(sel_bf16, h) + (_dot(sel_bf16, m) + _dot(sel_bf16, l))


def _bmm(spec, a, b, passes):
    e = functools.partial(jnp.einsum, spec, preferred_element_type=F32)
    if passes == 1:
        return e(a.astype(BF16), b.astype(BF16))
    ah, al = _split2(a)
    bh, bl = _split2(b)
    return e(ah, bh) + (e(ah, bl) + e(al, bh))


def _pick_tile(n, cands):
    for c in cands:
        if n % c == 0:
            return c
    raise ValueError(f"no tile for {n}")


def _mod_body(c_ref, w_ref, b_ref, o_ref):
    cc = c_ref[...]
    s = cc * jax.nn.sigmoid(cc)
    tn = o_ref.shape[-1]
    for j in range(0, tn, 1024):
        o_ref[0, :, j:j + 1024] = _dot3(s, w_ref[0, :, j:j + 1024]) + b_ref[0, :, j:j + 1024]


def _modulation(cc, ada_w, ada_b):
    depth, d, n6 = ada_w.shape
    r = cc.shape[0]
    tn = 2048
    return _call(
        _mod_body, (depth, n6 // tn),
        [pl.BlockSpec((r, d), lambda l, n: (0, 0)),
         pl.BlockSpec((1, d, tn), lambda l, n: (l, 0, n)),
         pl.BlockSpec((1, 1, tn), lambda l, n: (l, 0, n))],
        pl.BlockSpec((1, r, tn), lambda l, n: (l, 0, n)),
        _sds((depth, r, n6), F32), name="adaln_mod")(cc, ada_w, ada_b.reshape(depth, 1, n6))


def _modnorm(x, g, shift, scale):
    y = x * lax.rsqrt(jnp.mean(x * x, axis=-1, keepdims=True) + NORM_EPS)
    return (y * g) * (1.0 + scale) + shift


def _norm1_body(x_ref, g_ref, md_ref, h_ref):
    h = _modnorm(x_ref[0], g_ref[...], md_ref[0, 0, 0:1, :], md_ref[0, 0, 1:2, :])
    h_ref[0] = h.astype(BF16)


def _norm2_body(x_ref, g_ref, md_ref, wr_ref, h_ref, lg_ref):
    h = _modnorm(x_ref[0], g_ref[...], md_ref[0, 0, 3:4, :], md_ref[0, 0, 4:5, :])
    h_ref[0] = h.astype(BF16)
    lg_ref[0] = _dot3(h, wr_ref[...])


def _md_spec(d, nl):
    return pl.BlockSpec((1, 1, 6, d), lambda b, i: (b, i // nl, 0, 0))


def _norm1(xs, g, md, tr, nl):
    b, t, d = xs.shape
    return _call(
        _norm1_body, (b, t // tr),
        [pl.BlockSpec((1, tr, d), lambda b, i: (b, i, 0)), _full(g), _md_spec(d, nl)],
        pl.BlockSpec((1, tr, d), lambda b, i: (b, i, 0)),
        _sds((b, t, d), BF16), name="norm1")(xs, g, md)


def _norm2_router(x1, g, md, wr, tr, nl):
    b, t, d = x1.shape
    return _call(
        _norm2_body, (b, t // tr),
        [pl.BlockSpec((1, tr, d), lambda b, i: (b, i, 0)), _full(g), _md_spec(d, nl), _full(wr)],
        [pl.BlockSpec((1, tr, d), lambda b, i: (b, i, 0)), pl.BlockSpec((1, tr, 128), lambda b, i: (b, i, 0))],
        [_sds((b, t, d), BF16), _sds((b, t, 128), F32)], name="norm2_router")(x1, g, md, wr)


def _mm_body(a_ref, w_ref, o_ref):
    o_ref[...] = _dot(a_ref[...], w_ref[...]).astype(o_ref.dtype)


def _matmul(a, w, out_dtype):
    r, k = a.shape
    n = w.shape[1]
    tm = _pick_tile(r, (512, 384, 256, 128, 64, 32, 16, 8))
    return _call(
        _mm_body, (r // tm,),
        [pl.BlockSpec((tm, k), lambda i: (i, 0)),
         pl.BlockSpec((k, n), lambda i: (0, 0), pipeline_mode=pl.Buffered(1))],
        pl.BlockSpec((tm, n), lambda i: (i, 0)),
        _sds((r, n), out_dtype), name="in_proj")(a, w)


def _head_sum(x, ones_bd):
    hi, lo = _split2(x)
    return _dot(hi, ones_bd) + _dot(lo, ones_bd)


def _head_sumsq(x, ones_bd):
    return _head_sum(x * x, ones_bd)


def _head_ones(width, dh):
    i = jnp.arange(width) // dh
    return (i[:, None] == i[None, :]).astype(BF16)


def _na_prep_body(z_ref, gq_ref, gk_ref, e_ref, q_ref, k_ref, v_ref):
    z = z_ref[0]
    q, k, v = z[:, :512], z[:, 512:1024], z[:, 1024:]
    e = e_ref[...]
    qn = (q * lax.rsqrt(_head_sumsq(q, e) / NA_DH + NORM_EPS) * (gq_ref[...] * (NA_SCALE * LOG2E))).astype(BF16)
    kn = (k * lax.rsqrt(_head_sumsq(k, e) / NA_DH + NORM_EPS) * gk_ref[...]).astype(BF16)
    vb = v.astype(BF16)
    for h in range(NA_HEADS):
        lo = h * NA_DH
        q_ref[0, h] = qn[:, lo:lo + NA_DH]
        k_ref[0, h] = kn[:, lo:lo + NA_DH]
        v_ref[0, h] = vb[:, lo:lo + NA_DH]


def _softmax_pv(s, vs):
    m = s[0].max(axis=-1, keepdims=True)
    for t in s[1:]:
        m = jnp.maximum(m, t.max(axis=-1, keepdims=True))
    den = None
    acc = None
    for t, v in zip(s, vs):
        p = jnp.exp2(t - m)
        ps = p.sum(axis=-1, keepdims=True)
        o = jnp.einsum('hqk,hkd->hqd', p.astype(BF16), v, preferred_element_type=F32)
        den = ps if den is None else den + ps
        acc = o if acc is None else acc + o
    return acc / den


def _qk(q, k):
    return jnp.einsum('hqd,hkd->hqk', q, k, preferred_element_type=F32)


def _heads_to_lanes(o):
    return jnp.concatenate([o[h] for h in range(o.shape[0])], axis=-1)


NA_ROWS_PER_STEP = 4


def _na_body(q_ref, k_ref, v_ref, bias_ref, o_ref, *, rows, kh, l_tok, m_tok, rps):
    kc = k_ref[0, :, l_tok:l_tok + m_tok, :]
    vc = v_ref[0, :, l_tok:l_tok + m_tok, :]

    @pl.when(pl.program_id(1) < rows // rps)
    def _():
        for rr in range(rps):
            r = pl.program_id(1) * rps + rr
            rs = jnp.clip(r - kh // 2, 0, rows - kh)
            delta = r - rs
            start = pl.multiple_of(rs * GRID_W, GRID_W)
            q = q_ref[0, :, rr * GRID_W:(rr + 1) * GRID_W, :]
            kb = k_ref[0, :, pl.ds(start, kh * GRID_W), :]
            vb = v_ref[0, :, pl.ds(start, kh * GRID_W), :]
            o = _softmax_pv([_qk(q, kb) + bias_ref[delta], _qk(q, kc)], [vb, vc])
            o_ref[0, rr * GRID_W:(rr + 1) * GRID_W, :] = _heads_to_lanes(o).astype(BF16)

    @pl.when(pl.program_id(1) >= rows // rps)
    def _():
        o_ref[0] = _heads_to_lanes(_softmax_pv([_qk(q_ref[0], kc)], [vc])).astype(BF16)


def _na_attention(q, k, v, bias, l_tok, m_tok, out_rows):
    b, _, t, _ = q.shape
    rows = l_tok // GRID_W
    kh = min(NA_KH, rows)
    rps = math.gcd(math.gcd(rows, NA_ROWS_PER_STEP), (out_rows - l_tok) // GRID_W)
    kv = pl.BlockSpec((1, NA_HEADS, t, NA_DH), lambda b, r: (b, 0, 0, 0))
    return _call(
        functools.partial(_na_body, rows=rows, kh=kh, l_tok=l_tok, m_tok=m_tok, rps=rps),
        (b, out_rows // (rps * GRID_W)),
        [pl.BlockSpec((1, NA_HEADS, rps * GRID_W, NA_DH), lambda b, r: (b, 0, r, 0)), kv, kv, _full(bias)],
        pl.BlockSpec((1, rps * GRID_W, BRANCH_W), lambda b, r: (b, r, 0)),
        _sds((b, out_rows, BRANCH_W), BF16), name="na_attention")(q, k, v, bias)


def _toeplitz_body(r_ref, oh_ref, valid_ref, o_ref):
    h, m, l = _split3(r_ref[...])
    oh = oh_ref[...]
    t = _dot(h, oh) + (_dot(m, oh) + _dot(l, oh))
    o_ref[...] = jnp.where(valid_ref[...] > 0.0, t * LOG2E, NEG_INF)


def _na_bias_table(rpb, rows):
    kh = min(NA_KH, rows)
    nh, ndr, ndc = rpb.shape
    col = jnp.arange(GRID_W)
    cs = jnp.clip(col - NA_KW // 2, 0, GRID_W - NA_KW)
    valid = (col[None, :] >= cs[:, None]) & (col[None, :] < cs[:, None] + NA_KW)
    dc = jnp.clip(col[None, :] - col[:, None] + NA_KW - 1, 0, 2 * NA_KW - 2)
    onehot = (jnp.arange(128)[:, None] == dc.reshape(1, -1)).astype(BF16)
    rp = jnp.pad(rpb.astype(F32).reshape(nh * ndr, ndc), ((0, 0), (0, 128 - ndc)))
    vmask = valid.reshape(1, -1).astype(F32)
    toep = pl.pallas_call(_toeplitz_body, out_shape=_sds((nh * ndr, GRID_W * GRID_W), F32),
                          name="na_bias")(rp, onehot, vmask)
    toep = toep.reshape(nh, ndr, GRID_W, GRID_W)
    per_delta = []
    for delta in range(kh):
        lo = NA_KH - 1 - delta
        t = toep[:, lo:lo + kh].transpose(0, 2, 1, 3)
        per_delta.append(t.reshape(nh, GRID_W, kh * GRID_W))
    return jnp.stack(per_delta, axis=0)


def _mla_body(q_ref, k_ref, v_ref, o_ref, *, nl, l_tok):
    @pl.when(pl.program_id(1) < nl)
    def _():
        o_ref[0] = _heads_to_lanes(_softmax_pv([_qk(q_ref[0], k_ref[0])], [v_ref[0]])).astype(BF16)

    @pl.when(pl.program_id(1) >= nl)
    def _():
        kc, vc = k_ref[0, :, l_tok:, :], v_ref[0, :, l_tok:, :]
        o_ref[0] = _heads_to_lanes(_softmax_pv([_qk(q_ref[0], kc)], [vc])).astype(BF16)


def _zero_tail(body, nvalid):
    def wrapped(*refs):
        @pl.when(pl.program_id(1) < nvalid)
        def _():
            body(*refs)

        @pl.when(pl.program_id(1) >= nvalid)
        def _():
            refs[-1][...] = jnp.zeros_like(refs[-1])

    return wrapped


def _into(prev, n_in):
    if prev is None:
        return [], [], None
    return [pl.BlockSpec(memory_space=pl.ANY)], [prev], {n_in: 0}


def _mla_attention(q, k, v, tq, l_tok, out_rows):
    b, heads, t, dv = v.shape
    kv = lambda a: pl.BlockSpec((1, heads, t, a.shape[-1]), lambda b, i: (b, 0, 0, 0))
    return _call(
        functools.partial(_mla_body, nl=l_tok // tq, l_tok=l_tok), (b, out_rows // tq),
        [pl.BlockSpec((1, heads, tq, q.shape[-1]), lambda b, i: (b, 0, i, 0)), kv(k), kv(v)],
        pl.BlockSpec((1, tq, heads * dv), lambda b, i: (b, i, 0)),
        _sds((b, out_rows, heads * dv), BF16), name="mla_attention")(q, k, v)


def _mla_q_body(z_ref, gc_ref, w_ref, gh_ref, ct_ref, st_ref, q_ref):
    cq = z_ref[0]
    cqn = (cq * lax.rsqrt(jnp.mean(cq * cq, axis=-1, keepdims=True) + NORM_EPS) * gc_ref[...]).astype(BF16)
    q = _dot(cqn, w_ref[...])
    gh = gh_ref[...]
    ct, st = ct_ref[...], st_ref[...]
    for h in range(MLA_HEADS):
        qh = q[:, h * 256:(h + 1) * 256]
        nope, pe, sw = qh[:, :128], qh[:, 128:192], qh[:, 192:256]
        ms = (jnp.sum(nope * nope, axis=-1, keepdims=True) + jnp.sum(pe * pe, axis=-1, keepdims=True)) \
            / (MLA_NOPE + MLA_ROPE)
        rinv = lax.rsqrt(ms + NORM_EPS)
        rot = (pe * rinv * gh[:, 128:192]) * ct + (sw * rinv * gh[:, 192:256]) * st
        qh = jnp.concatenate([nope * rinv * gh[:, :128], rot, jnp.zeros_like(rot)], axis=-1)
        q_ref[0, h] = (qh * (MLA_SCALE * LOG2E)).astype(BF16)


def _mla_kv_body(zc_ref, zp_ref, gc_ref, w_ref, gh_ref, ct_ref, st_ref, k_ref, v_ref):
    ckv = zc_ref[0]
    cn = (ckv * lax.rsqrt(jnp.mean(ckv * ckv, axis=-1, keepdims=True) + NORM_EPS) * gc_ref[...]).astype(BF16)
    kv = _dot(cn, w_ref[...])
    zp = zp_ref[0]
    pe, sw = zp[:, :64], zp[:, 64:128]
    pe2 = jnp.sum(pe * pe, axis=-1, keepdims=True)
    gh = gh_ref[...]
    ct, st = ct_ref[...], st_ref[...]
    for h in range(MLA_HEADS):
        nope = kv[:, h * 256:h * 256 + 128]
        ms = (jnp.sum(nope * nope, axis=-1, keepdims=True) + pe2) / (MLA_NOPE + MLA_ROPE)
        rinv = lax.rsqrt(ms + NORM_EPS)
        rot = (pe * rinv * gh[:, 128:192]) * ct + (sw * rinv * gh[:, 192:256]) * st
        k_ref[0, h] = jnp.concatenate([nope * rinv * gh[:, :128], rot, jnp.zeros_like(rot)], axis=-1).astype(BF16)
        v_ref[0, h] = kv[:, h * 256 + 128:(h + 1) * 256].astype(BF16)


def _rope_tables(l_tok, m_tok):
    half = MLA_ROPE // 4
    freqs = ROPE_THETA ** (-jnp.arange(half, dtype=F32) / half)
    pos = jnp.arange(l_tok)
    ar = (pos // GRID_W).astype(F32)[:, None] * freqs[None, :]
    ac = (pos % GRID_W).astype(F32)[:, None] * freqs[None, :]
    ct = jnp.concatenate([jnp.cos(ar), jnp.cos(ar), jnp.cos(ac), jnp.cos(ac)], axis=-1)
    st = jnp.concatenate([-jnp.sin(ar), jnp.sin(ar), -jnp.sin(ac), jnp.sin(ac)], axis=-1)
    ct = jnp.concatenate([ct, jnp.ones((m_tok, MLA_ROPE), F32)], axis=0)
    st = jnp.concatenate([st, jnp.zeros((m_tok, MLA_ROPE), F32)], axis=0)
    return ct, st


def _swap_halves(a):
    return jnp.concatenate([a[..., 16:32], a[..., 0:16], a[..., 48:64], a[..., 32:48]], axis=-1)


def _dft1_body(x_ref, bh_ref, bl_ref, o_ref):
    xh, xl = _split2(x_ref[0])
    r = _dot(xh, bh_ref[...]) + (_dot(xh, bl_ref[...]) + _dot(xl, bh_ref[...]))
    o_ref[0] = r.astype(BF16)


def _dft2_body(c_ref, s_ref, xc_ref, xs_ref, *rest):
    rest[-1][0] = (_dot(c_ref[...], xc_ref[0]) - _dot(s_ref[...], xs_ref[0])).astype(BF16)


def _dft_mats(n):
    k = jnp.arange(n, dtype=jnp.int32)
    ang = ((k[:, None] * k[None, :]) % n).astype(F32) * (2.0 * math.pi / n)
    s = 1.0 / math.sqrt(n)
    return jnp.cos(ang) * s, jnp.sin(ang) * s


def _fourier_tables(l_tok, m_tok, ctx_out):
    cw, sw = _dft_mats(BRANCH_W // 4)
    eye = jnp.eye(4, dtype=F32)
    bd = jnp.concatenate([jnp.kron(eye, cw), jnp.kron(eye, sw)], axis=1)
    tabs = {'bd': _split2(bd), 'lat': tuple(m.astype(BF16) for m in _dft_mats(l_tok))}
    if ctx_out:
        tabs['ctx'] = tuple(m.astype(BF16) for m in _dft_mats(m_tok))
    return tabs


def _mixer_prep_body(zna, zcq, zckv, zkpe, zfn, gq, gk, e, gcq, wuq, ghq, ct, st, gckv, wukv, ghk, bh, bl,
                     nq, nk, nv, mq, mk, mv, xw):
    _na_prep_body(zna, gq, gk, e, nq, nk, nv)
    _mla_q_body(zcq, gcq, wuq, ghq, ct, st, mq)
    _mla_kv_body(zckv, zkpe, gckv, wukv, ghk, ct, st, mk, mv)
    _dft1_body(zfn, bh, bl, xw)


def _mixer_prep(z3, p, tabs, tr):
    b, t, _ = z3.shape
    bh, bl = tabs['bd']
    e = _head_ones(NA_HEADS * NA_DH, NA_DH)
    gq, gk = jnp.tile(p['na_gq'], (1, NA_HEADS)), jnp.tile(p['na_gk'], (1, NA_HEADS))
    consts = [gq, gk, e, p['g_cq'], p['w_uq'], p['g_q'], None, None, p['g_ckv'], p['w_ukv'], p['g_k'], bh, bl]
    rope = pl.BlockSpec((tr, MLA_ROPE), lambda b, i: (i, 0))

    def zcol(off, w):
        return pl.BlockSpec((1, tr, w), lambda b, i: (b, i, off // w))

    def hm(heads, d):
        return pl.BlockSpec((1, heads, tr, d), lambda b, i: (b, 0, i, 0)), _sds((b, heads, t, d), BF16)

    outs = [hm(NA_HEADS, NA_DH)] * 3 + [hm(MLA_HEADS, 256), hm(MLA_HEADS, 256), hm(MLA_HEADS, 128),
                                        (pl.BlockSpec((1, tr, 1024), lambda b, i: (b, i, 0)), _sds((b, t, 1024), BF16))]
    args = [tabs['rope_c'] if c is None else c for c in consts]
    args[7] = tabs['rope_s']
    return _call(
        _mixer_prep_body, (b, t // tr),
        [zcol(ZC_NA, 1536), zcol(ZC_CQ, 512), zcol(ZC_CKV, 256), zcol(ZC_KPE, 128), zcol(ZC_FN, 512)]
        + [rope if c is None else _full(c) for c in consts],
        [o[0] for o in outs], [o[1] for o in outs], name="mixer_prep")(z3, z3, z3, z3, z3, *args)


def _fourier(xw, tabs, tr, l_tok, m_tok, ctx_out):
    b = xw.shape[0]
    nrow = (l_tok + m_tok if ctx_out else l_tok) // tr

    def seq_dft(n, blk, mats, name, prev=None):
        rest = 0 if prev is not None else nrow * tr - n
        tm = _pick_tile(math.gcd(n, rest), (512, 256, 128, 64, 32, 16, 8))
        nq, nfill = n // tm, rest // tm
        mspec = pl.BlockSpec((tm, n), lambda b, i: (jnp.minimum(i, nq - 1), 0))
        xc = pl.BlockSpec((1, n, 512), lambda b, i: (b, blk, 0))
        xs = pl.BlockSpec((1, n, 512), lambda b, i: (b, blk, 1))
        off = blk * n // tm
        ps, pa, aliases = _into(prev, 4)
        return _call(
            _zero_tail(_dft2_body, nq) if nfill else _dft2_body, (b, nq + nfill), [mspec, mspec, xc, xs] + ps,
            pl.BlockSpec((1, tm, 512), lambda b, i: (b, off + i, 0)),
            _sds((b, nrow * tr, 512), BF16), name=name, aliases=aliases)(*mats, xw, xw, *pa)

    y = seq_dft(l_tok, 0, tabs['lat'], "dft_seq_latent")
    if ctx_out:
        y = seq_dft(m_tok, l_tok // m_tok, tabs['ctx'], "dft_seq_ctx", prev=y)
    return y


def _rw_prep_body(zk_ref, zv_ref, zr_ref, zl_ref, zg_ref, halo_ref, mu_ref, kk_ref, ka_ref, w0_ref, a0_ref,
                  w2_ref, a2_ref, trf_ref, trb_ref, e_ref,
                  v_ref, r_ref, g_ref, cl_ref, kd_ref, b_ref, kx_ref, *, tr):
    row = lax.broadcasted_iota(jnp.int32, (tr, 1), 0)
    halo = halo_ref[0, 0]
    mu = mu_ref[...]

    def mix(x, lo, hi):
        xp = jnp.where(row == 0, halo[0:1, lo:hi], pltpu.roll(x, 1, 0))
        xn = jnp.where(row == tr - 1, halo[1:2, lo:hi], pltpu.roll(x, tr - 1, 0))
        return x + (0.5 * (xp + xn) - x) * mu[:, lo:hi]

    k = mix(zk_ref[0], 0, 512)
    v = mix(zv_ref[0], 512, 1024)
    r = mix(zr_ref[0], 1024, 1536)
    l4 = mix(zl_ref[0], 1536, 1664)
    g_ref[0] = mix(zg_ref[0], 1664, 1792)

    wcat = w0_ref[...] + _dot3(jnp.tanh(l4), w2_ref[...])
    acat = a0_ref[...] + _dot3(l4, a2_ref[...])
    kk = k * kk_ref[...]
    kkn = kk / jnp.maximum(jnp.sqrt(_head_sumsq(kk, e_ref[...])), 1e-12)
    for d, tri_ref in enumerate((trf_ref, trb_ref)):
        w = wcat[:, d * 512:(d + 1) * 512]
        lw = -math.exp(-0.5) * jax.nn.sigmoid(w)
        a = jax.nn.sigmoid(acat[:, d * 512:(d + 1) * 512])
        kd = k * (1.0 + (a - 1.0) * ka_ref[...])
        cl = _dot_sel(tri_ref[...], lw)
        bb = kkn * a
        kx = kkn * jnp.exp(-lw)
        for h in range(RW_HEADS):
            lo = h * RW_DH
            cl_ref[0, d, h] = cl[:, lo:lo + RW_DH]
            kd_ref[0, d, h] = kd[:, lo:lo + RW_DH]
            b_ref[0, d, h] = bb[:, lo:lo + RW_DH]
            kx_ref[0, d, h] = kx[:, lo:lo + RW_DH]
    for h in range(RW_HEADS):
        lo = h * RW_DH
        v_ref[0, h] = v[:, lo:lo + RW_DH]
        r_ref[0, h] = r[:, lo:lo + RW_DH]


def _rw_prep(z3, halo, p, tr):
    b, t, _ = z3.shape
    c = RW_CHUNK
    ti = jnp.arange(tr)
    same = (ti[:, None] // c) == (ti[None, :] // c)
    trf = (same & (ti[None, :] <= ti[:, None])).astype(BF16)
    trb = (same & (ti[None, :] >= ti[:, None])).astype(BF16)

    def zcol(off, w):
        return pl.BlockSpec((1, tr, w), lambda b, i: (b, i, off // w))

    hm = pl.BlockSpec((1, RW_HEADS, tr, RW_DH), lambda b, i: (b, 0, i, 0))
    hmd = pl.BlockSpec((1, 2, RW_HEADS, tr, RW_DH), lambda b, i: (b, 0, 0, i, 0))
    s1 = _sds((b, RW_HEADS, t, RW_DH), F32)
    s2 = _sds((b, 2, RW_HEADS, t, RW_DH), F32)
    consts = [p['rw_mu'], p['rw_k_k'], p['rw_k_a'], p['rw_w0'], p['rw_a0'], p['rw_w2'], p['rw_a2'], trf, trb,
              _head_ones(RW_W, RW_DH)]
    return _call(
        functools.partial(_rw_prep_body, tr=tr), (b, t // tr),
        [zcol(ZC_RWK, 512), zcol(ZC_RWK + 512, 512), zcol(ZC_RWK + 1024, 512), zcol(ZC_L4, 128), zcol(ZC_G, 128),
         pl.BlockSpec((1, 1, 2, 1792), lambda b, i: (b, i, 0, 0))] + [_full(a) for a in consts],
        [hm, hm, pl.BlockSpec((1, tr, 128), lambda b, i: (b, i, 0)), hmd, hmd, hmd, hmd],
        [s1, s1, _sds((b, t, 128), F32), s2, s2, s2, s2], name="rw_prep")(z3, z3, z3, z3, z3, halo, *consts)


RW_P_HI = 1
RW_P_LO = 1


RW_INV_BASE = 4


def _unit_tri_inverse(x, upper):
    del upper
    c = x.shape[-1]
    ii = lax.broadcasted_iota(jnp.int32, (c, c), 0)
    jj = lax.broadcasted_iota(jnp.int32, (c, c), 1)

    def same_block(s):
        return (ii // s) == (jj // s)

    d = jnp.where(same_block(RW_INV_BASE), x, 0.0)
    t = (ii == jj).astype(F32) + d
    xp = d
    for _ in range(int(math.log2(RW_INV_BASE)) - 1):
        xp = _bmm('gts,gsu->gtu', xp, xp, RW_P_HI)
        t = t + _bmm('gts,gsu->gtu', xp, t, RW_P_HI)
    size = RW_INV_BASE
    while size < c:
        xoff = jnp.where(same_block(2 * size) & ~same_block(size), x, 0.0)
        t = t + _bmm('gts,gsu->gtu', t, _bmm('gts,gsu->gtu', xoff, t, RW_P_HI), RW_P_HI)
        size *= 2
    return t


def _rw_scan_tile(r_ref, v_ref, cl_ref, kd_ref, b_ref, kx_ref, y_ref, s_ref, *, tr, rev):
    c = RW_CHUNK
    nc = tr // c
    g = RW_HEADS * nc

    def ld(x):
        return x.reshape(g, c, RW_DH)

    r, v = ld(r_ref[0]), ld(v_ref[0])
    cl, kd, bb, kx = ld(cl_ref[0, 0]), ld(kd_ref[0, 0]), ld(b_ref[0, 0]), ld(kx_ref[0, 0])
    last = 0 if rev else c - 1
    ctot = cl[:, last:last + 1, :]
    e = jnp.exp(cl)
    ei = jnp.exp(-cl)
    ec = jnp.exp(ctot - cl)
    kkt, rt = kx * e, r * e
    kw, bw = kd * ei, bb * ei
    kc, bc = kd * ec, bb * ec
    a_cat = jnp.concatenate([kkt, rt], axis=1)
    pp = _bmm('gtd,gsd->gts', a_cat, jnp.concatenate([kw, bw], axis=1), RW_P_HI)
    ti = lax.broadcasted_iota(jnp.int32, (c, c), 0)
    si = lax.broadcasted_iota(jnp.int32, (c, c), 1)
    strict = (si > ti) if rev else (si < ti)
    incl = (si >= ti) if rev else (si <= ti)
    nmat = jnp.where(strict, pp[:, :c, :c], 0.0)
    x = -jnp.where(strict, pp[:, :c, c:], 0.0)
    ark = jnp.where(incl, pp[:, c:, :c], 0.0)
    arb = jnp.where(incl, pp[:, c:, c:], 0.0)
    tm = _unit_tri_inverse(x, rev)
    nav = _bmm('gts,gsd->gtd', jnp.concatenate([nmat, ark], axis=1), v, RW_P_HI)
    nv, arkv = nav[:, :c], nav[:, c:]
    ta = _bmm('gts,gsd->gtd', tm, jnp.concatenate([kkt, nv], axis=2), RW_P_HI)
    ata = _bmm('gts,gsd->gtd', arb, ta, RW_P_HI)
    a2 = rt - ata[:, :, :RW_DH]
    y0 = arkv - ata[:, :, RW_DH:]
    tb = _bmm('gtk,gtd->gkd', bc, ta, RW_P_HI)
    di = lax.broadcasted_iota(jnp.int32, (RW_DH, RW_DH), 0)
    dj = lax.broadcasted_iota(jnp.int32, (RW_DH, RW_DH), 1)
    gmt = jnp.where(di == dj, jnp.exp(ctot), 0.0) - tb[:, :, :RW_DH]
    hmt = _bmm('gtk,gtv->gkv', kc, v, RW_P_HI) - tb[:, :, RW_DH:]

    def per_chunk(x):
        return x.reshape(RW_HEADS, nc, x.shape[1], x.shape[2])

    a2, y0, gmt, hmt = per_chunk(a2), per_chunk(y0), per_chunk(gmt), per_chunk(hmt)
    st = s_ref[...]
    ys = [None] * nc
    for ci in (range(nc - 1, -1, -1) if rev else range(nc)):
        ys[ci] = _bmm('htk,hkv->htv', a2[:, ci], st, RW_P_HI) + y0[:, ci]
        st = _bmm('hke,hev->hkv', gmt[:, ci], st, RW_P_HI) + hmt[:, ci]
    s_ref[...] = st
    y_ref[0] = jnp.concatenate(ys, axis=1)


def _rw_scan_body(*refs, tr):
    fwd, bwd, (yf_ref, yb_ref, sf_ref, sb_ref) = refs[0:6], refs[6:12], refs[12:16]

    @pl.when(pl.program_id(1) == 0)
    def _():
        sf_ref[...] = jnp.zeros_like(sf_ref)
        sb_ref[...] = jnp.zeros_like(sb_ref)

    _rw_scan_tile(*fwd, yf_ref, sf_ref, tr=tr, rev=False)
    _rw_scan_tile(*bwd, yb_ref, sb_ref, tr=tr, rev=True)


def _rw_scan(r, v, cl, kd, bb, kx, tr, nl, nm):
    b, _, t, _ = r.shape
    nt = nl + nm
    tile_f = lambda j: jnp.where(j < nm, nl + j, j - nm)
    tile_b = lambda j: nt - 1 - j

    def specs(tile, d):
        hm = pl.BlockSpec((1, RW_HEADS, tr, RW_DH), lambda b, j: (b, 0, tile(j), 0))
        hmd = pl.BlockSpec((1, 1, RW_HEADS, tr, RW_DH), lambda b, j: (b, d, 0, tile(j), 0))
        return hm, hmd

    hf, hfd = specs(tile_f, 0)
    hb, hbd = specs(tile_b, 1)
    shp = _sds((b, RW_HEADS, t, RW_DH), F32)
    state = pltpu.VMEM((RW_HEADS, RW_DH, RW_DH), F32)
    return _call(
        functools.partial(_rw_scan_body, tr=tr), (b, nt),
        [hf, hf, hfd, hfd, hfd, hfd, hb, hb, hbd, hbd, hbd, hbd], [hf, hb], [shp, shp],
        scratch=[state, state], sem=("parallel", "arbitrary"),
        name="rw_scan")(r, v, cl, kd, bb, kx, r, v, cl, kd, bb, kx)


def _rw_out_body(yf_ref, yb_ref, r_ref, v_ref, kd_ref, g_ref, lnw_ref, lnb_ref, rk_ref, g2_ref, o_ref):
    y = yf_ref[0] + yb_ref[0]
    mu = jnp.mean(y, axis=-1, keepdims=True)
    var = jnp.mean(jnp.square(y - mu), axis=-1, keepdims=True)
    yn = (y - mu) * lax.rsqrt(var + RW_GN_EPS) * lnw_ref[...] + lnb_ref[...]
    ksum = kd_ref[0, 0] + kd_ref[0, 1]
    bonus = jnp.sum(r_ref[0] * ksum * rk_ref[...], axis=-1, keepdims=True) * v_ref[0]
    o = yn + bonus
    o = jnp.concatenate([o[h] for h in range(RW_HEADS)], axis=-1)
    gate = _dot3(jax.nn.sigmoid(g_ref[0]), g2_ref[...])
    o_ref[0] = (o * gate).astype(BF16)


def _rw_out(yf, yb, r, v, kd, g, p, tr, nrow):
    b = r.shape[0]
    hm = pl.BlockSpec((1, RW_HEADS, tr, RW_DH), lambda b, i: (b, 0, i, 0))
    consts = [p['rw_ln_w'], p['rw_ln_b'], p['rw_r_k'], p['rw_g2']]
    return _call(
        _rw_out_body, (b, nrow),
        [hm, hm, hm, hm, pl.BlockSpec((1, 2, RW_HEADS, tr, RW_DH), lambda b, i: (b, 0, 0, i, 0)),
         pl.BlockSpec((1, tr, 128), lambda b, i: (b, i, 0))] + [_full(a) for a in consts],
        pl.BlockSpec((1, tr, RW_W), lambda b, i: (b, i, 0)),
        _sds((b, nrow * tr, RW_W), BF16), name="rw_out")(yf, yb, r, v, kd, g, *consts)


GATE_COL0 = KEY_COLS + 2144


def _merge_body(h_ref, y0_ref, y1_ref, y2_ref, y3_ref, g0_ref, g1_ref, g2_ref, g3_ref, wb_ref, o_ref):
    h = h_ref[0]
    acc = None
    for i, (y_ref, wg_ref) in enumerate(zip((y0_ref, y1_ref, y2_ref, y3_ref), (g0_ref, g1_ref, g2_ref, g3_ref))):
        t = jax.nn.sigmoid(_dot(h, wg_ref[...])) * _dot(y_ref[0], wb_ref[0, i])
        acc = t if acc is None else acc + t
    o_ref[0] = acc.astype(BF16)


def _merge(h, ys, wg, wbr, layer, rows):
    b, _, d = h.shape
    tn = 512
    nn = d // tn
    tm = _pick_tile(rows, (1024, 768, 512, 256, 128, 64))
    row = lambda w: pl.BlockSpec((1, tm, w), lambda n, b, i: (b, i, 0))
    gspecs = [pl.BlockSpec((d, tn), functools.partial(lambda n, b, i, k: (0, k * nn + n), k=k))
              for k in range(N_BRANCH)]
    return _call(
        _merge_body, (nn, b, rows // tm),
        [row(d)] + [row(BRANCH_W)] * 4 + gspecs
        + [pl.BlockSpec((1, N_BRANCH, BRANCH_W, tn), lambda n, b, i: (layer, 0, 0, n))],
        pl.BlockSpec((1, tm, tn), lambda n, b, i: (b, i, n)),
        _sds((b, rows, d), BF16), name="merge")(h, *ys, wg, wg, wg, wg, wbr)


def _outproj_body(a_ref, w_ref, x_ref, md_ref, o_ref, *, tm, l_tok):
    row = pl.program_id(1) * tm + lax.broadcasted_iota(jnp.int32, (tm, 1), 0)
    gate = jnp.where(row < l_tok, md_ref[0, 0, 2:3, :], md_ref[0, 1, 2:3, :])
    o_ref[0] = x_ref[0] + gate * _dot(a_ref[0], w_ref[0])


def _outproj(acc, w_out, layer, xs, md, l_tok):
    b, rows, d = acc.shape
    tm = _pick_tile(rows, (768, 512, 256, 128, 64))
    row = pl.BlockSpec((1, tm, d), lambda b, i: (b, i, 0))
    return _call(
        functools.partial(_outproj_body, tm=tm, l_tok=l_tok), (b, rows // tm),
        [row, pl.BlockSpec((1, d, d), lambda b, i: (layer, 0, 0)), row,
         pl.BlockSpec((1, 2, 6, d), lambda b, i: (b, 0, 0, 0))], row,
        _sds((b, rows, d), F32), name="out_proj")(acc, w_out, xs, md)


def _select_body(lg_ref, slot_ref, aff_ref, slotc_ref, *, n, cap):
    lg = lg_ref[0]
    lane = lax.broadcasted_iota(jnp.int32, lg.shape, 1)
    lg = jnp.where(lane < N_EXPERTS, lg, NEG_INF)
    ex = jnp.exp(lg - lg.max(axis=-1, keepdims=True))
    aff = ex / ex.sum(axis=-1, keepdims=True)
    aff_t = aff.T[:N_EXPERTS]

    def count_ge(t):
        return jnp.sum((aff_t >= t).astype(F32), axis=1, keepdims=True)

    tiny = 2.0 ** -126
    lo = jnp.full((N_EXPERTS, 1), tiny, F32)
    for sh in (64, 32, 16, 8, 4, 2, 1):
        cand = lo * (2.0 ** sh)
        lo = jnp.where(count_ge(cand) >= cap, cand, lo)
    hi = lo * 2.0
    below = count_ge(tiny) < cap
    lo = jnp.where(below, 0.0, lo)
    hi = jnp.where(below, tiny, hi)
    for _ in range(40):
        mid = lo + (hi - lo) * 0.5
        ok = count_ge(mid) >= cap
        lo = jnp.where(ok, mid, lo)
        hi = jnp.where(ok, hi, mid)
    gt = aff_t >= hi
    eq = (aff_t >= lo) & (aff_t < hi)
    need = cap - jnp.sum(gt.astype(F32), axis=1, keepdims=True)
    both = jnp.concatenate([gt.astype(BF16), eq.astype(BF16)], axis=0)
    cw = min(n, 512)
    pre = []
    for cb in range(n // cw):
        tp = lax.broadcasted_iota(jnp.int32, (n, cw), 0)
        tt = lax.broadcasted_iota(jnp.int32, (n, cw), 1) + cb * cw
        pre.append(_dot(both, (tp < tt).astype(BF16)))
    pre = jnp.concatenate(pre, axis=1) if len(pre) > 1 else pre[0]
    pre_gt, pre_eq = pre[:N_EXPERTS], pre[N_EXPERTS:]
    sel = gt | (eq & (pre_eq < need))
    slot = jnp.where(sel, pre_gt + jnp.minimum(pre_eq, need), -1.0)
    slot_ref[0] = slot.astype(jnp.int32)
    aff_ref[0] = aff_t
    pad = jnp.full((128 - N_EXPERTS, n), -1.0, F32)
    slotc_ref[0] = jnp.concatenate([slot, pad], axis=0).T.astype(jnp.int32)


def _select(logits, n, blk, cap, name):
    b = logits.shape[0]
    er = pl.BlockSpec((1, N_EXPERTS, n), lambda b: (b, 0, 0))
    return _call(
        functools.partial(_select_body, n=n, cap=cap), (b,),
        [pl.BlockSpec((1, n, 128), lambda b: (b, blk, 0))],
        [er, er, pl.BlockSpec((1, n, 128), lambda b: (b, 0, 0))],
        [_sds((b, N_EXPERTS, n), jnp.int32), _sds((b, N_EXPERTS, n), F32), _sds((b, n, 128), jnp.int32)],
        name=name)(logits)


def _ffn_body(h_ref, slot_ref, aff_ref, w1_ref, w3_ref, w2_ref, o_ref, *, grp, n, cap):
    xs, gates = [], []
    jrow = lax.broadcasted_iota(jnp.int32, (cap, n), 0)
    for gi in range(grp):
        pick = jrow == slot_ref[gi, 0]
        xs.append(_dot(pick.astype(BF16), h_ref[gi]).astype(BF16))
        gates.append(jnp.sum(jnp.where(pick, aff_ref[gi, 0], 0.0), axis=1, keepdims=True))
    xe = jnp.concatenate(xs, axis=0) if grp > 1 else xs[0]
    gate = jnp.concatenate(gates, axis=0) if grp > 1 else gates[0]
    a = _dot(xe, w1_ref[0, 0])
    u = _dot(xe, w3_ref[0, 0])
    hm = (a * jax.nn.sigmoid(a) * u).astype(BF16)
    ye = (_dot(hm, w2_ref[0, 0]) * gate).astype(BF16)
    for gi in range(grp):
        o_ref[0, gi] = ye[gi * cap:(gi + 1) * cap]


def _moe_ffn(h2, slot, aff, w1, w3, w2, layer, n, blk, cap, grp, name):
    b, _, d = h2.shape
    ff = w1.shape[-1]
    sr = pl.BlockSpec((grp, 1, 1, n), lambda e, g: (g, e, 0, 0))
    return _call(
        functools.partial(_ffn_body, grp=grp, n=n, cap=cap), (N_EXPERTS, b // grp),
        [pl.BlockSpec((grp, n, d), lambda e, g: (g, blk, 0)), sr, sr,
         pl.BlockSpec((1, 1, d, ff), lambda e, g: (layer, e, 0, 0)),
         pl.BlockSpec((1, 1, d, ff), lambda e, g: (layer, e, 0, 0)),
         pl.BlockSpec((1, 1, ff, d), lambda e, g: (layer, e, 0, 0))],
        pl.BlockSpec((1, grp, cap, d), lambda e, g: (e, g, 0, 0)),
        _sds((N_EXPERTS, b, cap, d), BF16), name=name)(
            h2, slot.reshape(b, N_EXPERTS, 1, n), aff.reshape(b, N_EXPERTS, 1, n), w1, w3, w2)


def _combine_body(sc_ref, ye_ref, x_ref, md_ref, *rest, cap):
    o_ref = rest[-1]
    sc = sc_ref[0]
    jj = lax.broadcasted_iota(jnp.int32, (sc.shape[0], cap), 1)
    acc = None
    for e in range(N_EXPERTS):
        put = (sc[:, e:e + 1] == jj).astype(BF16)
        t = _dot(put, ye_ref[e, 0])
        acc = t if acc is None else acc + t
    o_ref[0] = x_ref[0] + md_ref[0, 0, 5:6, :] * acc


def _moe_combine(slotc, ye, x1, md, n, blk_rows, is_ctx, cap, name, out_rows, prev=None):
    b, _, d = x1.shape
    rest = 0 if prev is not None else out_rows - blk_rows - n
    tm = _pick_tile(math.gcd(n, rest), (512, 256, 128, 64, 32, 16, 8))
    nq, nfill = n // tm, rest // tm
    off = blk_rows // tm
    extra_specs, extra_args, aliases = _into(prev, 4)
    body = functools.partial(_combine_body, cap=cap)
    clamp = lambda i: jnp.minimum(i, nq - 1)
    return _call(
        _zero_tail(body, nq) if nfill else body, (b, nq + nfill),
        [pl.BlockSpec((1, tm, 128), lambda b, i: (b, clamp(i), 0)),
         pl.BlockSpec((N_EXPERTS, 1, cap, d), lambda b, i: (0, b, 0, 0)),
         pl.BlockSpec((1, tm, d), lambda b, i: (b, off + clamp(i), 0)),
         pl.BlockSpec((1, 1, 6, d), lambda b, i: (b, 1 if is_ctx else 0, 0, 0))] + extra_specs,
        pl.BlockSpec((1, tm, d), lambda b, i: (b, off + i, 0)),
        _sds((b, out_rows, d), F32), name=name, aliases=aliases)(slotc, ye, x1, md, *extra_args)


def _moe(x1, h2, logits, md, w1, w3, w2, layer, l_tok, m_tok, ctx_out):
    b = x1.shape[0]
    cap_l = CAPACITY_FACTOR * l_tok // N_EXPERTS
    slot, aff, slotc = _select(logits, l_tok, 0, cap_l, "moe_select_latent")
    ye = _moe_ffn(h2, slot, aff, w1, w3, w2, layer, l_tok, 0, cap_l, 1, "moe_ffn_latent")
    out_rows = l_tok + m_tok if ctx_out else l_tok
    x2 = _moe_combine(slotc, ye, x1, md, l_tok, 0, False, cap_l, "moe_combine_latent", out_rows)
    if not ctx_out:
        return x2
    cap_c = CAPACITY_FACTOR * m_tok // N_EXPERTS
    blk = l_tok // m_tok
    slot, aff, slotc = _select(logits, m_tok, blk, cap_c, "moe_select_ctx")
    ye = _moe_ffn(h2, slot, aff, w1, w3, w2, layer, m_tok, blk, cap_c, b, "moe_ffn_ctx")
    return _moe_combine(slotc, ye, x1, md, m_tok, l_tok, True, cap_c, "moe_combine_ctx", out_rows, prev=x2)


def _layer_params(l, w_in, na_q_norm, na_k_norm, na_rpb, mla_cq_norm, mla_ckv_norm, mla_w_uq, mla_w_ukv,
                  mla_q_norm, mla_k_norm, rw_mu_ks, rw_mu_qs, rw_w0, rw_w2, rw_a0, rw_a2, rw_g2, rw_k_k, rw_k_a,
                  rw_r_k, rw_ln_w, rw_ln_b, w_br, w_out, rows):
    w = w_in[l]
    d = w.shape[0]
    kq = KEY_COLS
    kpe = w[:, 1280:1344]
    zpad = lambda n: jnp.zeros((d, n), w.dtype)
    wz = jnp.concatenate([
        w[:, kq:kq + 512], w[:, 0:512], w[:, 512:1024],
        w[:, kq + 512:kq + 1024],
        w[:, 1344:1856], w[:, 1856:2368], w[:, kq + 1024:kq + 1536],
        w[:, kq + 1632:kq + 2144],
        w[:, 1024:1280],
        kpe, _swap_halves(kpe),
        w[:, 2368:2496],
        w[:, kq + 1536:kq + 1632], zpad(32)], axis=1).astype(BF16)
    assert wz.shape[1] == NZ
    wg = w[:, GATE_COL0:].astype(BF16)

    def head_gain(g):
        return jnp.concatenate([g[:128], g[128:], _swap_halves(g[128:])])[None, :]

    wq = mla_w_uq[l].reshape(MLA_QLORA, MLA_HEADS, MLA_NOPE + MLA_ROPE)
    wq = jnp.concatenate([wq, _swap_halves(wq[:, :, MLA_NOPE:])], axis=-1).reshape(MLA_QLORA, MLA_HEADS * 256)
    w2cat = jnp.zeros((128, 1024), F32).at[0:32, 0:512].set(rw_w2[l, 0]).at[32:64, 512:].set(rw_w2[l, 1])
    a2cat = jnp.zeros((128, 1024), F32).at[64:96, 0:512].set(rw_a2[l, 0]).at[96:128, 512:].set(rw_a2[l, 1])
    mu = jnp.concatenate([rw_mu_ks[l][:1024], rw_mu_qs[l][:512], rw_mu_ks[l][1024:], rw_mu_qs[l][512:],
                          jnp.zeros((32,), F32)])[None, :]
    hd = lambda a: a.reshape(RW_HEADS, 1, RW_DH)
    return {
        'wz': wz, 'wg': wg,
        'na_gq': na_q_norm[l][None, :], 'na_gk': na_k_norm[l][None, :],
        'na_bias': _na_bias_table(na_rpb[l], rows),
        'g_cq': mla_cq_norm[l][None, :], 'g_ckv': mla_ckv_norm[l][None, :],
        'w_uq': wq.astype(BF16), 'w_ukv': mla_w_ukv[l].astype(BF16),
        'g_q': head_gain(mla_q_norm[l]), 'g_k': head_gain(mla_k_norm[l]),
        'rw_mu': mu, 'rw_k_k': rw_k_k[l][None, :], 'rw_k_a': rw_k_a[l][None, :],
        'rw_w0': rw_w0[l].reshape(1, 1024), 'rw_a0': rw_a0[l].reshape(1, 1024), 'rw_w2': w2cat, 'rw_a2': a2cat,
        'rw_ln_w': hd(rw_ln_w[l]), 'rw_ln_b': hd(rw_ln_b[l]), 'rw_r_k': hd(rw_r_k[l]),
        'rw_g2': jnp.concatenate([rw_g2[l], jnp.zeros((32, RW_W), F32)], axis=0),
    }


def _rw_halo(z3, tr, nl):
    b, t, _ = z3.shape
    nt = t // tr
    z4 = z3.reshape(b, nt, tr, z3.shape[-1])
    pick = lambda a: jnp.concatenate([a[..., ZC_RWK:ZC_RWK + 1536], a[..., ZC_L4:ZC_L4 + 256]], axis=-1)
    first, last = pick(z4[:, :, 0, :]), pick(z4[:, :, tr - 1, :])
    zero = jnp.zeros_like(first[:, :1])
    prev = jnp.concatenate([zero, last[:, :-1]], axis=1)
    nxt = jnp.concatenate([first[:, 1:], zero], axis=1)
    tile = jnp.arange(nt)[None, :, None]
    prev = jnp.where(tile == nl, 0.0, prev)
    nxt = jnp.where(tile == nl - 1, 0.0, nxt)
    return jnp.stack([prev, nxt], axis=2)


def _layer(xs, md, p, g1n, g2n, wr, deep, layer, tabs, l_tok, m_tok, tr, ctx_out):
    b, t, d = xs.shape
    nl, nm = l_tok // tr, m_tok // tr
    nrow = (nl + nm) if ctx_out else nl

    h = _norm1(xs, g1n, md, tr, nl)
    z3 = _matmul(h.reshape(b * t, d), p['wz'], F32).reshape(b, t, NZ)

    nq, nk, nv, mq, mk, mv, xw = _mixer_prep(z3, p, tabs, tr)
    rows = nrow * tr
    y_na = _na_attention(nq, nk, nv, p['na_bias'], l_tok, m_tok, rows)
    y_mla = _mla_attention(mq, mk, mv, tr, l_tok, rows)
    rv, rr, rg, cl, kd, bb, kx = _rw_prep(z3, _rw_halo(z3, tr, nl), p, tr)
    yf, yb = _rw_scan(rr, rv, cl, kd, bb, kx, tr, nl, nm)
    y_rw = _rw_out(yf, yb, rr, rv, kd, rg, p, tr, nrow)
    y_fn = _fourier(xw, tabs, tr, l_tok, m_tok, ctx_out)

    acc = _merge(h, (y_fn, y_na, y_mla, y_rw), p['wg'], deep['w_br'], layer, rows)
    x1 = _outproj(acc, deep['w_out'], layer, xs, md, l_tok)
    h2, logits = _norm2_router(x1, g2n, md, wr, tr, nl)
    return _moe(x1, h2, logits, md, deep['w1'], deep['w3'], deep['w2'], layer, l_tok, m_tok, ctx_out)


def kernel(x, c, ctx, c_ctx, ada_w, ada_b, norm1_g, norm2_g, w_in, na_q_norm, na_k_norm, na_rpb, mla_cq_norm, mla_ckv_norm, mla_w_uq, mla_w_ukv, mla_q_norm, mla_k_norm, rw_mu_ks, rw_mu_qs, rw_w0, rw_w2, rw_a0, rw_a2, rw_g2, rw_k_k, rw_k_a, rw_r_k, rw_ln_w, rw_ln_b, w_br, w_out, moe_router, moe_w1, moe_w3, moe_w2):
    b, l_tok, d = x.shape
    m_tok = ctx.shape[1]
    depth = ada_w.shape[0]
    tr = min(m_tok, 256)
    assert l_tok % tr == 0 and m_tok % tr == 0 and l_tok % m_tok == 0 and tr % RW_CHUNK == 0
    assert l_tok % GRID_W == 0 and m_tok % 128 == 0

    nr = -(-(b + 1) // 8) * 8
    cc = jnp.concatenate([c, c_ctx[None, :], jnp.zeros((nr - b - 1, d), F32)], axis=0)
    mod = _modulation(cc, ada_w, ada_b)
    tabs = {}
    tabs['rope_c'], tabs['rope_s'] = _rope_tables(l_tok, m_tok)

    deep = {'w1': moe_w1.astype(BF16), 'w3': moe_w3.astype(BF16), 'w2': moe_w2.astype(BF16),
            'w_br': w_br.astype(BF16), 'w_out': w_out.astype(BF16)}
    tabs.update(_fourier_tables(l_tok, m_tok, depth > 1))
    xs = jnp.concatenate([x, ctx], axis=1)
    for l in range(depth):
        ctx_out = l < depth - 1
        p = _layer_params(l, w_in, na_q_norm, na_k_norm, na_rpb, mla_cq_norm, mla_ckv_norm, mla_w_uq, mla_w_ukv,
                          mla_q_norm, mla_k_norm, rw_mu_ks, rw_mu_qs, rw_w0, rw_w2, rw_a0, rw_a2, rw_g2, rw_k_k,
                          rw_k_a, rw_r_k, rw_ln_w, rw_ln_b, w_br, w_out, l_tok // GRID_W)
        md = jnp.stack([mod[l, :b], jnp.broadcast_to(mod[l, b], (b, 6 * d))], axis=1).reshape(b, 2, 6, d)
        wr = jnp.concatenate([moe_router[l], jnp.zeros((d, 128 - N_EXPERTS), F32)], axis=1)
        xs = _layer(xs, md, p, norm1_g[l][None, :], norm2_g[l][None, :], wr, deep, l, tabs, l_tok, m_tok, tr, ctx_out)
    return xs
```
